```python
import math
import jax, jax.numpy as jnp
from jax import lax
import numpy as np

D_MODEL = 1024
BATCH = 16
SEQ = 256
DEPTH = 2
DEC_BATCH = 8
DEC_SEQ = 1024
PAST_LEN = 256

GRID_W = 64
EPS = 1e-6
MIN_GATE = 1e-30
ROPE_BASE = 10000.0
Q_BLOCK = 128
H_DA = 8
DA_QK = 32
DA_V = 64
H_RET = 4
RET_DK = 64
RET_DV = 64
RET_CHUNK = 64
H_HG = 4
HG_DK = 64
HG_DV = 64
HG_CHUNK = 16
D_DA_QK = H_DA * 2 * DA_QK
D_DA_V = H_DA * DA_V
D_RET_QK = H_RET * RET_DK
D_RET_V = H_RET * RET_DV
D_HG_K = H_HG * HG_DK
D_HG_V = H_HG * HG_DV
IN_SPLITS = (D_DA_QK, D_DA_QK, D_DA_V,
             D_RET_QK, D_RET_QK, D_RET_V, D_RET_V,
             D_HG_K, D_HG_K, D_HG_K, D_HG_V, D_HG_V)
D_IN = 2 * D_DA_QK + D_DA_V + 2 * D_RET_QK + 2 * D_RET_V + 3 * D_HG_K + 2 * D_HG_V
D_MIX = D_DA_V + D_RET_V + D_HG_V
N_EXPERTS = 16
N_GROUPS = 4
EXPERTS_PER_GROUP = 4
TOP_K = 2
D_EXPERT = 512
MASKED_SCORE = -2.0

kernel_name = "hybrid_diffusion_prefix_step"


def rmsnorm(x, w):
    xf = x.astype(jnp.float32)
    y = xf * lax.rsqrt(jnp.mean(xf * xf, axis=-1, keepdims=True) + EPS)
    return (y * w.astype(jnp.float32)).astype(x.dtype)


def _split_columns(h):
    parts, start = [], 0
    for size in IN_SPLITS:
        parts.append(h[..., start:start + size])
        start += size
    return parts


def rope_tables(n_tokens):
    rows = n_tokens // GRID_W
    pos_r = jnp.repeat(jnp.arange(rows, dtype=jnp.float32), GRID_W)
    pos_c = jnp.tile(jnp.arange(GRID_W, dtype=jnp.float32), rows)
    n_freq = DA_QK // 4
    inv = ROPE_BASE ** (-jnp.arange(n_freq, dtype=jnp.float32) / n_freq)
    ang = jnp.concatenate([pos_r[:, None] * inv, pos_c[:, None] * inv], axis=-1)
    return jnp.cos(ang), jnp.sin(ang)


def apply_rope(x, cos, sin):
    xp = x.reshape(*x.shape[:-1], DA_QK // 2, 2)
    x1, x2 = xp[..., 0], xp[..., 1]
    c = cos[None, :, None, None, :].astype(x.dtype)
    s = sin[None, :, None, None, :].astype(x.dtype)
    return jnp.stack([x1 * c - x2 * s, x1 * s + x2 * c], axis=-1).reshape(x.shape)


def diff_attention(q, k, v, lam):
    B, Lq = q.shape[0], q.shape[1]
    nb = Lq // Q_BLOCK
    qb = q.reshape(B, nb, Q_BLOCK, *q.shape[2:]).swapaxes(0, 1)
    scale = DA_QK ** -0.5

    def block(qblk):
        s = jnp.einsum('bqhrd,bkhrd->brhqk', qblk, k, preferred_element_type=jnp.float32) * scale
        p = jax.nn.softmax(s, axis=-1)
        w = (p[:, 0] - lam * p[:, 1]).astype(v.dtype)
        return jnp.einsum('bhqk,bkhd->bqhd', w, v)

    out = lax.map(block, qb)
    return out.swapaxes(0, 1).reshape(B, Lq, *v.shape[2:])


def chunk_recurrence(q, k, v, log_a, s0, chunk):
    f32 = jnp.float32
    B, H, L, _ = q.shape
    n = L // chunk
    rs = lambda t: t.astype(f32).reshape(B, H, n, chunk, t.shape[-1])
    q, k, v, la = rs(q), rs(k), rs(v), rs(log_a)
    b = jnp.cumsum(la, axis=-2)
    b_end = b[..., -1:, :]
    mask = jnp.tril(jnp.ones((chunk, chunk), dtype=bool))[:, :, None]
    diff = b[..., :, None, :] - b[..., None, :, :]
    decay = jnp.where(mask, jnp.exp(jnp.minimum(diff, 0.0)), 0.0)
    if la.shape[-1] == 1:
        scores = jnp.einsum('bhncd,bhnsd->bhncs', q, k) * decay[..., 0]
    else:
        scores = jnp.einsum('bhncd,bhnsd,bhncsd->bhncs', q, k, decay)
    o_intra = jnp.einsum('bhncs,bhnse->bhnce', scores, v)
    kv = jnp.einsum('bhncd,bhnce->bhnde', k * jnp.exp(b_end - b), v)
    a_chunk = jnp.exp(b_end[..., 0, :])

    def step(S, inp):
        a_c, kv_c = inp
        return a_c[..., :, None] * S + kv_c, S

    s_fin, s_prev = lax.scan(step, s0.astype(f32),
                             (jnp.moveaxis(a_chunk, 2, 0), jnp.moveaxis(kv, 2, 0)))
    s_prev = jnp.moveaxis(s_prev, 0, 2)
    o_inter = jnp.einsum('bhncd,bhnde->bhnce', q * jnp.exp(b), s_prev)
    return (o_intra + o_inter).reshape(B, H, L, v.shape[-1]), s_fin


def bidir_recurrence(q, k_f, k_b, v, la_f, la_b, s0_f, s0_b, chunk):
    flip = lambda t: jnp.flip(t, axis=2)
    o_f, s_f = chunk_recurrence(q, k_f, v, la_f, s0_f, chunk)
    o_b, s_b = chunk_recurrence(flip(q), flip(k_b), flip(v), flip(la_b), s0_b, chunk)
    return o_f + flip(o_b), s_f, s_b


def token_mixers(hn, l, p, ctx):
    f32 = jnp.float32
    B, L, _ = hn.shape
    proj = jnp.einsum('bld,de->ble', hn, p['w_in'][l])
    (da_q, da_k, da_v, r_q, r_k, r_v, r_g, g_q, g_ff, g_fb, g_i, g_g) = _split_columns(proj)

    lam_init = 0.8 - 0.6 * math.exp(-0.3 * l)
    lp = p['da_lambda'][l].astype(f32)
    lam = jnp.exp(jnp.sum(lp[0] * lp[1])) - jnp.exp(jnp.sum(lp[2] * lp[3])) + lam_init
    q = rmsnorm(da_q.reshape(B, L, H_DA, 2, DA_QK), p['da_qnorm'][l])
    k = rmsnorm(da_k.reshape(B, L, H_DA, 2, DA_QK), p['da_knorm'][l])
    v = da_v.reshape(B, L, H_DA, DA_V)
    if ctx is None:
        o_da = diff_attention(q, k, v, lam)
        s_ret0 = jnp.zeros((B, 2, H_RET, RET_DK, RET_DV), f32)
        s_hg0 = jnp.zeros((B, 2, H_HG, HG_DK, HG_DV), f32)
    else:
        ctx_k, ctx_v, s_ret0, s_hg0 = ctx
        cos, sin = rope_tables(L)
        q = apply_rope(q, cos, sin)
        k_lat = apply_rope(k, cos, sin)
        o_da = diff_attention(q, jnp.concatenate([k_lat, ctx_k.astype(k_lat.dtype)], axis=1),
                              jnp.concatenate([v, ctx_v.astype(v.dtype)], axis=1), lam)
    o_da = rmsnorm(o_da, p['da_subln'][l]) * (1.0 - lam_init)

    to_heads = lambda t, h, d: t.reshape(B, L, h, d).transpose(0, 2, 1, 3)

    rq = to_heads(r_q, H_RET, RET_DK)
    rk = to_heads(r_k, H_RET, RET_DK) * (RET_DK ** -0.5)
    rv = to_heads(r_v, H_RET, RET_DV)
    log_gamma = -jnp.exp(p['ret_decay'][l].astype(f32))
    la = lambda g: jnp.broadcast_to(g[None, :, None, None], (B, H_RET, L, 1))
    o_ret, sr_f, sr_b = bidir_recurrence(rq, rk, rk, rv, la(log_gamma[0]), la(log_gamma[1]),
                                         s_ret0[:, 0], s_ret0[:, 1], RET_CHUNK)
    o_ret = o_ret.transpose(0, 2, 1, 3).astype(hn.dtype)
    o_ret = rmsnorm(o_ret, p['ret_norm'][l]) * jax.nn.silu(r_g.reshape(B, L, H_RET, RET_DV))

    p_lb = jax.nn.softmax(p['hg_lb'].astype(f32), axis=0)
    lb = (jnp.cumsum(p_lb, axis=0) - p_lb[0])[l].reshape(H_HG, 1, HG_DK)
    hq = to_heads(jax.nn.silu(g_q), H_HG, HG_DK)
    zf = to_heads(g_ff, H_HG, HG_DK).astype(f32)
    zb = to_heads(g_fb, H_HG, HG_DK).astype(f32)
    log_f = lambda z: jnp.log(jnp.maximum(lb + (1.0 - lb) * jax.nn.sigmoid(z), MIN_GATE))
    one_minus_f = lambda z: (1.0 - lb) * jax.nn.sigmoid(-z)
    hv = to_heads(g_i, H_HG, HG_DV)
    o_hg, sh_f, sh_b = bidir_recurrence(hq, one_minus_f(zf), one_minus_f(zb), hv, log_f(zf), log_f(zb),
                                        s_hg0[:, 0], s_hg0[:, 1], HG_CHUNK)
    o_hg = o_hg.transpose(0, 2, 1, 3).astype(hn.dtype)
    o_hg = rmsnorm(o_hg, p['hg_norm'][l]) * jax.nn.silu(g_g.reshape(B, L, H_HG, HG_DV))

    mixed = jnp.concatenate([o_da.reshape(B, L, D_DA_V), o_ret.reshape(B, L, D_RET_V),
                             o_hg.reshape(B, L, D_HG_V)], axis=-1)
    out = jnp.einsum('blm,md->bld', mixed, p['w_out'][l])
    if ctx is None:
        return out, (k, v, jnp.stack([sr_f, sr_b], axis=1), jnp.stack([sh_f, sh_b], axis=1))
    return out, None


def moe(h, l, p):
    f32 = jnp.float32
    B, L, _ = h.shape
    scores = jax.nn.sigmoid(jnp.einsum('bld,de->ble', h, p['w_router'], preferred_element_type=f32))
    sel = scores + p['router_bias'].astype(f32)
    group_score = jnp.sum(lax.top_k(sel.reshape(B, L, N_GROUPS, EXPERTS_PER_GROUP), TOP_K)[0], axis=-1)
    best_group = jnp.argmax(group_score, axis=-1)
    in_group = best_group[..., None] == jnp.arange(N_GROUPS)
    masked = jnp.where(jnp.repeat(in_group, EXPERTS_PER_GROUP, axis=-1), sel, MASKED_SCORE)
    _, idx = lax.top_k(masked, TOP_K)
    w = jnp.take_along_axis(scores, idx, axis=-1)
    w = w / jnp.sum(w, axis=-1, keepdims=True)
    gate = jnp.sum(jax.nn.one_hot(idx, N_EXPERTS, dtype=f32) * w[..., None], axis=-2)
    hg = jnp.einsum('bld,edf->blef', h, p['w_gate'][l])
    hu = jnp.einsum('bld,edf->blef', h, p['w_up'][l])
    act = jax.nn.silu(hg) * hu
    return jnp.einsum('blef,ble,efd->bld', act, gate.astype(act.dtype), p['w_down'][l])


def layer(x, cond, l, p, ctx):
    mod = jnp.einsum('...d,de->...e', jax.nn.silu(cond), p['w_ada'][l]) + p['b_ada'][l]
    mod = mod.reshape(-1, 1, 6 * D_MODEL)
    sh1, sc1, g1, sh2, sc2, g2 = jnp.split(mod, 6, axis=-1)
    hn = rmsnorm(x, p['norm_mix'][l]) * (1.0 + sc1) + sh1
    mix, ctx_tensors = token_mixers(hn, l, p, ctx)
    x = x + g1 * mix
    hn = rmsnorm(x, p['norm_ffn'][l]) * (1.0 + sc2) + sh2
    x = x + g2 * moe(hn, l, p)
    return x, ctx_tensors


def setup_inputs(seed: int = 0) -> dict:
    key = jax.random.key(seed)
    ks = jax.random.split(key, 32)
    nrm = lambda k, shape, s=1.0: s * jax.random.normal(k, shape, jnp.float32)
    ret_base = -(5.0 + jnp.arange(H_RET, dtype=jnp.float32)) * math.log(2.0)
    return {
        "x_prompt": nrm(ks[0], (BATCH, SEQ, D_MODEL)),
        "x_sample": nrm(ks[1], (DEC_BATCH, DEC_SEQ, D_MODEL)),
        "cache_k": nrm(ks[2], (DEC_BATCH, DEPTH, PAST_LEN, H_DA, 2, DA_QK)),
        "cache_v": nrm(ks[3], (DEC_BATCH, DEPTH, PAST_LEN, H_DA, DA_V)),
        "state_ret": nrm(ks[4], (DEC_BATCH, DEPTH, 2, H_RET, RET_DK, RET_DV), 0.5),
        "state_hgrn": nrm(ks[5], (DEC_BATCH, DEPTH, 2, H_HG, HG_DK, HG_DV), 0.5),
        "c": nrm(ks[6], (DEC_BATCH, D_MODEL)),
        "c_ctx": nrm(ks[7], (D_MODEL,)),
        "w_in": nrm(ks[8], (DEPTH, D_MODEL, D_IN), D_MODEL ** -0.5),
        "w_out": nrm(ks[9], (DEPTH, D_MIX, D_MODEL), D_MIX ** -0.5),
        "w_ada": nrm(ks[10], (DEPTH, D_MODEL, 6 * D_MODEL), 0.5 * D_MODEL ** -0.5),
        "b_ada": nrm(ks[11], (DEPTH, 6 * D_MODEL), 0.01),
        "norm_mix": 1.0 + nrm(ks[12], (DEPTH, D_MODEL), 0.01),
        "norm_ffn": 1.0 + nrm(ks[13], (DEPTH, D_MODEL), 0.01),
        "da_qnorm": 1.0 + nrm(ks[14], (DEPTH, DA_QK), 0.01),
        "da_knorm": 1.0 + nrm(ks[15], (DEPTH, DA_QK), 0.01),
        "da_lambda": nrm(ks[16], (DEPTH, 4, DA_QK), 0.1),
        "da_subln": 1.0 + nrm(ks[17], (DEPTH, DA_V), 0.01),
        "ret_decay": ret_base + nrm(ks[18], (DEPTH, 2, H_RET), 0.1),
        "ret_norm": 1.0 + nrm(ks[19], (DEPTH, RET_DV), 0.01),
        "hg_lb": nrm(ks[20], (DEPTH, D_HG_K), 0.1),
        "hg_norm": 1.0 + nrm(ks[21], (DEPTH, HG_DV), 0.01),
        "w_router": nrm(ks[22], (D_MODEL, N_EXPERTS), D_MODEL ** -0.5),
        "router_bias": nrm(ks[23], (N_EXPERTS,), 0.01),
        "w_gate": nrm(ks[24], (DEPTH, N_EXPERTS, D_MODEL, D_EXPERT), D_MODEL ** -0.5),
        "w_up": nrm(ks[25], (DEPTH, N_EXPERTS, D_MODEL, D_EXPERT), D_MODEL ** -0.5),
        "w_down": nrm(ks[26], (DEPTH, N_EXPERTS, D_EXPERT, D_MODEL), D_EXPERT ** -0.5),
    }


def reference(x_prompt, x_sample, cache_k, cache_v, state_ret, state_hgrn, c, c_ctx,
              w_in, w_out, w_ada, b_ada, norm_mix, norm_ffn, da_qnorm, da_knorm, da_lambda,
              da_subln, ret_decay, ret_norm, hg_lb, hg_norm, w_router, router_bias,
              w_gate, w_up, w_down):
    p = dict(w_in=w_in, w_out=w_out, w_ada=w_ada, b_ada=b_ada, norm_mix=norm_mix,
             norm_ffn=norm_ffn, da_qnorm=da_qnorm, da_knorm=da_knorm, da_lambda=da_lambda,
             da_subln=da_subln, ret_decay=ret_decay, ret_norm=ret_norm, hg_lb=hg_lb,
             hg_norm=hg_norm, w_router=w_router, router_bias=router_bias,
             w_gate=w_gate, w_up=w_up, w_down=w_down)

    x = x_prompt
    ks_, vs_, srs_, shs_ = [], [], [], []
    for l in range(DEPTH):
        x, (k_l, v_l, sr_l, sh_l) = layer(x, c_ctx, l, p, None)
        ks_.append(k_l)
        vs_.append(v_l)
        srs_.append(sr_l)
        shs_.append(sh_l)
    y_prompt = x
    new_cache_k = jnp.stack(ks_, axis=1)
    new_cache_v = jnp.stack(vs_, axis=1)
    new_state_ret = jnp.stack(srs_, axis=1)
    new_state_hgrn = jnp.stack(shs_, axis=1)

    x = x_sample
    for l in range(DEPTH):
        x, _ = layer(x, c, l, p, (cache_k[:, l], cache_v[:, l], state_ret[:, l], state_hgrn[:, l]))
    y_sample = x

    return (y_prompt, y_sample, new_cache_k, new_cache_v, new_state_ret, new_state_hgrn)
```

```python
import functools
import math

import numpy as np
import jax
import jax.numpy as jnp
from jax import lax
from jax.experimental import pallas as pl
from jax.experimental.pallas import tpu as pltpu

F32 = jnp.float32
BF16 = jnp.bfloat16

D_MODEL = 1024
DEPTH = 2
GRID_W = 64
EPS = 1e-6
MIN_GATE = 1e-30
ROPE_BASE = 10000.0
H_DA, DA_QK, DA_V = 8, 32, 64
H_RET, RET_DK, RET_DV = 4, 64, 64
H_HG, HG_DK, HG_DV = 4, 64, 64
D_QK = H_DA * 2 * DA_QK
D_V = H_DA * DA_V
D_REC = 256
D_REST = 9 * D_REC
D_IN = 2 * D_QK + D_V + D_REST
N_EXPERTS, N_GROUPS, EXPERTS_PER_GROUP = 16, 4, 4
D_EXPERT = 512
MASKED_SCORE = -2.0

LANES = 128
TOKEN_TILE = 256
Q_TILE = 256
REC_CHUNK = 128
MOE_TILE = 1024
VMEM_LIMIT = 56 * 1024 * 1024

_NT = (((1,), (1,)), ((), ()))
_TN = (((0,), (0,)), ((), ()))


def _params(*sem):
    return pltpu.CompilerParams(dimension_semantics=sem, vmem_limit_bytes=VMEM_LIMIT)


def _dot(a, b):
    return jnp.dot(a, b, preferred_element_type=F32)


def _split_bf16(t):
    hi = t.astype(BF16)
    lo = (t - hi.astype(F32)).astype(BF16)
    return hi, lo


def _group_mean_sq(t, g):
    hi, lo = _split_bf16(t * t)
    outs = []
    for s in range(t.shape[1] // LANES):
        sl = slice(s * LANES, (s + 1) * LANES)
        outs.append(_dot(hi[:, sl], g) + _dot(lo[:, sl], g))
    return jnp.concatenate(outs, axis=1) if len(outs) > 1 else outs[0]


def _silu(t):
    return t * jax.nn.sigmoid(t)


def _ada_kernel(c_ref, w_ref, b_ref, o_ref):
    s = _silu(c_ref[...]).astype(BF16)
    o_ref[0] = _dot(s, w_ref[0]) + b_ref[0]


def _ada(cond, w_ada16, b_ada):
    rows = cond.shape[0]
    n_tile = 1536
    return pl.pallas_call(
        _ada_kernel,
        grid=(DEPTH, 6 * D_MODEL // n_tile),
        in_specs=[pl.BlockSpec((rows, D_MODEL), lambda l, j: (0, 0)),
                  pl.BlockSpec((1, D_MODEL, n_tile), lambda l, j: (l, 0, j)),
                  pl.BlockSpec((1, 1, n_tile), lambda l, j: (l, 0, j))],
        out_specs=pl.BlockSpec((1, rows, n_tile), lambda l, j: (l, 0, j)),
        out_shape=jax.ShapeDtypeStruct((DEPTH, rows, 6 * D_MODEL), F32),
        compiler_params=_params("arbitrary", "arbitrary"),
        name="ada",
    )(cond, w_ada16, b_ada.reshape(DEPTH, 1, 6 * D_MODEL))


def _inproj_kernel(x_ref, mod_ref, nw_ref, w_ref, qw_ref, kw_ref, g32_ref, cos_ref, sin_ref,
                   q_ref, k_ref, kb_ref, v_ref, vb_ref, rest_ref):
    x = x_ref[...]
    y = x * lax.rsqrt(jnp.mean(x * x, axis=-1, keepdims=True) + EPS) * nw_ref[...]
    mod = mod_ref[0]
    hn = (y * (1.0 + mod[:, D_MODEL:2 * D_MODEL]) + mod[:, 0:D_MODEL]).astype(BF16)
    g32 = g32_ref[...]
    lane = lax.broadcasted_iota(jnp.int32, (1, D_QK), 1)
    even = (lane & 1) == 0

    def qk_norm_rope(t, w):
        t = t * lax.rsqrt(_group_mean_sq(t, g32) + EPS) * w
        partner = jnp.where(even, pltpu.roll(t, D_QK - 1, 1), pltpu.roll(t, 1, 1))
        return t * cos_ref[...] + partner * sin_ref[...]

    q_ref[...] = qk_norm_rope(_dot(hn, w_ref[:, 0:D_QK]), qw_ref[...]).astype(BF16)
    k = qk_norm_rope(_dot(hn, w_ref[:, D_QK:2 * D_QK]), kw_ref[...])
    k_ref[...] = k
    kb_ref[...] = k.astype(BF16)
    v = _dot(hn, w_ref[:, 2 * D_QK:2 * D_QK + D_V])
    v_ref[...] = v
    vb_ref[...] = v.astype(BF16)
    rest_ref[...] = _dot(hn, w_ref[:, 2 * D_QK + D_V:D_IN])


def _inproj(x, mod, norm_w, w_in16, qw, kw, g32, cos_t, sin_t, n_ctx_tiles, lat_tiles_per_seq, ctx_row):
    t = x.shape[0]
    tm = TOKEN_TILE

    def mod_idx(i):
        return (jnp.where(i < n_ctx_tiles, ctx_row, (i - n_ctx_tiles) // lat_tiles_per_seq), 0, 0)

    def rope_idx(i):
        return (jnp.where(i < n_ctx_tiles, 0, 1 + (i - n_ctx_tiles) % lat_tiles_per_seq), 0)

    row = lambda i: (i, 0)
    const = lambda i: (0, 0)
    return pl.pallas_call(
        _inproj_kernel,
        grid=(t // tm,),
        in_specs=[pl.BlockSpec((tm, D_MODEL), row),
                  pl.BlockSpec((1, 1, 6 * D_MODEL), mod_idx),
                  pl.BlockSpec((1, D_MODEL), const),
                  pl.BlockSpec((D_MODEL, D_IN), const),
                  pl.BlockSpec((1, D_QK), const),
                  pl.BlockSpec((1, D_QK), const),
                  pl.BlockSpec((LANES, LANES), const),
                  pl.BlockSpec((tm, D_QK), rope_idx),
                  pl.BlockSpec((tm, D_QK), rope_idx)],
        out_specs=[pl.BlockSpec((tm, D_QK), row), pl.BlockSpec((tm, D_QK), row),
                   pl.BlockSpec((tm, D_QK), row), pl.BlockSpec((tm, D_V), row),
                   pl.BlockSpec((tm, D_V), row), pl.BlockSpec((tm, D_REST), row)],
        out_shape=[jax.ShapeDtypeStruct((t, D_QK), BF16), jax.ShapeDtypeStruct((t, D_QK), F32),
                   jax.ShapeDtypeStruct((t, D_QK), BF16), jax.ShapeDtypeStruct((t, D_V), F32),
                   jax.ShapeDtypeStruct((t, D_V), BF16), jax.ShapeDtypeStruct((t, D_REST), F32)],
        compiler_params=_params("arbitrary"),
        name="inproj",
    )(x, mod, norm_w, w_in16, qw, kw, g32, cos_t, sin_t)


def _attn_kernel(*refs, with_cache):
    if with_cache:
        lam_ref, q_ref, k_ref, v_ref, ck_ref, cv_ref, _, o_ref = refs
    else:
        lam_ref, q_ref, k_ref, v_ref, o_ref = refs
    lam = lam_ref[0]
    q = q_ref[...]
    lane = lax.broadcasted_iota(jnp.int32, (1, LANES), 1)
    left = lane < DA_V
    one = jnp.ones((), BF16)
    zero = jnp.zeros((), BF16)
    segs = [(k_ref[...], v_ref[...])]
    if with_cache:
        segs.append((ck_ref[0], cv_ref[0]))

    acc = jnp.zeros(o_ref.shape, F32)
    for side in range(2):
        own = left if side == 0 else jnp.logical_not(left)
        vms = [jnp.where(own, v, one) for _, v in segs]
        for r in range(2):
            lo = side * DA_V + r * DA_QK
            qm = jnp.where((lane >= lo) & (lane < lo + DA_QK), q, zero)
            ss = [lax.dot_general(qm, k, _NT, preferred_element_type=F32) for k, _ in segs]
            m = functools.reduce(jnp.maximum, [jnp.max(s, axis=-1, keepdims=True) for s in ss])
            res = functools.reduce(
                jnp.add, [_dot(jnp.exp(s - m).astype(BF16), vm) for s, vm in zip(ss, vms)])
            den = pltpu.roll(res, DA_V, 1)
            coef = 1.0 if r == 0 else -lam
            acc = acc + jnp.where(own, coef * (res / den), 0.0)
    o_ref[...] = acc


def _attention(lam, qb, kb, vb, row0, n_seq, seq_len, cache=None, into=None):
    t = qb.shape[0]
    tq = Q_TILE
    nq = seq_len // tq
    qb0 = row0 // tq
    sb0 = row0 // seq_len
    n_pair = D_QK // LANES
    qmap = lambda b, p, i: (qb0 + b * nq + i, p)
    kmap = lambda b, p, i: (sb0 + b, p)
    smem = pl.BlockSpec(memory_space=pltpu.SMEM)
    in_specs = [smem, pl.BlockSpec((tq, LANES), qmap), pl.BlockSpec((seq_len, LANES), kmap),
                pl.BlockSpec((seq_len, LANES), kmap)]
    args = [lam, qb, kb, vb]
    aliases = {}
    if cache is not None:
        ck, cv = cache
        cmap = lambda b, p, i: (b, 0, p)
        in_specs += [pl.BlockSpec((1, ck.shape[1], LANES), cmap), pl.BlockSpec((1, cv.shape[1], LANES), cmap),
                     pl.BlockSpec(memory_space=pl.ANY)]
        args += [ck, cv, into]
        aliases = {6: 0}
    return pl.pallas_call(
        functools.partial(_attn_kernel, with_cache=cache is not None),
        grid=(n_seq, n_pair, nq),
        in_specs=in_specs,
        out_specs=pl.BlockSpec((tq, LANES), qmap),
        out_shape=jax.ShapeDtypeStruct((t, D_V), F32),
        input_output_aliases=aliases,
        compiler_params=_params("arbitrary", "arbitrary", "arbitrary"),
        name="attn_lat" if cache is not None else "attn_ctx",
    )(*args)


def _level_table(c):
    t = np.arange(c)[:, None] ^ np.arange(c)[None, :]
    lv = np.where(t == 0, 32, 31 - np.floor(np.log2(np.maximum(t, 1))).astype(np.int64))
    return np.tile(lv, (4, 1)).astype(np.int32)


def _gla_chunk(q, k, v16, la, st_ref, lv, reverse):
    c = q.shape[0]
    row = lax.broadcasted_iota(jnp.int32, (c, 1), 0)
    lane = lax.broadcasted_iota(jnp.int32, (1, D_REC), 1)
    head = lane >> 6

    def expand(t16):
        return jnp.concatenate([jnp.where(head == h, t16, jnp.zeros((), BF16)) for h in range(4)], axis=0)

    def nt(a, b):
        return lax.dot_general(a, b, _NT, preferred_element_type=F32)

    scores = jnp.where(lv == 32, nt(expand(q.astype(BF16)), k.astype(BF16)), 0.0)
    pre = la
    tot = la
    h, level = 1, 31
    while h < c:
        bit = (row & h) != 0
        query_side = jnp.logical_not(bit) if reverse else bit
        e = jnp.where(query_side, pre, tot - pre)
        x = jnp.exp(jnp.minimum(e, 0.0))
        qx = jnp.where(query_side, q * x, 0.0).astype(BF16)
        kx = jnp.where(query_side, 0.0, k * x).astype(BF16)
        scores = scores + jnp.where(lv == level, nt(expand(qx), kx), 0.0)
        partner = jnp.where(bit, pltpu.roll(tot, h, 0), pltpu.roll(tot, c - h, 0))
        pre = pre + jnp.where(query_side, partner, 0.0)
        tot = tot + partner
        h, level = 2 * h, level - 1

    st = st_ref[...]
    o = nt((q * jnp.exp(jnp.minimum(pre, 0.0))).astype(BF16), st.astype(BF16))
    o_stack = _dot(scores.astype(BF16), v16)
    for hh in range(4):
        o = o + jnp.where(head == hh, o_stack[hh * c:(hh + 1) * c], 0.0)
    k_end = (k * jnp.exp(jnp.minimum(tot - pre, 0.0))).astype(BF16)
    kv_t = lax.dot_general(v16, k_end, _TN, preferred_element_type=F32)
    row_head = lax.broadcasted_iota(jnp.int32, (D_REC, 1), 0) >> 6
    st_ref[...] = st * jnp.exp(tot[0:1, :]) + jnp.where(row_head == head, kv_t, 0.0)
    return o


def _rec_kernel(*refs, aliased):
    if aliased:
        (rf_ref, rb_ref, lg_ref, lb_ref, lv_ref, st0_ref, _, _, of_ref, ob_ref, st_ref, st_scr) = refs
    else:
        (rf_ref, rb_ref, lg_ref, lb_ref, lv_ref, st0_ref, of_ref, ob_ref, st_ref, st_scr) = refs
    ci = pl.program_id(1)

    @pl.when(ci == 0)
    def _():
        st_scr[...] = st0_ref[0]

    lv = lv_ref[...]
    lb = lb_ref[...]
    col = lambda ref, j: ref[:, j * D_REC:(j + 1) * D_REC]
    for d, (r_ref, o_ref) in enumerate(((rf_ref, of_ref), (rb_ref, ob_ref))):
        c = r_ref.shape[0]
        la = jnp.broadcast_to(lg_ref[d], (c, D_REC))
        o_ret = _gla_chunk(col(r_ref, 0), col(r_ref, 1) * (RET_DK ** -0.5), col(r_ref, 2).astype(BF16),
                           la, st_scr.at[d], lv, reverse=d == 1)
        z = col(r_ref, 5 + d)
        la = jnp.log(jnp.maximum(lb + (1.0 - lb) * jax.nn.sigmoid(z), MIN_GATE))
        kk = (1.0 - lb) * jax.nn.sigmoid(-z)
        o_hg = _gla_chunk(_silu(col(r_ref, 4)), kk, col(r_ref, 7).astype(BF16),
                          la, st_scr.at[2 + d], lv, reverse=d == 1)
        o_ref[:, 0:D_REC] = o_ret
        o_ref[:, D_REC:2 * D_REC] = o_hg

    @pl.when(ci == pl.num_programs(1) - 1)
    def _():
        st_ref[0] = st_scr[...]


def _recurrence(rest, lg_rows, lb_row, lv, st0, row0, n_seq, seq_len, into=None):
    t = rest.shape[0]
    c = REC_CHUNK
    nc = seq_len // c
    cb0 = row0 // c
    fmap = lambda b, i: (cb0 + b * nc + i, 0)
    bmap = lambda b, i: (cb0 + b * nc + nc - 1 - i, 0)
    const2 = lambda b, i: (0, 0)
    smap = lambda b, i: (b, 0, 0, 0)
    in_specs = [pl.BlockSpec((c, D_REST), fmap), pl.BlockSpec((c, D_REST), bmap),
                pl.BlockSpec((2, 1, D_REC), lambda b, i: (0, 0, 0)),
                pl.BlockSpec((1, D_REC), const2),
                pl.BlockSpec((4 * c, c), const2),
                pl.BlockSpec((1, 4, D_REC, D_REC), smap)]
    args = [rest, rest, lg_rows, lb_row, lv, st0]
    aliases = {}
    if into is not None:
        in_specs += [pl.BlockSpec(memory_space=pl.ANY)] * 2
        args += list(into)
        aliases = {6: 0, 7: 1}
    return pl.pallas_call(
        functools.partial(_rec_kernel, aliased=into is not None),
        grid=(n_seq, nc),
        in_specs=in_specs,
        out_specs=[pl.BlockSpec((c, 2 * D_REC), fmap), pl.BlockSpec((c, 2 * D_REC), bmap),
                   pl.BlockSpec((1, 4, D_REC, D_REC), smap)],
        out_shape=[jax.ShapeDtypeStruct((t, 2 * D_REC), F32), jax.ShapeDtypeStruct((t, 2 * D_REC), F32),
                   jax.ShapeDtypeStruct((n_seq, 4, D_REC, D_REC), F32)],
        scratch_shapes=[pltpu.VMEM((4, D_REC, D_REC), F32)],
        input_output_aliases=aliases,
        compiler_params=_params("arbitrary", "arbitrary"),
        name="rec_lat" if into is not None else "rec_ctx",
    )(*args)


def _route(logits, bias):
    scores = jax.nn.sigmoid(logits)
    sel = scores + bias
    srow = [scores[e:e + 1, :] for e in range(N_EXPERTS)]
    rows = [sel[e:e + 1, :] for e in range(N_EXPERTS)]
    gs = []
    for g in range(N_GROUPS):
        a, b, c, d = rows[4 * g:4 * g + 4]
        gs.append(functools.reduce(jnp.maximum, [a + b, a + c, a + d, b + c, b + d, c + d]))
    best = jnp.zeros_like(gs[0], dtype=jnp.int32)
    best_v = gs[0]
    for g in range(1, N_GROUPS):
        upd = gs[g] > best_v
        best = jnp.where(upd, g, best)
        best_v = jnp.where(upd, gs[g], best_v)
    masked = [jnp.where(best == e // EXPERTS_PER_GROUP, rows[e], MASKED_SCORE) for e in range(N_EXPERTS)]
    i1 = jnp.zeros_like(best)
    v1 = masked[0]
    for e in range(1, N_EXPERTS):
        upd = masked[e] > v1
        i1 = jnp.where(upd, e, i1)
        v1 = jnp.where(upd, masked[e], v1)
    i2 = jnp.zeros_like(best)
    v2 = jnp.full_like(v1, -jnp.inf)
    for e in range(N_EXPERTS):
        upd = (masked[e] > v2) & (i1 != e)
        i2 = jnp.where(upd, e, i2)
        v2 = jnp.where(upd, masked[e], v2)
    w1 = functools.reduce(jnp.add, [jnp.where(i1 == e, srow[e], 0.0) for e in range(N_EXPERTS)])
    w2 = functools.reduce(jnp.add, [jnp.where(i2 == e, srow[e], 0.0) for e in range(N_EXPERTS)])
    tot = w1 + w2
    w1, w2 = w1 / tot, w2 / tot
    expert = lax.broadcasted_iota(jnp.int32, logits.shape, 0)
    return jnp.where(expert == i1, w1, 0.0) + jnp.where(expert == i2, w2, 0.0)


def _outproj_kernel(x_ref, oda_ref, of_ref, ob_ref, rg_ref, gg_ref, mod_ref, subln_ref, retn_ref, hgn_ref,
                    g64_ref, wout_ref, nffn_ref, wrh_ref, wrl_ref, rb_ref,
                    xmid_ref, hn2_ref, gate_ref):
    g64 = g64_ref[...]

    def gnorm(t, w):
        return t * lax.rsqrt(_group_mean_sq(t, g64) + EPS) * w

    o_da = gnorm(oda_ref[...], subln_ref[...])
    o_rec = of_ref[...] + ob_ref[...]
    o_ret = gnorm(o_rec[:, 0:D_REC], retn_ref[...]) * _silu(rg_ref[...])
    o_hg = gnorm(o_rec[:, D_REC:2 * D_REC], hgn_ref[...]) * _silu(gg_ref[...])
    mixed = jnp.concatenate([o_da, o_ret, o_hg], axis=1).astype(BF16)
    mod = mod_ref[0]
    x = x_ref[...] + mod[:, 2 * D_MODEL:3 * D_MODEL] * _dot(mixed, wout_ref[...])
    xmid_ref[...] = x
    y = x * lax.rsqrt(jnp.mean(x * x, axis=-1, keepdims=True) + EPS) * nffn_ref[...]
    hn = y * (1.0 + mod[:, 4 * D_MODEL:5 * D_MODEL]) + mod[:, 3 * D_MODEL:4 * D_MODEL]
    hi, lo = _split_bf16(hn)
    hn2_ref[...] = hi
    wrh = wrh_ref[...]
    logits = (lax.dot_general(wrh, hi, _NT, preferred_element_type=F32)
              + lax.dot_general(wrh, lo, _NT, preferred_element_type=F32)
              + lax.dot_general(wrl_ref[...], hi, _NT, preferred_element_type=F32))
    gate_ref[...] = _route(logits, rb_ref[...])


def _outproj(x, oda, o_f, o_b, rest, mod, subln, retn, hgn, g64, w_out16, norm_w, wr_hi, wr_lo, rbias,
             n_ctx_tiles, lat_tiles_per_seq, ctx_row):
    t = x.shape[0]
    tm = TOKEN_TILE

    def mod_idx(i):
        return (jnp.where(i < n_ctx_tiles, ctx_row, (i - n_ctx_tiles) // lat_tiles_per_seq), 0, 0)

    row = lambda i: (i, 0)
    const = lambda i: (0, 0)
    return pl.pallas_call(
        _outproj_kernel,
        grid=(t // tm,),
        in_specs=[pl.BlockSpec((tm, D_MODEL), row),
                  pl.BlockSpec((tm, D_V), row),
                  pl.BlockSpec((tm, 2 * D_REC), row),
                  pl.BlockSpec((tm, 2 * D_REC), row),
                  pl.BlockSpec((tm, D_REC), lambda i: (i, 3)),
                  pl.BlockSpec((tm, D_REC), lambda i: (i, 8)),
                  pl.BlockSpec((1, 1, 6 * D_MODEL), mod_idx),
                  pl.BlockSpec((1, D_V), const),
                  pl.BlockSpec((1, D_REC), const),
                  pl.BlockSpec((1, D_REC), const),
                  pl.BlockSpec((LANES, LANES), const),
                  pl.BlockSpec((D_MODEL, D_MODEL), const),
                  pl.BlockSpec((1, D_MODEL), const),
                  pl.BlockSpec((N_EXPERTS, D_MODEL), const),
                  pl.BlockSpec((N_EXPERTS, D_MODEL), const),
                  pl.BlockSpec((N_EXPERTS, 1), const)],
        out_specs=[pl.BlockSpec((tm, D_MODEL), row), pl.BlockSpec((tm, D_MODEL), row),
                   pl.BlockSpec((N_EXPERTS, tm), lambda i: (0, i))],
        out_shape=[jax.ShapeDtypeStruct((t, D_MODEL), F32), jax.ShapeDtypeStruct((t, D_MODEL), BF16),
                   jax.ShapeDtypeStruct((N_EXPERTS, t), F32)],
        compiler_params=_params("arbitrary"),
        name="outproj",
    )(x, oda, o_f, o_b, rest, rest, mod, subln, retn, hgn, g64, w_out16, norm_w, wr_hi, wr_lo, rbias)


def _moe_kernel(h_ref, gate_ref, wg_ref, wu_ref, wd_ref, x_ref, mod_ref, o_ref, acc_ref):
    e = pl.program_id(1)

    @pl.when(e == 0)
    def _():
        acc_ref[...] = jnp.zeros_like(acc_ref)

    h = h_ref[...]
    lane = lax.broadcasted_iota(jnp.int32, (1, N_EXPERTS), 1)
    g = jnp.sum(jnp.where(lane == e, gate_ref[...], 0.0), axis=1, keepdims=True)
    act = _silu(_dot(h, wg_ref[0].astype(BF16))) * _dot(h, wu_ref[0].astype(BF16))
    acc_ref[...] += _dot((act * g).astype(BF16), wd_ref[0].astype(BF16))

    @pl.when(e == pl.num_programs(1) - 1)
    def _():
        o_ref[...] = x_ref[...] + mod_ref[0][:, 5 * D_MODEL:6 * D_MODEL] * acc_ref[...]


def _moe(hn2, gate, w_gate, w_up, w_down, x_mid, mod, n_ctx_tiles, lat_tiles_per_seq, ctx_row):
    t = hn2.shape[0]
    tm = MOE_TILE

    def mod_idx(i, e):
        return (jnp.where(i < n_ctx_tiles, ctx_row, (i - n_ctx_tiles) // lat_tiles_per_seq), 0, 0)

    row = lambda i, e: (i, 0)
    return pl.pallas_call(
        _moe_kernel,
        grid=(t // tm, N_EXPERTS),
        in_specs=[pl.BlockSpec((tm, D_MODEL), row),
                  pl.BlockSpec((tm, N_EXPERTS), row),
                  pl.BlockSpec((1, D_MODEL, D_EXPERT), lambda i, e: (e, 0, 0)),
                  pl.BlockSpec((1, D_MODEL, D_EXPERT), lambda i, e: (e, 0, 0)),
                  pl.BlockSpec((1, D_EXPERT, D_MODEL), lambda i, e: (e, 0, 0)),
                  pl.BlockSpec((tm, D_MODEL), row),
                  pl.BlockSpec((1, 1, 6 * D_MODEL), mod_idx)],
        out_specs=pl.BlockSpec((tm, D_MODEL), row),
        out_shape=jax.ShapeDtypeStruct((t, D_MODEL), F32),
        scratch_shapes=[pltpu.VMEM((tm, D_MODEL), F32)],
        compiler_params=_params("arbitrary", "arbitrary"),
        name="moe",
    )(hn2, gate, w_gate, w_up, w_down, x_mid, mod)


def _block_avg(group):
    i = np.arange(LANES)
    return jnp.asarray((i[:, None] // group == i[None, :] // group) / group, dtype=BF16)


def _rope_tables(n_tokens, n_identity):
    rows = n_tokens // GRID_W
    pos_r = jnp.repeat(jnp.arange(rows, dtype=F32), GRID_W)
    pos_c = jnp.tile(jnp.arange(GRID_W, dtype=F32), rows)
    n_freq = DA_QK // 4
    inv = ROPE_BASE ** (-jnp.arange(n_freq, dtype=F32) / n_freq)
    ang = jnp.concatenate([pos_r[:, None] * inv, pos_c[:, None] * inv], axis=-1)
    cos = jnp.repeat(jnp.cos(ang), 2, axis=-1)
    sin = jnp.repeat(jnp.sin(ang), 2, axis=-1) * jnp.tile(jnp.asarray([-1.0, 1.0], F32), DA_QK // 2)
    cos = jnp.concatenate([jnp.ones((n_identity, DA_QK), F32), cos], axis=0)
    sin = jnp.concatenate([jnp.zeros((n_identity, DA_QK), F32), sin], axis=0)
    return jnp.tile(cos, (1, 2 * H_DA)), jnp.tile(sin, (1, 2 * H_DA))


def _states_to_blockdiag(s):
    b = s.shape[0]
    eye = jnp.eye(4, dtype=s.dtype)
    return jnp.einsum('bdhkv,hg->bdhvgk', s, eye).reshape(b, 2, D_REC, D_REC)


def _blockdiag_to_states(st):
    b = st.shape[0]
    return jnp.einsum('bdhvhk->bdhkv', st.reshape(b, 2, 4, 64, 4, 64))


def kernel(x_prompt, x_sample, cache_k, cache_v, state_ret, state_hgrn, c, c_ctx, w_in, w_out, w_ada, b_ada,
           norm_mix, norm_ffn, da_qnorm, da_knorm, da_lambda, da_subln, ret_decay, ret_norm, hg_lb, hg_norm,
           w_router, router_bias, w_gate, w_up, w_down):
    n_ctx, l_ctx, _ = x_prompt.shape
    n_lat, l_lat, _ = x_sample.shape
    t_ctx = n_ctx * l_ctx
    past = cache_k.shape[2]
    assert l_ctx == TOKEN_TILE and l_lat % TOKEN_TILE == 0 and l_lat % GRID_W == 0
    n_ctx_tiles = t_ctx // TOKEN_TILE
    lat_tiles = l_lat // TOKEN_TILE
    moe_ctx_tiles = t_ctx // MOE_TILE
    moe_lat_tiles = l_lat // MOE_TILE
    assert t_ctx % MOE_TILE == 0 and l_lat % MOE_TILE == 0

    x = jnp.concatenate([x_prompt.reshape(t_ctx, D_MODEL), x_sample.reshape(n_lat * l_lat, D_MODEL)], axis=0)

    ctx_row = n_lat
    n_cond = -(-(n_lat + 1) // 8) * 8
    cond = jnp.zeros((n_cond, D_MODEL), F32).at[:n_lat].set(c).at[ctx_row].set(c_ctx)
    mod_all = _ada(cond, w_ada.astype(BF16), b_ada)

    g32, g64 = _block_avg(DA_QK), _block_avg(DA_V)
    cos_t, sin_t = _rope_tables(l_lat, TOKEN_TILE)
    lv = jnp.asarray(_level_table(REC_CHUNK))
    p_lb = jax.nn.softmax(hg_lb.astype(F32), axis=0)
    lb_all = jnp.cumsum(p_lb, axis=0) - p_lb[0]
    wr_hi = w_router.T.astype(BF16)
    wr_lo = (w_router.T - wr_hi.astype(F32)).astype(BF16)
    rbias = router_bias.astype(F32).reshape(N_EXPERTS, 1)
    st0_ctx = jnp.zeros((n_ctx, 4, D_REC, D_REC), F32)

    cache_ks, cache_vs, states_ret, states_hg = [], [], [], []
    for l in range(DEPTH):
        lam_init = 0.8 - 0.6 * math.exp(-0.3 * l)
        lp = da_lambda[l].astype(F32)
        lam = (jnp.exp(jnp.sum(lp[0] * lp[1])) - jnp.exp(jnp.sum(lp[2] * lp[3])) + lam_init).reshape(1)
        mod = mod_all[l].reshape(n_cond, 1, 6 * D_MODEL)
        qw = (jnp.tile(da_qnorm[l].astype(F32), 2 * H_DA) * (DA_QK ** -0.5)).reshape(1, D_QK)
        kw = jnp.tile(da_knorm[l].astype(F32), 2 * H_DA).reshape(1, D_QK)

        qb, k32, kb, v32, vb, rest = _inproj(x, mod, norm_mix[l].reshape(1, D_MODEL), w_in[l].astype(BF16),
                                             qw, kw, g32, cos_t, sin_t, n_ctx_tiles, lat_tiles, ctx_row)
        cache_ks.append(k32[:t_ctx].reshape(n_ctx, l_ctx, H_DA, 2, DA_QK))
        cache_vs.append(v32[:t_ctx].reshape(n_ctx, l_ctx, H_DA, DA_V))

        oda = _attention(lam, qb, kb, vb, 0, n_ctx, l_ctx)
        ck = cache_k[:, l].reshape(n_lat, past, D_QK).astype(BF16)
        cv = cache_v[:, l].reshape(n_lat, past, D_V).astype(BF16)
        oda = _attention(lam, qb, kb, vb, t_ctx, n_lat, l_lat, cache=(ck, cv), into=oda)

        log_gamma = -jnp.exp(ret_decay[l].astype(F32))
        lg_rows = jnp.repeat(log_gamma, RET_DK, axis=1).reshape(2, 1, D_REC)
        lb_row = lb_all[l].reshape(1, D_REC)
        o_f, o_b, st_ctx = _recurrence(rest, lg_rows, lb_row, lv, st0_ctx, 0, n_ctx, l_ctx)
        st0_lat = jnp.concatenate([_states_to_blockdiag(state_ret[:, l].astype(F32)),
                                   _states_to_blockdiag(state_hgrn[:, l].astype(F32))], axis=1)
        o_f, o_b, _ = _recurrence(rest, lg_rows, lb_row, lv, st0_lat, t_ctx, n_lat, l_lat, into=(o_f, o_b))
        states_ret.append(_blockdiag_to_states(st_ctx[:, 0:2]))
        states_hg.append(_blockdiag_to_states(st_ctx[:, 2:4]))

        subln = (jnp.tile(da_subln[l].astype(F32), H_DA) * (1.0 - lam_init)).reshape(1, D_V)
        retn = jnp.tile(ret_norm[l].astype(F32), H_RET).reshape(1, D_REC)
        hgn = jnp.tile(hg_norm[l].astype(F32), H_HG).reshape(1, D_REC)
        x_mid, hn2, gate_t = _outproj(x, oda, o_f, o_b, rest, mod, subln, retn, hgn, g64, w_out[l].astype(BF16),
                                      norm_ffn[l].reshape(1, D_MODEL), wr_hi, wr_lo, rbias,
                                      n_ctx_tiles, lat_tiles, ctx_row)
        x = _moe(hn2, gate_t.T, w_gate[l], w_up[l], w_down[l], x_mid, mod,
                 moe_ctx_tiles, moe_lat_tiles, ctx_row)

    y_prompt = x[:t_ctx].reshape(n_ctx, l_ctx, D_MODEL)
    y_sample = x[t_ctx:].reshape(n_lat, l_lat, D_MODEL)
    return (y_prompt, y_sample, jnp.stack(cache_ks, axis=1), jnp.stack(cache_vs, axis=1),
            jnp.stack(states_ret, axis=1), jnp.stack(states_hg, axis=1))
```

```python
import functools
import math

import numpy as np
import jax
import jax.numpy as jnp
from jax import lax
from jax.experimental import pallas as pl
from jax.experimental.pallas import tpu as pltpu

F32 = jnp.float32
BF16 = jnp.bfloat16

D_MODEL = 1024
DEPTH = 2
GRID_W = 64
EPS = 1e-6
MIN_GATE = 1e-30
ROPE_BASE = 10000.0
H_DA, DA_QK, DA_V = 8, 32, 64
H_RET, RET_DK, RET_DV = 4, 64, 64
H_HG, HG_DK, HG_DV = 4, 64, 64
D_QK = H_DA * 2 * DA_QK
D_V = H_DA * DA_V
N_REC_HEADS = 4
D_HEAD = 64
D_REC = N_REC_HEADS * D_HEAD
D_REST = 9 * D_REC
D_IN = 2 * D_QK + D_V + D_REST
N_EXPERTS, N_GROUPS, EXPERTS_PER_GROUP = 16, 4, 4
D_EXPERT = 512
MASKED_SCORE = -2.0

LANES = 128
TOKEN_TILE = 256
Q_TILE = 256
REC_CHUNK = 128
MOE_TILE = 1024
VMEM_LIMIT = 56 * 1024 * 1024

_NT = (((1,), (1,)), ((), ()))
_TN = (((0,), (0,)), ((), ()))


def _params(*sem):
    return pltpu.CompilerParams(dimension_semantics=sem, vmem_limit_bytes=VMEM_LIMIT)


def _dot(a, b):
    return jnp.dot(a, b, preferred_element_type=F32)


def _split_bf16(t, terms=2):
    out = []
    for _ in range(terms - 1):
        hi = t.astype(BF16)
        out.append(hi)
        t = t - hi.astype(F32)
    out.append(t.astype(BF16))
    return out


def _group_mean_sq(t, g):
    hi, lo = _split_bf16(t * t)
    outs = []
    for s in range(t.shape[1] // LANES):
        sl = slice(s * LANES, (s + 1) * LANES)
        outs.append(_dot(hi[:, sl], g) + _dot(lo[:, sl], g))
    return jnp.concatenate(outs, axis=1) if len(outs) > 1 else outs[0]


def _silu(t):
    return t * jax.nn.sigmoid(t)


class _Stream:
    def __init__(self, n_ctx_rows, n_lat_seq, lat_len, tile, ctx_row):
        self.tile = tile
        self.n_ctx = n_ctx_rows // tile
        self.per_seq = lat_len // tile
        self.n_tiles = self.n_ctx + n_lat_seq * self.per_seq
        self.ctx_row = ctx_row

    def ctx_idx(self, i):
        return jnp.minimum(i, self.n_ctx - 1)

    def lat_idx(self, i):
        return jnp.maximum(i - self.n_ctx, 0)

    def mod_idx(self, i):
        return jnp.where(i < self.n_ctx, self.ctx_row, (i - self.n_ctx) // self.per_seq)

    def pos_idx(self, i):
        return jnp.where(i < self.n_ctx, 0, 1 + (i - self.n_ctx) % self.per_seq)


def _ada_kernel(c_ref, w_ref, b_ref, o_ref):
    s = _silu(c_ref[...]).astype(BF16)
    o_ref[0] = _dot(s, w_ref[0].astype(BF16)) + b_ref[0]


def _ada(cond, w_ada, b_ada):
    rows = cond.shape[0]
    n_tile = 1536
    return pl.pallas_call(
        _ada_kernel,
        grid=(DEPTH, 6 * D_MODEL // n_tile),
        in_specs=[pl.BlockSpec((rows, D_MODEL), lambda l, j: (0, 0)),
                  pl.BlockSpec((1, D_MODEL, n_tile), lambda l, j: (l, 0, j)),
                  pl.BlockSpec((1, 1, n_tile), lambda l, j: (l, 0, j))],
        out_specs=pl.BlockSpec((1, rows, n_tile), lambda l, j: (l, 0, j)),
        out_shape=jax.ShapeDtypeStruct((DEPTH, rows, 6 * D_MODEL), F32),
        compiler_params=_params("arbitrary", "arbitrary"),
        name="ada",
    )(cond, w_ada, b_ada.reshape(DEPTH, 1, 6 * D_MODEL))


def _inproj_kernel(*refs, n_ctx_tiles, aliased):
    (xa_ref, xb_ref, mod_ref, nw_ref, w_ref, qw_ref, kw_ref, g32_ref, cos_ref, sin_ref) = refs[:10]
    q_ref, kb_ref, vb_ref, rest_ref, kc_ref, vc_ref = refs[12:] if aliased else refs[10:]
    is_ctx = pl.program_id(0) < n_ctx_tiles
    x = jnp.where(is_ctx, xa_ref[...], xb_ref[...])
    y = x * lax.rsqrt(jnp.mean(x * x, axis=-1, keepdims=True) + EPS) * nw_ref[...]
    mod = mod_ref[0]
    hn = (y * (1.0 + mod[:, D_MODEL:2 * D_MODEL]) + mod[:, 0:D_MODEL]).astype(BF16)
    g32 = g32_ref[...]
    lane = lax.broadcasted_iota(jnp.int32, (1, D_QK), 1)
    even = (lane & 1) == 0

    def qk_norm_rope(t, w):
        t = t * lax.rsqrt(_group_mean_sq(t, g32) + EPS) * w
        partner = jnp.where(even, pltpu.roll(t, D_QK - 1, 1), pltpu.roll(t, 1, 1))
        return t * cos_ref[...] + partner * sin_ref[...]

    q_ref[...] = qk_norm_rope(_dot(hn, w_ref[:, 0:D_QK]), qw_ref[...]).astype(BF16)
    k = qk_norm_rope(_dot(hn, w_ref[:, D_QK:2 * D_QK]), kw_ref[...])
    kb_ref[...] = k.astype(BF16)
    v = _dot(hn, w_ref[:, 2 * D_QK:2 * D_QK + D_V])
    vb_ref[...] = v.astype(BF16)
    rest_ref[...] = _dot(hn, w_ref[:, 2 * D_QK + D_V:D_IN])

    @pl.when(is_ctx)
    def _():
        kc_ref[0, 0] = k
        vc_ref[0, 0] = v


def _inproj(xa, xb, mod, norm_w, w_in16, qw, kw, g32, cos_t, sin_t, st, layer, n_ctx_seq, caches):
    tm = st.tile
    t = st.n_tiles * tm
    row = lambda i: (i, 0)
    const = lambda i: (0, 0)
    cache_map = lambda i: (st.ctx_idx(i), layer, 0, 0)
    in_specs = [pl.BlockSpec((tm, D_MODEL), lambda i: (st.ctx_idx(i), 0)),
                pl.BlockSpec((tm, D_MODEL), lambda i: (st.lat_idx(i), 0)),
                pl.BlockSpec((1, 1, 6 * D_MODEL), lambda i: (st.mod_idx(i), 0, 0)),
                pl.BlockSpec((1, D_MODEL), const),
                pl.BlockSpec((D_MODEL, D_IN), const, pipeline_mode=pl.Buffered(1)),
                pl.BlockSpec((1, D_QK), const),
                pl.BlockSpec((1, D_QK), const),
                pl.BlockSpec((LANES, LANES), const),
                pl.BlockSpec((tm, D_QK), lambda i: (st.pos_idx(i), 0)),
                pl.BlockSpec((tm, D_QK), lambda i: (st.pos_idx(i), 0))]
    args = [xa, xb, mod, norm_w, w_in16, qw, kw, g32, cos_t, sin_t]
    aliases = {}
    if caches is not None:
        in_specs += [pl.BlockSpec(memory_space=pl.ANY)] * 2
        args += list(caches)
        aliases = {10: 4, 11: 5}
    cache_shape = jax.ShapeDtypeStruct((n_ctx_seq, DEPTH, tm, D_QK), F32)
    return pl.pallas_call(
        functools.partial(_inproj_kernel, n_ctx_tiles=st.n_ctx, aliased=caches is not None),
        grid=(st.n_tiles,),
        in_specs=in_specs,
        out_specs=[pl.BlockSpec((tm, D_QK), row), pl.BlockSpec((tm, D_QK), row),
                   pl.BlockSpec((tm, D_V), row), pl.BlockSpec((tm, D_REST), row),
                   pl.BlockSpec((1, 1, tm, D_QK), cache_map), pl.BlockSpec((1, 1, tm, D_V), cache_map)],
        out_shape=[jax.ShapeDtypeStruct((t, D_QK), BF16), jax.ShapeDtypeStruct((t, D_QK), BF16),
                   jax.ShapeDtypeStruct((t, D_V), BF16), jax.ShapeDtypeStruct((t, D_REST), F32),
                   cache_shape, cache_shape],
        input_output_aliases=aliases,
        compiler_params=_params("arbitrary"),
        name="inproj",
    )(*args)


def _attn_kernel(*refs, with_cache):
    if with_cache:
        lam_ref, q_ref, k_ref, v_ref, ck_ref, cv_ref, _, o_ref = refs
    else:
        lam_ref, q_ref, k_ref, v_ref, o_ref = refs
    lam = lam_ref[0]
    q = q_ref[...]
    lane = lax.broadcasted_iota(jnp.int32, (1, LANES), 1)
    left = lane < DA_V
    one = jnp.ones((), BF16)
    zero = jnp.zeros((), BF16)
    segs = [(k_ref[...], v_ref[...])]
    if with_cache:
        segs.append((ck_ref[0, 0], cv_ref[0, 0]))

    acc = jnp.zeros(o_ref.shape, F32)
    for side in range(2):
        own = left if side == 0 else jnp.logical_not(left)
        vms = [jnp.where(own, v, one) for _, v in segs]
        for r in range(2):
            lo = side * DA_V + r * DA_QK
            qm = jnp.where((lane >= lo) & (lane < lo + DA_QK), q, zero)
            ss = [lax.dot_general(qm, k, _NT, preferred_element_type=F32) for k, _ in segs]
            m = functools.reduce(jnp.maximum, [jnp.max(s, axis=-1, keepdims=True) for s in ss])
            res = functools.reduce(
                jnp.add, [_dot(jnp.exp(s - m).astype(BF16), vm) for s, vm in zip(ss, vms)])
            den = pltpu.roll(res, DA_V, 1)
            coef = 1.0 if r == 0 else -lam
            acc = acc + jnp.where(own, coef * (res / den), 0.0)
    o_ref[...] = acc


def _attention(lam, qb, kb, vb, row0, n_seq, seq_len, layer=0, cache=None, into=None):
    t = qb.shape[0]
    tq = Q_TILE
    nq = seq_len // tq
    qb0 = row0 // tq
    sb0 = row0 // seq_len
    n_pair = D_QK // LANES
    qmap = lambda b, p, i: (qb0 + b * nq + i, p)
    kmap = lambda b, p, i: (sb0 + b, p)
    smem = pl.BlockSpec(memory_space=pltpu.SMEM)
    in_specs = [smem, pl.BlockSpec((tq, LANES), qmap), pl.BlockSpec((seq_len, LANES), kmap),
                pl.BlockSpec((seq_len, LANES), kmap)]
    args = [lam, qb, kb, vb]
    aliases = {}
    if cache is not None:
        ck, cv = cache
        cmap = lambda b, p, i: (b, layer, 0, p)
        in_specs += [pl.BlockSpec((1, 1, ck.shape[2], LANES), cmap),
                     pl.BlockSpec((1, 1, cv.shape[2], LANES), cmap),
                     pl.BlockSpec(memory_space=pl.ANY)]
        args += [ck, cv, into]
        aliases = {6: 0}
    return pl.pallas_call(
        functools.partial(_attn_kernel, with_cache=cache is not None),
        grid=(n_seq, n_pair, nq),
        in_specs=in_specs,
        out_specs=pl.BlockSpec((tq, LANES), qmap),
        out_shape=jax.ShapeDtypeStruct((t, D_V), F32),
        input_output_aliases=aliases,
        compiler_params=_params("arbitrary", "arbitrary", "arbitrary"),
        name="attn_lat" if cache is not None else "attn_ctx",
    )(*args)


def _level_table(c):
    t = np.arange(c)[:, None] ^ np.arange(c)[None, :]
    lv = np.where(t == 0, 32, 31 - np.floor(np.log2(np.maximum(t, 1))).astype(np.int64))
    return np.tile(lv, (N_REC_HEADS, 1)).astype(np.int32)


def _retention_tables(log_gamma, c):
    g = jnp.repeat(log_gamma, D_HEAD, axis=1)
    t = jnp.arange(c, dtype=F32)
    diff = t[:, None] - t[None, :]
    gh = log_gamma[:, :, None, None]
    d_f = jnp.where(diff >= 0, jnp.exp(gh[0] * jnp.maximum(diff, 0.0)), 0.0)
    d_b = jnp.where(diff <= 0, jnp.exp(gh[1] * jnp.maximum(-diff, 0.0)), 0.0)
    d = jnp.stack([d_f, d_b]).reshape(2, N_REC_HEADS * c, c)
    q_f, k_f = jnp.exp(g[0] * (t[:, None] + 1.0)), jnp.exp(g[0] * (c - 1.0 - t[:, None]))
    q_b, k_b = jnp.exp(g[1] * (c - t[:, None])), jnp.exp(g[1] * t[:, None])
    f = jnp.stack([jnp.stack([q_f, k_f]), jnp.stack([q_b, k_b])])
    a = jnp.exp(g * c).reshape(2, 1, D_REC)
    return d, f, a


def _rec_kernel(*refs, state_in, state_out, aliased):
    refs = list(refs)
    rf_ref, rb_ref, rd_ref, rfac_ref, ra_ref, lb_ref, lv_ref = refs[:7]
    del refs[:7]
    if state_in:
        sret_ref, shg_ref, t4_ref = refs[:3]
        del refs[:3]
    if state_out:
        t4t_ref = refs.pop(0)
    del refs[:aliased]
    of_ref, ob_ref = refs[:2]
    del refs[:2]
    if state_out:
        oret_ref, ohg_ref = refs[:2]
        del refs[:2]
    (st_scr,) = refs

    ci = pl.program_id(1)
    c = rf_ref.shape[0]
    lane = lax.broadcasted_iota(jnp.int32, (1, D_REC), 1)
    head = lane >> 6
    row = lax.broadcasted_iota(jnp.int32, (c, 1), 0)
    srow = lax.broadcasted_iota(jnp.int32, (D_REC, 1), 0)
    same_head = (srow >> 6) == head
    eye = srow == lane
    head_is = [head == h for h in range(N_REC_HEADS)]
    head_m16 = [jnp.broadcast_to(jnp.where(m, 1.0, 0.0), (c, D_REC)).astype(BF16) for m in head_is]

    @pl.when(ci == 0)
    def _():
        if state_in:
            for d in range(2):
                for idx, ref in ((d, sret_ref), (2 + d, shg_ref)):
                    tiled = functools.reduce(jnp.add, [_dot(p, t4_ref[...]) for p in _split_bf16(ref[0, 0, d], 3)])
                    st_scr[idx] = jnp.where(same_head, tiled, 0.0)
        else:
            st_scr[...] = jnp.zeros_like(st_scr)

    def expand(t16):
        return jnp.concatenate([t16 * m for m in head_m16], axis=0)

    def nt(a, b):
        return lax.dot_general(a, b, _NT, preferred_element_type=F32)

    def finish(q, k, v16, scores, qfac, kfac, a_row, st_ref):
        st = st_ref[...]
        o = _dot((q * qfac).astype(BF16), st.astype(BF16))
        o_stack = _dot(scores.astype(BF16), v16)
        for h in range(N_REC_HEADS):
            o = o + jnp.where(head_is[h], o_stack[h * c:(h + 1) * c], 0.0)
        kv = lax.dot_general((k * kfac).astype(BF16), v16, _TN, preferred_element_type=F32)
        a_col = jnp.sum(jnp.where(eye, a_row, 0.0), axis=1, keepdims=True)
        st_ref[...] = st * a_col + jnp.where(same_head, kv, 0.0)
        return o

    def tree_scores(q, k, la, reverse):
        lv = lv_ref[...]
        scores = jnp.where(lv == 32, nt(expand(q.astype(BF16)), k.astype(BF16)), 0.0)
        pre, tot = la, la
        h, level = 1, 31
        while h < c:
            bit = (row & h) != 0
            query_side = jnp.logical_not(bit) if reverse else bit
            x = jnp.exp(jnp.minimum(jnp.where(query_side, pre, tot - pre), 0.0))
            qx = jnp.where(query_side, q * x, 0.0).astype(BF16)
            kx = jnp.where(query_side, 0.0, k * x).astype(BF16)
            scores = jnp.where(lv == level, nt(expand(qx), kx), scores)
            partner = jnp.where(bit, pltpu.roll(tot, h, 0), pltpu.roll(tot, c - h, 0))
            pre = pre + jnp.where(query_side, partner, 0.0)
            tot = tot + partner
            h, level = 2 * h, level - 1
        return scores, pre, tot

    lb = lb_ref[...]
    col = lambda ref, j: ref[:, j * D_REC:(j + 1) * D_REC]
    for d, (r_ref, o_ref) in enumerate(((rf_ref, of_ref), (rb_ref, ob_ref))):
        q, k, v16 = col(r_ref, 0), col(r_ref, 1) * (RET_DK ** -0.5), col(r_ref, 2).astype(BF16)
        scores = nt(expand(q.astype(BF16)), k.astype(BF16)) * rd_ref[d]
        o_ref[:, 0:D_REC] = finish(q, k, v16, scores, rfac_ref[d, 0], rfac_ref[d, 1], ra_ref[d], st_scr.at[d])
        z = col(r_ref, 5 + d)
        la = jnp.log(jnp.maximum(lb + (1.0 - lb) * jax.nn.sigmoid(z), MIN_GATE))
        q, k, v16 = _silu(col(r_ref, 4)), (1.0 - lb) * jax.nn.sigmoid(-z), col(r_ref, 7).astype(BF16)
        scores, pre, tot = tree_scores(q, k, la, reverse=d == 1)
        o_ref[:, D_REC:2 * D_REC] = finish(q, k, v16, scores, jnp.exp(jnp.minimum(pre, 0.0)),
                                           jnp.exp(jnp.minimum(tot - pre, 0.0)), jnp.exp(tot[0:1, :]),
                                           st_scr.at[2 + d])

    if state_out:
        @pl.when(ci == pl.num_programs(1) - 1)
        def _():
            for d in range(2):
                for idx, ref in ((d, oret_ref), (2 + d, ohg_ref)):
                    ref[0, 0, d] = functools.reduce(
                        jnp.add, [_dot(p, t4t_ref[...]) for p in _split_bf16(st_scr[idx], 3)])


def _recurrence(rest, ret_tabs, lb_row, lv, row0, n_seq, seq_len, layer, *, states_in=None, states_out=None,
                first_state_out=False, into=None):
    t = rest.shape[0]
    c = REC_CHUNK
    nc = seq_len // c
    cb0 = row0 // c
    fmap = lambda b, i: (cb0 + b * nc + i, 0)
    bmap = lambda b, i: (cb0 + b * nc + nc - 1 - i, 0)
    const2 = lambda b, i: (0, 0)
    const3 = lambda b, i: (0, 0, 0)
    smap = lambda b, i: (b, layer, 0, 0, 0)
    rd, rfac, ra = ret_tabs
    in_specs = [pl.BlockSpec((c, D_REST), fmap), pl.BlockSpec((c, D_REST), bmap),
                pl.BlockSpec((2, N_REC_HEADS * c, c), const3),
                pl.BlockSpec((2, 2, c, D_REC), lambda b, i: (0, 0, 0, 0)),
                pl.BlockSpec((2, 1, D_REC), const3),
                pl.BlockSpec((1, D_REC), const2),
                pl.BlockSpec((N_REC_HEADS * c, c), const2)]
    args = [rest, rest, rd, rfac, ra, lb_row, lv]
    tile4 = jnp.tile(jnp.eye(D_HEAD, dtype=BF16), (1, N_REC_HEADS))
    state_block = pl.BlockSpec((1, 1, 2, D_REC, D_HEAD), smap)
    if states_in is not None:
        in_specs += [state_block, state_block, pl.BlockSpec((D_HEAD, D_REC), const2)]
        args += [states_in[0], states_in[1], tile4]
    want_state = first_state_out or states_out is not None
    if want_state:
        in_specs += [pl.BlockSpec((D_REC, D_HEAD), const2)]
        args += [tile4.T]
    out_specs = [pl.BlockSpec((c, 2 * D_REC), fmap), pl.BlockSpec((c, 2 * D_REC), bmap)]
    out_shape = [jax.ShapeDtypeStruct((t, 2 * D_REC), F32), jax.ShapeDtypeStruct((t, 2 * D_REC), F32)]
    aliases = {}
    n_alias = 0
    if into is not None:
        aliases = {len(args): 0, len(args) + 1: 1}
        in_specs += [pl.BlockSpec(memory_space=pl.ANY)] * 2
        args += list(into)
        n_alias = 2
    if want_state:
        out_specs += [state_block, state_block]
        st_shape = jax.ShapeDtypeStruct((n_seq, DEPTH, 2, D_REC, D_HEAD), F32)
        out_shape += [st_shape, st_shape]
        if states_out is not None:
            aliases.update({len(args): 2, len(args) + 1: 3})
            in_specs += [pl.BlockSpec(memory_space=pl.ANY)] * 2
            args += list(states_out)
            n_alias += 2
    return pl.pallas_call(
        functools.partial(_rec_kernel, state_in=states_in is not None, state_out=want_state, aliased=n_alias),
        grid=(n_seq, nc),
        in_specs=in_specs,
        out_specs=out_specs,
        out_shape=out_shape,
        scratch_shapes=[pltpu.VMEM((4, D_REC, D_REC), F32)],
        input_output_aliases=aliases,
        compiler_params=_params("arbitrary", "arbitrary"),
        name="rec_lat" if states_in is not None else "rec_ctx",
    )(*args)


def _route(logits, bias):
    scores = jax.nn.sigmoid(logits)
    sel = scores + bias
    srow = [scores[e:e + 1, :] for e in range(N_EXPERTS)]
    rows = [sel[e:e + 1, :] for e in range(N_EXPERTS)]
    gs = []
    for g in range(N_GROUPS):
        a, b, c, d = rows[4 * g:4 * g + 4]
        gs.append(functools.reduce(jnp.maximum, [a + b, a + c, a + d, b + c, b + d, c + d]))
    best = jnp.zeros_like(gs[0], dtype=jnp.int32)
    best_v = gs[0]
    for g in range(1, N_GROUPS):
        upd = gs[g] > best_v
        best = jnp.where(upd, g, best)
        best_v = jnp.where(upd, gs[g], best_v)
    masked = [jnp.where(best == e // EXPERTS_PER_GROUP, rows[e], MASKED_SCORE) for e in range(N_EXPERTS)]
    i1 = jnp.zeros_like(best)
    v1 = masked[0]
    for e in range(1, N_EXPERTS):
        upd = masked[e] > v1
        i1 = jnp.where(upd, e, i1)
        v1 = jnp.where(upd, masked[e], v1)
    i2 = jnp.zeros_like(best)
    v2 = jnp.full_like(v1, -jnp.inf)
    for e in range(N_EXPERTS):
        upd = (masked[e] > v2) & (i1 != e)
        i2 = jnp.where(upd, e, i2)
        v2 = jnp.where(upd, masked[e], v2)
    w1 = functools.reduce(jnp.add, [jnp.where(i1 == e, srow[e], 0.0) for e in range(N_EXPERTS)])
    w2 = functools.reduce(jnp.add, [jnp.where(i2 == e, srow[e], 0.0) for e in range(N_EXPERTS)])
    tot = w1 + w2
    w1, w2 = w1 / tot, w2 / tot
    expert = lax.broadcasted_iota(jnp.int32, logits.shape, 0)
    return jnp.where(expert == i1, w1, 0.0) + jnp.where(expert == i2, w2, 0.0)


def _outproj_kernel(xa_ref, xb_ref, oda_ref, of_ref, ob_ref, rg_ref, gg_ref, mod_ref, subln_ref, retn_ref,
                    hgn_ref, g64_ref, wout_ref, nffn_ref, wrh_ref, wrl_ref, rb_ref,
                    xmid_ref, hn2_ref, gate_ref, *, n_ctx_tiles):
    g64 = g64_ref[...]

    def gnorm(t, w):
        return t * lax.rsqrt(_group_mean_sq(t, g64) + EPS) * w

    o_da = gnorm(oda_ref[...], subln_ref[...])
    o_rec = of_ref[...] + ob_ref[...]
    o_ret = gnorm(o_rec[:, 0:D_REC], retn_ref[...]) * _silu(rg_ref[...])
    o_hg = gnorm(o_rec[:, D_REC:2 * D_REC], hgn_ref[...]) * _silu(gg_ref[...])
    mixed = jnp.concatenate([o_da, o_ret, o_hg], axis=1).astype(BF16)
    mod = mod_ref[0]
    x = jnp.where(pl.program_id(0) < n_ctx_tiles, xa_ref[...], xb_ref[...])
    x = x + mod[:, 2 * D_MODEL:3 * D_MODEL] * _dot(mixed, wout_ref[...])
    xmid_ref[...] = x
    y = x * lax.rsqrt(jnp.mean(x * x, axis=-1, keepdims=True) + EPS) * nffn_ref[...]
    hn = y * (1.0 + mod[:, 4 * D_MODEL:5 * D_MODEL]) + mod[:, 3 * D_MODEL:4 * D_MODEL]
    hi, lo = _split_bf16(hn)
    hn2_ref[...] = hi
    wrh = wrh_ref[...]
    logits = (lax.dot_general(wrh, hi, _NT, preferred_element_type=F32)
              + lax.dot_general(wrh, lo, _NT, preferred_element_type=F32)
              + lax.dot_general(wrl_ref[...], hi, _NT, preferred_element_type=F32))
    gate_ref[...] = _route(logits, rb_ref[...])


def _outproj(xa, xb, oda, o_f, o_b, rest, mod, subln, retn, hgn, g64, w_out16, norm_w, wr_hi, wr_lo, rbias, st):
    tm = st.tile
    t = st.n_tiles * tm
    row = lambda i: (i, 0)
    const = lambda i: (0, 0)
    return pl.pallas_call(
        functools.partial(_outproj_kernel, n_ctx_tiles=st.n_ctx),
        grid=(st.n_tiles,),
        in_specs=[pl.BlockSpec((tm, D_MODEL), lambda i: (st.ctx_idx(i), 0)),
                  pl.BlockSpec((tm, D_MODEL), lambda i: (st.lat_idx(i), 0)),
                  pl.BlockSpec((tm, D_V), row),
                  pl.BlockSpec((tm, 2 * D_REC), row),
                  pl.BlockSpec((tm, 2 * D_REC), row),
                  pl.BlockSpec((tm, D_REC), lambda i: (i, 3)),
                  pl.BlockSpec((tm, D_REC), lambda i: (i, 8)),
                  pl.BlockSpec((1, 1, 6 * D_MODEL), lambda i: (st.mod_idx(i), 0, 0)),
                  pl.BlockSpec((1, D_V), const),
                  pl.BlockSpec((1, D_REC), const),
                  pl.BlockSpec((1, D_REC), const),
                  pl.BlockSpec((LANES, LANES), const),
                  pl.BlockSpec((D_MODEL, D_MODEL), const, pipeline_mode=pl.Buffered(1)),
                  pl.BlockSpec((1, D_MODEL), const),
                  pl.BlockSpec((N_EXPERTS, D_MODEL), const),
                  pl.BlockSpec((N_EXPERTS, D_MODEL), const),
                  pl.BlockSpec((N_EXPERTS, 1), const)],
        out_specs=[pl.BlockSpec((tm, D_MODEL), row), pl.BlockSpec((tm, D_MODEL), row),
                   pl.BlockSpec((N_EXPERTS, tm), lambda i: (0, i))],
        out_shape=[jax.ShapeDtypeStruct((t, D_MODEL), F32), jax.ShapeDtypeStruct((t, D_MODEL), BF16),
                   jax.ShapeDtypeStruct((N_EXPERTS, t), F32)],
        compiler_params=_params("arbitrary"),
        name="outproj",
    )(xa, xb, oda, o_f, o_b, rest, rest, mod, subln, retn, hgn, g64, w_out16, norm_w, wr_hi, wr_lo, rbias)


def _moe_kernel(h_ref, gate_ref, wg_ref, wu_ref, wd_ref, x_ref, mod_ref, oa_ref, ob_ref, acc_ref, *, n_ctx_tiles):
    i = pl.program_id(0)
    e = pl.program_id(1)

    @pl.when(e == 0)
    def _():
        acc_ref[...] = jnp.zeros_like(acc_ref)

    h = h_ref[...]
    lane = lax.broadcasted_iota(jnp.int32, (1, N_EXPERTS), 1)
    g = jnp.sum(jnp.where(lane == e, gate_ref[...], 0.0), axis=1, keepdims=True)
    act = _silu(_dot(h, wg_ref[0].astype(BF16))) * _dot(h, wu_ref[0].astype(BF16))
    acc_ref[...] += _dot((act * g).astype(BF16), wd_ref[0].astype(BF16))

    last = e == pl.num_programs(1) - 1
    result = lambda: x_ref[...] + mod_ref[0][:, 5 * D_MODEL:6 * D_MODEL] * acc_ref[...]

    @pl.when(last & (i < n_ctx_tiles))
    def _():
        oa_ref[...] = result()

    @pl.when(last & (i >= n_ctx_tiles))
    def _():
        ob_ref[...] = result()


def _moe(hn2, gate, w_gate, w_up, w_down, x_mid, mod, st, layer):
    tm = st.tile
    row = lambda i, e: (i, 0)
    wmap = lambda i, e: (layer * N_EXPERTS + e, 0, 0)
    return pl.pallas_call(
        functools.partial(_moe_kernel, n_ctx_tiles=st.n_ctx),
        grid=(st.n_tiles, N_EXPERTS),
        in_specs=[pl.BlockSpec((tm, D_MODEL), row),
                  pl.BlockSpec((tm, N_EXPERTS), row),
                  pl.BlockSpec((1, D_MODEL, D_EXPERT), wmap),
                  pl.BlockSpec((1, D_MODEL, D_EXPERT), wmap),
                  pl.BlockSpec((1, D_EXPERT, D_MODEL), wmap),
                  pl.BlockSpec((tm, D_MODEL), row),
                  pl.BlockSpec((1, 1, 6 * D_MODEL), lambda i, e: (st.mod_idx(i), 0, 0))],
        out_specs=[pl.BlockSpec((tm, D_MODEL), lambda i, e: (st.ctx_idx(i), 0)),
                   pl.BlockSpec((tm, D_MODEL), lambda i, e: (st.lat_idx(i), 0))],
        out_shape=[jax.ShapeDtypeStruct((st.n_ctx * tm, D_MODEL), F32),
                   jax.ShapeDtypeStruct(((st.n_tiles - st.n_ctx) * tm, D_MODEL), F32)],
        scratch_shapes=[pltpu.VMEM((tm, D_MODEL), F32)],
        compiler_params=_params("arbitrary", "arbitrary"),
        name="moe",
    )(hn2, gate, w_gate, w_up, w_down, x_mid, mod)


def _block_avg(group):
    i = np.arange(LANES)
    return jnp.asarray((i[:, None] // group == i[None, :] // group) / group, dtype=BF16)


def _rope_tables(n_tokens, n_identity):
    rows = n_tokens // GRID_W
    pos_r = jnp.repeat(jnp.arange(rows, dtype=F32), GRID_W)
    pos_c = jnp.tile(jnp.arange(GRID_W, dtype=F32), rows)
    n_freq = DA_QK // 4
    inv = ROPE_BASE ** (-jnp.arange(n_freq, dtype=F32) / n_freq)
    ang = jnp.concatenate([pos_r[:, None] * inv, pos_c[:, None] * inv], axis=-1)
    cos = jnp.repeat(jnp.cos(ang), 2, axis=-1)
    sin = jnp.repeat(jnp.sin(ang), 2, axis=-1) * jnp.tile(jnp.asarray([-1.0, 1.0], F32), DA_QK // 2)
    cos = jnp.concatenate([jnp.ones((n_identity, DA_QK), F32), cos], axis=0)
    sin = jnp.concatenate([jnp.zeros((n_identity, DA_QK), F32), sin], axis=0)
    return jnp.tile(cos, (1, 2 * H_DA)), jnp.tile(sin, (1, 2 * H_DA))


def kernel(x_prompt, x_sample, cache_k, cache_v, state_ret, state_hgrn, c, c_ctx, w_in, w_out, w_ada, b_ada,
           norm_mix, norm_ffn, da_qnorm, da_knorm, da_lambda, da_subln, ret_decay, ret_norm, hg_lb, hg_norm,
           w_router, router_bias, w_gate, w_up, w_down):
    n_ctx, l_ctx, _ = x_prompt.shape
    n_lat, l_lat, _ = x_sample.shape
    t_ctx = n_ctx * l_ctx
    past = cache_k.shape[2]
    assert l_ctx == TOKEN_TILE and l_lat % MOE_TILE == 0 and t_ctx % MOE_TILE == 0 and l_lat % GRID_W == 0

    ctx_row = n_lat
    n_cond = -(-(n_lat + 1) // 8) * 8
    cond = jnp.zeros((n_cond, D_MODEL), F32).at[:n_lat].set(c).at[ctx_row].set(c_ctx)
    mod_all = _ada(cond, w_ada, b_ada)

    st = _Stream(t_ctx, n_lat, l_lat, TOKEN_TILE, ctx_row)
    st_moe = _Stream(t_ctx, n_lat, l_lat, MOE_TILE, ctx_row)
    g32, g64 = _block_avg(DA_QK), _block_avg(DA_V)
    cos_t, sin_t = _rope_tables(l_lat, TOKEN_TILE)
    lv = jnp.asarray(_level_table(REC_CHUNK))
    p_lb = jax.nn.softmax(hg_lb.astype(F32), axis=0)
    lb_all = jnp.cumsum(p_lb, axis=0) - p_lb[0]
    wr_hi = w_router.T.astype(BF16)
    wr_lo = (w_router.T - wr_hi.astype(F32)).astype(BF16)
    rbias = router_bias.astype(F32).reshape(N_EXPERTS, 1)
    flat_w = lambda w: w.reshape(DEPTH * N_EXPERTS, w.shape[2], w.shape[3])
    w_gate, w_up, w_down = flat_w(w_gate), flat_w(w_up), flat_w(w_down)
    flat_s = lambda s: s.astype(F32).reshape(n_lat, DEPTH, 2, D_REC, D_HEAD)
    states_in = (flat_s(state_ret), flat_s(state_hgrn))
    cache_k16 = cache_k.reshape(n_lat, DEPTH, past, D_QK).astype(BF16)
    cache_v16 = cache_v.reshape(n_lat, DEPTH, past, D_V).astype(BF16)

    xa, xb = x_prompt.reshape(t_ctx, D_MODEL), x_sample.reshape(n_lat * l_lat, D_MODEL)
    caches, states_out = None, None
    for l in range(DEPTH):
        lam_init = 0.8 - 0.6 * math.exp(-0.3 * l)
        lp = da_lambda[l].astype(F32)
        lam = (jnp.exp(jnp.sum(lp[0] * lp[1])) - jnp.exp(jnp.sum(lp[2] * lp[3])) + lam_init).reshape(1)
        mod = mod_all[l].reshape(n_cond, 1, 6 * D_MODEL)
        qw = (jnp.tile(da_qnorm[l].astype(F32), 2 * H_DA) * (DA_QK ** -0.5)).reshape(1, D_QK)
        kw = jnp.tile(da_knorm[l].astype(F32), 2 * H_DA).reshape(1, D_QK)

        qb, kb, vb, rest, kc, vc = _inproj(xa, xb, mod, norm_mix[l].reshape(1, D_MODEL), w_in[l].astype(BF16),
                                           qw, kw, g32, cos_t, sin_t, st, l, n_ctx, caches)
        caches = (kc, vc)

        oda = _attention(lam, qb, kb, vb, 0, n_ctx, l_ctx)
        oda = _attention(lam, qb, kb, vb, t_ctx, n_lat, l_lat, l, cache=(cache_k16, cache_v16), into=oda)

        ret_tabs = _retention_tables(-jnp.exp(ret_decay[l].astype(F32)), REC_CHUNK)
        lb_row = lb_all[l].reshape(1, D_REC)
        o_f, o_b, s_ret, s_hg = _recurrence(rest, ret_tabs, lb_row, lv, 0, n_ctx, l_ctx, l,
                                            states_out=states_out, first_state_out=True)
        states_out = (s_ret, s_hg)
        o_f, o_b = _recurrence(rest, ret_tabs, lb_row, lv, t_ctx, n_lat, l_lat, l,
                               states_in=states_in, into=(o_f, o_b))

        subln = (jnp.tile(da_subln[l].astype(F32), H_DA) * (1.0 - lam_init)).reshape(1, D_V)
        retn = jnp.tile(ret_norm[l].astype(F32), H_RET).reshape(1, D_REC)
        hgn = jnp.tile(hg_norm[l].astype(F32), H_HG).reshape(1, D_REC)
        x_mid, hn2, gate_t = _outproj(xa, xb, oda, o_f, o_b, rest, mod, subln, retn, hgn, g64,
                                      w_out[l].astype(BF16), norm_ffn[l].reshape(1, D_MODEL), wr_hi, wr_lo,
                                      rbias, st)
        xa, xb = _moe(hn2, gate_t.T, w_gate, w_up, w_down, x_mid, mod, st_moe, l)

    kc, vc = caches
    s_ret, s_hg = states_out
    return (xa.reshape(n_ctx, l_ctx, D_MODEL), xb.reshape(n_lat, l_lat, D_MODEL),
            kc.reshape(n_ctx, DEPTH, l_ctx, H_DA, 2, DA_QK), vc.reshape(n_ctx, DEPTH, l_ctx, H_DA, DA_V),
            s_ret.reshape(n_ctx, DEPTH, 2, H_RET, RET_DK, RET_DV),
            s_hg.reshape(n_ctx, DEPTH, 2, H_HG, HG_DK, HG_DV))
```

```python
import functools
import math

import numpy as np
import jax
import jax.numpy as jnp
from jax import lax
from jax.experimental import pallas as pl
from jax.experimental.pallas import tpu as pltpu

F32 = jnp.float32
BF16 = jnp.bfloat16
I32 = jnp.int32

D_MODEL = 1024
DEPTH = 2
GRID_W = 64
EPS = 1e-6
MIN_GATE = 1e-30
ROPE_BASE = 10000.0
H_DA, DA_QK, DA_V = 8, 32, 64
H_RET, RET_DK, RET_DV = 4, 64, 64
H_HG, HG_DK, HG_DV = 4, 64, 64
D_QK = H_DA * 2 * DA_QK
D_V = H_DA * DA_V
N_REC_HEADS = 4
D_HEAD = 64
D_REC = N_REC_HEADS * D_HEAD
D_REST = 9 * D_REC
D_IN = 2 * D_QK + D_V + D_REST
N_EXPERTS, N_GROUPS, EXPERTS_PER_GROUP = 16, 4, 4
D_EXPERT = 512
MASKED_SCORE = -2.0

LANES = 128
TOKEN_TILE = 256
Q_TILE = 256
REC_CHUNK = 128
MOE_BLOCK = 1024
MOE_TILE = 128
MOE_ITEMS_PER_BLOCK = 2 * MOE_BLOCK // MOE_TILE + N_GROUPS - 1
VMEM_LIMIT = 56 * 1024 * 1024

_NT = (((1,), (1,)), ((), ()))
_TN = (((0,), (0,)), ((), ()))


def _params(*sem):
    return pltpu.CompilerParams(dimension_semantics=sem, vmem_limit_bytes=VMEM_LIMIT)


def _dot(a, b):
    return jnp.dot(a, b, preferred_element_type=F32)


def _split_bf16(t, terms=2):
    out = []
    for _ in range(terms - 1):
        hi = t.astype(BF16)
        out.append(hi)
        t = t - hi.astype(F32)
    out.append(t.astype(BF16))
    return out


def _group_mean_sq(t, g):
    hi, lo = _split_bf16(t * t)
    outs = []
    for s in range(t.shape[1] // LANES):
        sl = slice(s * LANES, (s + 1) * LANES)
        outs.append(_dot(hi[:, sl], g) + _dot(lo[:, sl], g))
    return jnp.concatenate(outs, axis=1) if len(outs) > 1 else outs[0]


def _silu(t):
    return t * jax.nn.sigmoid(t)


class _Stream:
    def __init__(self, n_ctx_rows, n_lat_seq, lat_len, tile, ctx_row):
        self.tile = tile
        self.n_ctx = n_ctx_rows // tile
        self.per_seq = lat_len // tile
        self.n_tiles = self.n_ctx + n_lat_seq * self.per_seq
        self.ctx_row = ctx_row

    def ctx_idx(self, i):
        return jnp.minimum(i, self.n_ctx - 1)

    def lat_idx(self, i):
        return jnp.maximum(i - self.n_ctx, 0)

    def mod_idx(self, i):
        return jnp.where(i < self.n_ctx, self.ctx_row, (i - self.n_ctx) // self.per_seq)

    def pos_idx(self, i):
        return jnp.where(i < self.n_ctx, 0, 1 + (i - self.n_ctx) % self.per_seq)


def _ada_kernel(c_ref, w_ref, b_ref, o_ref):
    s = _silu(c_ref[...]).astype(BF16)
    o_ref[0] = _dot(s, w_ref[0].astype(BF16)) + b_ref[0]


def _ada(cond, w_ada, b_ada):
    rows = cond.shape[0]
    n_tile = 1536
    return pl.pallas_call(
        _ada_kernel,
        grid=(DEPTH, 6 * D_MODEL // n_tile),
        in_specs=[pl.BlockSpec((rows, D_MODEL), lambda l, j: (0, 0)),
                  pl.BlockSpec((1, D_MODEL, n_tile), lambda l, j: (l, 0, j)),
                  pl.BlockSpec((1, 1, n_tile), lambda l, j: (l, 0, j))],
        out_specs=pl.BlockSpec((1, rows, n_tile), lambda l, j: (l, 0, j)),
        out_shape=jax.ShapeDtypeStruct((DEPTH, rows, 6 * D_MODEL), F32),
        compiler_params=_params("arbitrary", "arbitrary"),
        name="ada",
    )(cond, w_ada, b_ada.reshape(DEPTH, 1, 6 * D_MODEL))


def _inproj_kernel(*refs, n_ctx_tiles, first):
    (xa_ref, xb_ref, mod_ref, nw_ref, w_ref, qw_ref, kw_ref, g32_ref, cos_ref, sin_ref) = refs[:10]
    q_ref, kb_ref, vb_ref, rest_ref, kc_ref, vc_ref = refs[10:] if first else refs[12:]
    is_ctx = pl.program_id(0) < n_ctx_tiles
    x = jnp.where(is_ctx, xa_ref[...], xb_ref[...])
    y = x * lax.rsqrt(jnp.mean(x * x, axis=-1, keepdims=True) + EPS) * nw_ref[...]
    mod = mod_ref[0]
    hn = (y * (1.0 + mod[:, D_MODEL:2 * D_MODEL]) + mod[:, 0:D_MODEL]).astype(BF16)
    g32 = g32_ref[...]
    lane = lax.broadcasted_iota(I32, (1, D_QK), 1)
    even = (lane & 1) == 0

    def qk_norm_rope(t, w):
        t = t * lax.rsqrt(_group_mean_sq(t, g32) + EPS) * w
        partner = jnp.where(even, pltpu.roll(t, D_QK - 1, 1), pltpu.roll(t, 1, 1))
        return t * cos_ref[...] + partner * sin_ref[...]

    q_ref[...] = qk_norm_rope(_dot(hn, w_ref[:, 0:D_QK]), qw_ref[...]).astype(BF16)
    k = qk_norm_rope(_dot(hn, w_ref[:, D_QK:2 * D_QK]), kw_ref[...])
    kb_ref[...] = k.astype(BF16)
    v = _dot(hn, w_ref[:, 2 * D_QK:2 * D_QK + D_V])
    vb_ref[...] = v.astype(BF16)
    rest_ref[...] = _dot(hn, w_ref[:, 2 * D_QK + D_V:D_IN])

    @pl.when(is_ctx)
    def _():
        kc_ref[0, 0] = k
        vc_ref[0, 0] = v
        if first:
            kc_ref[0, 1:] = jnp.zeros_like(kc_ref[0, 1:])
            vc_ref[0, 1:] = jnp.zeros_like(vc_ref[0, 1:])


def _inproj(xa, xb, mod, norm_w, w_in16, qw, kw, g32, cos_t, sin_t, st, layer, n_ctx_seq, caches):
    tm = st.tile
    t = st.n_tiles * tm
    first = caches is None
    row = lambda i: (i, 0)
    const = lambda i: (0, 0)
    in_specs = [pl.BlockSpec((tm, D_MODEL), lambda i: (st.ctx_idx(i), 0)),
                pl.BlockSpec((tm, D_MODEL), lambda i: (st.lat_idx(i), 0)),
                pl.BlockSpec((1, 1, 6 * D_MODEL), lambda i: (st.mod_idx(i), 0, 0)),
                pl.BlockSpec((1, D_MODEL), const),
                pl.BlockSpec((D_MODEL, D_IN), const, pipeline_mode=pl.Buffered(1)),
                pl.BlockSpec((1, D_QK), const),
                pl.BlockSpec((1, D_QK), const),
                pl.BlockSpec((LANES, LANES), const),
                pl.BlockSpec((tm, D_QK), lambda i: (st.pos_idx(i), 0)),
                pl.BlockSpec((tm, D_QK), lambda i: (st.pos_idx(i), 0))]
    args = [xa, xb, mod, norm_w, w_in16, qw, kw, g32, cos_t, sin_t]
    aliases = {}
    if first:
        cache_spec = pl.BlockSpec((1, DEPTH, tm, D_QK), lambda i: (st.ctx_idx(i), 0, 0, 0))
    else:
        cache_spec = pl.BlockSpec((1, 1, tm, D_QK), lambda i: (st.ctx_idx(i), layer, 0, 0))
        in_specs += [pl.BlockSpec(memory_space=pl.ANY)] * 2
        args += list(caches)
        aliases = {10: 4, 11: 5}
    cache_shape = jax.ShapeDtypeStruct((n_ctx_seq, DEPTH, tm, D_QK), F32)
    return pl.pallas_call(
        functools.partial(_inproj_kernel, n_ctx_tiles=st.n_ctx, first=first),
        grid=(st.n_tiles,),
        in_specs=in_specs,
        out_specs=[pl.BlockSpec((tm, D_QK), row), pl.BlockSpec((tm, D_QK), row),
                   pl.BlockSpec((tm, D_V), row), pl.BlockSpec((tm, D_REST), row), cache_spec, cache_spec],
        out_shape=[jax.ShapeDtypeStruct((t, D_QK), BF16), jax.ShapeDtypeStruct((t, D_QK), BF16),
                   jax.ShapeDtypeStruct((t, D_V), BF16), jax.ShapeDtypeStruct((t, D_REST), F32),
                   cache_shape, cache_shape],
        input_output_aliases=aliases,
        compiler_params=_params("arbitrary"),
        name="inproj",
    )(*args)


def _attn_kernel(*refs, with_cache):
    if with_cache:
        lam_ref, q_ref, k_ref, v_ref, ck_ref, cv_ref, o_ref = refs
    else:
        lam_ref, q_ref, k_ref, v_ref, o_ref = refs
    lam = lam_ref[0]
    q = q_ref[...]
    lane = lax.broadcasted_iota(I32, (1, LANES), 1)
    left = lane < DA_V
    one = jnp.ones((), BF16)
    zero = jnp.zeros((), BF16)
    segs = [(k_ref[...], v_ref[...])]
    if with_cache:
        segs.append((ck_ref[0, 0], cv_ref[0, 0]))

    acc = jnp.zeros(o_ref.shape, F32)
    for side in range(2):
        own = left if side == 0 else jnp.logical_not(left)
        vms = [jnp.where(own, v, one) for _, v in segs]
        for r in range(2):
            lo = side * DA_V + r * DA_QK
            qm = jnp.where((lane >= lo) & (lane < lo + DA_QK), q, zero)
            ss = [lax.dot_general(qm, k, _NT, preferred_element_type=F32) for k, _ in segs]
            m = functools.reduce(jnp.maximum, [jnp.max(s, axis=-1, keepdims=True) for s in ss])
            res = functools.reduce(
                jnp.add, [_dot(jnp.exp(s - m).astype(BF16), vm) for s, vm in zip(ss, vms)])
            den = pltpu.roll(res, DA_V, 1)
            coef = 1.0 if r == 0 else -lam
            acc = acc + jnp.where(own, coef * (res / den), 0.0)
    o_ref[...] = acc


def _attention(lam, qb, kb, vb, row0, n_seq, seq_len, layer=0, cache=None):
    tq = Q_TILE
    nq = seq_len // tq
    qb0 = row0 // tq
    sb0 = row0 // seq_len
    n_pair = D_QK // LANES
    qmap = lambda b, p, i: (qb0 + b * nq + i, p)
    kmap = lambda b, p, i: (sb0 + b, p)
    smem = pl.BlockSpec(memory_space=pltpu.SMEM)
    in_specs = [smem, pl.BlockSpec((tq, LANES), qmap), pl.BlockSpec((seq_len, LANES), kmap),
                pl.BlockSpec((seq_len, LANES), kmap)]
    args = [lam, qb, kb, vb]
    if cache is not None:
        ck, cv = cache
        cmap = lambda b, p, i: (b, layer, 0, p)
        in_specs += [pl.BlockSpec((1, 1, ck.shape[2], LANES), cmap),
                     pl.BlockSpec((1, 1, cv.shape[2], LANES), cmap)]
        args += [ck, cv]
    return pl.pallas_call(
        functools.partial(_attn_kernel, with_cache=cache is not None),
        grid=(n_seq, n_pair, nq),
        in_specs=in_specs,
        out_specs=pl.BlockSpec((tq, LANES), lambda b, p, i: (b * nq + i, p)),
        out_shape=jax.ShapeDtypeStruct((n_seq * seq_len, D_V), F32),
        compiler_params=_params("arbitrary", "arbitrary", "arbitrary"),
        name="attn_lat" if cache is not None else "attn_ctx",
    )(*args)


def _level_table(c):
    t = np.arange(c)[:, None] ^ np.arange(c)[None, :]
    lv = np.where(t == 0, 32, 31 - np.floor(np.log2(np.maximum(t, 1))).astype(np.int64))
    return np.tile(lv, (N_REC_HEADS, 1)).astype(np.int32)


def _retention_tables(log_gamma, c):
    g = jnp.repeat(log_gamma, D_HEAD, axis=1)
    t = jnp.arange(c, dtype=F32)
    diff = t[:, None] - t[None, :]
    gh = log_gamma[:, :, None, None]
    d_f = jnp.where(diff >= 0, jnp.exp(gh[0] * jnp.maximum(diff, 0.0)), 0.0)
    d_b = jnp.where(diff <= 0, jnp.exp(gh[1] * jnp.maximum(-diff, 0.0)), 0.0)
    d = jnp.stack([d_f, d_b]).reshape(2, N_REC_HEADS * c, c)
    q_f, k_f = jnp.exp(g[0] * (t[:, None] + 1.0)), jnp.exp(g[0] * (c - 1.0 - t[:, None]))
    q_b, k_b = jnp.exp(g[1] * (c - t[:, None])), jnp.exp(g[1] * t[:, None])
    f = jnp.stack([jnp.stack([q_f, k_f]), jnp.stack([q_b, k_b])])
    a = jnp.exp(g * c).reshape(2, 1, D_REC)
    return d, f, a


def _rec_kernel(*refs, state_in, state_out, aliased):
    refs = list(refs)
    rf_ref, rb_ref, rd_ref, rfac_ref, ra_ref, lb_ref, lv_ref = refs[:7]
    del refs[:7]
    if state_in:
        sret_ref, shg_ref, t4_ref = refs[:3]
        del refs[:3]
    if state_out:
        t4t_ref = refs.pop(0)
    del refs[:aliased]
    of_ref, ob_ref = refs[:2]
    del refs[:2]
    if state_out:
        oret_ref, ohg_ref = refs[:2]
        del refs[:2]
    (st_scr,) = refs

    ci = pl.program_id(1)
    c = rf_ref.shape[0]
    lane = lax.broadcasted_iota(I32, (1, D_REC), 1)
    head = lane >> 6
    row = lax.broadcasted_iota(I32, (c, 1), 0)
    srow = lax.broadcasted_iota(I32, (D_REC, 1), 0)
    same_head = (srow >> 6) == head
    eye = srow == lane
    head_is = [head == h for h in range(N_REC_HEADS)]
    head_m16 = [jnp.broadcast_to(jnp.where(m, 1.0, 0.0), (c, D_REC)).astype(BF16) for m in head_is]

    @pl.when(ci == 0)
    def _():
        if state_in:
            for d in range(2):
                for idx, ref in ((d, sret_ref), (2 + d, shg_ref)):
                    tiled = functools.reduce(jnp.add, [_dot(p, t4_ref[...]) for p in _split_bf16(ref[0, 0, d], 3)])
                    st_scr[idx] = jnp.where(same_head, tiled, 0.0)
        else:
            st_scr[...] = jnp.zeros_like(st_scr)

    def expand(t16):
        return jnp.concatenate([t16 * m for m in head_m16], axis=0)

    def nt(a, b):
        return lax.dot_general(a, b, _NT, preferred_element_type=F32)

    def finish(q, k, v16, scores, qfac, kfac, a_row, st_ref):
        st = st_ref[...]
        o = _dot((q * qfac).astype(BF16), st.astype(BF16))
        o_stack = _dot(scores.astype(BF16), v16)
        for h in range(N_REC_HEADS):
            o = o + jnp.where(head_is[h], o_stack[h * c:(h + 1) * c], 0.0)
        kv = lax.dot_general((k * kfac).astype(BF16), v16, _TN, preferred_element_type=F32)
        a_col = jnp.sum(jnp.where(eye, a_row, 0.0), axis=1, keepdims=True)
        st_ref[...] = st * a_col + jnp.where(same_head, kv, 0.0)
        return o

    def tree_scores(q, k, la, reverse):
        lv = lv_ref[...]
        scores = jnp.where(lv == 32, nt(expand(q.astype(BF16)), k.astype(BF16)), 0.0)
        pre, tot = la, la
        h, level = 1, 31
        while h < c:
            bit = (row & h) != 0
            query_side = jnp.logical_not(bit) if reverse else bit
            x = jnp.exp(jnp.minimum(jnp.where(query_side, pre, tot - pre), 0.0))
            qx = jnp.where(query_side, q * x, 0.0).astype(BF16)
            kx = jnp.where(query_side, 0.0, k * x).astype(BF16)
            scores = jnp.where(lv == level, nt(expand(qx), kx), scores)
            partner = jnp.where(bit, pltpu.roll(tot, h, 0), pltpu.roll(tot, c - h, 0))
            pre = pre + jnp.where(query_side, partner, 0.0)
            tot = tot + partner
            h, level = 2 * h, level - 1
        return scores, pre, tot

    lb = lb_ref[...]
    col = lambda ref, j: ref[:, j * D_REC:(j + 1) * D_REC]
    for d, (r_ref, o_ref) in enumerate(((rf_ref, of_ref), (rb_ref, ob_ref))):
        q, k, v16 = col(r_ref, 0), col(r_ref, 1) * (RET_DK ** -0.5), col(r_ref, 2).astype(BF16)
        scores = nt(expand(q.astype(BF16)), k.astype(BF16)) * rd_ref[d]
        o_ref[:, 0:D_REC] = finish(q, k, v16, scores, rfac_ref[d, 0], rfac_ref[d, 1], ra_ref[d], st_scr.at[d])
        z = col(r_ref, 5 + d)
        la = jnp.log(jnp.maximum(lb + (1.0 - lb) * jax.nn.sigmoid(z), MIN_GATE))
        q, k, v16 = _silu(col(r_ref, 4)), (1.0 - lb) * jax.nn.sigmoid(-z), col(r_ref, 7).astype(BF16)
        scores, pre, tot = tree_scores(q, k, la, reverse=d == 1)
        o_ref[:, D_REC:2 * D_REC] = finish(q, k, v16, scores, jnp.exp(jnp.minimum(pre, 0.0)),
                                           jnp.exp(jnp.minimum(tot - pre, 0.0)), jnp.exp(tot[0:1, :]),
                                           st_scr.at[2 + d])

    if state_out:
        @pl.when(ci == pl.num_programs(1) - 1)
        def _():
            for d in range(2):
                for idx, ref in ((d, oret_ref), (2 + d, ohg_ref)):
                    ref[0, 0, d] = functools.reduce(
                        jnp.add, [_dot(p, t4t_ref[...]) for p in _split_bf16(st_scr[idx], 3)])
            if state_out == "first":
                oret_ref[0, 1:] = jnp.zeros_like(oret_ref[0, 1:])
                ohg_ref[0, 1:] = jnp.zeros_like(ohg_ref[0, 1:])


def _recurrence(rest, ret_tabs, lb_row, lv, row0, n_seq, seq_len, layer, *, states_in=None, want_states=False,
                states_out=None):
    c = REC_CHUNK
    nc = seq_len // c
    cb0 = row0 // c
    const2 = lambda b, i: (0, 0)
    const3 = lambda b, i: (0, 0, 0)
    rd, rfac, ra = ret_tabs
    in_specs = [pl.BlockSpec((c, D_REST), lambda b, i: (cb0 + b * nc + i, 0)),
                pl.BlockSpec((c, D_REST), lambda b, i: (cb0 + b * nc + nc - 1 - i, 0)),
                pl.BlockSpec((2, N_REC_HEADS * c, c), const3),
                pl.BlockSpec((2, 2, c, D_REC), lambda b, i: (0, 0, 0, 0)),
                pl.BlockSpec((2, 1, D_REC), const3),
                pl.BlockSpec((1, D_REC), const2),
                pl.BlockSpec((N_REC_HEADS * c, c), const2)]
    args = [rest, rest, rd, rfac, ra, lb_row, lv]
    tile4 = jnp.tile(jnp.eye(D_HEAD, dtype=BF16), (1, N_REC_HEADS))
    if states_in is not None:
        blk = pl.BlockSpec((1, 1, 2, D_REC, D_HEAD), lambda b, i: (b, layer, 0, 0, 0))
        in_specs += [blk, blk, pl.BlockSpec((D_HEAD, D_REC), const2)]
        args += [states_in[0], states_in[1], tile4]
    if want_states:
        in_specs += [pl.BlockSpec((D_REC, D_HEAD), const2)]
        args += [tile4.T]
    out_specs = [pl.BlockSpec((c, 2 * D_REC), lambda b, i: (b * nc + i, 0)),
                 pl.BlockSpec((c, 2 * D_REC), lambda b, i: (b * nc + nc - 1 - i, 0))]
    out_shape = [jax.ShapeDtypeStruct((n_seq * seq_len, 2 * D_REC), F32)] * 2
    aliases = {}
    state_mode = None
    if want_states:
        st_shape = jax.ShapeDtypeStruct((n_seq, DEPTH, 2, D_REC, D_HEAD), F32)
        out_shape += [st_shape, st_shape]
        if states_out is None:
            state_mode = "first"
            out_specs += [pl.BlockSpec((1, DEPTH, 2, D_REC, D_HEAD), lambda b, i: (b, 0, 0, 0, 0))] * 2
        else:
            state_mode = "next"
            out_specs += [pl.BlockSpec((1, 1, 2, D_REC, D_HEAD), lambda b, i: (b, layer, 0, 0, 0))] * 2
            aliases = {len(args): 2, len(args) + 1: 3}
            in_specs += [pl.BlockSpec(memory_space=pl.ANY)] * 2
            args += list(states_out)
    return pl.pallas_call(
        functools.partial(_rec_kernel, state_in=states_in is not None, state_out=state_mode,
                          aliased=len(aliases)),
        grid=(n_seq, nc),
        in_specs=in_specs,
        out_specs=out_specs,
        out_shape=out_shape,
        scratch_shapes=[pltpu.VMEM((4, D_REC, D_REC), F32)],
        input_output_aliases=aliases,
        compiler_params=_params("arbitrary", "arbitrary"),
        name="rec_lat" if states_in is not None else "rec_ctx",
    )(*args)


def _route(logits, bias):
    scores = jax.nn.sigmoid(logits)
    sel = scores + bias
    srow = [scores[e:e + 1, :] for e in range(N_EXPERTS)]
    rows = [sel[e:e + 1, :] for e in range(N_EXPERTS)]
    gs = []
    for g in range(N_GROUPS):
        a, b, c, d = rows[4 * g:4 * g + 4]
        gs.append(functools.reduce(jnp.maximum, [a + b, a + c, a + d, b + c, b + d, c + d]))
    best = jnp.zeros_like(gs[0], dtype=I32)
    best_v = gs[0]
    for g in range(1, N_GROUPS):
        upd = gs[g] > best_v
        best = jnp.where(upd, g, best)
        best_v = jnp.where(upd, gs[g], best_v)
    masked = [jnp.where(best == e // EXPERTS_PER_GROUP, rows[e], MASKED_SCORE) for e in range(N_EXPERTS)]
    i1 = jnp.zeros_like(best)
    v1 = masked[0]
    for e in range(1, N_EXPERTS):
        upd = masked[e] > v1
        i1 = jnp.where(upd, e, i1)
        v1 = jnp.where(upd, masked[e], v1)
    i2 = jnp.zeros_like(best)
    v2 = jnp.full_like(v1, -jnp.inf)
    for e in range(N_EXPERTS):
        upd = (masked[e] > v2) & (i1 != e)
        i2 = jnp.where(upd, e, i2)
        v2 = jnp.where(upd, masked[e], v2)
    w1 = functools.reduce(jnp.add, [jnp.where(i1 == e, srow[e], 0.0) for e in range(N_EXPERTS)])
    w2 = functools.reduce(jnp.add, [jnp.where(i2 == e, srow[e], 0.0) for e in range(N_EXPERTS)])
    tot = w1 + w2
    w1, w2 = w1 / tot, w2 / tot
    expert = lax.broadcasted_iota(I32, logits.shape, 0)
    return jnp.where(expert == i1, w1, 0.0) + jnp.where(expert == i2, w2, 0.0)


def _outproj_kernel(xa_ref, xb_ref, odaa_ref, odab_ref, ofa_ref, ofb_ref, oba_ref, obb_ref, rg_ref, gg_ref,
                    mod_ref, subln_ref, retn_ref, hgn_ref, g64_ref, wout_ref, nffn_ref, wrh_ref, wrl_ref, rb_ref,
                    xmid_ref, hn2_ref, gate_ref, *, n_ctx_tiles):
    g64 = g64_ref[...]
    is_ctx = pl.program_id(0) < n_ctx_tiles
    pick = lambda a_ref, b_ref: jnp.where(is_ctx, a_ref[...], b_ref[...])

    def gnorm(t, w):
        return t * lax.rsqrt(_group_mean_sq(t, g64) + EPS) * w

    o_da = gnorm(pick(odaa_ref, odab_ref), subln_ref[...])
    o_rec = pick(ofa_ref, ofb_ref) + pick(oba_ref, obb_ref)
    o_ret = gnorm(o_rec[:, 0:D_REC], retn_ref[...]) * _silu(rg_ref[...])
    o_hg = gnorm(o_rec[:, D_REC:2 * D_REC], hgn_ref[...]) * _silu(gg_ref[...])
    mixed = jnp.concatenate([o_da, o_ret, o_hg], axis=1).astype(BF16)
    mod = mod_ref[0]
    x = pick(xa_ref, xb_ref) + mod[:, 2 * D_MODEL:3 * D_MODEL] * _dot(mixed, wout_ref[...])
    xmid_ref[...] = x
    y = x * lax.rsqrt(jnp.mean(x * x, axis=-1, keepdims=True) + EPS) * nffn_ref[...]
    hn = y * (1.0 + mod[:, 4 * D_MODEL:5 * D_MODEL]) + mod[:, 3 * D_MODEL:4 * D_MODEL]
    hi, lo = _split_bf16(hn)
    hn2_ref[...] = hi
    wrh = wrh_ref[...]
    logits = (lax.dot_general(wrh, hi, _NT, preferred_element_type=F32)
              + lax.dot_general(wrh, lo, _NT, preferred_element_type=F32)
              + lax.dot_general(wrl_ref[...], hi, _NT, preferred_element_type=F32))
    gate_ref[...] = _route(logits, rb_ref[...])


def _outproj(x, oda, o_f, o_b, rest, mod, subln, retn, hgn, g64, w_out16, norm_w, wr_hi, wr_lo, rbias, st):
    tm = st.tile
    t = st.n_tiles * tm
    row = lambda i: (i, 0)
    const = lambda i: (0, 0)
    pair = lambda width: [pl.BlockSpec((tm, width), lambda i: (st.ctx_idx(i), 0)),
                          pl.BlockSpec((tm, width), lambda i: (st.lat_idx(i), 0))]
    return pl.pallas_call(
        functools.partial(_outproj_kernel, n_ctx_tiles=st.n_ctx),
        grid=(st.n_tiles,),
        in_specs=pair(D_MODEL) + pair(D_V) + pair(2 * D_REC) + pair(2 * D_REC) + [
            pl.BlockSpec((tm, D_REC), lambda i: (i, 3)),
            pl.BlockSpec((tm, D_REC), lambda i: (i, 8)),
            pl.BlockSpec((1, 1, 6 * D_MODEL), lambda i: (st.mod_idx(i), 0, 0)),
            pl.BlockSpec((1, D_V), const),
            pl.BlockSpec((1, D_REC), const),
            pl.BlockSpec((1, D_REC), const),
            pl.BlockSpec((LANES, LANES), const),
            pl.BlockSpec((D_MODEL, D_MODEL), const, pipeline_mode=pl.Buffered(1)),
            pl.BlockSpec((1, D_MODEL), const),
            pl.BlockSpec((N_EXPERTS, D_MODEL), const),
            pl.BlockSpec((N_EXPERTS, D_MODEL), const),
            pl.BlockSpec((N_EXPERTS, 1), const)],
        out_specs=[pl.BlockSpec((tm, D_MODEL), row), pl.BlockSpec((tm, D_MODEL), row),
                   pl.BlockSpec((N_EXPERTS, tm), lambda i: (0, i))],
        out_shape=[jax.ShapeDtypeStruct((t, D_MODEL), F32), jax.ShapeDtypeStruct((t, D_MODEL), BF16),
                   jax.ShapeDtypeStruct((N_EXPERTS, t), F32)],
        compiler_params=_params("arbitrary"),
        name="outproj",
    )(*x, *oda, *o_f, *o_b, rest, rest, mod, subln, retn, hgn, g64, w_out16, norm_w, wr_hi, wr_lo, rbias)


def _moe_plan(gate_t, n_blocks):
    nb, tr = MOE_BLOCK, MOE_TILE
    i_max = n_blocks * MOE_ITEMS_PER_BLOCK
    k_max = MOE_ITEMS_PER_BLOCK + MOE_ITEMS_PER_BLOCK % 2
    member = (gate_t.reshape(N_GROUPS, EXPERTS_PER_GROUP, -1) > 0).any(axis=1)
    mb = member.reshape(N_GROUPS, n_blocks, nb)
    cnt = mb.sum(-1).astype(I32)
    rank = jnp.cumsum(mb, axis=-1, dtype=I32) - mb
    off = jnp.cumsum(cnt, axis=0) - cnt
    dest = jnp.where(mb, off[..., None] + rank, -1)
    first = off // tr
    n_it = jnp.where(cnt > 0, (off + cnt - 1) // tr - first + 1, 0)
    flat = n_it.reshape(-1)
    ends = jnp.cumsum(flat)
    starts = ends - flat
    total = ends[-1]
    i = jnp.arange(i_max, dtype=I32)
    ic = jnp.minimum(i, total - 1)
    j = (ends[None, :] <= ic[:, None]).sum(axis=1).astype(I32)
    item_g, item_b = j // n_blocks, j % n_blocks
    item_tile = first.reshape(-1)[j] + ic - starts[j]
    item_valid = (i < total).astype(I32)
    cum_g = jnp.cumsum(n_it, axis=0)
    tot_b = cum_g[-1]
    k = jnp.arange(k_max, dtype=I32)[None, :]
    kc = jnp.minimum(k, tot_b[:, None] - 1)
    gk = (kc[None] >= cum_g[:, :, None]).sum(0).astype(I32)
    bidx = jnp.arange(n_blocks, dtype=I32)[:, None]
    within = kc - (cum_g[gk, bidx] - n_it[gk, bidx])
    blk_item = (starts[gk * n_blocks + bidx] + within).astype(I32)
    blk_valid = (k < tot_b[:, None]).astype(I32)
    return dict(dest=dest.reshape(N_GROUPS * n_blocks, 1, nb), item_g=item_g, item_b=item_b,
                item_tile=item_tile.astype(I32), item_valid=item_valid,
                blk_item=blk_item.reshape(-1), blk_valid=blk_valid.reshape(-1), k_max=k_max, i_max=i_max)


def _moe_ffn_kernel(ig_ref, ib_ref, it_ref, iv_ref, h_ref, dest_ref, gate_ref, wg_ref, wu_ref, wd_ref,
                    y_ref, wg16, wu16, wd16):
    i = pl.program_id(0)
    g = ig_ref[i]
    prev_g = ig_ref[jnp.maximum(i - 1, 0)]

    @pl.when((i == 0) | (g != prev_g))
    def _():
        wg16[...] = wg_ref[...].astype(BF16)
        wu16[...] = wu_ref[...].astype(BF16)
        wd16[...] = wd_ref[...].astype(BF16)

    @pl.when(iv_ref[i] == 0)
    def _():
        y_ref[...] = jnp.zeros_like(y_ref)

    @pl.when(iv_ref[i] == 1)
    def _():
        rows = lax.broadcasted_iota(I32, (MOE_TILE, 1), 0) + it_ref[i] * MOE_TILE
        sort = jnp.where(dest_ref[0] == rows, 1.0, 0.0).astype(BF16)
        x = _dot(sort, h_ref[...]).astype(BF16)
        gates = functools.reduce(jnp.add, [_dot(sort, p) for p in _split_bf16(gate_ref[...])])
        lane = lax.broadcasted_iota(I32, (1, N_EXPERTS), 1)
        acc = jnp.zeros((MOE_TILE, D_MODEL), F32)
        for e in range(EXPERTS_PER_GROUP):
            ge = jnp.sum(jnp.where(lane == g * EXPERTS_PER_GROUP + e, gates, 0.0), axis=1, keepdims=True)
            act = _silu(_dot(x, wg16[e])) * _dot(x, wu16[e])
            acc = acc + _dot((act * ge).astype(BF16), wd16[e])
        y_ref[...] = acc.astype(BF16)


def _moe_unsort_kernel(bi_ref, bv_ref, ig_ref, it_ref, y0_ref, y1_ref, d0_ref, d1_ref, x_ref, mod_ref,
                       oa_ref, ob_ref, acc_ref, *, n_ctx_blocks, k_max):
    b = pl.program_id(0)
    m = pl.program_id(1)

    @pl.when(m == 0)
    def _():
        acc_ref[...] = jnp.zeros_like(acc_ref)

    rows = lax.broadcasted_iota(I32, (MOE_TILE, 1), 0)
    for half, (y_ref, d_ref) in enumerate(((y0_ref, d0_ref), (y1_ref, d1_ref))):
        k = b * k_max + 2 * m + half

        @pl.when(bv_ref[k] == 1)
        def _():
            sort = jnp.where(d_ref[0] == rows + it_ref[bi_ref[k]] * MOE_TILE, 1.0, 0.0).astype(BF16)
            acc_ref[...] += lax.dot_general(sort, y_ref[...], _TN, preferred_element_type=F32)

    last = m == pl.num_programs(1) - 1
    result = lambda: x_ref[...] + mod_ref[0][:, 5 * D_MODEL:6 * D_MODEL] * acc_ref[...]

    @pl.when(last & (b < n_ctx_blocks))
    def _():
        oa_ref[...] = result()

    @pl.when(last & (b >= n_ctx_blocks))
    def _():
        ob_ref[...] = result()


def _moe(hn2, gate_t, w_gate, w_up, w_down, x_mid, mod, st, layer):
    nb, tr = st.tile, MOE_TILE
    n_blocks = st.n_tiles
    plan = _moe_plan(gate_t, n_blocks)
    i_max, k_max = plan["i_max"], plan["k_max"]
    gate = gate_t.T

    wmap = lambda i, ig, ib, it, iv: (layer * N_GROUPS + ig[i], 0, 0)
    once = pl.Buffered(1)
    y = pl.pallas_call(
        _moe_ffn_kernel,
        grid_spec=pltpu.PrefetchScalarGridSpec(
            num_scalar_prefetch=4,
            grid=(i_max,),
            in_specs=[pl.BlockSpec((nb, D_MODEL), lambda i, ig, ib, it, iv: (ib[i], 0)),
                      pl.BlockSpec((1, 1, nb), lambda i, ig, ib, it, iv: (ig[i] * n_blocks + ib[i], 0, 0)),
                      pl.BlockSpec((nb, N_EXPERTS), lambda i, ig, ib, it, iv: (ib[i], 0)),
                      pl.BlockSpec((EXPERTS_PER_GROUP, D_MODEL, D_EXPERT), wmap, pipeline_mode=once),
                      pl.BlockSpec((EXPERTS_PER_GROUP, D_MODEL, D_EXPERT), wmap, pipeline_mode=once),
                      pl.BlockSpec((EXPERTS_PER_GROUP, D_EXPERT, D_MODEL), wmap, pipeline_mode=once)],
            out_specs=pl.BlockSpec((tr, D_MODEL), lambda i, ig, ib, it, iv: (i, 0)),
            scratch_shapes=[pltpu.VMEM((EXPERTS_PER_GROUP, D_MODEL, D_EXPERT), BF16),
                            pltpu.VMEM((EXPERTS_PER_GROUP, D_MODEL, D_EXPERT), BF16),
                            pltpu.VMEM((EXPERTS_PER_GROUP, D_EXPERT, D_MODEL), BF16)]),
        out_shape=jax.ShapeDtypeStruct((i_max * tr, D_MODEL), BF16),
        compiler_params=_params("arbitrary"),
        name="moe_ffn",
    )(plan["item_g"], plan["item_b"], plan["item_tile"], plan["item_valid"],
      hn2, plan["dest"], gate, w_gate, w_up, w_down)

    def ymap(half):
        return lambda b, m, bi, bv, ig, it: (bi[b * k_max + 2 * m + half], 0)

    def dmap(half):
        return lambda b, m, bi, bv, ig, it: (ig[bi[b * k_max + 2 * m + half]] * n_blocks + b, 0, 0)

    return pl.pallas_call(
        functools.partial(_moe_unsort_kernel, n_ctx_blocks=st.n_ctx, k_max=k_max),
        grid_spec=pltpu.PrefetchScalarGridSpec(
            num_scalar_prefetch=4,
            grid=(n_blocks, k_max // 2),
            in_specs=[pl.BlockSpec((tr, D_MODEL), ymap(0)), pl.BlockSpec((tr, D_MODEL), ymap(1)),
                      pl.BlockSpec((1, 1, nb), dmap(0)), pl.BlockSpec((1, 1, nb), dmap(1)),
                      pl.BlockSpec((nb, D_MODEL), lambda b, m, *_: (b, 0)),
                      pl.BlockSpec((1, 1, 6 * D_MODEL), lambda b, m, *_: (st.mod_idx(b), 0, 0))],
            out_specs=[pl.BlockSpec((nb, D_MODEL), lambda b, m, *_: (st.ctx_idx(b), 0)),
                       pl.BlockSpec((nb, D_MODEL), lambda b, m, *_: (st.lat_idx(b), 0))],
            scratch_shapes=[pltpu.VMEM((nb, D_MODEL), F32)]),
        out_shape=[jax.ShapeDtypeStruct((st.n_ctx * nb, D_MODEL), F32),
                   jax.ShapeDtypeStruct(((n_blocks - st.n_ctx) * nb, D_MODEL), F32)],
        compiler_params=_params("arbitrary", "arbitrary"),
        name="moe_unsort",
    )(plan["blk_item"], plan["blk_valid"], plan["item_g"], plan["item_tile"],
      y, y, plan["dest"], plan["dest"], x_mid, mod)


def _block_avg(group):
    i = np.arange(LANES)
    return jnp.asarray((i[:, None] // group == i[None, :] // group) / group, dtype=BF16)


def _rope_tables(n_tokens, n_identity):
    rows = n_tokens // GRID_W
    pos_r = jnp.repeat(jnp.arange(rows, dtype=F32), GRID_W)
    pos_c = jnp.tile(jnp.arange(GRID_W, dtype=F32), rows)
    n_freq = DA_QK // 4
    inv = ROPE_BASE ** (-jnp.arange(n_freq, dtype=F32) / n_freq)
    ang = jnp.concatenate([pos_r[:, None] * inv, pos_c[:, None] * inv], axis=-1)
    cos = jnp.repeat(jnp.cos(ang), 2, axis=-1)
    sin = jnp.repeat(jnp.sin(ang), 2, axis=-1) * jnp.tile(jnp.asarray([-1.0, 1.0], F32), DA_QK // 2)
    cos = jnp.concatenate([jnp.ones((n_identity, DA_QK), F32), cos], axis=0)
    sin = jnp.concatenate([jnp.zeros((n_identity, DA_QK), F32), sin], axis=0)
    return jnp.tile(cos, (1, 2 * H_DA)), jnp.tile(sin, (1, 2 * H_DA))


def kernel(x_prompt, x_sample, cache_k, cache_v, state_ret, state_hgrn, c, c_ctx, w_in, w_out, w_ada, b_ada,
           norm_mix, norm_ffn, da_qnorm, da_knorm, da_lambda, da_subln, ret_decay, ret_norm, hg_lb, hg_norm,
           w_router, router_bias, w_gate, w_up, w_down):
    n_ctx, l_ctx, _ = x_prompt.shape
    n_lat, l_lat, _ = x_sample.shape
    t_ctx = n_ctx * l_ctx
    past = cache_k.shape[2]
    assert l_ctx == TOKEN_TILE and l_lat % MOE_BLOCK == 0 and t_ctx % MOE_BLOCK == 0 and l_lat % GRID_W == 0

    ctx_row = n_lat
    n_cond = -(-(n_lat + 1) // 8) * 8
    cond = jnp.zeros((n_cond, D_MODEL), F32).at[:n_lat].set(c).at[ctx_row].set(c_ctx)
    mod_all = _ada(cond, w_ada, b_ada)

    st = _Stream(t_ctx, n_lat, l_lat, TOKEN_TILE, ctx_row)
    st_moe = _Stream(t_ctx, n_lat, l_lat, MOE_BLOCK, ctx_row)
    g32, g64 = _block_avg(DA_QK), _block_avg(DA_V)
    cos_t, sin_t = _rope_tables(l_lat, TOKEN_TILE)
    lv = jnp.asarray(_level_table(REC_CHUNK))
    p_lb = jax.nn.softmax(hg_lb.astype(F32), axis=0)
    lb_all = jnp.cumsum(p_lb, axis=0) - p_lb[0]
    wr_hi = w_router.T.astype(BF16)
    wr_lo = (w_router.T - wr_hi.astype(F32)).astype(BF16)
    rbias = router_bias.astype(F32).reshape(N_EXPERTS, 1)
    flat_w = lambda w: w.reshape(DEPTH * N_EXPERTS, w.shape[2], w.shape[3])
    w_gate, w_up, w_down = flat_w(w_gate), flat_w(w_up), flat_w(w_down)
    flat_s = lambda s: s.astype(F32).reshape(n_lat, DEPTH, 2, D_REC, D_HEAD)
    states_in = (flat_s(state_ret), flat_s(state_hgrn))
    cache_k16 = cache_k.reshape(n_lat, DEPTH, past, D_QK).astype(BF16)
    cache_v16 = cache_v.reshape(n_lat, DEPTH, past, D_V).astype(BF16)

    x = (x_prompt.reshape(t_ctx, D_MODEL), x_sample.reshape(n_lat * l_lat, D_MODEL))
    caches, states_out = None, None
    for l in range(DEPTH):
        lam_init = 0.8 - 0.6 * math.exp(-0.3 * l)
        lp = da_lambda[l].astype(F32)
        lam = (jnp.exp(jnp.sum(lp[0] * lp[1])) - jnp.exp(jnp.sum(lp[2] * lp[3])) + lam_init).reshape(1)
        mod = mod_all[l].reshape(n_cond, 1, 6 * D_MODEL)
        qw = (jnp.tile(da_qnorm[l].astype(F32), 2 * H_DA) * (DA_QK ** -0.5)).reshape(1, D_QK)
        kw = jnp.tile(da_knorm[l].astype(F32), 2 * H_DA).reshape(1, D_QK)

        qb, kb, vb, rest, kc, vc = _inproj(x[0], x[1], mod, norm_mix[l].reshape(1, D_MODEL),
                                           w_in[l].astype(BF16), qw, kw, g32, cos_t, sin_t, st, l, n_ctx, caches)
        caches = (kc, vc)

        oda = (_attention(lam, qb, kb, vb, 0, n_ctx, l_ctx),
               _attention(lam, qb, kb, vb, t_ctx, n_lat, l_lat, l, cache=(cache_k16, cache_v16)))

        ret_tabs = _retention_tables(-jnp.exp(ret_decay[l].astype(F32)), REC_CHUNK)
        lb_row = lb_all[l].reshape(1, D_REC)
        ofc, obc, s_ret, s_hg = _recurrence(rest, ret_tabs, lb_row, lv, 0, n_ctx, l_ctx, l,
                                            want_states=True, states_out=states_out)
        states_out = (s_ret, s_hg)
        ofl, obl = _recurrence(rest, ret_tabs, lb_row, lv, t_ctx, n_lat, l_lat, l, states_in=states_in)

        subln = (jnp.tile(da_subln[l].astype(F32), H_DA) * (1.0 - lam_init)).reshape(1, D_V)
        retn = jnp.tile(ret_norm[l].astype(F32), H_RET).reshape(1, D_REC)
        hgn = jnp.tile(hg_norm[l].astype(F32), H_HG).reshape(1, D_REC)
        x_mid, hn2, gate_t = _outproj(x, oda, (ofc, ofl), (obc, obl), rest, mod, subln, retn, hgn, g64,
                                      w_out[l].astype(BF16), norm_ffn[l].reshape(1, D_MODEL), wr_hi, wr_lo,
                                      rbias, st)
        x = _moe(hn2, gate_t, w_gate, w_up, w_down, x_mid, mod, st_moe, l)

    kc, vc = caches
    s_ret, s_hg = states_out
    return (x[0].reshape(n_ctx, l_ctx, D_MODEL), x[1].reshape(n_lat, l_lat, D_MODEL),
            kc.reshape(n_ctx, DEPTH, l_ctx, H_DA, 2, DA_QK), vc.reshape(n_ctx, DEPTH, l_ctx, H_DA, DA_V),
            s_ret.reshape(n_ctx, DEPTH, 2, H_RET, RET_DK, RET_DV),
            s_hg.reshape(n_ctx, DEPTH, 2, H_HG, HG_DK, HG_DV))
```

```python
import functools
import math

import numpy as np
import jax
import jax.numpy as jnp
from jax import lax
from jax.experimental import pallas as pl
from jax.experimental.pallas import tpu as pltpu

F32 = jnp.float32
BF16 = jnp.bfloat16
I32 = jnp.int32

D_MODEL = 1024
DEPTH = 2
GRID_W = 64
EPS = 1e-6
MIN_GATE = 1e-30
ROPE_BASE = 10000.0
H_DA, DA_QK, DA_V = 8, 32, 64
H_RET, RET_DK, RET_DV = 4, 64, 64
H_HG, HG_DK, HG_DV = 4, 64, 64
D_QK = H_DA * 2 * DA_QK
D_V = H_DA * DA_V
N_REC_HEADS = 4
D_HEAD = 64
D_REC = N_REC_HEADS * D_HEAD
D_REST = 9 * D_REC
D_IN = 2 * D_QK + D_V + D_REST
N_EXPERTS, N_GROUPS, EXPERTS_PER_GROUP = 16, 4, 4
D_EXPERT = 512
MASKED_SCORE = -2.0

LANES = 128
TOKEN_TILE = 256
Q_TILE = 256
REC_CHUNK = 128
MOE_BLOCK = 1024
MOE_TILE = 128
MOE_SLOTS_PER_BLOCK = 2 * MOE_BLOCK // MOE_TILE + N_GROUPS
VMEM_LIMIT = 56 * 1024 * 1024

_NT = (((1,), (1,)), ((), ()))
_TN = (((0,), (0,)), ((), ()))


def _params(*sem):
    return pltpu.CompilerParams(dimension_semantics=sem, vmem_limit_bytes=VMEM_LIMIT)


def _dot(a, b):
    return jnp.dot(a, b, preferred_element_type=F32)


def _split_bf16(t, terms=2):
    out = []
    for _ in range(terms - 1):
        hi = t.astype(BF16)
        out.append(hi)
        t = t - hi.astype(F32)
    out.append(t.astype(BF16))
    return out


def _group_mean_sq(t, g):
    hi, lo = _split_bf16(t * t)
    outs = []
    for s in range(t.shape[1] // LANES):
        sl = slice(s * LANES, (s + 1) * LANES)
        outs.append(_dot(hi[:, sl], g) + _dot(lo[:, sl], g))
    return jnp.concatenate(outs, axis=1) if len(outs) > 1 else outs[0]


def _silu(t):
    return t * jax.nn.sigmoid(t)


class _Stream:
    def __init__(self, n_ctx_rows, n_lat_seq, lat_len, tile, ctx_row):
        self.tile = tile
        self.n_ctx = n_ctx_rows // tile
        self.per_seq = lat_len // tile
        self.n_tiles = self.n_ctx + n_lat_seq * self.per_seq
        self.ctx_row = ctx_row

    def ctx_idx(self, i):
        return jnp.minimum(i, self.n_ctx - 1)

    def lat_idx(self, i):
        return jnp.maximum(i - self.n_ctx, 0)

    def mod_idx(self, i):
        return jnp.where(i < self.n_ctx, self.ctx_row, (i - self.n_ctx) // self.per_seq)

    def pos_idx(self, i):
        return jnp.where(i < self.n_ctx, 0, 1 + (i - self.n_ctx) % self.per_seq)


def _ada_kernel(c_ref, w_ref, b_ref, o_ref):
    s = _silu(c_ref[...]).astype(BF16)
    o_ref[0] = _dot(s, w_ref[0].astype(BF16)) + b_ref[0]


def _ada(cond, w_ada, b_ada):
    rows = cond.shape[0]
    n_tile = 1536
    return pl.pallas_call(
        _ada_kernel,
        grid=(DEPTH, 6 * D_MODEL // n_tile),
        in_specs=[pl.BlockSpec((rows, D_MODEL), lambda l, j: (0, 0)),
                  pl.BlockSpec((1, D_MODEL, n_tile), lambda l, j: (l, 0, j)),
                  pl.BlockSpec((1, 1, n_tile), lambda l, j: (l, 0, j))],
        out_specs=pl.BlockSpec((1, rows, n_tile), lambda l, j: (l, 0, j)),
        out_shape=jax.ShapeDtypeStruct((DEPTH, rows, 6 * D_MODEL), F32),
        compiler_params=_params("arbitrary", "arbitrary"),
        name="ada",
    )(cond, w_ada, b_ada.reshape(DEPTH, 1, 6 * D_MODEL))


def _inproj_kernel(*refs, n_ctx_tiles, first):
    (xa_ref, xb_ref, mod_ref, nw_ref, w_ref, qw_ref, kw_ref, g32_ref, cos_ref, sin_ref) = refs[:10]
    q_ref, kb_ref, vb_ref, rest_ref, kc_ref, vc_ref = refs[10:] if first else refs[12:]
    is_ctx = pl.program_id(0) < n_ctx_tiles
    x = jnp.where(is_ctx, xa_ref[...], xb_ref[...])
    y = x * lax.rsqrt(jnp.mean(x * x, axis=-1, keepdims=True) + EPS) * nw_ref[...]
    mod = mod_ref[0]
    hn = (y * (1.0 + mod[:, D_MODEL:2 * D_MODEL]) + mod[:, 0:D_MODEL]).astype(BF16)
    g32 = g32_ref[...]
    lane = lax.broadcasted_iota(I32, (1, D_QK), 1)
    even = (lane & 1) == 0

    def qk_norm_rope(t, w):
        t = t * lax.rsqrt(_group_mean_sq(t, g32) + EPS) * w
        partner = jnp.where(even, pltpu.roll(t, D_QK - 1, 1), pltpu.roll(t, 1, 1))
        return t * cos_ref[...] + partner * sin_ref[...]

    q_ref[...] = qk_norm_rope(_dot(hn, w_ref[:, 0:D_QK]), qw_ref[...]).astype(BF16)
    k = qk_norm_rope(_dot(hn, w_ref[:, D_QK:2 * D_QK]), kw_ref[...])
    kb_ref[...] = k.astype(BF16)
    v = _dot(hn, w_ref[:, 2 * D_QK:2 * D_QK + D_V])
    vb_ref[...] = v.astype(BF16)
    rest_ref[...] = _dot(hn, w_ref[:, 2 * D_QK + D_V:D_IN])

    @pl.when(is_ctx)
    def _():
        kc_ref[0, 0] = k
        vc_ref[0, 0] = v
        if first:
            kc_ref[0, 1:] = jnp.zeros_like(kc_ref[0, 1:])
            vc_ref[0, 1:] = jnp.zeros_like(vc_ref[0, 1:])


def _inproj(xa, xb, mod, norm_w, w_in16, qw, kw, g32, cos_t, sin_t, st, layer, n_ctx_seq, caches):
    tm = st.tile
    t = st.n_tiles * tm
    first = caches is None
    row = lambda i: (i, 0)
    const = lambda i: (0, 0)
    in_specs = [pl.BlockSpec((tm, D_MODEL), lambda i: (st.ctx_idx(i), 0)),
                pl.BlockSpec((tm, D_MODEL), lambda i: (st.lat_idx(i), 0)),
                pl.BlockSpec((1, 1, 6 * D_MODEL), lambda i: (st.mod_idx(i), 0, 0)),
                pl.BlockSpec((1, D_MODEL), const),
                pl.BlockSpec((D_MODEL, D_IN), const, pipeline_mode=pl.Buffered(1)),
                pl.BlockSpec((1, D_QK), const),
                pl.BlockSpec((1, D_QK), const),
                pl.BlockSpec((LANES, LANES), const),
                pl.BlockSpec((tm, D_QK), lambda i: (st.pos_idx(i), 0)),
                pl.BlockSpec((tm, D_QK), lambda i: (st.pos_idx(i), 0))]
    args = [xa, xb, mod, norm_w, w_in16, qw, kw, g32, cos_t, sin_t]
    aliases = {}
    if first:
        cache_spec = pl.BlockSpec((1, DEPTH, tm, D_QK), lambda i: (st.ctx_idx(i), 0, 0, 0))
    else:
        cache_spec = pl.BlockSpec((1, 1, tm, D_QK), lambda i: (st.ctx_idx(i), layer, 0, 0))
        in_specs += [pl.BlockSpec(memory_space=pl.ANY)] * 2
        args += list(caches)
        aliases = {10: 4, 11: 5}
    cache_shape = jax.ShapeDtypeStruct((n_ctx_seq, DEPTH, tm, D_QK), F32)
    return pl.pallas_call(
        functools.partial(_inproj_kernel, n_ctx_tiles=st.n_ctx, first=first),
        grid=(st.n_tiles,),
        in_specs=in_specs,
        out_specs=[pl.BlockSpec((tm, D_QK), row), pl.BlockSpec((tm, D_QK), row),
                   pl.BlockSpec((tm, D_V), row), pl.BlockSpec((tm, D_REST), row), cache_spec, cache_spec],
        out_shape=[jax.ShapeDtypeStruct((t, D_QK), BF16), jax.ShapeDtypeStruct((t, D_QK), BF16),
                   jax.ShapeDtypeStruct((t, D_V), BF16), jax.ShapeDtypeStruct((t, D_REST), F32),
                   cache_shape, cache_shape],
        input_output_aliases=aliases,
        compiler_params=_params("arbitrary"),
        name="inproj",
    )(*args)


def _attn_kernel(*refs, with_cache):
    if with_cache:
        lam_ref, q_ref, k_ref, v_ref, ck_ref, cv_ref, o_ref = refs
    else:
        lam_ref, q_ref, k_ref, v_ref, o_ref = refs
    lam = lam_ref[0]
    q = q_ref[...]
    lane = lax.broadcasted_iota(I32, (1, LANES), 1)
    left = lane < DA_V
    one = jnp.ones((), BF16)
    zero = jnp.zeros((), BF16)
    segs = [(k_ref[...], v_ref[...])]
    if with_cache:
        segs.append((ck_ref[0, 0], cv_ref[0, 0]))

    acc = jnp.zeros(o_ref.shape, F32)
    for side in range(2):
        own = left if side == 0 else jnp.logical_not(left)
        vms = [jnp.where(own, v, one) for _, v in segs]
        for r in range(2):
            lo = side * DA_V + r * DA_QK
            qm = jnp.where((lane >= lo) & (lane < lo + DA_QK), q, zero)
            ss = [lax.dot_general(qm, k, _NT, preferred_element_type=F32) for k, _ in segs]
            m = functools.reduce(jnp.maximum, [jnp.max(s, axis=-1, keepdims=True) for s in ss])
            res = functools.reduce(
                jnp.add, [_dot(jnp.exp(s - m).astype(BF16), vm) for s, vm in zip(ss, vms)])
            den = pltpu.roll(res, DA_V, 1)
            coef = 1.0 if r == 0 else -lam
            acc = acc + jnp.where(own, coef * (res / den), 0.0)
    o_ref[...] = acc


def _attention(lam, qb, kb, vb, row0, n_seq, seq_len, layer=0, cache=None):
    tq = Q_TILE
    nq = seq_len // tq
    qb0 = row0 // tq
    sb0 = row0 // seq_len
    n_pair = D_QK // LANES
    qmap = lambda b, p, i: (qb0 + b * nq + i, p)
    kmap = lambda b, p, i: (sb0 + b, p)
    smem = pl.BlockSpec(memory_space=pltpu.SMEM)
    in_specs = [smem, pl.BlockSpec((tq, LANES), qmap), pl.BlockSpec((seq_len, LANES), kmap),
                pl.BlockSpec((seq_len, LANES), kmap)]
    args = [lam, qb, kb, vb]
    if cache is not None:
        ck, cv = cache
        cmap = lambda b, p, i: (b, layer, 0, p)
        in_specs += [pl.BlockSpec((1, 1, ck.shape[2], LANES), cmap),
                     pl.BlockSpec((1, 1, cv.shape[2], LANES), cmap)]
        args += [ck, cv]
    return pl.pallas_call(
        functools.partial(_attn_kernel, with_cache=cache is not None),
        grid=(n_seq, n_pair, nq),
        in_specs=in_specs,
        out_specs=pl.BlockSpec((tq, LANES), lambda b, p, i: (b * nq + i, p)),
        out_shape=jax.ShapeDtypeStruct((n_seq * seq_len, D_V), F32),
        compiler_params=_params("arbitrary", "arbitrary", "arbitrary"),
        name="attn_lat" if cache is not None else "attn_ctx",
    )(*args)


def _level_table(c):
    t = np.arange(c)[:, None] ^ np.arange(c)[None, :]
    lv = np.where(t == 0, 32, 31 - np.floor(np.log2(np.maximum(t, 1))).astype(np.int64))
    return np.tile(lv, (N_REC_HEADS, 1)).astype(np.int32)


def _retention_tables(log_gamma, c):
    g = jnp.repeat(log_gamma, D_HEAD, axis=1)
    t = jnp.arange(c, dtype=F32)
    diff = t[:, None] - t[None, :]
    gh = log_gamma[:, :, None, None]
    d_f = jnp.where(diff >= 0, jnp.exp(gh[0] * jnp.maximum(diff, 0.0)), 0.0)
    d_b = jnp.where(diff <= 0, jnp.exp(gh[1] * jnp.maximum(-diff, 0.0)), 0.0)
    d = jnp.stack([d_f, d_b]).reshape(2, N_REC_HEADS * c, c)
    q_f, k_f = jnp.exp(g[0] * (t[:, None] + 1.0)), jnp.exp(g[0] * (c - 1.0 - t[:, None]))
    q_b, k_b = jnp.exp(g[1] * (c - t[:, None])), jnp.exp(g[1] * t[:, None])
    f = jnp.stack([jnp.stack([q_f, k_f]), jnp.stack([q_b, k_b])])
    a = jnp.exp(g * c).reshape(2, 1, D_REC)
    return d, f, a


def _rec_kernel(*refs, state_in, state_out, aliased):
    refs = list(refs)
    rf_ref, rb_ref, rd_ref, rfac_ref, ra_ref, lb_ref, lv_ref = refs[:7]
    del refs[:7]
    if state_in:
        sret_ref, shg_ref, t4_ref = refs[:3]
        del refs[:3]
    if state_out:
        t4t_ref = refs.pop(0)
    del refs[:aliased]
    of_ref, ob_ref = refs[:2]
    del refs[:2]
    if state_out:
        oret_ref, ohg_ref = refs[:2]
        del refs[:2]
    (st_scr,) = refs

    ci = pl.program_id(1)
    c = rf_ref.shape[0]
    lane = lax.broadcasted_iota(I32, (1, D_REC), 1)
    head = lane >> 6
    row = lax.broadcasted_iota(I32, (c, 1), 0)
    srow = lax.broadcasted_iota(I32, (D_REC, 1), 0)
    same_head = (srow >> 6) == head
    eye = srow == lane
    head_is = [head == h for h in range(N_REC_HEADS)]
    head_m16 = [jnp.broadcast_to(jnp.where(m, 1.0, 0.0), (c, D_REC)).astype(BF16) for m in head_is]

    @pl.when(ci == 0)
    def _():
        if state_in:
            for d in range(2):
                for idx, ref in ((d, sret_ref), (2 + d, shg_ref)):
                    tiled = functools.reduce(jnp.add, [_dot(p, t4_ref[...]) for p in _split_bf16(ref[0, 0, d], 3)])
                    st_scr[idx] = jnp.where(same_head, tiled, 0.0)
        else:
            st_scr[...] = jnp.zeros_like(st_scr)

    def expand(t16):
        return jnp.concatenate([t16 * m for m in head_m16], axis=0)

    def nt(a, b):
        return lax.dot_general(a, b, _NT, preferred_element_type=F32)

    def finish(q, k, v16, scores, qfac, kfac, a_row, st_ref):
        st = st_ref[...]
        o = _dot((q * qfac).astype(BF16), st.astype(BF16))
        o_stack = _dot(scores.astype(BF16), v16)
        for h in range(N_REC_HEADS):
            o = o + jnp.where(head_is[h], o_stack[h * c:(h + 1) * c], 0.0)
        kv = lax.dot_general((k * kfac).astype(BF16), v16, _TN, preferred_element_type=F32)
        a_col = jnp.sum(jnp.where(eye, a_row, 0.0), axis=1, keepdims=True)
        st_ref[...] = st * a_col + jnp.where(same_head, kv, 0.0)
        return o

    def tree_scores(q, k, la, reverse):
        lv = lv_ref[...]
        scores = jnp.where(lv == 32, nt(expand(q.astype(BF16)), k.astype(BF16)), 0.0)
        pre, tot = la, la
        h, level = 1, 31
        while h < c:
            bit = (row & h) != 0
            query_side = jnp.logical_not(bit) if reverse else bit
            x = jnp.exp(jnp.minimum(jnp.where(query_side, pre, tot - pre), 0.0))
            qx = jnp.where(query_side, q * x, 0.0).astype(BF16)
            kx = jnp.where(query_side, 0.0, k * x).astype(BF16)
            scores = jnp.where(lv == level, nt(expand(qx), kx), scores)
            partner = jnp.where(bit, pltpu.roll(tot, h, 0), pltpu.roll(tot, c - h, 0))
            pre = pre + jnp.where(query_side, partner, 0.0)
            tot = tot + partner
            h, level = 2 * h, level - 1
        return scores, pre, tot

    lb = lb_ref[...]
    col = lambda ref, j: ref[:, j * D_REC:(j + 1) * D_REC]
    for d, (r_ref, o_ref) in enumerate(((rf_ref, of_ref), (rb_ref, ob_ref))):
        q, k, v16 = col(r_ref, 0), col(r_ref, 1) * (RET_DK ** -0.5), col(r_ref, 2).astype(BF16)
        scores = nt(expand(q.astype(BF16)), k.astype(BF16)) * rd_ref[d]
        o_ref[:, 0:D_REC] = finish(q, k, v16, scores, rfac_ref[d, 0], rfac_ref[d, 1], ra_ref[d], st_scr.at[d])
        z = col(r_ref, 5 + d)
        la = jnp.log(jnp.maximum(lb + (1.0 - lb) * jax.nn.sigmoid(z), MIN_GATE))
        q, k, v16 = _silu(col(r_ref, 4)), (1.0 - lb) * jax.nn.sigmoid(-z), col(r_ref, 7).astype(BF16)
        scores, pre, tot = tree_scores(q, k, la, reverse=d == 1)
        o_ref[:, D_REC:2 * D_REC] = finish(q, k, v16, scores, jnp.exp(jnp.minimum(pre, 0.0)),
                                           jnp.exp(jnp.minimum(tot - pre, 0.0)), jnp.exp(tot[0:1, :]),
                                           st_scr.at[2 + d])

    if state_out:
        @pl.when(ci == pl.num_programs(1) - 1)
        def _():
            for d in range(2):
                for idx, ref in ((d, oret_ref), (2 + d, ohg_ref)):
                    ref[0, 0, d] = functools.reduce(
                        jnp.add, [_dot(p, t4t_ref[...]) for p in _split_bf16(st_scr[idx], 3)])
            if state_out == "first":
                oret_ref[0, 1:] = jnp.zeros_like(oret_ref[0, 1:])
                ohg_ref[0, 1:] = jnp.zeros_like(ohg_ref[0, 1:])


def _recurrence(rest, ret_tabs, lb_row, lv, row0, n_seq, seq_len, layer, *, states_in=None, want_states=False,
                states_out=None):
    c = REC_CHUNK
    nc = seq_len // c
    cb0 = row0 // c
    const2 = lambda b, i: (0, 0)
    const3 = lambda b, i: (0, 0, 0)
    rd, rfac, ra = ret_tabs
    in_specs = [pl.BlockSpec((c, D_REST), lambda b, i: (cb0 + b * nc + i, 0)),
                pl.BlockSpec((c, D_REST), lambda b, i: (cb0 + b * nc + nc - 1 - i, 0)),
                pl.BlockSpec((2, N_REC_HEADS * c, c), const3),
                pl.BlockSpec((2, 2, c, D_REC), lambda b, i: (0, 0, 0, 0)),
                pl.BlockSpec((2, 1, D_REC), const3),
                pl.BlockSpec((1, D_REC), const2),
                pl.BlockSpec((N_REC_HEADS * c, c), const2)]
    args = [rest, rest, rd, rfac, ra, lb_row, lv]
    tile4 = jnp.tile(jnp.eye(D_HEAD, dtype=BF16), (1, N_REC_HEADS))
    if states_in is not None:
        blk = pl.BlockSpec((1, 1, 2, D_REC, D_HEAD), lambda b, i: (b, layer, 0, 0, 0))
        in_specs += [blk, blk, pl.BlockSpec((D_HEAD, D_REC), const2)]
        args += [states_in[0], states_in[1], tile4]
    if want_states:
        in_specs += [pl.BlockSpec((D_REC, D_HEAD), const2)]
        args += [tile4.T]
    out_specs = [pl.BlockSpec((c, 2 * D_REC), lambda b, i: (b * nc + i, 0)),
                 pl.BlockSpec((c, 2 * D_REC), lambda b, i: (b * nc + nc - 1 - i, 0))]
    out_shape = [jax.ShapeDtypeStruct((n_seq * seq_len, 2 * D_REC), F32)] * 2
    aliases = {}
    state_mode = None
    if want_states:
        st_shape = jax.ShapeDtypeStruct((n_seq, DEPTH, 2, D_REC, D_HEAD), F32)
        out_shape += [st_shape, st_shape]
        if states_out is None:
            state_mode = "first"
            out_specs += [pl.BlockSpec((1, DEPTH, 2, D_REC, D_HEAD), lambda b, i: (b, 0, 0, 0, 0))] * 2
        else:
            state_mode = "next"
            out_specs += [pl.BlockSpec((1, 1, 2, D_REC, D_HEAD), lambda b, i: (b, layer, 0, 0, 0))] * 2
            aliases = {len(args): 2, len(args) + 1: 3}
            in_specs += [pl.BlockSpec(memory_space=pl.ANY)] * 2
            args += list(states_out)
    return pl.pallas_call(
        functools.partial(_rec_kernel, state_in=states_in is not None, state_out=state_mode,
                          aliased=len(aliases)),
        grid=(n_seq, nc),
        in_specs=in_specs,
        out_specs=out_specs,
        out_shape=out_shape,
        scratch_shapes=[pltpu.VMEM((4, D_REC, D_REC), F32)],
        input_output_aliases=aliases,
        compiler_params=_params("arbitrary", "arbitrary"),
        name="rec_lat" if states_in is not None else "rec_ctx",
    )(*args)


def _route(logits, bias):
    scores = jax.nn.sigmoid(logits)
    sel = scores + bias
    srow = [scores[e:e + 1, :] for e in range(N_EXPERTS)]
    rows = [sel[e:e + 1, :] for e in range(N_EXPERTS)]
    gs = []
    for g in range(N_GROUPS):
        a, b, c, d = rows[4 * g:4 * g + 4]
        gs.append(functools.reduce(jnp.maximum, [a + b, a + c, a + d, b + c, b + d, c + d]))
    best = jnp.zeros_like(gs[0], dtype=I32)
    best_v = gs[0]
    for g in range(1, N_GROUPS):
        upd = gs[g] > best_v
        best = jnp.where(upd, g, best)
        best_v = jnp.where(upd, gs[g], best_v)
    masked = [jnp.where(best == e // EXPERTS_PER_GROUP, rows[e], MASKED_SCORE) for e in range(N_EXPERTS)]
    i1 = jnp.zeros_like(best)
    v1 = masked[0]
    for e in range(1, N_EXPERTS):
        upd = masked[e] > v1
        i1 = jnp.where(upd, e, i1)
        v1 = jnp.where(upd, masked[e], v1)
    i2 = jnp.zeros_like(best)
    v2 = jnp.full_like(v1, -jnp.inf)
    for e in range(N_EXPERTS):
        upd = (masked[e] > v2) & (i1 != e)
        i2 = jnp.where(upd, e, i2)
        v2 = jnp.where(upd, masked[e], v2)
    w1 = functools.reduce(jnp.add, [jnp.where(i1 == e, srow[e], 0.0) for e in range(N_EXPERTS)])
    w2 = functools.reduce(jnp.add, [jnp.where(i2 == e, srow[e], 0.0) for e in range(N_EXPERTS)])
    tot = w1 + w2
    w1, w2 = w1 / tot, w2 / tot
    expert = lax.broadcasted_iota(I32, logits.shape, 0)
    return jnp.where(expert == i1, w1, 0.0) + jnp.where(expert == i2, w2, 0.0)


def _outproj_kernel(xa_ref, xb_ref, odaa_ref, odab_ref, ofa_ref, ofb_ref, oba_ref, obb_ref, rg_ref, gg_ref,
                    mod_ref, subln_ref, retn_ref, hgn_ref, g64_ref, wout_ref, nffn_ref, wrh_ref, wrl_ref, rb_ref,
                    xmid_ref, hn2_ref, gate_ref, *, n_ctx_tiles):
    g64 = g64_ref[...]
    is_ctx = pl.program_id(0) < n_ctx_tiles
    pick = lambda a_ref, b_ref: jnp.where(is_ctx, a_ref[...], b_ref[...])

    def gnorm(t, w):
        return t * lax.rsqrt(_group_mean_sq(t, g64) + EPS) * w

    o_da = gnorm(pick(odaa_ref, odab_ref), subln_ref[...])
    o_rec = pick(ofa_ref, ofb_ref) + pick(oba_ref, obb_ref)
    o_ret = gnorm(o_rec[:, 0:D_REC], retn_ref[...]) * _silu(rg_ref[...])
    o_hg = gnorm(o_rec[:, D_REC:2 * D_REC], hgn_ref[...]) * _silu(gg_ref[...])
    mixed = jnp.concatenate([o_da, o_ret, o_hg], axis=1).astype(BF16)
    mod = mod_ref[0]
    x = pick(xa_ref, xb_ref) + mod[:, 2 * D_MODEL:3 * D_MODEL] * _dot(mixed, wout_ref[...])
    xmid_ref[...] = x
    y = x * lax.rsqrt(jnp.mean(x * x, axis=-1, keepdims=True) + EPS) * nffn_ref[...]
    hn = y * (1.0 + mod[:, 4 * D_MODEL:5 * D_MODEL]) + mod[:, 3 * D_MODEL:4 * D_MODEL]
    hi, lo = _split_bf16(hn)
    hn2_ref[...] = hi
    wrh = wrh_ref[...]
    logits = (lax.dot_general(wrh, hi, _NT, preferred_element_type=F32)
              + lax.dot_general(wrh, lo, _NT, preferred_element_type=F32)
              + lax.dot_general(wrl_ref[...], hi, _NT, preferred_element_type=F32))
    gate_ref[...] = _route(logits, rb_ref[...])


def _outproj(x, oda, o_f, o_b, rest, mod, subln, retn, hgn, g64, w_out16, norm_w, wr_hi, wr_lo, rbias, st):
    tm = st.tile
    t = st.n_tiles * tm
    row = lambda i: (i, 0)
    const = lambda i: (0, 0)
    pair = lambda width: [pl.BlockSpec((tm, width), lambda i: (st.ctx_idx(i), 0)),
                          pl.BlockSpec((tm, width), lambda i: (st.lat_idx(i), 0))]
    return pl.pallas_call(
        functools.partial(_outproj_kernel, n_ctx_tiles=st.n_ctx),
        grid=(st.n_tiles,),
        in_specs=pair(D_MODEL) + pair(D_V) + pair(2 * D_REC) + pair(2 * D_REC) + [
            pl.BlockSpec((tm, D_REC), lambda i: (i, 3)),
            pl.BlockSpec((tm, D_REC), lambda i: (i, 8)),
            pl.BlockSpec((1, 1, 6 * D_MODEL), lambda i: (st.mod_idx(i), 0, 0)),
            pl.BlockSpec((1, D_V), const),
            pl.BlockSpec((1, D_REC), const),
            pl.BlockSpec((1, D_REC), const),
            pl.BlockSpec((LANES, LANES), const),
            pl.BlockSpec((D_MODEL, D_MODEL), const, pipeline_mode=pl.Buffered(1)),
            pl.BlockSpec((1, D_MODEL), const),
            pl.BlockSpec((N_EXPERTS, D_MODEL), const),
            pl.BlockSpec((N_EXPERTS, D_MODEL), const),
            pl.BlockSpec((N_EXPERTS, 1), const)],
        out_specs=[pl.BlockSpec((tm, D_MODEL), row), pl.BlockSpec((tm, D_MODEL), row),
                   pl.BlockSpec((N_EXPERTS, tm), lambda i: (0, i))],
        out_shape=[jax.ShapeDtypeStruct((t, D_MODEL), F32), jax.ShapeDtypeStruct((t, D_MODEL), BF16),
                   jax.ShapeDtypeStruct((N_EXPERTS, t), F32)],
        compiler_params=_params("arbitrary"),
        name="outproj",
    )(*x, *oda, *o_f, *o_b, rest, rest, mod, subln, retn, hgn, g64, w_out16, norm_w, wr_hi, wr_lo, rbias)


def _moe_plan(gate_t, n_blocks):
    nb, tr, spb = MOE_BLOCK, MOE_TILE, MOE_SLOTS_PER_BLOCK
    n_slots = n_blocks * spb
    member = (gate_t.reshape(N_GROUPS, EXPERTS_PER_GROUP, -1) > 0).any(axis=1)
    mb = member.reshape(N_GROUPS, n_blocks, nb)
    mi = mb.astype(I32)
    rank = jnp.cumsum(mi, axis=-1) - mi
    tiles = (mi.sum(-1) + tr - 1) // tr
    t_end = jnp.cumsum(tiles, axis=0)
    t_off = t_end - tiles
    n_tiles = t_end[-1]
    dest = jnp.where(mb, t_off[..., None] * tr + rank, -1)
    order = jnp.cumsum(mi, axis=0) - mi
    n_mem = mi.sum(0)
    row_of = lambda k: jnp.where(n_mem > k, jnp.where(mb & (order == k), dest, 0).sum(0), -1)
    s = jnp.arange(n_slots, dtype=I32)
    s_blk, s_tile = s // spb, s % spb
    used = s_tile < n_tiles[s_blk]
    s_grp = (s_tile[None, :] >= t_end[:, s_blk]).sum(0).astype(I32)
    key = jnp.where(used, s_grp, N_GROUPS) * n_slots + s
    pos = (key[None, :] < key[:, None]).sum(1).astype(I32)
    slot_of_item = jnp.where(pos[None, :] == s[:, None], s[None, :], 0).sum(1).astype(I32)
    n_valid = used.sum().astype(I32)
    src = slot_of_item[jnp.minimum(s, n_valid - 1)]
    return dict(dest=dest.reshape(N_GROUPS * n_blocks, 1, nb),
                row1=row_of(0).reshape(n_blocks, 1, nb), row2=row_of(1).reshape(n_blocks, 1, nb),
                n_tiles=n_tiles.astype(I32), item_g=s_grp[src], item_b=s_blk[src], item_tile=s_tile[src],
                item_valid=(s < n_valid).astype(I32), item_slot=slot_of_item)


def _moe_ffn_kernel(ig_ref, ib_ref, it_ref, iv_ref, is_ref, h_ref, dest_ref, gate_ref, wg_ref, wu_ref, wd_ref,
                    y_ref, wg16, wu16, wd16):
    i = pl.program_id(0)
    g = ig_ref[i]
    prev_g = ig_ref[jnp.maximum(i - 1, 0)]

    @pl.when((i == 0) | (g != prev_g))
    def _():
        wg16[...] = wg_ref[...].astype(BF16)
        wu16[...] = wu_ref[...].astype(BF16)
        wd16[...] = wd_ref[...].astype(BF16)

    @pl.when(iv_ref[i] == 0)
    def _():
        y_ref[...] = jnp.zeros_like(y_ref)

    @pl.when(iv_ref[i] == 1)
    def _():
        rows = lax.broadcasted_iota(I32, (MOE_TILE, 1), 0) + it_ref[i] * MOE_TILE
        sort = jnp.where(dest_ref[0] == rows, 1.0, 0.0).astype(BF16)
        x = _dot(sort, h_ref[...]).astype(BF16)
        gates = functools.reduce(jnp.add, [_dot(sort, p) for p in _split_bf16(gate_ref[...])])
        lane = lax.broadcasted_iota(I32, (1, N_EXPERTS), 1)
        acc = jnp.zeros((MOE_TILE, D_MODEL), F32)
        for e in range(EXPERTS_PER_GROUP):
            ge = jnp.sum(jnp.where(lane == g * EXPERTS_PER_GROUP + e, gates, 0.0), axis=1, keepdims=True)
            act = _silu(_dot(x, wg16[e])) * _dot(x, wu16[e])
            acc = acc + _dot((act * ge).astype(BF16), wd16[e])
        y_ref[...] = acc.astype(BF16)


def _moe_unsort_kernel(nt_ref, y_ref, r1_ref, r2_ref, x_ref, mod_ref, oa_ref, ob_ref, acc_ref, *, n_ctx_blocks):
    b = pl.program_id(0)
    acc_ref[...] = jnp.zeros_like(acc_ref)
    chunk = 2 * MOE_TILE
    row1, row2 = r1_ref[0], r2_ref[0]

    def body(ci, carry):
        r0 = pl.multiple_of(ci * chunk, chunk)
        rows = lax.broadcasted_iota(I32, (chunk, 1), 0) + r0
        sort = jnp.where(row1 == rows, 1.0, jnp.where(row2 == rows, 1.0, 0.0)).astype(BF16)
        acc_ref[...] += lax.dot_general(sort, y_ref[pl.ds(r0, chunk), :], _TN, preferred_element_type=F32)
        return carry

    lax.fori_loop(0, (nt_ref[b] + 1) // 2, body, 0)
    result = lambda: x_ref[...] + mod_ref[0][:, 5 * D_MODEL:6 * D_MODEL] * acc_ref[...]

    @pl.when(b < n_ctx_blocks)
    def _():
        oa_ref[...] = result()

    @pl.when(b >= n_ctx_blocks)
    def _():
        ob_ref[...] = result()


def _moe(hn2, gate_t, w_gate, w_up, w_down, x_mid, mod, st, layer):
    nb, tr, spb = st.tile, MOE_TILE, MOE_SLOTS_PER_BLOCK
    n_blocks = st.n_tiles
    plan = _moe_plan(gate_t, n_blocks)
    gate = gate_t.T

    wmap = lambda i, ig, ib, it, iv, isl: (layer * N_GROUPS + ig[i], 0, 0)
    once = pl.Buffered(1)
    y = pl.pallas_call(
        _moe_ffn_kernel,
        grid_spec=pltpu.PrefetchScalarGridSpec(
            num_scalar_prefetch=5,
            grid=(n_blocks * spb,),
            in_specs=[pl.BlockSpec((nb, D_MODEL), lambda i, ig, ib, it, iv, isl: (ib[i], 0)),
                      pl.BlockSpec((1, 1, nb), lambda i, ig, ib, it, iv, isl: (ig[i] * n_blocks + ib[i], 0, 0)),
                      pl.BlockSpec((nb, N_EXPERTS), lambda i, ig, ib, it, iv, isl: (ib[i], 0)),
                      pl.BlockSpec((EXPERTS_PER_GROUP, D_MODEL, D_EXPERT), wmap, pipeline_mode=once),
                      pl.BlockSpec((EXPERTS_PER_GROUP, D_MODEL, D_EXPERT), wmap, pipeline_mode=once),
                      pl.BlockSpec((EXPERTS_PER_GROUP, D_EXPERT, D_MODEL), wmap, pipeline_mode=once)],
            out_specs=pl.BlockSpec((tr, D_MODEL), lambda i, ig, ib, it, iv, isl: (isl[i], 0)),
            scratch_shapes=[pltpu.VMEM((EXPERTS_PER_GROUP, D_MODEL, D_EXPERT), BF16),
                            pltpu.VMEM((EXPERTS_PER_GROUP, D_MODEL, D_EXPERT), BF16),
                            pltpu.VMEM((EXPERTS_PER_GROUP, D_EXPERT, D_MODEL), BF16)]),
        out_shape=jax.ShapeDtypeStruct((n_blocks * spb * tr, D_MODEL), BF16),
        compiler_params=_params("arbitrary"),
        name="moe_ffn",
    )(plan["item_g"], plan["item_b"], plan["item_tile"], plan["item_valid"], plan["item_slot"],
      hn2, plan["dest"], gate, w_gate, w_up, w_down)

    return pl.pallas_call(
        functools.partial(_moe_unsort_kernel, n_ctx_blocks=st.n_ctx),
        grid_spec=pltpu.PrefetchScalarGridSpec(
            num_scalar_prefetch=1,
            grid=(n_blocks,),
            in_specs=[pl.BlockSpec((spb * tr, D_MODEL), lambda b, nt: (b, 0)),
                      pl.BlockSpec((1, 1, nb), lambda b, nt: (b, 0, 0)),
                      pl.BlockSpec((1, 1, nb), lambda b, nt: (b, 0, 0)),
                      pl.BlockSpec((nb, D_MODEL), lambda b, nt: (b, 0)),
                      pl.BlockSpec((1, 1, 6 * D_MODEL), lambda b, nt: (st.mod_idx(b), 0, 0))],
            out_specs=[pl.BlockSpec((nb, D_MODEL), lambda b, nt: (st.ctx_idx(b), 0)),
                       pl.BlockSpec((nb, D_MODEL), lambda b, nt: (st.lat_idx(b), 0))],
            scratch_shapes=[pltpu.VMEM((nb, D_MODEL), F32)]),
        out_shape=[jax.ShapeDtypeStruct((st.n_ctx * nb, D_MODEL), F32),
                   jax.ShapeDtypeStruct(((n_blocks - st.n_ctx) * nb, D_MODEL), F32)],
        compiler_params=_params("arbitrary"),
        name="moe_unsort",
    )(plan["n_tiles"], y, plan["row1"], plan["row2"], x_mid, mod)


def _block_avg(group):
    i = np.arange(LANES)
    return jnp.asarray((i[:, None] // group == i[None, :] // group) / group, dtype=BF16)


def _rope_tables(n_tokens, n_identity):
    rows = n_tokens // GRID_W
    pos_r = jnp.repeat(jnp.arange(rows, dtype=F32), GRID_W)
    pos_c = jnp.tile(jnp.arange(GRID_W, dtype=F32), rows)
    n_freq = DA_QK // 4
    inv = ROPE_BASE ** (-jnp.arange(n_freq, dtype=F32) / n_freq)
    ang = jnp.concatenate([pos_r[:, None] * inv, pos_c[:, None] * inv], axis=-1)
    cos = jnp.repeat(jnp.cos(ang), 2, axis=-1)
    sin = jnp.repeat(jnp.sin(ang), 2, axis=-1) * jnp.tile(jnp.asarray([-1.0, 1.0], F32), DA_QK // 2)
    cos = jnp.concatenate([jnp.ones((n_identity, DA_QK), F32), cos], axis=0)
    sin = jnp.concatenate([jnp.zeros((n_identity, DA_QK), F32), sin], axis=0)
    return jnp.tile(cos, (1, 2 * H_DA)), jnp.tile(sin, (1, 2 * H_DA))


def kernel(x_prompt, x_sample, cache_k, cache_v, state_ret, state_hgrn, c, c_ctx, w_in, w_out, w_ada, b_ada,
           norm_mix, norm_ffn, da_qnorm, da_knorm, da_lambda, da_subln, ret_decay, ret_norm, hg_lb, hg_norm,
           w_router, router_bias, w_gate, w_up, w_down):
    n_ctx, l_ctx, _ = x_prompt.shape
    n_lat, l_lat, _ = x_sample.shape
    t_ctx = n_ctx * l_ctx
    past = cache_k.shape[2]
    assert l_ctx == TOKEN_TILE and l_lat % MOE_BLOCK == 0 and t_ctx % MOE_BLOCK == 0 and l_lat % GRID_W == 0

    ctx_row = n_lat
    n_cond = -(-(n_lat + 1) // 8) * 8
    cond = jnp.zeros((n_cond, D_MODEL), F32).at[:n_lat].set(c).at[ctx_row].set(c_ctx)
    mod_all = _ada(cond, w_ada, b_ada)

    st = _Stream(t_ctx, n_lat, l_lat, TOKEN_TILE, ctx_row)
    st_moe = _Stream(t_ctx, n_lat, l_lat, MOE_BLOCK, ctx_row)
    g32, g64 = _block_avg(DA_QK), _block_avg(DA_V)
    cos_t, sin_t = _rope_tables(l_lat, TOKEN_TILE)
    lv = jnp.asarray(_level_table(REC_CHUNK))
    p_lb = jax.nn.softmax(hg_lb.astype(F32), axis=0)
    lb_all = jnp.cumsum(p_lb, axis=0) - p_lb[0]
    wr_hi = w_router.T.astype(BF16)
    wr_lo = (w_router.T - wr_hi.astype(F32)).astype(BF16)
    rbias = router_bias.astype(F32).reshape(N_EXPERTS, 1)
    flat_w = lambda w: w.reshape(DEPTH * N_EXPERTS, w.shape[2], w.shape[3])
    w_gate, w_up, w_down = flat_w(w_gate), flat_w(w_up), flat_w(w_down)
    flat_s = lambda s: s.astype(F32).reshape(n_lat, DEPTH, 2, D_REC, D_HEAD)
    states_in = (flat_s(state_ret), flat_s(state_hgrn))
    cache_k16 = cache_k.reshape(n_lat, DEPTH, past, D_QK).astype(BF16)
    cache_v16 = cache_v.reshape(n_lat, DEPTH, past, D_V).astype(BF16)

    x = (x_prompt.reshape(t_ctx, D_MODEL), x_sample.reshape(n_lat * l_lat, D_MODEL))
    caches, states_out = None, None
    for l in range(DEPTH):
        lam_init = 0.8 - 0.6 * math.exp(-0.3 * l)
        lp = da_lambda[l].astype(F32)
        lam = (jnp.exp(jnp.sum(lp[0] * lp[1])) - jnp.exp(jnp.sum(lp[2] * lp[3])) + lam_init).reshape(1)
        mod = mod_all[l].reshape(n_cond, 1, 6 * D_MODEL)
        qw = (jnp.tile(da_qnorm[l].astype(F32), 2 * H_DA) * (DA_QK ** -0.5)).reshape(1, D_QK)
        kw = jnp.tile(da_knorm[l].astype(F32), 2 * H_DA).reshape(1, D_QK)

        qb, kb, vb, rest, kc, vc = _inproj(x[0], x[1], mod, norm_mix[l].reshape(1, D_MODEL),
                                           w_in[l].astype(BF16), qw, kw, g32, cos_t, sin_t, st, l, n_ctx, caches)
        caches = (kc, vc)

        oda = (_attention(lam, qb, kb, vb, 0, n_ctx, l_ctx),
               _attention(lam, qb, kb, vb, t_ctx, n_lat, l_lat, l, cache=(cache_k16, cache_v16)))

        ret_tabs = _retention_tables(-jnp.exp(ret_decay[l].astype(F32)), REC_CHUNK)
        lb_row = lb_all[l].reshape(1, D_REC)
        ofc, obc, s_ret, s_hg = _recurrence(rest, ret_tabs, lb_row, lv, 0, n_ctx, l_ctx, l,
                                            want_states=True, states_out=states_out)
        states_out = (s_ret, s_hg)
        ofl, obl = _recurrence(rest, ret_tabs, lb_row, lv, t_ctx, n_lat, l_lat, l, states_in=states_in)

        subln = (jnp.tile(da_subln[l].astype(F32), H_DA) * (1.0 - lam_init)).reshape(1, D_V)
        retn = jnp.tile(ret_norm[l].astype(F32), H_RET).reshape(1, D_REC)
        hgn = jnp.tile(hg_norm[l].astype(F32), H_HG).reshape(1, D_REC)
        x_mid, hn2, gate_t = _outproj(x, oda, (ofc, ofl), (obc, obl), rest, mod, subln, retn, hgn, g64,
                                      w_out[l].astype(BF16), norm_ffn[l].reshape(1, D_MODEL), wr_hi, wr_lo,
                                      rbias, st)
        x = _moe(hn2, gate_t, w_gate, w_up, w_down, x_mid, mod, st_moe, l)

    kc, vc = caches
    s_ret, s_hg = states_out
    return (x[0].reshape(n_ctx, l_ctx, D_MODEL), x[1].reshape(n_lat, l_lat, D_MODEL),
            kc.reshape(n_ctx, DEPTH, l_ctx, H_DA, 2, DA_QK), vc.reshape(n_ctx, DEPTH, l_ctx, H_DA, DA_V),
            s_ret.reshape(n_ctx, DEPTH, 2, H_RET, RET_DK, RET_DV),
            s_hg.reshape(n_ctx, DEPTH, 2, H_HG, HG_DK, HG_DV))
```

```python
import functools
import math

import numpy as np
import jax
import jax.numpy as jnp
from jax import lax
from jax.experimental import pallas as pl
from jax.experimental.pallas import tpu as pltpu

F32 = jnp.float32
BF16 = jnp.bfloat16
I32 = jnp.int32

D_MODEL = 1024
DEPTH = 2
GRID_W = 64
EPS = 1e-6
MIN_GATE = 1e-30
ROPE_BASE = 10000.0
H_DA, DA_QK, DA_V = 8, 32, 64
H_RET, RET_DK, RET_DV = 4, 64, 64
H_HG, HG_DK, HG_DV = 4, 64, 64
D_QK = H_DA * 2 * DA_QK
D_V = H_DA * DA_V
N_REC_HEADS = 4
D_HEAD = 64
D_REC = N_REC_HEADS * D_HEAD
D_REST = 9 * D_REC
D_IN = 2 * D_QK + D_V + D_REST
N_EXPERTS, N_GROUPS, EXPERTS_PER_GROUP = 16, 4, 4
D_EXPERT = 512
MASKED_SCORE = -2.0

LANES = 128
TOKEN_TILE = 256
Q_TILE = 256
REC_CHUNK = 128
MOE_BLOCK = 1024
MOE_TILE = 256
MXU_DEPTH = 256
MOE_SLOTS_PER_BLOCK = 2 * MOE_BLOCK // MOE_TILE + N_GROUPS
VMEM_LIMIT = 56 * 1024 * 1024

_NT = (((1,), (1,)), ((), ()))
_TN = (((0,), (0,)), ((), ()))


def _params(*sem):
    return pltpu.CompilerParams(dimension_semantics=sem, vmem_limit_bytes=VMEM_LIMIT)


def _dot(a, b):
    return jnp.dot(a, b, preferred_element_type=F32)


def _split_bf16(t, terms=2):
    out = []
    for _ in range(terms - 1):
        hi = t.astype(BF16)
        out.append(hi)
        t = t - hi.astype(F32)
    out.append(t.astype(BF16))
    return out


def _group_mean_sq(t, g):
    hi, lo = _split_bf16(t * t)
    outs = []
    for s in range(t.shape[1] // LANES):
        sl = slice(s * LANES, (s + 1) * LANES)
        outs.append(_dot(hi[:, sl], g) + _dot(lo[:, sl], g))
    return jnp.concatenate(outs, axis=1) if len(outs) > 1 else outs[0]


def _silu(t):
    return t * jax.nn.sigmoid(t)


class _Stream:
    def __init__(self, n_ctx_rows, n_lat_seq, lat_len, tile, ctx_row):
        self.tile = tile
        self.n_ctx = n_ctx_rows // tile
        self.per_seq = lat_len // tile
        self.n_tiles = self.n_ctx + n_lat_seq * self.per_seq
        self.ctx_row = ctx_row

    def ctx_idx(self, i):
        return jnp.minimum(i, self.n_ctx - 1)

    def lat_idx(self, i):
        return jnp.maximum(i - self.n_ctx, 0)

    def mod_idx(self, i):
        return jnp.where(i < self.n_ctx, self.ctx_row, (i - self.n_ctx) // self.per_seq)

    def pos_idx(self, i):
        return jnp.where(i < self.n_ctx, 0, 1 + (i - self.n_ctx) % self.per_seq)


def _ada_kernel(c_ref, w_ref, b_ref, o_ref):
    s = _silu(c_ref[...]).astype(BF16)
    o_ref[0] = _dot(s, w_ref[0].astype(BF16)) + b_ref[0]


def _ada(cond, w_ada, b_ada):
    rows = cond.shape[0]
    n_tile = 1536
    return pl.pallas_call(
        _ada_kernel,
        grid=(DEPTH, 6 * D_MODEL // n_tile),
        in_specs=[pl.BlockSpec((rows, D_MODEL), lambda l, j: (0, 0)),
                  pl.BlockSpec((1, D_MODEL, n_tile), lambda l, j: (l, 0, j)),
                  pl.BlockSpec((1, 1, n_tile), lambda l, j: (l, 0, j))],
        out_specs=pl.BlockSpec((1, rows, n_tile), lambda l, j: (l, 0, j)),
        out_shape=jax.ShapeDtypeStruct((DEPTH, rows, 6 * D_MODEL), F32),
        compiler_params=_params("arbitrary", "arbitrary"),
        name="ada",
    )(cond, w_ada, b_ada.reshape(DEPTH, 1, 6 * D_MODEL))


def _inproj_kernel(*refs, n_ctx_tiles, first):
    (xa_ref, xb_ref, mod_ref, nw_ref, w_ref, qw_ref, kw_ref, g32_ref, cos_ref, sin_ref) = refs[:10]
    q_ref, kb_ref, vb_ref, rest_ref, kc_ref, vc_ref = refs[10:] if first else refs[12:]
    is_ctx = pl.program_id(0) < n_ctx_tiles
    x = jnp.where(is_ctx, xa_ref[...], xb_ref[...])
    y = x * lax.rsqrt(jnp.mean(x * x, axis=-1, keepdims=True) + EPS) * nw_ref[...]
    mod = mod_ref[0]
    hn = (y * (1.0 + mod[:, D_MODEL:2 * D_MODEL]) + mod[:, 0:D_MODEL]).astype(BF16)
    g32 = g32_ref[...]
    lane = lax.broadcasted_iota(I32, (1, D_QK), 1)
    even = (lane & 1) == 0

    def qk_norm_rope(t, w):
        t = t * lax.rsqrt(_group_mean_sq(t, g32) + EPS) * w
        partner = jnp.where(even, pltpu.roll(t, D_QK - 1, 1), pltpu.roll(t, 1, 1))
        return t * cos_ref[...] + partner * sin_ref[...]

    q_ref[...] = qk_norm_rope(_dot(hn, w_ref[:, 0:D_QK]), qw_ref[...]).astype(BF16)
    k = qk_norm_rope(_dot(hn, w_ref[:, D_QK:2 * D_QK]), kw_ref[...])
    kb_ref[...] = k.astype(BF16)
    v = _dot(hn, w_ref[:, 2 * D_QK:2 * D_QK + D_V])
    vb_ref[...] = v.astype(BF16)
    rest_ref[...] = _dot(hn, w_ref[:, 2 * D_QK + D_V:D_IN])

    @pl.when(is_ctx)
    def _():
        kc_ref[0, 0] = k
        vc_ref[0, 0] = v
        if first:
            kc_ref[0, 1:] = jnp.zeros_like(kc_ref[0, 1:])
            vc_ref[0, 1:] = jnp.zeros_like(vc_ref[0, 1:])


def _inproj(xa, xb, mod, norm_w, w_in16, qw, kw, g32, cos_t, sin_t, st, layer, n_ctx_seq, caches):
    tm = st.tile
    t = st.n_tiles * tm
    first = caches is None
    row = lambda i: (i, 0)
    const = lambda i: (0, 0)
    in_specs = [pl.BlockSpec((tm, D_MODEL), lambda i: (st.ctx_idx(i), 0)),
                pl.BlockSpec((tm, D_MODEL), lambda i: (st.lat_idx(i), 0)),
                pl.BlockSpec((1, 1, 6 * D_MODEL), lambda i: (st.mod_idx(i), 0, 0)),
                pl.BlockSpec((1, D_MODEL), const),
                pl.BlockSpec((D_MODEL, D_IN), const, pipeline_mode=pl.Buffered(1)),
                pl.BlockSpec((1, D_QK), const),
                pl.BlockSpec((1, D_QK), const),
                pl.BlockSpec((LANES, LANES), const),
                pl.BlockSpec((tm, D_QK), lambda i: (st.pos_idx(i), 0)),
                pl.BlockSpec((tm, D_QK), lambda i: (st.pos_idx(i), 0))]
    args = [xa, xb, mod, norm_w, w_in16, qw, kw, g32, cos_t, sin_t]
    aliases = {}
    if first:
        cache_spec = pl.BlockSpec((1, DEPTH, tm, D_QK), lambda i: (st.ctx_idx(i), 0, 0, 0))
    else:
        cache_spec = pl.BlockSpec((1, 1, tm, D_QK), lambda i: (st.ctx_idx(i), layer, 0, 0))
        in_specs += [pl.BlockSpec(memory_space=pl.ANY)] * 2
        args += list(caches)
        aliases = {10: 4, 11: 5}
    cache_shape = jax.ShapeDtypeStruct((n_ctx_seq, DEPTH, tm, D_QK), F32)
    return pl.pallas_call(
        functools.partial(_inproj_kernel, n_ctx_tiles=st.n_ctx, first=first),
        grid=(st.n_tiles,),
        in_specs=in_specs,
        out_specs=[pl.BlockSpec((tm, D_QK), row), pl.BlockSpec((tm, D_QK), row),
                   pl.BlockSpec((tm, D_V), row), pl.BlockSpec((tm, D_REST), row), cache_spec, cache_spec],
        out_shape=[jax.ShapeDtypeStruct((t, D_QK), BF16), jax.ShapeDtypeStruct((t, D_QK), BF16),
                   jax.ShapeDtypeStruct((t, D_V), BF16), jax.ShapeDtypeStruct((t, D_REST), F32),
                   cache_shape, cache_shape],
        input_output_aliases=aliases,
        compiler_params=_params("arbitrary"),
        name="inproj",
    )(*args)


def _attn_kernel(*refs, with_cache):
    if with_cache:
        lam_ref, q_ref, k_ref, v_ref, ck_ref, cv_ref, o_ref = refs
    else:
        lam_ref, q_ref, k_ref, v_ref, o_ref = refs
    lam = lam_ref[0]
    lane = lax.broadcasted_iota(I32, (1, LANES), 1)
    left = lane < DA_V
    one = jnp.ones((), BF16)
    zero = jnp.zeros((), BF16)
    segs = [(k_ref[...], v_ref[...])]
    if with_cache:
        segs.append((ck_ref[0, 0], cv_ref[0, 0]))
    owns = [left, jnp.logical_not(left)]
    vms = [[jnp.where(own, v, one) for _, v in segs] for own in owns]
    tq = min(Q_TILE, q_ref.shape[0])

    def q_block(i, carry):
        r0 = pl.multiple_of(i * tq, tq)
        q = q_ref[pl.ds(r0, tq), :]
        acc = jnp.zeros((tq, LANES), F32)
        for side in range(2):
            for r in range(2):
                lo = side * DA_V + r * DA_QK
                qm = jnp.where((lane >= lo) & (lane < lo + DA_QK), q, zero)
                ss = [lax.dot_general(qm, k, _NT, preferred_element_type=F32) for k, _ in segs]
                m = functools.reduce(jnp.maximum, [jnp.max(s, axis=-1, keepdims=True) for s in ss])
                res = functools.reduce(
                    jnp.add, [_dot(jnp.exp(s - m).astype(BF16), vm) for s, vm in zip(ss, vms[side])])
                den = pltpu.roll(res, DA_V, 1)
                coef = 1.0 if r == 0 else -lam
                acc = acc + jnp.where(owns[side], coef * (res / den), 0.0)
        o_ref[pl.ds(r0, tq), :] = acc
        return carry

    lax.fori_loop(0, q_ref.shape[0] // tq, q_block, 0)


def _attention(lam, qb, kb, vb, row0, n_seq, seq_len, layer=0, cache=None):
    sb0 = row0 // seq_len
    n_pair = D_QK // LANES
    smap = lambda b, p: (sb0 + b, p)
    smem = pl.BlockSpec(memory_space=pltpu.SMEM)
    in_specs = [smem] + [pl.BlockSpec((seq_len, LANES), smap)] * 3
    args = [lam, qb, kb, vb]
    if cache is not None:
        ck, cv = cache
        cmap = lambda b, p: (b, layer, 0, p)
        in_specs += [pl.BlockSpec((1, 1, ck.shape[2], LANES), cmap),
                     pl.BlockSpec((1, 1, cv.shape[2], LANES), cmap)]
        args += [ck, cv]
    return pl.pallas_call(
        functools.partial(_attn_kernel, with_cache=cache is not None),
        grid=(n_seq, n_pair),
        in_specs=in_specs,
        out_specs=pl.BlockSpec((seq_len, LANES), lambda b, p: (b, p)),
        out_shape=jax.ShapeDtypeStruct((n_seq * seq_len, D_V), F32),
        compiler_params=_params("arbitrary", "arbitrary"),
        name="attn_lat" if cache is not None else "attn_ctx",
    )(*args)


def _level_table(c):
    t = np.arange(c)[:, None] ^ np.arange(c)[None, :]
    lv = np.where(t == 0, 32, 31 - np.floor(np.log2(np.maximum(t, 1))).astype(np.int64))
    return np.tile(lv, (N_REC_HEADS, 1)).astype(np.int32)


def _retention_tables(log_gamma, c):
    g = jnp.repeat(log_gamma, D_HEAD, axis=1)
    t = jnp.arange(c, dtype=F32)
    diff = t[:, None] - t[None, :]
    gh = log_gamma[:, :, None, None]
    d_f = jnp.where(diff >= 0, jnp.exp(gh[0] * jnp.maximum(diff, 0.0)), 0.0)
    d_b = jnp.where(diff <= 0, jnp.exp(gh[1] * jnp.maximum(-diff, 0.0)), 0.0)
    d = jnp.stack([d_f, d_b]).reshape(2, N_REC_HEADS * c, c)
    q_f, k_f = jnp.exp(g[0] * (t[:, None] + 1.0)), jnp.exp(g[0] * (c - 1.0 - t[:, None]))
    q_b, k_b = jnp.exp(g[1] * (c - t[:, None])), jnp.exp(g[1] * t[:, None])
    f = jnp.stack([jnp.stack([q_f, k_f]), jnp.stack([q_b, k_b])])
    a = jnp.exp(g * c).reshape(2, 1, D_REC)
    return d, f, a


def _rec_kernel(*refs, state_in, state_out, aliased):
    refs = list(refs)
    rf_ref, rb_ref, rd_ref, rfac_ref, ra_ref, lb_ref, lv_ref = refs[:7]
    del refs[:7]
    if state_in:
        sret_ref, shg_ref, t4_ref = refs[:3]
        del refs[:3]
    if state_out:
        t4t_ref = refs.pop(0)
    del refs[:aliased]
    of_ref, ob_ref = refs[:2]
    del refs[:2]
    if state_out:
        oret_ref, ohg_ref = refs[:2]
        del refs[:2]
    (st_scr,) = refs

    ci = pl.program_id(1)
    c = rf_ref.shape[0]
    lane = lax.broadcasted_iota(I32, (1, D_REC), 1)
    head = lane >> 6
    row = lax.broadcasted_iota(I32, (c, 1), 0)
    srow = lax.broadcasted_iota(I32, (D_REC, 1), 0)
    same_head = (srow >> 6) == head
    eye = srow == lane
    head_is = [head == h for h in range(N_REC_HEADS)]
    head_m16 = [jnp.broadcast_to(jnp.where(m, 1.0, 0.0), (c, D_REC)).astype(BF16) for m in head_is]

    @pl.when(ci == 0)
    def _():
        if state_in:
            for d in range(2):
                for idx, ref in ((d, sret_ref), (2 + d, shg_ref)):
                    tiled = functools.reduce(jnp.add, [_dot(p, t4_ref[...]) for p in _split_bf16(ref[0, 0, d], 3)])
                    st_scr[idx] = jnp.where(same_head, tiled, 0.0)
        else:
            st_scr[...] = jnp.zeros_like(st_scr)

    def expand(t16):
        return jnp.concatenate([t16 * m for m in head_m16], axis=0)

    def nt(a, b):
        return lax.dot_general(a, b, _NT, preferred_element_type=F32)

    def finish(q, k, v16, scores, qfac, kfac, a_row, st_ref):
        st = st_ref[...]
        o = _dot((q * qfac).astype(BF16), st.astype(BF16))
        o_stack = _dot(scores.astype(BF16), v16)
        for h in range(N_REC_HEADS):
            o = o + jnp.where(head_is[h], o_stack[h * c:(h + 1) * c], 0.0)
        kv = lax.dot_general((k * kfac).astype(BF16), v16, _TN, preferred_element_type=F32)
        a_col = jnp.sum(jnp.where(eye, a_row, 0.0), axis=1, keepdims=True)
        st_ref[...] = st * a_col + jnp.where(same_head, kv, 0.0)
        return o

    def tree_scores(q, k, la, reverse):
        lv = lv_ref[...]
        scores = jnp.where(lv == 32, nt(expand(q.astype(BF16)), k.astype(BF16)), 0.0)
        pre, tot = la, la
        h, level = 1, 31
        while h < c:
            bit = (row & h) != 0
            query_side = jnp.logical_not(bit) if reverse else bit
            x = jnp.exp(jnp.minimum(jnp.where(query_side, pre, tot - pre), 0.0))
            qx = jnp.where(query_side, q * x, 0.0).astype(BF16)
            kx = jnp.where(query_side, 0.0, k * x).astype(BF16)
            scores = jnp.where(lv == level, nt(expand(qx), kx), scores)
            partner = jnp.where(bit, pltpu.roll(tot, h, 0), pltpu.roll(tot, c - h, 0))
            pre = pre + jnp.where(query_side, partner, 0.0)
            tot = tot + partner
            h, level = 2 * h, level - 1
        return scores, pre, tot

    lb = lb_ref[...]
    col = lambda ref, j: ref[:, j * D_REC:(j + 1) * D_REC]
    for d, (r_ref, o_ref) in enumerate(((rf_ref, of_ref), (rb_ref, ob_ref))):
        q, k, v16 = col(r_ref, 0), col(r_ref, 1) * (RET_DK ** -0.5), col(r_ref, 2).astype(BF16)
        scores = nt(expand(q.astype(BF16)), k.astype(BF16)) * rd_ref[d]
        o_ref[:, 0:D_REC] = finish(q, k, v16, scores, rfac_ref[d, 0], rfac_ref[d, 1], ra_ref[d], st_scr.at[d])
        z = col(r_ref, 5 + d)
        la = jnp.log(jnp.maximum(lb + (1.0 - lb) * jax.nn.sigmoid(z), MIN_GATE))
        q, k, v16 = _silu(col(r_ref, 4)), (1.0 - lb) * jax.nn.sigmoid(-z), col(r_ref, 7).astype(BF16)
        scores, pre, tot = tree_scores(q, k, la, reverse=d == 1)
        o_ref[:, D_REC:2 * D_REC] = finish(q, k, v16, scores, jnp.exp(jnp.minimum(pre, 0.0)),
                                           jnp.exp(jnp.minimum(tot - pre, 0.0)), jnp.exp(tot[0:1, :]),
                                           st_scr.at[2 + d])

    if state_out:
        @pl.when(ci == pl.num_programs(1) - 1)
        def _():
            for d in range(2):
                for idx, ref in ((d, oret_ref), (2 + d, ohg_ref)):
                    ref[0, 0, d] = functools.reduce(
                        jnp.add, [_dot(p, t4t_ref[...]) for p in _split_bf16(st_scr[idx], 3)])
            if state_out == "first":
                oret_ref[0, 1:] = jnp.zeros_like(oret_ref[0, 1:])
                ohg_ref[0, 1:] = jnp.zeros_like(ohg_ref[0, 1:])


def _recurrence(rest, ret_tabs, lb_row, lv, row0, n_seq, seq_len, layer, *, states_in=None, want_states=False,
                states_out=None):
    c = REC_CHUNK
    nc = seq_len // c
    cb0 = row0 // c
    const2 = lambda b, i: (0, 0)
    const3 = lambda b, i: (0, 0, 0)
    rd, rfac, ra = ret_tabs
    in_specs = [pl.BlockSpec((c, D_REST), lambda b, i: (cb0 + b * nc + i, 0)),
                pl.BlockSpec((c, D_REST), lambda b, i: (cb0 + b * nc + nc - 1 - i, 0)),
                pl.BlockSpec((2, N_REC_HEADS * c, c), const3),
                pl.BlockSpec((2, 2, c, D_REC), lambda b, i: (0, 0, 0, 0)),
                pl.BlockSpec((2, 1, D_REC), const3),
                pl.BlockSpec((1, D_REC), const2),
                pl.BlockSpec((N_REC_HEADS * c, c), const2)]
    args = [rest, rest, rd, rfac, ra, lb_row, lv]
    tile4 = jnp.tile(jnp.eye(D_HEAD, dtype=BF16), (1, N_REC_HEADS))
    if states_in is not None:
        blk = pl.BlockSpec((1, 1, 2, D_REC, D_HEAD), lambda b, i: (b, layer, 0, 0, 0))
        in_specs += [blk, blk, pl.BlockSpec((D_HEAD, D_REC), const2)]
        args += [states_in[0], states_in[1], tile4]
    if want_states:
        in_specs += [pl.BlockSpec((D_REC, D_HEAD), const2)]
        args += [tile4.T]
    out_specs = [pl.BlockSpec((c, 2 * D_REC), lambda b, i: (b * nc + i, 0)),
                 pl.BlockSpec((c, 2 * D_REC), lambda b, i: (b * nc + nc - 1 - i, 0))]
    out_shape = [jax.ShapeDtypeStruct((n_seq * seq_len, 2 * D_REC), F32)] * 2
    aliases = {}
    state_mode = None
    if want_states:
        st_shape = jax.ShapeDtypeStruct((n_seq, DEPTH, 2, D_REC, D_HEAD), F32)
        out_shape += [st_shape, st_shape]
        if states_out is None:
            state_mode = "first"
            out_specs += [pl.BlockSpec((1, DEPTH, 2, D_REC, D_HEAD), lambda b, i: (b, 0, 0, 0, 0))] * 2
        else:
            state_mode = "next"
            out_specs += [pl.BlockSpec((1, 1, 2, D_REC, D_HEAD), lambda b, i: (b, layer, 0, 0, 0))] * 2
            aliases = {len(args): 2, len(args) + 1: 3}
            in_specs += [pl.BlockSpec(memory_space=pl.ANY)] * 2
            args += list(states_out)
    return pl.pallas_call(
        functools.partial(_rec_kernel, state_in=states_in is not None, state_out=state_mode,
                          aliased=len(aliases)),
        grid=(n_seq, nc),
        in_specs=in_specs,
        out_specs=out_specs,
        out_shape=out_shape,
        scratch_shapes=[pltpu.VMEM((4, D_REC, D_REC), F32)],
        input_output_aliases=aliases,
        compiler_params=_params("arbitrary", "arbitrary"),
        name="rec_lat" if states_in is not None else "rec_ctx",
    )(*args)


def _route(logits, bias):
    scores = jax.nn.sigmoid(logits)
    sel = scores + bias
    srow = [scores[e:e + 1, :] for e in range(N_EXPERTS)]
    rows = [sel[e:e + 1, :] for e in range(N_EXPERTS)]
    gs = []
    for g in range(N_GROUPS):
        a, b, c, d = rows[4 * g:4 * g + 4]
        gs.append(functools.reduce(jnp.maximum, [a + b, a + c, a + d, b + c, b + d, c + d]))
    best = jnp.zeros_like(gs[0], dtype=I32)
    best_v = gs[0]
    for g in range(1, N_GROUPS):
        upd = gs[g] > best_v
        best = jnp.where(upd, g, best)
        best_v = jnp.where(upd, gs[g], best_v)
    masked = [jnp.where(best == e // EXPERTS_PER_GROUP, rows[e], MASKED_SCORE) for e in range(N_EXPERTS)]
    i1 = jnp.zeros_like(best)
    v1 = masked[0]
    for e in range(1, N_EXPERTS):
        upd = masked[e] > v1
        i1 = jnp.where(upd, e, i1)
        v1 = jnp.where(upd, masked[e], v1)
    i2 = jnp.zeros_like(best)
    v2 = jnp.full_like(v1, -jnp.inf)
    for e in range(N_EXPERTS):
        upd = (masked[e] > v2) & (i1 != e)
        i2 = jnp.where(upd, e, i2)
        v2 = jnp.where(upd, masked[e], v2)
    w1 = functools.reduce(jnp.add, [jnp.where(i1 == e, srow[e], 0.0) for e in range(N_EXPERTS)])
    w2 = functools.reduce(jnp.add, [jnp.where(i2 == e, srow[e], 0.0) for e in range(N_EXPERTS)])
    tot = w1 + w2
    w1, w2 = w1 / tot, w2 / tot
    expert = lax.broadcasted_iota(I32, logits.shape, 0)
    return jnp.where(expert == i1, w1, 0.0) + jnp.where(expert == i2, w2, 0.0)


def _outproj_kernel(xa_ref, xb_ref, odaa_ref, odab_ref, ofa_ref, ofb_ref, oba_ref, obb_ref, rg_ref, gg_ref,
                    mod_ref, subln_ref, retn_ref, hgn_ref, g64_ref, wout_ref, nffn_ref, wrh_ref, wrl_ref, rb_ref,
                    xmid_ref, hn2_ref, gate_ref, *, n_ctx_tiles):
    g64 = g64_ref[...]
    is_ctx = pl.program_id(0) < n_ctx_tiles
    pick = lambda a_ref, b_ref: jnp.where(is_ctx, a_ref[...], b_ref[...])

    def gnorm(t, w):
        return t * lax.rsqrt(_group_mean_sq(t, g64) + EPS) * w

    o_da = gnorm(pick(odaa_ref, odab_ref), subln_ref[...])
    o_rec = pick(ofa_ref, ofb_ref) + pick(oba_ref, obb_ref)
    o_ret = gnorm(o_rec[:, 0:D_REC], retn_ref[...]) * _silu(rg_ref[...])
    o_hg = gnorm(o_rec[:, D_REC:2 * D_REC], hgn_ref[...]) * _silu(gg_ref[...])
    mixed = jnp.concatenate([o_da, o_ret, o_hg], axis=1).astype(BF16)
    mod = mod_ref[0]
    x = pick(xa_ref, xb_ref) + mod[:, 2 * D_MODEL:3 * D_MODEL] * _dot(mixed, wout_ref[...])
    xmid_ref[...] = x
    y = x * lax.rsqrt(jnp.mean(x * x, axis=-1, keepdims=True) + EPS) * nffn_ref[...]
    hn = y * (1.0 + mod[:, 4 * D_MODEL:5 * D_MODEL]) + mod[:, 3 * D_MODEL:4 * D_MODEL]
    hi, lo = _split_bf16(hn)
    hn2_ref[...] = hi
    wrh = wrh_ref[...]
    logits = (lax.dot_general(wrh, hi, _NT, preferred_element_type=F32)
              + lax.dot_general(wrh, lo, _NT, preferred_element_type=F32)
              + lax.dot_general(wrl_ref[...], hi, _NT, preferred_element_type=F32))
    gate_ref[...] = _route(logits, rb_ref[...])


def _outproj(x, oda, o_f, o_b, rest, mod, subln, retn, hgn, g64, w_out16, norm_w, wr_hi, wr_lo, rbias, st):
    tm = st.tile
    t = st.n_tiles * tm
    row = lambda i: (i, 0)
    const = lambda i: (0, 0)
    pair = lambda width: [pl.BlockSpec((tm, width), lambda i: (st.ctx_idx(i), 0)),
                          pl.BlockSpec((tm, width), lambda i: (st.lat_idx(i), 0))]
    return pl.pallas_call(
        functools.partial(_outproj_kernel, n_ctx_tiles=st.n_ctx),
        grid=(st.n_tiles,),
        in_specs=pair(D_MODEL) + pair(D_V) + pair(2 * D_REC) + pair(2 * D_REC) + [
            pl.BlockSpec((tm, D_REC), lambda i: (i, 3)),
            pl.BlockSpec((tm, D_REC), lambda i: (i, 8)),
            pl.BlockSpec((1, 1, 6 * D_MODEL), lambda i: (st.mod_idx(i), 0, 0)),
            pl.BlockSpec((1, D_V), const),
            pl.BlockSpec((1, D_REC), const),
            pl.BlockSpec((1, D_REC), const),
            pl.BlockSpec((LANES, LANES), const),
            pl.BlockSpec((D_MODEL, D_MODEL), const, pipeline_mode=pl.Buffered(1)),
            pl.BlockSpec((1, D_MODEL), const),
            pl.BlockSpec((N_EXPERTS, D_MODEL), const),
            pl.BlockSpec((N_EXPERTS, D_MODEL), const),
            pl.BlockSpec((N_EXPERTS, 1), const)],
        out_specs=[pl.BlockSpec((tm, D_MODEL), row), pl.BlockSpec((tm, D_MODEL), row),
                   pl.BlockSpec((N_EXPERTS, tm), lambda i: (0, i))],
        out_shape=[jax.ShapeDtypeStruct((t, D_MODEL), F32), jax.ShapeDtypeStruct((t, D_MODEL), BF16),
                   jax.ShapeDtypeStruct((N_EXPERTS, t), F32)],
        compiler_params=_params("arbitrary"),
        name="outproj",
    )(*x, *oda, *o_f, *o_b, rest, rest, mod, subln, retn, hgn, g64, w_out16, norm_w, wr_hi, wr_lo, rbias)


def _moe_plan(gate_t, n_blocks):
    nb, tr, spb = MOE_BLOCK, MOE_TILE, MOE_SLOTS_PER_BLOCK
    n_slots = n_blocks * spb
    member = (gate_t.reshape(N_GROUPS, EXPERTS_PER_GROUP, -1) > 0).any(axis=1)
    mb = member.reshape(N_GROUPS, n_blocks, nb)
    mi = mb.astype(I32)
    rank = jnp.cumsum(mi, axis=-1) - mi
    tiles = (mi.sum(-1) + tr - 1) // tr
    t_end = jnp.cumsum(tiles, axis=0)
    t_off = t_end - tiles
    n_tiles = t_end[-1]
    dest = jnp.where(mb, t_off[..., None] * tr + rank, -1)
    order = jnp.cumsum(mi, axis=0) - mi
    n_mem = mi.sum(0)
    row_of = lambda k: jnp.where(n_mem > k, jnp.where(mb & (order == k), dest, 0).sum(0), -1)
    s = jnp.arange(n_slots, dtype=I32)
    s_blk, s_tile = s // spb, s % spb
    used = s_tile < n_tiles[s_blk]
    s_grp = (s_tile[None, :] >= t_end[:, s_blk]).sum(0).astype(I32)
    key = jnp.where(used, s_grp, N_GROUPS) * n_slots + s
    pos = (key[None, :] < key[:, None]).sum(1).astype(I32)
    slot_of_item = jnp.where(pos[None, :] == s[:, None], s[None, :], 0).sum(1).astype(I32)
    n_valid = used.sum().astype(I32)
    src = slot_of_item[jnp.minimum(s, n_valid - 1)]
    return dict(dest=dest.reshape(N_GROUPS * n_blocks, 1, nb),
                row1=row_of(0).reshape(n_blocks, 1, nb), row2=row_of(1).reshape(n_blocks, 1, nb),
                n_tiles=n_tiles.astype(I32), item_g=s_grp[src], item_b=s_blk[src], item_tile=s_tile[src],
                item_valid=(s < n_valid).astype(I32), item_slot=slot_of_item)


def _moe_ffn_kernel(ig_ref, ib_ref, it_ref, iv_ref, is_ref, h_ref, dest_ref, gate_ref, wg_ref, wu_ref, wd_ref,
                    y_ref, wg16, wu16, wd16):
    i = pl.program_id(0)
    g = ig_ref[i]
    prev_g = ig_ref[jnp.maximum(i - 1, 0)]

    @pl.when((i == 0) | (g != prev_g))
    def _():
        wg16[...] = wg_ref[...].astype(BF16)
        wu16[...] = wu_ref[...].astype(BF16)
        wd16[...] = wd_ref[...].astype(BF16)

    @pl.when(iv_ref[i] == 0)
    def _():
        y_ref[...] = jnp.zeros_like(y_ref)

    @pl.when(iv_ref[i] == 1)
    def _():
        rows = lax.broadcasted_iota(I32, (MOE_TILE, 1), 0) + it_ref[i] * MOE_TILE
        sort = jnp.where(dest_ref[0] == rows, 1.0, 0.0).astype(BF16)
        x = _dot(sort, h_ref[...]).astype(BF16)
        gates = functools.reduce(jnp.add, [lax.dot_general(sort, p, _NT, preferred_element_type=F32)
                                           for p in _split_bf16(gate_ref[...])])
        lane = lax.broadcasted_iota(I32, (1, N_EXPERTS), 1)
        acc = jnp.zeros((MOE_TILE, D_MODEL), F32)
        for e in range(EXPERTS_PER_GROUP):
            ge = jnp.sum(jnp.where(lane == g * EXPERTS_PER_GROUP + e, gates, 0.0), axis=1, keepdims=True)
            act = _silu(_dot(x, wg16[e])) * _dot(x, wu16[e])
            acc = acc + _dot((act * ge).astype(BF16), wd16[e])
        y_ref[...] = acc.astype(BF16)


def _moe_unsort_kernel(nt_ref, y_ref, r1_ref, r2_ref, x_ref, mod_ref, oa_ref, ob_ref, acc_ref, *, n_ctx_blocks):
    b = pl.program_id(0)
    acc_ref[...] = jnp.zeros_like(acc_ref)
    chunk = max(MOE_TILE, MXU_DEPTH)
    tiles_per_chunk = chunk // MOE_TILE
    row1, row2 = r1_ref[0], r2_ref[0]

    def body(ci, carry):
        r0 = pl.multiple_of(ci * chunk, chunk)
        rows = lax.broadcasted_iota(I32, (chunk, 1), 0) + r0
        sort = jnp.where(row1 == rows, 1.0, jnp.where(row2 == rows, 1.0, 0.0)).astype(BF16)
        acc_ref[...] += lax.dot_general(sort, y_ref[pl.ds(r0, chunk), :], _TN, preferred_element_type=F32)
        return carry

    lax.fori_loop(0, (nt_ref[b] + tiles_per_chunk - 1) // tiles_per_chunk, body, 0)
    result = lambda: x_ref[...] + mod_ref[0][:, 5 * D_MODEL:6 * D_MODEL] * acc_ref[...]

    @pl.when(b < n_ctx_blocks)
    def _():
        oa_ref[...] = result()

    @pl.when(b >= n_ctx_blocks)
    def _():
        ob_ref[...] = result()


def _moe(hn2, gate_t, w_gate, w_up, w_down, x_mid, mod, st, layer):
    nb, tr, spb = st.tile, MOE_TILE, MOE_SLOTS_PER_BLOCK
    n_blocks = st.n_tiles
    plan = _moe_plan(gate_t, n_blocks)

    wmap = lambda i, ig, ib, it, iv, isl: (layer * N_GROUPS + ig[i], 0, 0)
    once = pl.Buffered(1)
    y = pl.pallas_call(
        _moe_ffn_kernel,
        grid_spec=pltpu.PrefetchScalarGridSpec(
            num_scalar_prefetch=5,
            grid=(n_blocks * spb,),
            in_specs=[pl.BlockSpec((nb, D_MODEL), lambda i, ig, ib, it, iv, isl: (ib[i], 0)),
                      pl.BlockSpec((1, 1, nb), lambda i, ig, ib, it, iv, isl: (ig[i] * n_blocks + ib[i], 0, 0)),
                      pl.BlockSpec((N_EXPERTS, nb), lambda i, ig, ib, it, iv, isl: (0, ib[i])),
                      pl.BlockSpec((EXPERTS_PER_GROUP, D_MODEL, D_EXPERT), wmap, pipeline_mode=once),
                      pl.BlockSpec((EXPERTS_PER_GROUP, D_MODEL, D_EXPERT), wmap, pipeline_mode=once),
                      pl.BlockSpec((EXPERTS_PER_GROUP, D_EXPERT, D_MODEL), wmap, pipeline_mode=once)],
            out_specs=pl.BlockSpec((tr, D_MODEL), lambda i, ig, ib, it, iv, isl: (isl[i], 0)),
            scratch_shapes=[pltpu.VMEM((EXPERTS_PER_GROUP, D_MODEL, D_EXPERT), BF16),
                            pltpu.VMEM((EXPERTS_PER_GROUP, D_MODEL, D_EXPERT), BF16),
                            pltpu.VMEM((EXPERTS_PER_GROUP, D_EXPERT, D_MODEL), BF16)]),
        out_shape=jax.ShapeDtypeStruct((n_blocks * spb * tr, D_MODEL), BF16),
        compiler_params=_params("arbitrary"),
        name="moe_ffn",
    )(plan["item_g"], plan["item_b"], plan["item_tile"], plan["item_valid"], plan["item_slot"],
      hn2, plan["dest"], gate_t, w_gate, w_up, w_down)

    return pl.pallas_call(
        functools.partial(_moe_unsort_kernel, n_ctx_blocks=st.n_ctx),
        grid_spec=pltpu.PrefetchScalarGridSpec(
            num_scalar_prefetch=1,
            grid=(n_blocks,),
            in_specs=[pl.BlockSpec((spb * tr, D_MODEL), lambda b, nt: (b, 0)),
                      pl.BlockSpec((1, 1, nb), lambda b, nt: (b, 0, 0)),
                      pl.BlockSpec((1, 1, nb), lambda b, nt: (b, 0, 0)),
                      pl.BlockSpec((nb, D_MODEL), lambda b, nt: (b, 0)),
                      pl.BlockSpec((1, 1, 6 * D_MODEL), lambda b, nt: (st.mod_idx(b), 0, 0))],
            out_specs=[pl.BlockSpec((nb, D_MODEL), lambda b, nt: (st.ctx_idx(b), 0)),
                       pl.BlockSpec((nb, D_MODEL), lambda b, nt: (st.lat_idx(b), 0))],
            scratch_shapes=[pltpu.VMEM((nb, D_MODEL), F32)]),
        out_shape=[jax.ShapeDtypeStruct((st.n_ctx * nb, D_MODEL), F32),
                   jax.ShapeDtypeStruct(((n_blocks - st.n_ctx) * nb, D_MODEL), F32)],
        compiler_params=_params("arbitrary"),
        name="moe_unsort",
    )(plan["n_tiles"], y, plan["row1"], plan["row2"], x_mid, mod)


def _block_avg(group):
    i = np.arange(LANES)
    return jnp.asarray((i[:, None] // group == i[None, :] // group) / group, dtype=BF16)


def _rope_tables(n_tokens, n_identity):
    rows = n_tokens // GRID_W
    pos_r = jnp.repeat(jnp.arange(rows, dtype=F32), GRID_W)
    pos_c = jnp.tile(jnp.arange(GRID_W, dtype=F32), rows)
    n_freq = DA_QK // 4
    inv = ROPE_BASE ** (-jnp.arange(n_freq, dtype=F32) / n_freq)
    ang = jnp.concatenate([pos_r[:, None] * inv, pos_c[:, None] * inv], axis=-1)
    cos = jnp.repeat(jnp.cos(ang), 2, axis=-1)
    sin = jnp.repeat(jnp.sin(ang), 2, axis=-1) * jnp.tile(jnp.asarray([-1.0, 1.0], F32), DA_QK // 2)
    cos = jnp.concatenate([jnp.ones((n_identity, DA_QK), F32), cos], axis=0)
    sin = jnp.concatenate([jnp.zeros((n_identity, DA_QK), F32), sin], axis=0)
    return jnp.tile(cos, (1, 2 * H_DA)), jnp.tile(sin, (1, 2 * H_DA))


def kernel(x_prompt, x_sample, cache_k, cache_v, state_ret, state_hgrn, c, c_ctx, w_in, w_out, w_ada, b_ada,
           norm_mix, norm_ffn, da_qnorm, da_knorm, da_lambda, da_subln, ret_decay, ret_norm, hg_lb, hg_norm,
           w_router, router_bias, w_gate, w_up, w_down):
    n_ctx, l_ctx, _ = x_prompt.shape
    n_lat, l_lat, _ = x_sample.shape
    t_ctx = n_ctx * l_ctx
    past = cache_k.shape[2]
    assert l_ctx == TOKEN_TILE and l_lat % MOE_BLOCK == 0 and t_ctx % MOE_BLOCK == 0 and l_lat % GRID_W == 0

    ctx_row = n_lat
    n_cond = -(-(n_lat + 1) // 8) * 8
    cond = jnp.zeros((n_cond, D_MODEL), F32).at[:n_lat].set(c).at[ctx_row].set(c_ctx)
    mod_all = _ada(cond, w_ada, b_ada)

    st = _Stream(t_ctx, n_lat, l_lat, TOKEN_TILE, ctx_row)
    st_moe = _Stream(t_ctx, n_lat, l_lat, MOE_BLOCK, ctx_row)
    g32, g64 = _block_avg(DA_QK), _block_avg(DA_V)
    cos_t, sin_t = _rope_tables(l_lat, TOKEN_TILE)
    lv = jnp.asarray(_level_table(REC_CHUNK))
    p_lb = jax.nn.softmax(hg_lb.astype(F32), axis=0)
    lb_all = jnp.cumsum(p_lb, axis=0) - p_lb[0]
    wr_hi = w_router.T.astype(BF16)
    wr_lo = (w_router.T - wr_hi.astype(F32)).astype(BF16)
    rbias = router_bias.astype(F32).reshape(N_EXPERTS, 1)
    flat_w = lambda w: w.reshape(DEPTH * N_EXPERTS, w.shape[2], w.shape[3])
    w_gate, w_up, w_down = flat_w(w_gate), flat_w(w_up), flat_w(w_down)
    flat_s = lambda s: s.astype(F32).reshape(n_lat, DEPTH, 2, D_REC, D_HEAD)
    states_in = (flat_s(state_ret), flat_s(state_hgrn))
    cache_k16 = cache_k.reshape(n_lat, DEPTH, past, D_QK).astype(BF16)
    cache_v16 = cache_v.reshape(n_lat, DEPTH, past, D_V).astype(BF16)

    x = (x_prompt.reshape(t_ctx, D_MODEL), x_sample.reshape(n_lat * l_lat, D_MODEL))
    caches, states_out = None, None
    for l in range(DEPTH):
        lam_init = 0.8 - 0.6 * math.exp(-0.3 * l)
        lp = da_lambda[l].astype(F32)
        lam = (jnp.exp(jnp.sum(lp[0] * lp[1])) - jnp.exp(jnp.sum(lp[2] * lp[3])) + lam_init).reshape(1)
        mod = mod_all[l].reshape(n_cond, 1, 6 * D_MODEL)
        qw = (jnp.tile(da_qnorm[l].astype(F32), 2 * H_DA) * (DA_QK ** -0.5)).reshape(1, D_QK)
        kw = jnp.tile(da_knorm[l].astype(F32), 2 * H_DA).reshape(1, D_QK)

        qb, kb, vb, rest, kc, vc = _inproj(x[0], x[1], mod, norm_mix[l].reshape(1, D_MODEL),
                                           w_in[l].astype(BF16), qw, kw, g32, cos_t, sin_t, st, l, n_ctx, caches)
        caches = (kc, vc)

        oda = (_attention(lam, qb, kb, vb, 0, n_ctx, l_ctx),
               _attention(lam, qb, kb, vb, t_ctx, n_lat, l_lat, l, cache=(cache_k16, cache_v16)))

        ret_tabs = _retention_tables(-jnp.exp(ret_decay[l].astype(F32)), REC_CHUNK)
        lb_row = lb_all[l].reshape(1, D_REC)
        ofc, obc, s_ret, s_hg = _recurrence(rest, ret_tabs, lb_row, lv, 0, n_ctx, l_ctx, l,
                                            want_states=True, states_out=states_out)
        states_out = (s_ret, s_hg)
        ofl, obl = _recurrence(rest, ret_tabs, lb_row, lv, t_ctx, n_lat, l_lat, l, states_in=states_in)

        subln = (jnp.tile(da_subln[l].astype(F32), H_DA) * (1.0 - lam_init)).reshape(1, D_V)
        retn = jnp.tile(ret_norm[l].astype(F32), H_RET).reshape(1, D_REC)
        hgn = jnp.tile(hg_norm[l].astype(F32), H_HG).reshape(1, D_REC)
        x_mid, hn2, gate_t = _outproj(x, oda, (ofc, ofl), (obc, obl), rest, mod, subln, retn, hgn, g64,
                                      w_out[l].astype(BF16), norm_ffn[l].reshape(1, D_MODEL), wr_hi, wr_lo,
                                      rbias, st)
        x = _moe(hn2, gate_t, w_gate, w_up, w_down, x_mid, mod, st_moe, l)

    kc, vc = caches
    s_ret, s_hg = states_out
    return (x[0].reshape(n_ctx, l_ctx, D_MODEL), x[1].reshape(n_lat, l_lat, D_MODEL),
            kc.reshape(n_ctx, DEPTH, l_ctx, H_DA, 2, DA_QK), vc.reshape(n_ctx, DEPTH, l_ctx, H_DA, DA_V),
            s_ret.reshape(n_ctx, DEPTH, 2, H_RET, RET_DK, RET_DV),
            s_hg.reshape(n_ctx, DEPTH, 2, H_HG, HG_DK, HG_DV))
```

```python
import functools
import math

import numpy as np
import jax
import jax.numpy as jnp
from jax import lax
from jax.experimental import pallas as pl
from jax.experimental.pallas import tpu as pltpu

F32 = jnp.float32
BF16 = jnp.bfloat16
I32 = jnp.int32

D_MODEL = 1024
DEPTH = 2
GRID_W = 64
EPS = 1e-6
MIN_GATE = 1e-30
ROPE_BASE = 10000.0
H_DA, DA_QK, DA_V = 8, 32, 64
H_RET, RET_DK, RET_DV = 4, 64, 64
H_HG, HG_DK, HG_DV = 4, 64, 64
D_QK = H_DA * 2 * DA_QK
D_V = H_DA * DA_V
N_REC_HEADS = 4
D_HEAD = 64
D_REC = N_REC_HEADS * D_HEAD
D_REST = 9 * D_REC
D_IN = 2 * D_QK + D_V + D_REST
N_EXPERTS, N_GROUPS, EXPERTS_PER_GROUP = 16, 4, 4
D_EXPERT = 512
MASKED_SCORE = -2.0

LANES = 128
TOKEN_TILE = 256
Q_TILE = 1024
REC_CHUNK = 128
MOE_BLOCK = 1024
MOE_TILE = 256
MXU_DEPTH = 256
MOE_SLOTS_PER_BLOCK = 2 * MOE_BLOCK // MOE_TILE + N_GROUPS
VMEM_LIMIT = 56 * 1024 * 1024

_NT = (((1,), (1,)), ((), ()))
_TN = (((0,), (0,)), ((), ()))


def _params(*sem):
    return pltpu.CompilerParams(dimension_semantics=sem, vmem_limit_bytes=VMEM_LIMIT)


def _dot(a, b):
    return jnp.dot(a, b, preferred_element_type=F32)


def _split_bf16(t, terms=2):
    out = []
    for _ in range(terms - 1):
        hi = t.astype(BF16)
        out.append(hi)
        t = t - hi.astype(F32)
    out.append(t.astype(BF16))
    return out


def _group_mean_sq(t, g):
    hi, lo = _split_bf16(t * t)
    outs = []
    for s in range(t.shape[1] // LANES):
        sl = slice(s * LANES, (s + 1) * LANES)
        outs.append(_dot(hi[:, sl], g) + _dot(lo[:, sl], g))
    return jnp.concatenate(outs, axis=1) if len(outs) > 1 else outs[0]


def _silu(t):
    return t * jax.nn.sigmoid(t)


class _Stream:
    def __init__(self, n_ctx_rows, n_lat_seq, lat_len, tile, ctx_row):
        self.tile = tile
        self.n_ctx = n_ctx_rows // tile
        self.per_seq = lat_len // tile
        self.n_tiles = self.n_ctx + n_lat_seq * self.per_seq
        self.ctx_row = ctx_row

    def ctx_idx(self, i):
        return jnp.minimum(i, self.n_ctx - 1)

    def lat_idx(self, i):
        return jnp.maximum(i - self.n_ctx, 0)

    def mod_idx(self, i):
        return jnp.where(i < self.n_ctx, self.ctx_row, (i - self.n_ctx) // self.per_seq)

    def pos_idx(self, i):
        return jnp.where(i < self.n_ctx, 0, 1 + (i - self.n_ctx) % self.per_seq)


def _ada_kernel(c_ref, w_ref, b_ref, o_ref):
    s = _silu(c_ref[...]).astype(BF16)
    o_ref[0] = _dot(s, w_ref[0].astype(BF16)) + b_ref[0]


def _ada(cond, w_ada, b_ada):
    rows = cond.shape[0]
    n_tile = 1536
    return pl.pallas_call(
        _ada_kernel,
        grid=(DEPTH, 6 * D_MODEL // n_tile),
        in_specs=[pl.BlockSpec((rows, D_MODEL), lambda l, j: (0, 0)),
                  pl.BlockSpec((1, D_MODEL, n_tile), lambda l, j: (l, 0, j)),
                  pl.BlockSpec((1, 1, n_tile), lambda l, j: (l, 0, j))],
        out_specs=pl.BlockSpec((1, rows, n_tile), lambda l, j: (l, 0, j)),
        out_shape=jax.ShapeDtypeStruct((DEPTH, rows, 6 * D_MODEL), F32),
        compiler_params=_params("arbitrary", "arbitrary"),
        name="ada",
    )(cond, w_ada, b_ada.reshape(DEPTH, 1, 6 * D_MODEL))


def _inproj_kernel(*refs, n_ctx_tiles, first):
    (xa_ref, xb_ref, mod_ref, nw_ref, w_ref, qw_ref, kw_ref, g32_ref, cos_ref, sin_ref) = refs[:10]
    q_ref, kb_ref, vb_ref, rest_ref, kc_ref, vc_ref = refs[10:] if first else refs[12:]
    is_ctx = pl.program_id(0) < n_ctx_tiles
    x = jnp.where(is_ctx, xa_ref[...], xb_ref[...])
    y = x * lax.rsqrt(jnp.mean(x * x, axis=-1, keepdims=True) + EPS) * nw_ref[...]
    mod = mod_ref[0]
    hn = (y * (1.0 + mod[:, D_MODEL:2 * D_MODEL]) + mod[:, 0:D_MODEL]).astype(BF16)
    g32 = g32_ref[...]
    lane = lax.broadcasted_iota(I32, (1, D_QK), 1)
    even = (lane & 1) == 0

    def qk_norm_rope(t, w):
        t = t * lax.rsqrt(_group_mean_sq(t, g32) + EPS) * w
        partner = jnp.where(even, pltpu.roll(t, D_QK - 1, 1), pltpu.roll(t, 1, 1))
        return t * cos_ref[...] + partner * sin_ref[...]

    q_ref[...] = qk_norm_rope(_dot(hn, w_ref[:, 0:D_QK]), qw_ref[...]).astype(BF16)
    k = qk_norm_rope(_dot(hn, w_ref[:, D_QK:2 * D_QK]), kw_ref[...])
    kb_ref[...] = k.astype(BF16)
    v = _dot(hn, w_ref[:, 2 * D_QK:2 * D_QK + D_V])
    vb_ref[...] = v.astype(BF16)
    rest_ref[...] = _dot(hn, w_ref[:, 2 * D_QK + D_V:D_IN])

    @pl.when(is_ctx)
    def _():
        kc_ref[0, 0] = k
        vc_ref[0, 0] = v
        if first:
            kc_ref[0, 1:] = jnp.zeros_like(kc_ref[0, 1:])
            vc_ref[0, 1:] = jnp.zeros_like(vc_ref[0, 1:])


def _inproj(xa, xb, mod, norm_w, w_in16, qw, kw, g32, cos_t, sin_t, st, layer, n_ctx_seq, caches):
    tm = st.tile
    t = st.n_tiles * tm
    first = caches is None
    row = lambda i: (i, 0)
    const = lambda i: (0, 0)
    in_specs = [pl.BlockSpec((tm, D_MODEL), lambda i: (st.ctx_idx(i), 0)),
                pl.BlockSpec((tm, D_MODEL), lambda i: (st.lat_idx(i), 0)),
                pl.BlockSpec((1, 1, 6 * D_MODEL), lambda i: (st.mod_idx(i), 0, 0)),
                pl.BlockSpec((1, D_MODEL), const),
                pl.BlockSpec((D_MODEL, D_IN), const, pipeline_mode=pl.Buffered(1)),
                pl.BlockSpec((1, D_QK), const),
                pl.BlockSpec((1, D_QK), const),
                pl.BlockSpec((LANES, LANES), const),
                pl.BlockSpec((tm, D_QK), lambda i: (st.pos_idx(i), 0)),
                pl.BlockSpec((tm, D_QK), lambda i: (st.pos_idx(i), 0))]
    args = [xa, xb, mod, norm_w, w_in16, qw, kw, g32, cos_t, sin_t]
    aliases = {}
    if first:
        cache_spec = pl.BlockSpec((1, DEPTH, tm, D_QK), lambda i: (st.ctx_idx(i), 0, 0, 0))
    else:
        cache_spec = pl.BlockSpec((1, 1, tm, D_QK), lambda i: (st.ctx_idx(i), layer, 0, 0))
        in_specs += [pl.BlockSpec(memory_space=pl.ANY)] * 2
        args += list(caches)
        aliases = {10: 4, 11: 5}
    cache_shape = jax.ShapeDtypeStruct((n_ctx_seq, DEPTH, tm, D_QK), F32)
    return pl.pallas_call(
        functools.partial(_inproj_kernel, n_ctx_tiles=st.n_ctx, first=first),
        grid=(st.n_tiles,),
        in_specs=in_specs,
        out_specs=[pl.BlockSpec((tm, D_QK), row), pl.BlockSpec((tm, D_QK), row),
                   pl.BlockSpec((tm, D_V), row), pl.BlockSpec((tm, D_REST), row), cache_spec, cache_spec],
        out_shape=[jax.ShapeDtypeStruct((t, D_QK), BF16), jax.ShapeDtypeStruct((t, D_QK), BF16),
                   jax.ShapeDtypeStruct((t, D_V), BF16), jax.ShapeDtypeStruct((t, D_REST), F32),
                   cache_shape, cache_shape],
        input_output_aliases=aliases,
        compiler_params=_params("arbitrary"),
        name="inproj",
    )(*args)


def _attn_kernel(*refs, with_cache):
    if with_cache:
        lam_ref, q_ref, k_ref, v_ref, ck_ref, cv_ref, o_ref = refs
    else:
        lam_ref, q_ref, k_ref, v_ref, o_ref = refs
    lam = lam_ref[0]
    lane = lax.broadcasted_iota(I32, (1, LANES), 1)
    left = lane < DA_V
    one = jnp.ones((), BF16)
    zero = jnp.zeros((), BF16)
    segs = [(k_ref[...], v_ref[...])]
    if with_cache:
        segs.append((ck_ref[0, 0], cv_ref[0, 0]))
    owns = [left, jnp.logical_not(left)]
    vms = [[jnp.where(own, v, one) for _, v in segs] for own in owns]
    tq = min(Q_TILE, q_ref.shape[0])

    def q_block(i, carry):
        r0 = pl.multiple_of(i * tq, tq)
        q = q_ref[pl.ds(r0, tq), :]
        acc = jnp.zeros((tq, LANES), F32)
        for side in range(2):
            for r in range(2):
                lo = side * DA_V + r * DA_QK
                qm = jnp.where((lane >= lo) & (lane < lo + DA_QK), q, zero)
                ss = [lax.dot_general(qm, k, _NT, preferred_element_type=F32) for k, _ in segs]
                m = functools.reduce(jnp.maximum, [jnp.max(s, axis=-1, keepdims=True) for s in ss])
                res = functools.reduce(
                    jnp.add, [_dot(jnp.exp(s - m).astype(BF16), vm) for s, vm in zip(ss, vms[side])])
                den = pltpu.roll(res, DA_V, 1)
                coef = 1.0 if r == 0 else -lam
                acc = acc + jnp.where(owns[side], coef * (res / den), 0.0)
        o_ref[pl.ds(r0, tq), :] = acc
        return carry

    lax.fori_loop(0, q_ref.shape[0] // tq, q_block, 0)


def _attention(lam, qb, kb, vb, row0, n_seq, seq_len, layer=0, cache=None):
    sb0 = row0 // seq_len
    n_pair = D_QK // LANES
    smap = lambda b, p: (sb0 + b, p)
    smem = pl.BlockSpec(memory_space=pltpu.SMEM)
    in_specs = [smem] + [pl.BlockSpec((seq_len, LANES), smap)] * 3
    args = [lam, qb, kb, vb]
    if cache is not None:
        ck, cv = cache
        cmap = lambda b, p: (b, layer, 0, p)
        in_specs += [pl.BlockSpec((1, 1, ck.shape[2], LANES), cmap),
                     pl.BlockSpec((1, 1, cv.shape[2], LANES), cmap)]
        args += [ck, cv]
    return pl.pallas_call(
        functools.partial(_attn_kernel, with_cache=cache is not None),
        grid=(n_seq, n_pair),
        in_specs=in_specs,
        out_specs=pl.BlockSpec((seq_len, LANES), lambda b, p: (b, p)),
        out_shape=jax.ShapeDtypeStruct((n_seq * seq_len, D_V), F32),
        compiler_params=_params("arbitrary", "arbitrary"),
        name="attn_lat" if cache is not None else "attn_ctx",
    )(*args)


def _level_table(c):
    t = np.arange(c)[:, None] ^ np.arange(c)[None, :]
    lv = np.where(t == 0, 32, 31 - np.floor(np.log2(np.maximum(t, 1))).astype(np.int64))
    return np.tile(lv, (N_REC_HEADS, 1)).astype(np.int32)


def _retention_tables(log_gamma, c):
    g = jnp.repeat(log_gamma, D_HEAD, axis=1)
    t = jnp.arange(c, dtype=F32)
    diff = t[:, None] - t[None, :]
    gh = log_gamma[:, :, None, None]
    d_f = jnp.where(diff >= 0, jnp.exp(gh[0] * jnp.maximum(diff, 0.0)), 0.0)
    d_b = jnp.where(diff <= 0, jnp.exp(gh[1] * jnp.maximum(-diff, 0.0)), 0.0)
    d = jnp.stack([d_f, d_b]).reshape(2, N_REC_HEADS * c, c)
    q_f, k_f = jnp.exp(g[0] * (t[:, None] + 1.0)), jnp.exp(g[0] * (c - 1.0 - t[:, None]))
    q_b, k_b = jnp.exp(g[1] * (c - t[:, None])), jnp.exp(g[1] * t[:, None])
    f = jnp.stack([jnp.stack([q_f, k_f]), jnp.stack([q_b, k_b])])
    a = jnp.exp(g * c).reshape(2, 1, D_REC)
    return d, f, a


def _rec_kernel(*refs, state_in, state_out, aliased):
    refs = list(refs)
    rf_ref, rb_ref, rd_ref, rfac_ref, ra_ref, lb_ref, lv_ref = refs[:7]
    del refs[:7]
    if state_in:
        sret_ref, shg_ref, t4_ref = refs[:3]
        del refs[:3]
    if state_out:
        t4t_ref = refs.pop(0)
    del refs[:aliased]
    of_ref, ob_ref = refs[:2]
    del refs[:2]
    if state_out:
        oret_ref, ohg_ref = refs[:2]
        del refs[:2]
    (st_scr,) = refs

    ci = pl.program_id(1)
    c = rf_ref.shape[0]
    lane = lax.broadcasted_iota(I32, (1, D_REC), 1)
    head = lane >> 6
    row = lax.broadcasted_iota(I32, (c, 1), 0)
    srow = lax.broadcasted_iota(I32, (D_REC, 1), 0)
    same_head = (srow >> 6) == head
    eye = srow == lane
    head_is = [head == h for h in range(N_REC_HEADS)]
    head_m16 = [jnp.broadcast_to(jnp.where(m, 1.0, 0.0), (c, D_REC)).astype(BF16) for m in head_is]

    @pl.when(ci == 0)
    def _():
        if state_in:
            for d in range(2):
                for idx, ref in ((d, sret_ref), (2 + d, shg_ref)):
                    tiled = functools.reduce(jnp.add, [_dot(p, t4_ref[...]) for p in _split_bf16(ref[0, 0, d], 3)])
                    st_scr[idx] = jnp.where(same_head, tiled, 0.0)
        else:
            st_scr[...] = jnp.zeros_like(st_scr)

    def expand(t16):
        return jnp.concatenate([t16 * m for m in head_m16], axis=0)

    def nt(a, b):
        return lax.dot_general(a, b, _NT, preferred_element_type=F32)

    def finish(q, k, v16, scores, qfac, kfac, a_row, st_ref):
        st = st_ref[...]
        o = _dot((q * qfac).astype(BF16), st.astype(BF16))
        o_stack = _dot(scores.astype(BF16), v16)
        for h in range(N_REC_HEADS):
            o = o + jnp.where(head_is[h], o_stack[h * c:(h + 1) * c], 0.0)
        kv = lax.dot_general((k * kfac).astype(BF16), v16, _TN, preferred_element_type=F32)
        a_col = jnp.sum(jnp.where(eye, a_row, 0.0), axis=1, keepdims=True)
        st_ref[...] = st * a_col + jnp.where(same_head, kv, 0.0)
        return o

    def tree_scores(q, k, la, reverse):
        lv = lv_ref[...]
        scores = jnp.where(lv == 32, nt(expand(q.astype(BF16)), k.astype(BF16)), 0.0)
        pre, tot = la, la
        h, level = 1, 31
        while h < c:
            bit = (row & h) != 0
            query_side = jnp.logical_not(bit) if reverse else bit
            x = jnp.exp(jnp.minimum(jnp.where(query_side, pre, tot - pre), 0.0))
            qx = jnp.where(query_side, q * x, 0.0).astype(BF16)
            kx = jnp.where(query_side, 0.0, k * x).astype(BF16)
            scores = jnp.where(lv == level, nt(expand(qx), kx), scores)
            partner = jnp.where(bit, pltpu.roll(tot, h, 0), pltpu.roll(tot, c - h, 0))
            pre = pre + jnp.where(query_side, partner, 0.0)
            tot = tot + partner
            h, level = 2 * h, level - 1
        return scores, pre, tot

    lb = lb_ref[...]
    col = lambda ref, j: ref[:, j * D_REC:(j + 1) * D_REC]
    for d, (r_ref, o_ref) in enumerate(((rf_ref, of_ref), (rb_ref, ob_ref))):
        q, k, v16 = col(r_ref, 0), col(r_ref, 1) * (RET_DK ** -0.5), col(r_ref, 2).astype(BF16)
        scores = nt(expand(q.astype(BF16)), k.astype(BF16)) * rd_ref[d]
        o_ref[:, 0:D_REC] = finish(q, k, v16, scores, rfac_ref[d, 0], rfac_ref[d, 1], ra_ref[d], st_scr.at[d])
        z = col(r_ref, 5 + d)
        la = jnp.log(jnp.maximum(lb + (1.0 - lb) * jax.nn.sigmoid(z), MIN_GATE))
        q, k, v16 = _silu(col(r_ref, 4)), (1.0 - lb) * jax.nn.sigmoid(-z), col(r_ref, 7).astype(BF16)
        scores, pre, tot = tree_scores(q, k, la, reverse=d == 1)
        o_ref[:, D_REC:2 * D_REC] = finish(q, k, v16, scores, jnp.exp(jnp.minimum(pre, 0.0)),
                                           jnp.exp(jnp.minimum(tot - pre, 0.0)), jnp.exp(tot[0:1, :]),
                                           st_scr.at[2 + d])

    if state_out:
        @pl.when(ci == pl.num_programs(1) - 1)
        def _():
            for d in range(2):
                for idx, ref in ((d, oret_ref), (2 + d, ohg_ref)):
                    ref[0, 0, d] = functools.reduce(
                        jnp.add, [_dot(p, t4t_ref[...]) for p in _split_bf16(st_scr[idx], 3)])
            if state_out == "first":
                oret_ref[0, 1:] = jnp.zeros_like(oret_ref[0, 1:])
                ohg_ref[0, 1:] = jnp.zeros_like(ohg_ref[0, 1:])


def _recurrence(rest, ret_tabs, lb_row, lv, row0, n_seq, seq_len, layer, *, states_in=None, want_states=False,
                states_out=None):
    c = REC_CHUNK
    nc = seq_len // c
    cb0 = row0 // c
    const2 = lambda b, i: (0, 0)
    const3 = lambda b, i: (0, 0, 0)
    rd, rfac, ra = ret_tabs
    in_specs = [pl.BlockSpec((c, D_REST), lambda b, i: (cb0 + b * nc + i, 0)),
                pl.BlockSpec((c, D_REST), lambda b, i: (cb0 + b * nc + nc - 1 - i, 0)),
                pl.BlockSpec((2, N_REC_HEADS * c, c), const3),
                pl.BlockSpec((2, 2, c, D_REC), lambda b, i: (0, 0, 0, 0)),
                pl.BlockSpec((2, 1, D_REC), const3),
                pl.BlockSpec((1, D_REC), const2),
                pl.BlockSpec((N_REC_HEADS * c, c), const2)]
    args = [rest, rest, rd, rfac, ra, lb_row, lv]
    tile4 = jnp.tile(jnp.eye(D_HEAD, dtype=BF16), (1, N_REC_HEADS))
    if states_in is not None:
        blk = pl.BlockSpec((1, 1, 2, D_REC, D_HEAD), lambda b, i: (b, layer, 0, 0, 0))
        in_specs += [blk, blk, pl.BlockSpec((D_HEAD, D_REC), const2)]
        args += [states_in[0], states_in[1], tile4]
    if want_states:
        in_specs += [pl.BlockSpec((D_REC, D_HEAD), const2)]
        args += [tile4.T]
    out_specs = [pl.BlockSpec((c, 2 * D_REC), lambda b, i: (b * nc + i, 0)),
                 pl.BlockSpec((c, 2 * D_REC), lambda b, i: (b * nc + nc - 1 - i, 0))]
    out_shape = [jax.ShapeDtypeStruct((n_seq * seq_len, 2 * D_REC), F32)] * 2
    aliases = {}
    state_mode = None
    if want_states:
        st_shape = jax.ShapeDtypeStruct((n_seq, DEPTH, 2, D_REC, D_HEAD), F32)
        out_shape += [st_shape, st_shape]
        if states_out is None:
            state_mode = "first"
            out_specs += [pl.BlockSpec((1, DEPTH, 2, D_REC, D_HEAD), lambda b, i: (b, 0, 0, 0, 0))] * 2
        else:
            state_mode = "next"
            out_specs += [pl.BlockSpec((1, 1, 2, D_REC, D_HEAD), lambda b, i: (b, layer, 0, 0, 0))] * 2
            aliases = {len(args): 2, len(args) + 1: 3}
            in_specs += [pl.BlockSpec(memory_space=pl.ANY)] * 2
            args += list(states_out)
    return pl.pallas_call(
        functools.partial(_rec_kernel, state_in=states_in is not None, state_out=state_mode,
                          aliased=len(aliases)),
        grid=(n_seq, nc),
        in_specs=in_specs,
        out_specs=out_specs,
        out_shape=out_shape,
        scratch_shapes=[pltpu.VMEM((4, D_REC, D_REC), F32)],
        input_output_aliases=aliases,
        compiler_params=_params("arbitrary", "arbitrary"),
        name="rec_lat" if states_in is not None else "rec_ctx",
    )(*args)


def _route(logits, bias):
    scores = jax.nn.sigmoid(logits)
    sel = scores + bias
    srow = [scores[e:e + 1, :] for e in range(N_EXPERTS)]
    rows = [sel[e:e + 1, :] for e in range(N_EXPERTS)]
    gs = []
    for g in range(N_GROUPS):
        a, b, c, d = rows[4 * g:4 * g + 4]
        gs.append(functools.reduce(jnp.maximum, [a + b, a + c, a + d, b + c, b + d, c + d]))
    best = jnp.zeros_like(gs[0], dtype=I32)
    best_v = gs[0]
    for g in range(1, N_GROUPS):
        upd = gs[g] > best_v
        best = jnp.where(upd, g, best)
        best_v = jnp.where(upd, gs[g], best_v)
    masked = [jnp.where(best == e // EXPERTS_PER_GROUP, rows[e], MASKED_SCORE) for e in range(N_EXPERTS)]
    i1 = jnp.zeros_like(best)
    v1 = masked[0]
    for e in range(1, N_EXPERTS):
        upd = masked[e] > v1
        i1 = jnp.where(upd, e, i1)
        v1 = jnp.where(upd, masked[e], v1)
    i2 = jnp.zeros_like(best)
    v2 = jnp.full_like(v1, -jnp.inf)
    for e in range(N_EXPERTS):
        upd = (masked[e] > v2) & (i1 != e)
        i2 = jnp.where(upd, e, i2)
        v2 = jnp.where(upd, masked[e], v2)
    w1 = functools.reduce(jnp.add, [jnp.where(i1 == e, srow[e], 0.0) for e in range(N_EXPERTS)])
    w2 = functools.reduce(jnp.add, [jnp.where(i2 == e, srow[e], 0.0) for e in range(N_EXPERTS)])
    tot = w1 + w2
    w1, w2 = w1 / tot, w2 / tot
    expert = lax.broadcasted_iota(I32, logits.shape, 0)
    return jnp.where(expert == i1, w1, 0.0) + jnp.where(expert == i2, w2, 0.0)


def _outproj_kernel(xa_ref, xb_ref, odaa_ref, odab_ref, ofa_ref, ofb_ref, oba_ref, obb_ref, rg_ref, gg_ref,
                    mod_ref, subln_ref, retn_ref, hgn_ref, g64_ref, wout_ref, nffn_ref, wrh_ref, wrl_ref, rb_ref,
                    xmid_ref, hn2_ref, gate_ref, rank_ref, count_ref, *, n_ctx_tiles):
    g64 = g64_ref[...]
    is_ctx = pl.program_id(0) < n_ctx_tiles
    pick = lambda a_ref, b_ref: jnp.where(is_ctx, a_ref[...], b_ref[...])

    def gnorm(t, w):
        return t * lax.rsqrt(_group_mean_sq(t, g64) + EPS) * w

    o_da = gnorm(pick(odaa_ref, odab_ref), subln_ref[...])
    o_rec = pick(ofa_ref, ofb_ref) + pick(oba_ref, obb_ref)
    o_ret = gnorm(o_rec[:, 0:D_REC], retn_ref[...]) * _silu(rg_ref[...])
    o_hg = gnorm(o_rec[:, D_REC:2 * D_REC], hgn_ref[...]) * _silu(gg_ref[...])
    mixed = jnp.concatenate([o_da, o_ret, o_hg], axis=1).astype(BF16)
    mod = mod_ref[0]
    x = pick(xa_ref, xb_ref) + mod[:, 2 * D_MODEL:3 * D_MODEL] * _dot(mixed, wout_ref[...])
    xmid_ref[...] = x
    y = x * lax.rsqrt(jnp.mean(x * x, axis=-1, keepdims=True) + EPS) * nffn_ref[...]
    hn = y * (1.0 + mod[:, 4 * D_MODEL:5 * D_MODEL]) + mod[:, 3 * D_MODEL:4 * D_MODEL]
    hi, lo = _split_bf16(hn)
    hn2_ref[...] = hi
    wrh = wrh_ref[...]
    logits = (lax.dot_general(wrh, hi, _NT, preferred_element_type=F32)
              + lax.dot_general(wrh, lo, _NT, preferred_element_type=F32)
              + lax.dot_general(wrl_ref[...], hi, _NT, preferred_element_type=F32))
    gate = _route(logits, rb_ref[...])
    gate_ref[...] = gate

    tm = gate.shape[1]
    member = jnp.concatenate(
        [jnp.max(gate[EXPERTS_PER_GROUP * g:EXPERTS_PER_GROUP * (g + 1)], axis=0, keepdims=True)
         for g in range(N_GROUPS)] + [jnp.zeros((8 - N_GROUPS, tm), F32)], axis=0)
    member = jnp.where(member > 0, 1.0, 0.0)
    earlier = (lax.broadcasted_iota(I32, (tm, tm), 0) < lax.broadcasted_iota(I32, (tm, tm), 1))
    before = _dot(member.astype(BF16), jnp.where(earlier, 1.0, 0.0).astype(BF16))

    @pl.when(pl.program_id(0) % (MOE_BLOCK // tm) == 0)
    def _():
        count_ref[...] = jnp.zeros_like(count_ref)

    rank_ref[...] = (before + count_ref[:, 0:1]).astype(I32)
    count_ref[...] += jnp.sum(member, axis=1, keepdims=True)


def _outproj(x, oda, o_f, o_b, rest, mod, subln, retn, hgn, g64, w_out16, norm_w, wr_hi, wr_lo, rbias, st):
    tm = st.tile
    t = st.n_tiles * tm
    row = lambda i: (i, 0)
    const = lambda i: (0, 0)
    pair = lambda width: [pl.BlockSpec((tm, width), lambda i: (st.ctx_idx(i), 0)),
                          pl.BlockSpec((tm, width), lambda i: (st.lat_idx(i), 0))]
    return pl.pallas_call(
        functools.partial(_outproj_kernel, n_ctx_tiles=st.n_ctx),
        grid=(st.n_tiles,),
        in_specs=pair(D_MODEL) + pair(D_V) + pair(2 * D_REC) + pair(2 * D_REC) + [
            pl.BlockSpec((tm, D_REC), lambda i: (i, 3)),
            pl.BlockSpec((tm, D_REC), lambda i: (i, 8)),
            pl.BlockSpec((1, 1, 6 * D_MODEL), lambda i: (st.mod_idx(i), 0, 0)),
            pl.BlockSpec((1, D_V), const),
            pl.BlockSpec((1, D_REC), const),
            pl.BlockSpec((1, D_REC), const),
            pl.BlockSpec((LANES, LANES), const),
            pl.BlockSpec((D_MODEL, D_MODEL), const, pipeline_mode=pl.Buffered(1)),
            pl.BlockSpec((1, D_MODEL), const),
            pl.BlockSpec((N_EXPERTS, D_MODEL), const),
            pl.BlockSpec((N_EXPERTS, D_MODEL), const),
            pl.BlockSpec((N_EXPERTS, 1), const)],
        out_specs=[pl.BlockSpec((tm, D_MODEL), row), pl.BlockSpec((tm, D_MODEL), row),
                   pl.BlockSpec((N_EXPERTS, tm), lambda i: (0, i)), pl.BlockSpec((8, tm), lambda i: (0, i))],
        out_shape=[jax.ShapeDtypeStruct((t, D_MODEL), F32), jax.ShapeDtypeStruct((t, D_MODEL), BF16),
                   jax.ShapeDtypeStruct((N_EXPERTS, t), F32), jax.ShapeDtypeStruct((8, t), I32)],
        scratch_shapes=[pltpu.VMEM((8, LANES), F32)],
        compiler_params=_params("arbitrary"),
        name="outproj",
    )(*x, *oda, *o_f, *o_b, rest, rest, mod, subln, retn, hgn, g64, w_out16, norm_w, wr_hi, wr_lo, rbias)


def _moe_plan(gate_t, rank, n_blocks):
    nb, tr, spb = MOE_BLOCK, MOE_TILE, MOE_SLOTS_PER_BLOCK
    n_slots = n_blocks * spb
    member = (gate_t.reshape(N_GROUPS, EXPERTS_PER_GROUP, -1) > 0).any(axis=1)
    mb = member.reshape(N_GROUPS, n_blocks, nb)
    mi = mb.astype(I32)
    rank = rank[:N_GROUPS].reshape(N_GROUPS, n_blocks, nb)
    tiles = (mi.sum(-1) + tr - 1) // tr
    t_end = jnp.cumsum(tiles, axis=0)
    t_off = t_end - tiles
    n_tiles = t_end[-1]
    dest = jnp.where(mb, t_off[..., None] * tr + rank, -1)
    order = jnp.cumsum(mi, axis=0) - mi
    n_mem = mi.sum(0)
    row_of = lambda k: jnp.where(n_mem > k, jnp.where(mb & (order == k), dest, 0).sum(0), -1)
    s = jnp.arange(n_slots, dtype=I32)
    s_blk, s_tile = s // spb, s % spb
    used = s_tile < n_tiles[s_blk]
    s_grp = (s_tile[None, :] >= t_end[:, s_blk]).sum(0).astype(I32)
    key = jnp.where(used, s_grp, N_GROUPS) * n_slots + s
    pos = (key[None, :] < key[:, None]).sum(1).astype(I32)
    slot_of_item = jnp.where(pos[None, :] == s[:, None], s[None, :], 0).sum(1).astype(I32)
    n_valid = used.sum().astype(I32)
    src = slot_of_item[jnp.minimum(s, n_valid - 1)]
    return dict(dest=dest.reshape(N_GROUPS * n_blocks, 1, nb),
                row1=row_of(0).reshape(n_blocks, 1, nb), row2=row_of(1).reshape(n_blocks, 1, nb),
                n_tiles=n_tiles.astype(I32), item_g=s_grp[src], item_b=s_blk[src], item_tile=s_tile[src],
                item_valid=(s < n_valid).astype(I32), item_slot=slot_of_item)


def _moe_ffn_kernel(ig_ref, ib_ref, it_ref, iv_ref, is_ref, h_ref, dest_ref, gate_ref, wg_ref, wu_ref, wd_ref,
                    y_ref, wg16, wu16, wd16):
    i = pl.program_id(0)
    g = ig_ref[i]
    prev_g = ig_ref[jnp.maximum(i - 1, 0)]

    @pl.when((i == 0) | (g != prev_g))
    def _():
        wg16[...] = wg_ref[...].astype(BF16)
        wu16[...] = wu_ref[...].astype(BF16)
        wd16[...] = wd_ref[...].astype(BF16)

    @pl.when(iv_ref[i] == 0)
    def _():
        y_ref[...] = jnp.zeros_like(y_ref)

    @pl.when(iv_ref[i] == 1)
    def _():
        rows = lax.broadcasted_iota(I32, (MOE_TILE, 1), 0) + it_ref[i] * MOE_TILE
        sort = jnp.where(dest_ref[0] == rows, 1.0, 0.0).astype(BF16)
        x = _dot(sort, h_ref[...]).astype(BF16)
        gates = functools.reduce(jnp.add, [lax.dot_general(sort, p, _NT, preferred_element_type=F32)
                                           for p in _split_bf16(gate_ref[...])])
        lane = lax.broadcasted_iota(I32, (1, N_EXPERTS), 1)
        acc = jnp.zeros((MOE_TILE, D_MODEL), F32)
        for e in range(EXPERTS_PER_GROUP):
            ge = jnp.sum(jnp.where(lane == g * EXPERTS_PER_GROUP + e, gates, 0.0), axis=1, keepdims=True)
            act = _silu(_dot(x, wg16[e])) * _dot(x, wu16[e])
            acc = acc + _dot((act * ge).astype(BF16), wd16[e])
        y_ref[...] = acc.astype(BF16)


def _moe_unsort_kernel(nt_ref, y_ref, r1_ref, r2_ref, x_ref, mod_ref, oa_ref, ob_ref, acc_ref, *, n_ctx_blocks):
    b = pl.program_id(0)
    acc_ref[...] = jnp.zeros_like(acc_ref)
    chunk = max(MOE_TILE, MXU_DEPTH)
    tiles_per_chunk = chunk // MOE_TILE
    row1, row2 = r1_ref[0], r2_ref[0]

    def body(ci, carry):
        r0 = pl.multiple_of(ci * chunk, chunk)
        rows = lax.broadcasted_iota(I32, (chunk, 1), 0) + r0
        sort = jnp.where(row1 == rows, 1.0, jnp.where(row2 == rows, 1.0, 0.0)).astype(BF16)
        acc_ref[...] += lax.dot_general(sort, y_ref[pl.ds(r0, chunk), :], _TN, preferred_element_type=F32)
        return carry

    lax.fori_loop(0, (nt_ref[b] + tiles_per_chunk - 1) // tiles_per_chunk, body, 0)
    result = lambda: x_ref[...] + mod_ref[0][:, 5 * D_MODEL:6 * D_MODEL] * acc_ref[...]

    @pl.when(b < n_ctx_blocks)
    def _():
        oa_ref[...] = result()

    @pl.when(b >= n_ctx_blocks)
    def _():
        ob_ref[...] = result()


def _moe(hn2, gate_t, rank, w_gate, w_up, w_down, x_mid, mod, st, layer):
    nb, tr, spb = st.tile, MOE_TILE, MOE_SLOTS_PER_BLOCK
    n_blocks = st.n_tiles
    plan = _moe_plan(gate_t, rank, n_blocks)

    wmap = lambda i, ig, ib, it, iv, isl: (layer * N_GROUPS + ig[i], 0, 0)
    once = pl.Buffered(1)
    y = pl.pallas_call(
        _moe_ffn_kernel,
        grid_spec=pltpu.PrefetchScalarGridSpec(
            num_scalar_prefetch=5,
            grid=(n_blocks * spb,),
            in_specs=[pl.BlockSpec((nb, D_MODEL), lambda i, ig, ib, it, iv, isl: (ib[i], 0)),
                      pl.BlockSpec((1, 1, nb), lambda i, ig, ib, it, iv, isl: (ig[i] * n_blocks + ib[i], 0, 0)),
                      pl.BlockSpec((N_EXPERTS, nb), lambda i, ig, ib, it, iv, isl: (0, ib[i])),
                      pl.BlockSpec((EXPERTS_PER_GROUP, D_MODEL, D_EXPERT), wmap, pipeline_mode=once),
                      pl.BlockSpec((EXPERTS_PER_GROUP, D_MODEL, D_EXPERT), wmap, pipeline_mode=once),
                      pl.BlockSpec((EXPERTS_PER_GROUP, D_EXPERT, D_MODEL), wmap, pipeline_mode=once)],
            out_specs=pl.BlockSpec((tr, D_MODEL), lambda i, ig, ib, it, iv, isl: (isl[i], 0)),
            scratch_shapes=[pltpu.VMEM((EXPERTS_PER_GROUP, D_MODEL, D_EXPERT), BF16),
                            pltpu.VMEM((EXPERTS_PER_GROUP, D_MODEL, D_EXPERT), BF16),
                            pltpu.VMEM((EXPERTS_PER_GROUP, D_EXPERT, D_MODEL), BF16)]),
        out_shape=jax.ShapeDtypeStruct((n_blocks * spb * tr, D_MODEL), BF16),
        compiler_params=_params("arbitrary"),
        name="moe_ffn",
    )(plan["item_g"], plan["item_b"], plan["item_tile"], plan["item_valid"], plan["item_slot"],
      hn2, plan["dest"], gate_t, w_gate, w_up, w_down)

    return pl.pallas_call(
        functools.partial(_moe_unsort_kernel, n_ctx_blocks=st.n_ctx),
        grid_spec=pltpu.PrefetchScalarGridSpec(
            num_scalar_prefetch=1,
            grid=(n_blocks,),
            in_specs=[pl.BlockSpec((spb * tr, D_MODEL), lambda b, nt: (b, 0)),
                      pl.BlockSpec((1, 1, nb), lambda b, nt: (b, 0, 0)),
                      pl.BlockSpec((1, 1, nb), lambda b, nt: (b, 0, 0)),
                      pl.BlockSpec((nb, D_MODEL), lambda b, nt: (b, 0)),
                      pl.BlockSpec((1, 1, 6 * D_MODEL), lambda b, nt: (st.mod_idx(b), 0, 0))],
            out_specs=[pl.BlockSpec((nb, D_MODEL), lambda b, nt: (st.ctx_idx(b), 0)),
                       pl.BlockSpec((nb, D_MODEL), lambda b, nt: (st.lat_idx(b), 0))],
            scratch_shapes=[pltpu.VMEM((nb, D_MODEL), F32)]),
        out_shape=[jax.ShapeDtypeStruct((st.n_ctx * nb, D_MODEL), F32),
                   jax.ShapeDtypeStruct(((n_blocks - st.n_ctx) * nb, D_MODEL), F32)],
        compiler_params=_params("arbitrary"),
        name="moe_unsort",
    )(plan["n_tiles"], y, plan["row1"], plan["row2"], x_mid, mod)


def _block_avg(group):
    i = np.arange(LANES)
    return jnp.asarray((i[:, None] // group == i[None, :] // group) / group, dtype=BF16)


def _rope_tables(n_tokens, n_identity):
    rows = n_tokens // GRID_W
    pos_r = jnp.repeat(jnp.arange(rows, dtype=F32), GRID_W)
    pos_c = jnp.tile(jnp.arange(GRID_W, dtype=F32), rows)
    n_freq = DA_QK // 4
    inv = ROPE_BASE ** (-jnp.arange(n_freq, dtype=F32) / n_freq)
    ang = jnp.concatenate([pos_r[:, None] * inv, pos_c[:, None] * inv], axis=-1)
    cos = jnp.repeat(jnp.cos(ang), 2, axis=-1)
    sin = jnp.repeat(jnp.sin(ang), 2, axis=-1) * jnp.tile(jnp.asarray([-1.0, 1.0], F32), DA_QK // 2)
    cos = jnp.concatenate([jnp.ones((n_identity, DA_QK), F32), cos], axis=0)
    sin = jnp.concatenate([jnp.zeros((n_identity, DA_QK), F32), sin], axis=0)
    return jnp.tile(cos, (1, 2 * H_DA)), jnp.tile(sin, (1, 2 * H_DA))


def kernel(x_prompt, x_sample, cache_k, cache_v, state_ret, state_hgrn, c, c_ctx, w_in, w_out, w_ada, b_ada,
           norm_mix, norm_ffn, da_qnorm, da_knorm, da_lambda, da_subln, ret_decay, ret_norm, hg_lb, hg_norm,
           w_router, router_bias, w_gate, w_up, w_down):
    n_ctx, l_ctx, _ = x_prompt.shape
    n_lat, l_lat, _ = x_sample.shape
    t_ctx = n_ctx * l_ctx
    past = cache_k.shape[2]
    assert l_ctx == TOKEN_TILE and l_lat % MOE_BLOCK == 0 and t_ctx % MOE_BLOCK == 0 and l_lat % GRID_W == 0

    ctx_row = n_lat
    n_cond = -(-(n_lat + 1) // 8) * 8
    cond = jnp.zeros((n_cond, D_MODEL), F32).at[:n_lat].set(c).at[ctx_row].set(c_ctx)
    mod_all = _ada(cond, w_ada, b_ada)

    st = _Stream(t_ctx, n_lat, l_lat, TOKEN_TILE, ctx_row)
    st_moe = _Stream(t_ctx, n_lat, l_lat, MOE_BLOCK, ctx_row)
    g32, g64 = _block_avg(DA_QK), _block_avg(DA_V)
    cos_t, sin_t = _rope_tables(l_lat, TOKEN_TILE)
    lv = jnp.asarray(_level_table(REC_CHUNK))
    p_lb = jax.nn.softmax(hg_lb.astype(F32), axis=0)
    lb_all = jnp.cumsum(p_lb, axis=0) - p_lb[0]
    wr_hi = w_router.T.astype(BF16)
    wr_lo = (w_router.T - wr_hi.astype(F32)).astype(BF16)
    rbias = router_bias.astype(F32).reshape(N_EXPERTS, 1)
    flat_w = lambda w: w.reshape(DEPTH * N_EXPERTS, w.shape[2], w.shape[3])
    w_gate, w_up, w_down = flat_w(w_gate), flat_w(w_up), flat_w(w_down)
    flat_s = lambda s: s.astype(F32).reshape(n_lat, DEPTH, 2, D_REC, D_HEAD)
    states_in = (flat_s(state_ret), flat_s(state_hgrn))
    cache_k16 = cache_k.reshape(n_lat, DEPTH, past, D_QK).astype(BF16)
    cache_v16 = cache_v.reshape(n_lat, DEPTH, past, D_V).astype(BF16)

    x = (x_prompt.reshape(t_ctx, D_MODEL), x_sample.reshape(n_lat * l_lat, D_MODEL))
    caches, states_out = None, None
    for l in range(DEPTH):
        lam_init = 0.8 - 0.6 * math.exp(-0.3 * l)
        lp = da_lambda[l].astype(F32)
        lam = (jnp.exp(jnp.sum(lp[0] * lp[1])) - jnp.exp(jnp.sum(lp[2] * lp[3])) + lam_init).reshape(1)
        mod = mod_all[l].reshape(n_cond, 1, 6 * D_MODEL)
        qw = (jnp.tile(da_qnorm[l].astype(F32), 2 * H_DA) * (DA_QK ** -0.5)).reshape(1, D_QK)
        kw = jnp.tile(da_knorm[l].astype(F32), 2 * H_DA).reshape(1, D_QK)

        qb, kb, vb, rest, kc, vc = _inproj(x[0], x[1], mod, norm_mix[l].reshape(1, D_MODEL),
                                           w_in[l].astype(BF16), qw, kw, g32, cos_t, sin_t, st, l, n_ctx, caches)
        caches = (kc, vc)

        oda = (_attention(lam, qb, kb, vb, 0, n_ctx, l_ctx),
               _attention(lam, qb, kb, vb, t_ctx, n_lat, l_lat, l, cache=(cache_k16, cache_v16)))

        ret_tabs = _retention_tables(-jnp.exp(ret_decay[l].astype(F32)), REC_CHUNK)
        lb_row = lb_all[l].reshape(1, D_REC)
        ofc, obc, s_ret, s_hg = _recurrence(rest, ret_tabs, lb_row, lv, 0, n_ctx, l_ctx, l,
                                            want_states=True, states_out=states_out)
        states_out = (s_ret, s_hg)
        ofl, obl = _recurrence(rest, ret_tabs, lb_row, lv, t_ctx, n_lat, l_lat, l, states_in=states_in)

        subln = (jnp.tile(da_subln[l].astype(F32), H_DA) * (1.0 - lam_init)).reshape(1, D_V)
        retn = jnp.tile(ret_norm[l].astype(F32), H_RET).reshape(1, D_REC)
        hgn = jnp.tile(hg_norm[l].astype(F32), H_HG).reshape(1, D_REC)
        x_mid, hn2, gate_t, rank = _outproj(x, oda, (ofc, ofl), (obc, obl), rest, mod, subln, retn, hgn, g64,
                                      w_out[l].astype(BF16), norm_ffn[l].reshape(1, D_MODEL), wr_hi, wr_lo,
                                      rbias, st)
        x = _moe(hn2, gate_t, rank, w_gate, w_up, w_down, x_mid, mod, st_moe, l)

    kc, vc = caches
    s_ret, s_hg = states_out
    return (x[0].reshape(n_ctx, l_ctx, D_MODEL), x[1].reshape(n_lat, l_lat, D_MODEL),
            kc.reshape(n_ctx, DEPTH, l_ctx, H_DA, 2, DA_QK), vc.reshape(n_ctx, DEPTH, l_ctx, H_DA, DA_V),
            s_ret.reshape(n_ctx, DEPTH, 2, H_RET, RET_DK, RET_DV),
            s_hg.reshape(n_ctx, DEPTH, 2, H_HG, HG_DK, HG_DV))
```

```python
import functools
import math

import numpy as np
import jax
import jax.numpy as jnp
from jax import lax
from jax.experimental import pallas as pl
from jax.experimental.pallas import tpu as pltpu

F32 = jnp.float32
BF16 = jnp.bfloat16
I32 = jnp.int32

D_MODEL = 1024
DEPTH = 2
GRID_W = 64
EPS = 1e-6
MIN_GATE = 1e-30
ROPE_BASE = 10000.0
H_DA, DA_QK, DA_V = 8, 32, 64
H_RET, RET_DK, RET_DV = 4, 64, 64
H_HG, HG_DK, HG_DV = 4, 64, 64
D_QK = H_DA * 2 * DA_QK
D_V = H_DA * DA_V
N_REC_HEADS = 4
D_HEAD = 64
D_REC = N_REC_HEADS * D_HEAD
D_REST = 9 * D_REC
D_IN = 2 * D_QK + D_V + D_REST
N_EXPERTS, N_GROUPS, EXPERTS_PER_GROUP = 16, 4, 4
D_EXPERT = 512
MASKED_SCORE = -2.0

LANES = 128
TOKEN_TILE = 512
Q_TILE = 1024
REC_CHUNK = 128
MOE_BLOCK = 1024
MOE_TILE = 256
MXU_DEPTH = 256
MOE_SLOTS_PER_BLOCK = 2 * MOE_BLOCK // MOE_TILE + N_GROUPS
VMEM_LIMIT = 56 * 1024 * 1024

_NT = (((1,), (1,)), ((), ()))
_TN = (((0,), (0,)), ((), ()))


def _params(*sem):
    return pltpu.CompilerParams(dimension_semantics=sem, vmem_limit_bytes=VMEM_LIMIT)


def _dot(a, b):
    return jnp.dot(a, b, preferred_element_type=F32)


def _split_bf16(t, terms=2):
    out = []
    for _ in range(terms - 1):
        hi = t.astype(BF16)
        out.append(hi)
        t = t - hi.astype(F32)
    out.append(t.astype(BF16))
    return out


def _group_mean_sq(t, g):
    hi, lo = _split_bf16(t * t)
    outs = []
    for s in range(t.shape[1] // LANES):
        sl = slice(s * LANES, (s + 1) * LANES)
        outs.append(_dot(hi[:, sl], g) + _dot(lo[:, sl], g))
    return jnp.concatenate(outs, axis=1) if len(outs) > 1 else outs[0]


def _silu(t):
    return t * jax.nn.sigmoid(t)


class _Stream:
    def __init__(self, n_ctx_rows, n_lat_seq, lat_len, tile, ctx_row):
        self.tile = tile
        self.n_ctx = n_ctx_rows // tile
        self.per_seq = lat_len // tile
        self.n_tiles = self.n_ctx + n_lat_seq * self.per_seq
        self.ctx_row = ctx_row

    def ctx_idx(self, i):
        return jnp.minimum(i, self.n_ctx - 1)

    def lat_idx(self, i):
        return jnp.maximum(i - self.n_ctx, 0)

    def mod_idx(self, i):
        return jnp.where(i < self.n_ctx, self.ctx_row, (i - self.n_ctx) // self.per_seq)

    def pos_idx(self, i):
        return jnp.where(i < self.n_ctx, 0, 1 + (i - self.n_ctx) % self.per_seq)


def _ada_kernel(c_ref, w_ref, b_ref, o_ref):
    s = _silu(c_ref[...]).astype(BF16)
    o_ref[0] = _dot(s, w_ref[0].astype(BF16)) + b_ref[0]


def _ada(cond, w_ada, b_ada):
    rows = cond.shape[0]
    n_tile = 1536
    return pl.pallas_call(
        _ada_kernel,
        grid=(DEPTH, 6 * D_MODEL // n_tile),
        in_specs=[pl.BlockSpec((rows, D_MODEL), lambda l, j: (0, 0)),
                  pl.BlockSpec((1, D_MODEL, n_tile), lambda l, j: (l, 0, j)),
                  pl.BlockSpec((1, 1, n_tile), lambda l, j: (l, 0, j))],
        out_specs=pl.BlockSpec((1, rows, n_tile), lambda l, j: (l, 0, j)),
        out_shape=jax.ShapeDtypeStruct((DEPTH, rows, 6 * D_MODEL), F32),
        compiler_params=_params("arbitrary", "arbitrary"),
        name="ada",
    )(cond, w_ada, b_ada.reshape(DEPTH, 1, 6 * D_MODEL))


def _inproj_kernel(*refs, n_ctx_tiles, first):
    (xa_ref, xb_ref, mod_ref, nw_ref, w_ref, qw_ref, kw_ref, g32_ref, cos_ref, sin_ref) = refs[:10]
    q_ref, kb_ref, vb_ref, rest_ref, kc_ref, vc_ref = refs[10:] if first else refs[12:]
    is_ctx = pl.program_id(0) < n_ctx_tiles
    x = jnp.where(is_ctx, xa_ref[...], xb_ref[...])
    y = x * lax.rsqrt(jnp.mean(x * x, axis=-1, keepdims=True) + EPS) * nw_ref[...]
    mod = mod_ref[0]
    hn = (y * (1.0 + mod[:, D_MODEL:2 * D_MODEL]) + mod[:, 0:D_MODEL]).astype(BF16)
    g32 = g32_ref[...]
    lane = lax.broadcasted_iota(I32, (1, D_QK), 1)
    even = (lane & 1) == 0

    def qk_norm_rope(t, w):
        t = t * lax.rsqrt(_group_mean_sq(t, g32) + EPS) * w
        partner = jnp.where(even, pltpu.roll(t, D_QK - 1, 1), pltpu.roll(t, 1, 1))
        return t * cos_ref[...] + partner * sin_ref[...]

    q_ref[...] = qk_norm_rope(_dot(hn, w_ref[:, 0:D_QK]), qw_ref[...]).astype(BF16)
    k = qk_norm_rope(_dot(hn, w_ref[:, D_QK:2 * D_QK]), kw_ref[...])
    kb_ref[...] = k.astype(BF16)
    v = _dot(hn, w_ref[:, 2 * D_QK:2 * D_QK + D_V])
    vb_ref[...] = v.astype(BF16)
    rest_ref[...] = _dot(hn, w_ref[:, 2 * D_QK + D_V:D_IN])

    @pl.when(is_ctx)
    def _():
        n_seq, _, seq_len, _ = kc_ref.shape
        kc_ref[:, 0] = k.reshape(n_seq, seq_len, D_QK)
        vc_ref[:, 0] = v.reshape(n_seq, seq_len, D_V)
        if first:
            kc_ref[:, 1:] = jnp.zeros_like(kc_ref[:, 1:])
            vc_ref[:, 1:] = jnp.zeros_like(vc_ref[:, 1:])


def _inproj(xa, xb, mod, norm_w, w_in16, qw, kw, g32, cos_t, sin_t, st, layer, n_ctx_seq, l_ctx, caches):
    tm = st.tile
    t = st.n_tiles * tm
    first = caches is None
    row = lambda i: (i, 0)
    const = lambda i: (0, 0)
    in_specs = [pl.BlockSpec((tm, D_MODEL), lambda i: (st.ctx_idx(i), 0)),
                pl.BlockSpec((tm, D_MODEL), lambda i: (st.lat_idx(i), 0)),
                pl.BlockSpec((1, 1, 6 * D_MODEL), lambda i: (st.mod_idx(i), 0, 0)),
                pl.BlockSpec((1, D_MODEL), const),
                pl.BlockSpec((D_MODEL, D_IN), const, pipeline_mode=pl.Buffered(1)),
                pl.BlockSpec((1, D_QK), const),
                pl.BlockSpec((1, D_QK), const),
                pl.BlockSpec((LANES, LANES), const),
                pl.BlockSpec((tm, D_QK), lambda i: (st.pos_idx(i), 0)),
                pl.BlockSpec((tm, D_QK), lambda i: (st.pos_idx(i), 0))]
    args = [xa, xb, mod, norm_w, w_in16, qw, kw, g32, cos_t, sin_t]
    aliases = {}
    per_tile = tm // l_ctx
    if first:
        cache_spec = pl.BlockSpec((per_tile, DEPTH, l_ctx, D_QK), lambda i: (st.ctx_idx(i), 0, 0, 0))
    else:
        cache_spec = pl.BlockSpec((per_tile, 1, l_ctx, D_QK), lambda i: (st.ctx_idx(i), layer, 0, 0))
        in_specs += [pl.BlockSpec(memory_space=pl.ANY)] * 2
        args += list(caches)
        aliases = {10: 4, 11: 5}
    cache_shape = jax.ShapeDtypeStruct((n_ctx_seq, DEPTH, l_ctx, D_QK), F32)
    return pl.pallas_call(
        functools.partial(_inproj_kernel, n_ctx_tiles=st.n_ctx, first=first),
        grid=(st.n_tiles,),
        in_specs=in_specs,
        out_specs=[pl.BlockSpec((tm, D_QK), row), pl.BlockSpec((tm, D_QK), row),
                   pl.BlockSpec((tm, D_V), row), pl.BlockSpec((tm, D_REST), row), cache_spec, cache_spec],
        out_shape=[jax.ShapeDtypeStruct((t, D_QK), BF16), jax.ShapeDtypeStruct((t, D_QK), BF16),
                   jax.ShapeDtypeStruct((t, D_V), BF16), jax.ShapeDtypeStruct((t, D_REST), F32),
                   cache_shape, cache_shape],
        input_output_aliases=aliases,
        compiler_params=_params("arbitrary"),
        name="inproj",
    )(*args)


def _attn_kernel(*refs, with_cache):
    if with_cache:
        lam_ref, q_ref, k_ref, v_ref, ck_ref, cv_ref, o_ref = refs
    else:
        lam_ref, q_ref, k_ref, v_ref, o_ref = refs
    lam = lam_ref[0]
    lane = lax.broadcasted_iota(I32, (1, LANES), 1)
    left = lane < DA_V
    one = jnp.ones((), BF16)
    zero = jnp.zeros((), BF16)
    segs = [(k_ref[...], v_ref[...])]
    if with_cache:
        segs.append((ck_ref[0, 0], cv_ref[0, 0]))
    owns = [left, jnp.logical_not(left)]
    vms = [[jnp.where(own, v, one) for _, v in segs] for own in owns]
    tq = min(Q_TILE, q_ref.shape[0])

    def q_block(i, carry):
        r0 = pl.multiple_of(i * tq, tq)
        q = q_ref[pl.ds(r0, tq), :]
        acc = jnp.zeros((tq, LANES), F32)
        for side in range(2):
            for r in range(2):
                lo = side * DA_V + r * DA_QK
                qm = jnp.where((lane >= lo) & (lane < lo + DA_QK), q, zero)
                ss = [lax.dot_general(qm, k, _NT, preferred_element_type=F32) for k, _ in segs]
                m = functools.reduce(jnp.maximum, [jnp.max(s, axis=-1, keepdims=True) for s in ss])
                res = functools.reduce(
                    jnp.add, [_dot(jnp.exp(s - m).astype(BF16), vm) for s, vm in zip(ss, vms[side])])
                den = pltpu.roll(res, DA_V, 1)
                coef = 1.0 if r == 0 else -lam
                acc = acc + jnp.where(owns[side], coef * (res / den), 0.0)
        o_ref[pl.ds(r0, tq), :] = acc
        return carry

    lax.fori_loop(0, q_ref.shape[0] // tq, q_block, 0)


def _attention(lam, qb, kb, vb, row0, n_seq, seq_len, layer=0, cache=None):
    sb0 = row0 // seq_len
    n_pair = D_QK // LANES
    smap = lambda b, p: (sb0 + b, p)
    smem = pl.BlockSpec(memory_space=pltpu.SMEM)
    in_specs = [smem] + [pl.BlockSpec((seq_len, LANES), smap)] * 3
    args = [lam, qb, kb, vb]
    if cache is not None:
        ck, cv = cache
        cmap = lambda b, p: (b, layer, 0, p)
        in_specs += [pl.BlockSpec((1, 1, ck.shape[2], LANES), cmap),
                     pl.BlockSpec((1, 1, cv.shape[2], LANES), cmap)]
        args += [ck, cv]
    return pl.pallas_call(
        functools.partial(_attn_kernel, with_cache=cache is not None),
        grid=(n_seq, n_pair),
        in_specs=in_specs,
        out_specs=pl.BlockSpec((seq_len, LANES), lambda b, p: (b, p)),
        out_shape=jax.ShapeDtypeStruct((n_seq * seq_len, D_V), F32),
        compiler_params=_params("arbitrary", "arbitrary"),
        name="attn_lat" if cache is not None else "attn_ctx",
    )(*args)


def _level_table(c):
    t = np.arange(c)[:, None] ^ np.arange(c)[None, :]
    lv = np.where(t == 0, 32, 31 - np.floor(np.log2(np.maximum(t, 1))).astype(np.int64))
    return np.tile(lv, (N_REC_HEADS, 1)).astype(np.int32)


def _retention_tables(log_gamma, c):
    g = jnp.repeat(log_gamma, D_HEAD, axis=1)
    t = jnp.arange(c, dtype=F32)
    diff = t[:, None] - t[None, :]
    gh = log_gamma[:, :, None, None]
    d_f = jnp.where(diff >= 0, jnp.exp(gh[0] * jnp.maximum(diff, 0.0)), 0.0)
    d_b = jnp.where(diff <= 0, jnp.exp(gh[1] * jnp.maximum(-diff, 0.0)), 0.0)
    d = jnp.stack([d_f, d_b]).reshape(2, N_REC_HEADS * c, c)
    q_f, k_f = jnp.exp(g[0] * (t[:, None] + 1.0)), jnp.exp(g[0] * (c - 1.0 - t[:, None]))
    q_b, k_b = jnp.exp(g[1] * (c - t[:, None])), jnp.exp(g[1] * t[:, None])
    f = jnp.stack([jnp.stack([q_f, k_f]), jnp.stack([q_b, k_b])])
    a = jnp.exp(g * c).reshape(2, 1, D_REC)
    return d, f, a


def _rec_kernel(*refs, state_in, state_out, aliased):
    refs = list(refs)
    rf_ref, rb_ref, rd_ref, rfac_ref, ra_ref, lb_ref, lv_ref = refs[:7]
    del refs[:7]
    if state_in:
        sret_ref, shg_ref, t4_ref = refs[:3]
        del refs[:3]
    if state_out:
        t4t_ref = refs.pop(0)
    del refs[:aliased]
    of_ref, ob_ref = refs[:2]
    del refs[:2]
    if state_out:
        oret_ref, ohg_ref = refs[:2]
        del refs[:2]
    (st_scr,) = refs

    ci = pl.program_id(1)
    c = rf_ref.shape[0]
    lane = lax.broadcasted_iota(I32, (1, D_REC), 1)
    head = lane >> 6
    row = lax.broadcasted_iota(I32, (c, 1), 0)
    srow = lax.broadcasted_iota(I32, (D_REC, 1), 0)
    same_head = (srow >> 6) == head
    eye = srow == lane
    head_is = [head == h for h in range(N_REC_HEADS)]
    head_m16 = [jnp.broadcast_to(jnp.where(m, 1.0, 0.0), (c, D_REC)).astype(BF16) for m in head_is]
    head_m16_half = [jnp.broadcast_to(jnp.where(m, 1.0, 0.0), (c // 2, D_REC)).astype(BF16) for m in head_is]

    @pl.when(ci == 0)
    def _():
        if state_in:
            for d in range(2):
                for idx, ref in ((d, sret_ref), (2 + d, shg_ref)):
                    tiled = functools.reduce(jnp.add, [_dot(p, t4_ref[...]) for p in _split_bf16(ref[0, 0, d], 3)])
                    st_scr[idx] = jnp.where(same_head, tiled, 0.0)
        else:
            st_scr[...] = jnp.zeros_like(st_scr)

    def expand(t16):
        return jnp.concatenate([t16 * m for m in head_m16], axis=0)

    def nt(a, b):
        return lax.dot_general(a, b, _NT, preferred_element_type=F32)

    def half_rows(t, h, second):
        o = h if second else 0
        return jnp.concatenate([t[j * 2 * h + o:j * 2 * h + o + h] for j in range(c // (2 * h))], axis=0)

    def spread_rows(t, h, second):
        z = jnp.zeros((h, t.shape[1]), t.dtype)
        parts = []
        for j in range(c // (2 * h)):
            parts += [z, t[j * h:(j + 1) * h]] if second else [t[j * h:(j + 1) * h], z]
        return jnp.concatenate(parts, axis=0)

    def finish(q, k, v16, scores, qfac, kfac, a_row, st_ref, own_term=False):
        st = st_ref[...]
        o = _dot((q * qfac).astype(BF16), st.astype(BF16))
        if own_term:
            o = o + _dot((q * k).astype(BF16), jnp.where(same_head, 1.0, 0.0).astype(BF16)) * v16.astype(F32)
        o_stack = _dot(scores.astype(BF16), v16)
        for h in range(N_REC_HEADS):
            o = o + jnp.where(head_is[h], o_stack[h * c:(h + 1) * c], 0.0)
        kv = lax.dot_general((k * kfac).astype(BF16), v16, _TN, preferred_element_type=F32)
        a_col = jnp.sum(jnp.where(eye, a_row, 0.0), axis=1, keepdims=True)
        st_ref[...] = st * a_col + jnp.where(same_head, kv, 0.0)
        return o

    def tree_scores(q, k, la, reverse):
        lv = lv_ref[...]
        scores = jnp.zeros(lv.shape, F32)
        pre, tot = la, la
        h, level = 1, 31
        while h < c:
            bit = (row & h) != 0
            query_side = jnp.logical_not(bit) if reverse else bit
            x = jnp.exp(jnp.minimum(jnp.where(query_side, pre, tot - pre), 0.0))
            qx = q * x
            kx = jnp.where(query_side, 0.0, k * x).astype(BF16)
            if h < 8:
                s_level = nt(expand(qx.astype(BF16)), kx)
            else:
                qc = half_rows(qx, h, not reverse).astype(BF16)
                sc = nt(jnp.concatenate([qc * m for m in head_m16_half], axis=0), kx)
                s_level = jnp.concatenate([spread_rows(sc[hh * c // 2:(hh + 1) * c // 2], h, not reverse)
                                           for hh in range(N_REC_HEADS)], axis=0)
            scores = jnp.where(lv == level, s_level, scores)
            partner = jnp.where(bit, pltpu.roll(tot, h, 0), pltpu.roll(tot, c - h, 0))
            pre = pre + jnp.where(query_side, partner, 0.0)
            tot = tot + partner
            h, level = 2 * h, level - 1
        return scores, pre, tot

    lb = lb_ref[...]
    col = lambda ref, j: ref[:, j * D_REC:(j + 1) * D_REC]
    for d, (r_ref, o_ref) in enumerate(((rf_ref, of_ref), (rb_ref, ob_ref))):
        q, k, v16 = col(r_ref, 0), col(r_ref, 1) * (RET_DK ** -0.5), col(r_ref, 2).astype(BF16)
        scores = nt(expand(q.astype(BF16)), k.astype(BF16)) * rd_ref[d]
        o_ref[:, 0:D_REC] = finish(q, k, v16, scores, rfac_ref[d, 0], rfac_ref[d, 1], ra_ref[d], st_scr.at[d])
        z = col(r_ref, 5 + d)
        la = jnp.log(jnp.maximum(lb + (1.0 - lb) * jax.nn.sigmoid(z), MIN_GATE))
        q, k, v16 = _silu(col(r_ref, 4)), (1.0 - lb) * jax.nn.sigmoid(-z), col(r_ref, 7).astype(BF16)
        scores, pre, tot = tree_scores(q, k, la, reverse=d == 1)
        o_ref[:, D_REC:2 * D_REC] = finish(q, k, v16, scores, jnp.exp(jnp.minimum(pre, 0.0)),
                                           jnp.exp(jnp.minimum(tot - pre, 0.0)), jnp.exp(tot[0:1, :]),
                                           st_scr.at[2 + d], own_term=True)

    if state_out:
        @pl.when(ci == pl.num_programs(1) - 1)
        def _():
            for d in range(2):
                for idx, ref in ((d, oret_ref), (2 + d, ohg_ref)):
                    ref[0, 0, d] = functools.reduce(
                        jnp.add, [_dot(p, t4t_ref[...]) for p in _split_bf16(st_scr[idx], 3)])
            if state_out == "first":
                oret_ref[0, 1:] = jnp.zeros_like(oret_ref[0, 1:])
                ohg_ref[0, 1:] = jnp.zeros_like(ohg_ref[0, 1:])


def _recurrence(rest, ret_tabs, lb_row, lv, row0, n_seq, seq_len, layer, *, states_in=None, want_states=False,
                states_out=None):
    c = REC_CHUNK
    nc = seq_len // c
    cb0 = row0 // c
    const2 = lambda b, i: (0, 0)
    const3 = lambda b, i: (0, 0, 0)
    rd, rfac, ra = ret_tabs
    in_specs = [pl.BlockSpec((c, D_REST), lambda b, i: (cb0 + b * nc + i, 0)),
                pl.BlockSpec((c, D_REST), lambda b, i: (cb0 + b * nc + nc - 1 - i, 0)),
                pl.BlockSpec((2, N_REC_HEADS * c, c), const3),
                pl.BlockSpec((2, 2, c, D_REC), lambda b, i: (0, 0, 0, 0)),
                pl.BlockSpec((2, 1, D_REC), const3),
                pl.BlockSpec((1, D_REC), const2),
                pl.BlockSpec((N_REC_HEADS * c, c), const2)]
    args = [rest, rest, rd, rfac, ra, lb_row, lv]
    tile4 = jnp.tile(jnp.eye(D_HEAD, dtype=BF16), (1, N_REC_HEADS))
    if states_in is not None:
        blk = pl.BlockSpec((1, 1, 2, D_REC, D_HEAD), lambda b, i: (b, layer, 0, 0, 0))
        in_specs += [blk, blk, pl.BlockSpec((D_HEAD, D_REC), const2)]
        args += [states_in[0], states_in[1], tile4]
    if want_states:
        in_specs += [pl.BlockSpec((D_REC, D_HEAD), const2)]
        args += [tile4.T]
    out_specs = [pl.BlockSpec((c, 2 * D_REC), lambda b, i: (b * nc + i, 0)),
                 pl.BlockSpec((c, 2 * D_REC), lambda b, i: (b * nc + nc - 1 - i, 0))]
    out_shape = [jax.ShapeDtypeStruct((n_seq * seq_len, 2 * D_REC), F32)] * 2
    aliases = {}
    state_mode = None
    if want_states:
        st_shape = jax.ShapeDtypeStruct((n_seq, DEPTH, 2, D_REC, D_HEAD), F32)
        out_shape += [st_shape, st_shape]
        if states_out is None:
            state_mode = "first"
            out_specs += [pl.BlockSpec((1, DEPTH, 2, D_REC, D_HEAD), lambda b, i: (b, 0, 0, 0, 0))] * 2
        else:
            state_mode = "next"
            out_specs += [pl.BlockSpec((1, 1, 2, D_REC, D_HEAD), lambda b, i: (b, layer, 0, 0, 0))] * 2
            aliases = {len(args): 2, len(args) + 1: 3}
            in_specs += [pl.BlockSpec(memory_space=pl.ANY)] * 2
            args += list(states_out)
    return pl.pallas_call(
        functools.partial(_rec_kernel, state_in=states_in is not None, state_out=state_mode,
                          aliased=len(aliases)),
        grid=(n_seq, nc),
        in_specs=in_specs,
        out_specs=out_specs,
        out_shape=out_shape,
        scratch_shapes=[pltpu.VMEM((4, D_REC, D_REC), F32)],
        input_output_aliases=aliases,
        compiler_params=_params("arbitrary", "arbitrary"),
        name="rec_lat" if states_in is not None else "rec_ctx",
    )(*args)


def _route(logits, bias):
    scores = jax.nn.sigmoid(logits)
    sel = scores + bias
    srow = [scores[e:e + 1, :] for e in range(N_EXPERTS)]
    rows = [sel[e:e + 1, :] for e in range(N_EXPERTS)]
    gs = []
    for g in range(N_GROUPS):
        a, b, c, d = rows[4 * g:4 * g + 4]
        gs.append(functools.reduce(jnp.maximum, [a + b, a + c, a + d, b + c, b + d, c + d]))
    best = jnp.zeros_like(gs[0], dtype=I32)
    best_v = gs[0]
    for g in range(1, N_GROUPS):
        upd = gs[g] > best_v
        best = jnp.where(upd, g, best)
        best_v = jnp.where(upd, gs[g], best_v)
    masked = [jnp.where(best == e // EXPERTS_PER_GROUP, rows[e], MASKED_SCORE) for e in range(N_EXPERTS)]
    i1 = jnp.zeros_like(best)
    v1 = masked[0]
    for e in range(1, N_EXPERTS):
        upd = masked[e] > v1
        i1 = jnp.where(upd, e, i1)
        v1 = jnp.where(upd, masked[e], v1)
    i2 = jnp.zeros_like(best)
    v2 = jnp.full_like(v1, -jnp.inf)
    for e in range(N_EXPERTS):
        upd = (masked[e] > v2) & (i1 != e)
        i2 = jnp.where(upd, e, i2)
        v2 = jnp.where(upd, masked[e], v2)
    w1 = functools.reduce(jnp.add, [jnp.where(i1 == e, srow[e], 0.0) for e in range(N_EXPERTS)])
    w2 = functools.reduce(jnp.add, [jnp.where(i2 == e, srow[e], 0.0) for e in range(N_EXPERTS)])
    tot = w1 + w2
    w1, w2 = w1 / tot, w2 / tot
    expert = lax.broadcasted_iota(I32, logits.shape, 0)
    return jnp.where(expert == i1, w1, 0.0) + jnp.where(expert == i2, w2, 0.0)


def _outproj_kernel(xa_ref, xb_ref, odaa_ref, odab_ref, ofa_ref, ofb_ref, oba_ref, obb_ref, rg_ref, gg_ref,
                    mod_ref, subln_ref, retn_ref, hgn_ref, g64_ref, wout_ref, nffn_ref, wrh_ref, wrl_ref, rb_ref,
                    xmid_ref, hn2_ref, gate_ref, rank_ref, count_ref, *, n_ctx_tiles):
    g64 = g64_ref[...]
    is_ctx = pl.program_id(0) < n_ctx_tiles
    pick = lambda a_ref, b_ref: jnp.where(is_ctx, a_ref[...], b_ref[...])

    def gnorm(t, w):
        return t * lax.rsqrt(_group_mean_sq(t, g64) + EPS) * w

    o_da = gnorm(pick(odaa_ref, odab_ref), subln_ref[...])
    o_rec = pick(ofa_ref, ofb_ref) + pick(oba_ref, obb_ref)
    o_ret = gnorm(o_rec[:, 0:D_REC], retn_ref[...]) * _silu(rg_ref[...])
    o_hg = gnorm(o_rec[:, D_REC:2 * D_REC], hgn_ref[...]) * _silu(gg_ref[...])
    mixed = jnp.concatenate([o_da, o_ret, o_hg], axis=1).astype(BF16)
    mod = mod_ref[0]
    x = pick(xa_ref, xb_ref) + mod[:, 2 * D_MODEL:3 * D_MODEL] * _dot(mixed, wout_ref[...])
    xmid_ref[...] = x
    y = x * lax.rsqrt(jnp.mean(x * x, axis=-1, keepdims=True) + EPS) * nffn_ref[...]
    hn = y * (1.0 + mod[:, 4 * D_MODEL:5 * D_MODEL]) + mod[:, 3 * D_MODEL:4 * D_MODEL]
    hi, lo = _split_bf16(hn)
    hn2_ref[...] = hi
    wrh = wrh_ref[...]
    logits = (lax.dot_general(wrh, hi, _NT, preferred_element_type=F32)
              + lax.dot_general(wrh, lo, _NT, preferred_element_type=F32)
              + lax.dot_general(wrl_ref[...], hi, _NT, preferred_element_type=F32))
    gate = _route(logits, rb_ref[...])
    gate_ref[...] = gate

    tm = gate.shape[1]
    member = jnp.concatenate(
        [jnp.max(gate[EXPERTS_PER_GROUP * g:EXPERTS_PER_GROUP * (g + 1)], axis=0, keepdims=True)
         for g in range(N_GROUPS)] + [jnp.zeros((8 - N_GROUPS, tm), F32)], axis=0)
    member = jnp.where(member > 0, 1.0, 0.0)
    earlier = (lax.broadcasted_iota(I32, (tm, tm), 0) < lax.broadcasted_iota(I32, (tm, tm), 1))
    before = _dot(member.astype(BF16), jnp.where(earlier, 1.0, 0.0).astype(BF16))

    @pl.when(pl.program_id(0) % (MOE_BLOCK // tm) == 0)
    def _():
        count_ref[...] = jnp.zeros_like(count_ref)

    rank_ref[...] = (before + count_ref[:, 0:1]).astype(I32)
    count_ref[...] += jnp.sum(member, axis=1, keepdims=True)


def _outproj(x, oda, o_f, o_b, rest, mod, subln, retn, hgn, g64, w_out16, norm_w, wr_hi, wr_lo, rbias, st):
    tm = st.tile
    t = st.n_tiles * tm
    row = lambda i: (i, 0)
    const = lambda i: (0, 0)
    pair = lambda width: [pl.BlockSpec((tm, width), lambda i: (st.ctx_idx(i), 0)),
                          pl.BlockSpec((tm, width), lambda i: (st.lat_idx(i), 0))]
    return pl.pallas_call(
        functools.partial(_outproj_kernel, n_ctx_tiles=st.n_ctx),
        grid=(st.n_tiles,),
        in_specs=pair(D_MODEL) + pair(D_V) + pair(2 * D_REC) + pair(2 * D_REC) + [
            pl.BlockSpec((tm, D_REC), lambda i: (i, 3)),
            pl.BlockSpec((tm, D_REC), lambda i: (i, 8)),
            pl.BlockSpec((1, 1, 6 * D_MODEL), lambda i: (st.mod_idx(i), 0, 0)),
            pl.BlockSpec((1, D_V), const),
            pl.BlockSpec((1, D_REC), const),
            pl.BlockSpec((1, D_REC), const),
            pl.BlockSpec((LANES, LANES), const),
            pl.BlockSpec((D_MODEL, D_MODEL), const, pipeline_mode=pl.Buffered(1)),
            pl.BlockSpec((1, D_MODEL), const),
            pl.BlockSpec((N_EXPERTS, D_MODEL), const),
            pl.BlockSpec((N_EXPERTS, D_MODEL), const),
            pl.BlockSpec((N_EXPERTS, 1), const)],
        out_specs=[pl.BlockSpec((tm, D_MODEL), row), pl.BlockSpec((tm, D_MODEL), row),
                   pl.BlockSpec((N_EXPERTS, tm), lambda i: (0, i)), pl.BlockSpec((8, tm), lambda i: (0, i))],
        out_shape=[jax.ShapeDtypeStruct((t, D_MODEL), F32), jax.ShapeDtypeStruct((t, D_MODEL), BF16),
                   jax.ShapeDtypeStruct((N_EXPERTS, t), F32), jax.ShapeDtypeStruct((8, t), I32)],
        scratch_shapes=[pltpu.VMEM((8, LANES), F32)],
        compiler_params=_params("arbitrary"),
        name="outproj",
    )(*x, *oda, *o_f, *o_b, rest, rest, mod, subln, retn, hgn, g64, w_out16, norm_w, wr_hi, wr_lo, rbias)


def _moe_plan(gate_t, rank, n_blocks):
    nb, tr, spb = MOE_BLOCK, MOE_TILE, MOE_SLOTS_PER_BLOCK
    n_slots = n_blocks * spb
    member = (gate_t.reshape(N_GROUPS, EXPERTS_PER_GROUP, -1) > 0).any(axis=1)
    mb = member.reshape(N_GROUPS, n_blocks, nb)
    mi = mb.astype(I32)
    rank = rank[:N_GROUPS].reshape(N_GROUPS, n_blocks, nb)
    tiles = (mi.sum(-1) + tr - 1) // tr
    t_end = jnp.cumsum(tiles, axis=0)
    t_off = t_end - tiles
    n_tiles = t_end[-1]
    dest = jnp.where(mb, t_off[..., None] * tr + rank, -1)
    order = jnp.cumsum(mi, axis=0) - mi
    n_mem = mi.sum(0)
    row_of = lambda k: jnp.where(n_mem > k, jnp.where(mb & (order == k), dest, 0).sum(0), -1)
    s = jnp.arange(n_slots, dtype=I32)
    s_blk, s_tile = s // spb, s % spb
    used = s_tile < n_tiles[s_blk]
    s_grp = (s_tile[None, :] >= t_end[:, s_blk]).sum(0).astype(I32)
    key = jnp.where(used, s_grp, N_GROUPS) * n_slots + s
    pos = (key[None, :] < key[:, None]).sum(1).astype(I32)
    slot_of_item = jnp.where(pos[None, :] == s[:, None], s[None, :], 0).sum(1).astype(I32)
    n_valid = used.sum().astype(I32)
    src = slot_of_item[jnp.minimum(s, n_valid - 1)]
    return dict(dest=dest.reshape(N_GROUPS * n_blocks, 1, nb),
                row1=row_of(0).reshape(n_blocks, 1, nb), row2=row_of(1).reshape(n_blocks, 1, nb),
                n_tiles=n_tiles.astype(I32), item_g=s_grp[src], item_b=s_blk[src], item_tile=s_tile[src],
                item_valid=(s < n_valid).astype(I32), item_slot=slot_of_item)


def _moe_ffn_kernel(ig_ref, ib_ref, it_ref, iv_ref, is_ref, h_ref, dest_ref, gate_ref, wg_ref, wu_ref, wd_ref,
                    y_ref, wg16, wu16, wd16):
    i = pl.program_id(0)
    g = ig_ref[i]
    prev_g = ig_ref[jnp.maximum(i - 1, 0)]

    @pl.when((i == 0) | (g != prev_g))
    def _():
        wg16[...] = wg_ref[...].astype(BF16)
        wu16[...] = wu_ref[...].astype(BF16)
        wd16[...] = wd_ref[...].astype(BF16)

    @pl.when(iv_ref[i] == 0)
    def _():
        y_ref[...] = jnp.zeros_like(y_ref)

    @pl.when(iv_ref[i] == 1)
    def _():
        rows = lax.broadcasted_iota(I32, (MOE_TILE, 1), 0) + it_ref[i] * MOE_TILE
        sort = jnp.where(dest_ref[0] == rows, 1.0, 0.0).astype(BF16)
        x = _dot(sort, h_ref[...]).astype(BF16)
        gates = functools.reduce(jnp.add, [lax.dot_general(sort, p, _NT, preferred_element_type=F32)
                                           for p in _split_bf16(gate_ref[...])])
        lane = lax.broadcasted_iota(I32, (1, N_EXPERTS), 1)
        acc = jnp.zeros((MOE_TILE, D_MODEL), F32)
        for e in range(EXPERTS_PER_GROUP):
            ge = jnp.sum(jnp.where(lane == g * EXPERTS_PER_GROUP + e, gates, 0.0), axis=1, keepdims=True)
            act = _silu(_dot(x, wg16[e])) * _dot(x, wu16[e])
            acc = acc + _dot((act * ge).astype(BF16), wd16[e])
        y_ref[...] = acc.astype(BF16)


def _moe_unsort_kernel(nt_ref, y_ref, r1_ref, r2_ref, x_ref, mod_ref, oa_ref, ob_ref, acc_ref, *, n_ctx_blocks):
    b = pl.program_id(0)
    acc_ref[...] = jnp.zeros_like(acc_ref)
    chunk = max(MOE_TILE, MXU_DEPTH)
    tiles_per_chunk = chunk // MOE_TILE
    row1, row2 = r1_ref[0], r2_ref[0]

    def body(ci, carry):
        r0 = pl.multiple_of(ci * chunk, chunk)
        rows = lax.broadcasted_iota(I32, (chunk, 1), 0) + r0
        sort = jnp.where(row1 == rows, 1.0, jnp.where(row2 == rows, 1.0, 0.0)).astype(BF16)
        acc_ref[...] += lax.dot_general(sort, y_ref[pl.ds(r0, chunk), :], _TN, preferred_element_type=F32)
        return carry

    lax.fori_loop(0, (nt_ref[b] + tiles_per_chunk - 1) // tiles_per_chunk, body, 0)
    result = lambda: x_ref[...] + mod_ref[0][:, 5 * D_MODEL:6 * D_MODEL] * acc_ref[...]

    @pl.when(b < n_ctx_blocks)
    def _():
        oa_ref[...] = result()

    @pl.when(b >= n_ctx_blocks)
    def _():
        ob_ref[...] = result()


def _moe(hn2, gate_t, rank, w_gate, w_up, w_down, x_mid, mod, st, layer):
    nb, tr, spb = st.tile, MOE_TILE, MOE_SLOTS_PER_BLOCK
    n_blocks = st.n_tiles
    plan = _moe_plan(gate_t, rank, n_blocks)

    wmap = lambda i, ig, ib, it, iv, isl: (layer * N_GROUPS + ig[i], 0, 0)
    once = pl.Buffered(1)
    y = pl.pallas_call(
        _moe_ffn_kernel,
        grid_spec=pltpu.PrefetchScalarGridSpec(
            num_scalar_prefetch=5,
            grid=(n_blocks * spb,),
            in_specs=[pl.BlockSpec((nb, D_MODEL), lambda i, ig, ib, it, iv, isl: (ib[i], 0)),
                      pl.BlockSpec((1, 1, nb), lambda i, ig, ib, it, iv, isl: (ig[i] * n_blocks + ib[i], 0, 0)),
                      pl.BlockSpec((N_EXPERTS, nb), lambda i, ig, ib, it, iv, isl: (0, ib[i])),
                      pl.BlockSpec((EXPERTS_PER_GROUP, D_MODEL, D_EXPERT), wmap, pipeline_mode=once),
                      pl.BlockSpec((EXPERTS_PER_GROUP, D_MODEL, D_EXPERT), wmap, pipeline_mode=once),
                      pl.BlockSpec((EXPERTS_PER_GROUP, D_EXPERT, D_MODEL), wmap, pipeline_mode=once)],
            out_specs=pl.BlockSpec((tr, D_MODEL), lambda i, ig, ib, it, iv, isl: (isl[i], 0)),
            scratch_shapes=[pltpu.VMEM((EXPERTS_PER_GROUP, D_MODEL, D_EXPERT), BF16),
                            pltpu.VMEM((EXPERTS_PER_GROUP, D_MODEL, D_EXPERT), BF16),
                            pltpu.VMEM((EXPERTS_PER_GROUP, D_EXPERT, D_MODEL), BF16)]),
        out_shape=jax.ShapeDtypeStruct((n_blocks * spb * tr, D_MODEL), BF16),
        compiler_params=_params("arbitrary"),
        name="moe_ffn",
    )(plan["item_g"], plan["item_b"], plan["item_tile"], plan["item_valid"], plan["item_slot"],
      hn2, plan["dest"], gate_t, w_gate, w_up, w_down)

    return pl.pallas_call(
        functools.partial(_moe_unsort_kernel, n_ctx_blocks=st.n_ctx),
        grid_spec=pltpu.PrefetchScalarGridSpec(
            num_scalar_prefetch=1,
            grid=(n_blocks,),
            in_specs=[pl.BlockSpec((spb * tr, D_MODEL), lambda b, nt: (b, 0)),
                      pl.BlockSpec((1, 1, nb), lambda b, nt: (b, 0, 0)),
                      pl.BlockSpec((1, 1, nb), lambda b, nt: (b, 0, 0)),
                      pl.BlockSpec((nb, D_MODEL), lambda b, nt: (b, 0)),
                      pl.BlockSpec((1, 1, 6 * D_MODEL), lambda b, nt: (st.mod_idx(b), 0, 0))],
            out_specs=[pl.BlockSpec((nb, D_MODEL), lambda b, nt: (st.ctx_idx(b), 0)),
                       pl.BlockSpec((nb, D_MODEL), lambda b, nt: (st.lat_idx(b), 0))],
            scratch_shapes=[pltpu.VMEM((nb, D_MODEL), F32)]),
        out_shape=[jax.ShapeDtypeStruct((st.n_ctx * nb, D_MODEL), F32),
                   jax.ShapeDtypeStruct(((n_blocks - st.n_ctx) * nb, D_MODEL), F32)],
        compiler_params=_params("arbitrary"),
        name="moe_unsort",
    )(plan["n_tiles"], y, plan["row1"], plan["row2"], x_mid, mod)


def _block_avg(group):
    i = np.arange(LANES)
    return jnp.asarray((i[:, None] // group == i[None, :] // group) / group, dtype=BF16)


def _rope_tables(n_tokens, n_identity):
    rows = n_tokens // GRID_W
    pos_r = jnp.repeat(jnp.arange(rows, dtype=F32), GRID_W)
    pos_c = jnp.tile(jnp.arange(GRID_W, dtype=F32), rows)
    n_freq = DA_QK // 4
    inv = ROPE_BASE ** (-jnp.arange(n_freq, dtype=F32) / n_freq)
    ang = jnp.concatenate([pos_r[:, None] * inv, pos_c[:, None] * inv], axis=-1)
    cos = jnp.repeat(jnp.cos(ang), 2, axis=-1)
    sin = jnp.repeat(jnp.sin(ang), 2, axis=-1) * jnp.tile(jnp.asarray([-1.0, 1.0], F32), DA_QK // 2)
    cos = jnp.concatenate([jnp.ones((n_identity, DA_QK), F32), cos], axis=0)
    sin = jnp.concatenate([jnp.zeros((n_identity, DA_QK), F32), sin], axis=0)
    return jnp.tile(cos, (1, 2 * H_DA)), jnp.tile(sin, (1, 2 * H_DA))


def kernel(x_prompt, x_sample, cache_k, cache_v, state_ret, state_hgrn, c, c_ctx, w_in, w_out, w_ada, b_ada,
           norm_mix, norm_ffn, da_qnorm, da_knorm, da_lambda, da_subln, ret_decay, ret_norm, hg_lb, hg_norm,
           w_router, router_bias, w_gate, w_up, w_down):
    n_ctx, l_ctx, _ = x_prompt.shape
    n_lat, l_lat, _ = x_sample.shape
    t_ctx = n_ctx * l_ctx
    past = cache_k.shape[2]
    assert TOKEN_TILE % l_ctx == 0 and MOE_BLOCK % TOKEN_TILE == 0
    assert l_lat % MOE_BLOCK == 0 and t_ctx % MOE_BLOCK == 0 and l_lat % GRID_W == 0

    ctx_row = n_lat
    n_cond = -(-(n_lat + 1) // 8) * 8
    cond = jnp.zeros((n_cond, D_MODEL), F32).at[:n_lat].set(c).at[ctx_row].set(c_ctx)
    mod_all = _ada(cond, w_ada, b_ada)

    st = _Stream(t_ctx, n_lat, l_lat, TOKEN_TILE, ctx_row)
    st_moe = _Stream(t_ctx, n_lat, l_lat, MOE_BLOCK, ctx_row)
    g32, g64 = _block_avg(DA_QK), _block_avg(DA_V)
    cos_t, sin_t = _rope_tables(l_lat, TOKEN_TILE)
    lv = jnp.asarray(_level_table(REC_CHUNK))
    p_lb = jax.nn.softmax(hg_lb.astype(F32), axis=0)
    lb_all = jnp.cumsum(p_lb, axis=0) - p_lb[0]
    wr_hi = w_router.T.astype(BF16)
    wr_lo = (w_router.T - wr_hi.astype(F32)).astype(BF16)
    rbias = router_bias.astype(F32).reshape(N_EXPERTS, 1)
    flat_w = lambda w: w.reshape(DEPTH * N_EXPERTS, w.shape[2], w.shape[3])
    w_gate, w_up, w_down = flat_w(w_gate), flat_w(w_up), flat_w(w_down)
    flat_s = lambda s: s.astype(F32).reshape(n_lat, DEPTH, 2, D_REC, D_HEAD)
    states_in = (flat_s(state_ret), flat_s(state_hgrn))
    cache_k16 = cache_k.reshape(n_lat, DEPTH, past, D_QK).astype(BF16)
    cache_v16 = cache_v.reshape(n_lat, DEPTH, past, D_V).astype(BF16)

    x = (x_prompt.reshape(t_ctx, D_MODEL), x_sample.reshape(n_lat * l_lat, D_MODEL))
    caches, states_out = None, None
    for l in range(DEPTH):
        lam_init = 0.8 - 0.6 * math.exp(-0.3 * l)
        lp = da_lambda[l].astype(F32)
        lam = (jnp.exp(jnp.sum(lp[0] * lp[1])) - jnp.exp(jnp.sum(lp[2] * lp[3])) + lam_init).reshape(1)
        mod = mod_all[l].reshape(n_cond, 1, 6 * D_MODEL)
        qw = (jnp.tile(da_qnorm[l].astype(F32), 2 * H_DA) * (DA_QK ** -0.5)).reshape(1, D_QK)
        kw = jnp.tile(da_knorm[l].astype(F32), 2 * H_DA).reshape(1, D_QK)

        qb, kb, vb, rest, kc, vc = _inproj(x[0], x[1], mod, norm_mix[l].reshape(1, D_MODEL),
                                           w_in[l].astype(BF16), qw, kw, g32, cos_t, sin_t, st, l, n_ctx, l_ctx,
                                           caches)
        caches = (kc, vc)

        oda = (_attention(lam, qb, kb, vb, 0, n_ctx, l_ctx),
               _attention(lam, qb, kb, vb, t_ctx, n_lat, l_lat, l, cache=(cache_k16, cache_v16)))

        ret_tabs = _retention_tables(-jnp.exp(ret_decay[l].astype(F32)), REC_CHUNK)
        lb_row = lb_all[l].reshape(1, D_REC)
        ofc, obc, s_ret, s_hg = _recurrence(rest, ret_tabs, lb_row, lv, 0, n_ctx, l_ctx, l,
                                            want_states=True, states_out=states_out)
        states_out = (s_ret, s_hg)
        ofl, obl = _recurrence(rest, ret_tabs, lb_row, lv, t_ctx, n_lat, l_lat, l, states_in=states_in)

        subln = (jnp.tile(da_subln[l].astype(F32), H_DA) * (1.0 - lam_init)).reshape(1, D_V)
        retn = jnp.tile(ret_norm[l].astype(F32), H_RET).reshape(1, D_REC)
        hgn = jnp.tile(hg_norm[l].astype(F32), H_HG).reshape(1, D_REC)
        x_mid, hn2, gate_t, rank = _outproj(x, oda, (ofc, ofl), (obc, obl), rest, mod, subln, retn, hgn, g64,
                                      w_out[l].astype(BF16), norm_ffn[l].reshape(1, D_MODEL), wr_hi, wr_lo,
                                      rbias, st)
        x = _moe(hn2, gate_t, rank, w_gate, w_up, w_down, x_mid, mod, st_moe, l)

    kc, vc = caches
    s_ret, s_hg = states_out
    return (x[0].reshape(n_ctx, l_ctx, D_MODEL), x[1].reshape(n_lat, l_lat, D_MODEL),
            kc.reshape(n_ctx, DEPTH, l_ctx, H_DA, 2, DA_QK), vc.reshape(n_ctx, DEPTH, l_ctx, H_DA, DA_V),
            s_ret.reshape(n_ctx, DEPTH, 2, H_RET, RET_DK, RET_DV),
            s_hg.reshape(n_ctx, DEPTH, 2, H_HG, HG_DK, HG_DV))
```

```python
import functools
import math

import numpy as np
import jax
import jax.numpy as jnp
from jax import lax
from jax.experimental import pallas as pl
from jax.experimental.pallas import tpu as pltpu

F32 = jnp.float32
BF16 = jnp.bfloat16
I32 = jnp.int32

D_MODEL = 1024
DEPTH = 2
GRID_W = 64
EPS = 1e-6
MIN_GATE = 1e-30
ROPE_BASE = 10000.0
H_DA, DA_QK, DA_V = 8, 32, 64
H_RET, RET_DK, RET_DV = 4, 64, 64
H_HG, HG_DK, HG_DV = 4, 64, 64
D_QK = H_DA * 2 * DA_QK
D_V = H_DA * DA_V
N_REC_HEADS = 4
D_HEAD = 64
D_REC = N_REC_HEADS * D_HEAD
D_REST = 9 * D_REC
D_IN = 2 * D_QK + D_V + D_REST
N_EXPERTS, N_GROUPS, EXPERTS_PER_GROUP = 16, 4, 4
D_EXPERT = 512
MASKED_SCORE = -2.0

LANES = 128
TOKEN_TILE = 512
Q_TILE = 1024
REC_CHUNK = 128
MOE_BLOCK = 1024
MOE_TILE = 288
MXU_DEPTH = 256
MOE_SLOTS_PER_BLOCK = -(-2 * MOE_BLOCK // MOE_TILE) + N_GROUPS
VMEM_LIMIT = 56 * 1024 * 1024

_NT = (((1,), (1,)), ((), ()))
_TN = (((0,), (0,)), ((), ()))


def _params(*sem):
    return pltpu.CompilerParams(dimension_semantics=sem, vmem_limit_bytes=VMEM_LIMIT)


def _dot(a, b):
    return jnp.dot(a, b, preferred_element_type=F32)


def _split_bf16(t, terms=2):
    out = []
    for _ in range(terms - 1):
        hi = t.astype(BF16)
        out.append(hi)
        t = t - hi.astype(F32)
    out.append(t.astype(BF16))
    return out


def _group_mean_sq(t, g):
    hi, lo = _split_bf16(t * t)
    outs = []
    for s in range(t.shape[1] // LANES):
        sl = slice(s * LANES, (s + 1) * LANES)
        outs.append(_dot(hi[:, sl], g) + _dot(lo[:, sl], g))
    return jnp.concatenate(outs, axis=1) if len(outs) > 1 else outs[0]


def _silu(t):
    return t * jax.nn.sigmoid(t)


class _Stream:
    def __init__(self, n_ctx_rows, n_lat_seq, lat_len, tile, ctx_row):
        self.tile = tile
        self.n_ctx = n_ctx_rows // tile
        self.per_seq = lat_len // tile
        self.n_tiles = self.n_ctx + n_lat_seq * self.per_seq
        self.ctx_row = ctx_row

    def ctx_idx(self, i):
        return jnp.minimum(i, self.n_ctx - 1)

    def lat_idx(self, i):
        return jnp.maximum(i - self.n_ctx, 0)

    def mod_idx(self, i):
        return jnp.where(i < self.n_ctx, self.ctx_row, (i - self.n_ctx) // self.per_seq)

    def pos_idx(self, i):
        return jnp.where(i < self.n_ctx, 0, 1 + (i - self.n_ctx) % self.per_seq)


def _ada_kernel(c_ref, w_ref, b_ref, o_ref):
    s = _silu(c_ref[...]).astype(BF16)
    o_ref[0] = _dot(s, w_ref[0].astype(BF16)) + b_ref[0]


def _ada(cond, w_ada, b_ada):
    rows = cond.shape[0]
    n_tile = 1536
    return pl.pallas_call(
        _ada_kernel,
        grid=(DEPTH, 6 * D_MODEL // n_tile),
        in_specs=[pl.BlockSpec((rows, D_MODEL), lambda l, j: (0, 0)),
                  pl.BlockSpec((1, D_MODEL, n_tile), lambda l, j: (l, 0, j)),
                  pl.BlockSpec((1, 1, n_tile), lambda l, j: (l, 0, j))],
        out_specs=pl.BlockSpec((1, rows, n_tile), lambda l, j: (l, 0, j)),
        out_shape=jax.ShapeDtypeStruct((DEPTH, rows, 6 * D_MODEL), F32),
        compiler_params=_params("arbitrary", "arbitrary"),
        name="ada",
    )(cond, w_ada, b_ada.reshape(DEPTH, 1, 6 * D_MODEL))


def _inproj_kernel(*refs, n_ctx_tiles, first):
    (xa_ref, xb_ref, mod_ref, nw_ref, w_ref, qw_ref, kw_ref, g32_ref, cos_ref, sin_ref) = refs[:10]
    q_ref, kb_ref, vb_ref, rest_ref, kc_ref, vc_ref = refs[10:] if first else refs[12:]
    is_ctx = pl.program_id(0) < n_ctx_tiles
    x = jnp.where(is_ctx, xa_ref[...], xb_ref[...])
    y = x * lax.rsqrt(jnp.mean(x * x, axis=-1, keepdims=True) + EPS) * nw_ref[...]
    mod = mod_ref[0]
    hn = (y * (1.0 + mod[:, D_MODEL:2 * D_MODEL]) + mod[:, 0:D_MODEL]).astype(BF16)
    g32 = g32_ref[...]
    lane = lax.broadcasted_iota(I32, (1, D_QK), 1)
    even = (lane & 1) == 0

    def qk_norm_rope(t, w):
        t = t * lax.rsqrt(_group_mean_sq(t, g32) + EPS) * w
        partner = jnp.where(even, pltpu.roll(t, D_QK - 1, 1), pltpu.roll(t, 1, 1))
        return t * cos_ref[...] + partner * sin_ref[...]

    q_ref[...] = qk_norm_rope(_dot(hn, w_ref[:, 0:D_QK]), qw_ref[...]).astype(BF16)
    k = qk_norm_rope(_dot(hn, w_ref[:, D_QK:2 * D_QK]), kw_ref[...])
    kb_ref[...] = k.astype(BF16)
    v = _dot(hn, w_ref[:, 2 * D_QK:2 * D_QK + D_V])
    vb_ref[...] = v.astype(BF16)
    rest_ref[...] = _dot(hn, w_ref[:, 2 * D_QK + D_V:D_IN])

    @pl.when(is_ctx)
    def _():
        n_seq, _, seq_len, _ = kc_ref.shape
        kc_ref[:, 0] = k.reshape(n_seq, seq_len, D_QK)
        vc_ref[:, 0] = v.reshape(n_seq, seq_len, D_V)
        if first:
            kc_ref[:, 1:] = jnp.zeros_like(kc_ref[:, 1:])
            vc_ref[:, 1:] = jnp.zeros_like(vc_ref[:, 1:])


def _inproj(xa, xb, mod, norm_w, w_in16, qw, kw, g32, cos_t, sin_t, st, layer, n_ctx_seq, l_ctx, caches):
    tm = st.tile
    t = st.n_tiles * tm
    first = caches is None
    row = lambda i: (i, 0)
    const = lambda i: (0, 0)
    in_specs = [pl.BlockSpec((tm, D_MODEL), lambda i: (st.ctx_idx(i), 0)),
                pl.BlockSpec((tm, D_MODEL), lambda i: (st.lat_idx(i), 0)),
                pl.BlockSpec((1, 1, 6 * D_MODEL), lambda i: (st.mod_idx(i), 0, 0)),
                pl.BlockSpec((1, D_MODEL), const),
                pl.BlockSpec((D_MODEL, D_IN), const, pipeline_mode=pl.Buffered(1)),
                pl.BlockSpec((1, D_QK), const),
                pl.BlockSpec((1, D_QK), const),
                pl.BlockSpec((LANES, LANES), const),
                pl.BlockSpec((tm, D_QK), lambda i: (st.pos_idx(i), 0)),
                pl.BlockSpec((tm, D_QK), lambda i: (st.pos_idx(i), 0))]
    args = [xa, xb, mod, norm_w, w_in16, qw, kw, g32, cos_t, sin_t]
    aliases = {}
    per_tile = tm // l_ctx
    if first:
        cache_spec = pl.BlockSpec((per_tile, DEPTH, l_ctx, D_QK), lambda i: (st.ctx_idx(i), 0, 0, 0))
    else:
        cache_spec = pl.BlockSpec((per_tile, 1, l_ctx, D_QK), lambda i: (st.ctx_idx(i), layer, 0, 0))
        in_specs += [pl.BlockSpec(memory_space=pl.ANY)] * 2
        args += list(caches)
        aliases = {10: 4, 11: 5}
    cache_shape = jax.ShapeDtypeStruct((n_ctx_seq, DEPTH, l_ctx, D_QK), F32)
    return pl.pallas_call(
        functools.partial(_inproj_kernel, n_ctx_tiles=st.n_ctx, first=first),
        grid=(st.n_tiles,),
        in_specs=in_specs,
        out_specs=[pl.BlockSpec((tm, D_QK), row), pl.BlockSpec((tm, D_QK), row),
                   pl.BlockSpec((tm, D_V), row), pl.BlockSpec((tm, D_REST), row), cache_spec, cache_spec],
        out_shape=[jax.ShapeDtypeStruct((t, D_QK), BF16), jax.ShapeDtypeStruct((t, D_QK), BF16),
                   jax.ShapeDtypeStruct((t, D_V), BF16), jax.ShapeDtypeStruct((t, D_REST), F32),
                   cache_shape, cache_shape],
        input_output_aliases=aliases,
        compiler_params=_params("arbitrary"),
        name="inproj",
    )(*args)


def _attn_kernel(*refs, with_cache):
    if with_cache:
        lam_ref, q_ref, k_ref, v_ref, ck_ref, cv_ref, o_ref = refs
    else:
        lam_ref, q_ref, k_ref, v_ref, o_ref = refs
    lam = lam_ref[0]
    lane = lax.broadcasted_iota(I32, (1, LANES), 1)
    left = lane < DA_V
    one = jnp.ones((), BF16)
    zero = jnp.zeros((), BF16)
    segs = [(k_ref[...], v_ref[...])]
    if with_cache:
        segs.append((ck_ref[0, 0], cv_ref[0, 0]))
    owns = [left, jnp.logical_not(left)]
    vms = [[jnp.where(own, v, one) for _, v in segs] for own in owns]
    tq = min(Q_TILE, q_ref.shape[0])

    def q_block(i, carry):
        r0 = pl.multiple_of(i * tq, tq)
        q = q_ref[pl.ds(r0, tq), :]
        acc = jnp.zeros((tq, LANES), F32)
        for side in range(2):
            for r in range(2):
                lo = side * DA_V + r * DA_QK
                qm = jnp.where((lane >= lo) & (lane < lo + DA_QK), q, zero)
                ss = [lax.dot_general(qm, k, _NT, preferred_element_type=F32) for k, _ in segs]
                m = functools.reduce(jnp.maximum, [jnp.max(s, axis=-1, keepdims=True) for s in ss])
                res = functools.reduce(
                    jnp.add, [_dot(jnp.exp(s - m).astype(BF16), vm) for s, vm in zip(ss, vms[side])])
                den = pltpu.roll(res, DA_V, 1)
                coef = 1.0 if r == 0 else -lam
                acc = acc + jnp.where(owns[side], coef * (res / den), 0.0)
        o_ref[pl.ds(r0, tq), :] = acc
        return carry

    lax.fori_loop(0, q_ref.shape[0] // tq, q_block, 0)


def _attention(lam, qb, kb, vb, row0, n_seq, seq_len, layer=0, cache=None):
    sb0 = row0 // seq_len
    n_pair = D_QK // LANES
    smap = lambda b, p: (sb0 + b, p)
    smem = pl.BlockSpec(memory_space=pltpu.SMEM)
    in_specs = [smem] + [pl.BlockSpec((seq_len, LANES), smap)] * 3
    args = [lam, qb, kb, vb]
    if cache is not None:
        ck, cv = cache
        cmap = lambda b, p: (b, layer, 0, p)
        in_specs += [pl.BlockSpec((1, 1, ck.shape[2], LANES), cmap),
                     pl.BlockSpec((1, 1, cv.shape[2], LANES), cmap)]
        args += [ck, cv]
    return pl.pallas_call(
        functools.partial(_attn_kernel, with_cache=cache is not None),
        grid=(n_seq, n_pair),
        in_specs=in_specs,
        out_specs=pl.BlockSpec((seq_len, LANES), lambda b, p: (b, p)),
        out_shape=jax.ShapeDtypeStruct((n_seq * seq_len, D_V), F32),
        compiler_params=_params("arbitrary", "arbitrary"),
        name="attn_lat" if cache is not None else "attn_ctx",
    )(*args)


def _level_table(c):
    t = np.arange(c)[:, None] ^ np.arange(c)[None, :]
    lv = np.where(t == 0, 32, 31 - np.floor(np.log2(np.maximum(t, 1))).astype(np.int64))
    return np.tile(lv, (N_REC_HEADS, 1)).astype(np.int32)


def _retention_tables(log_gamma, c):
    g = jnp.repeat(log_gamma, D_HEAD, axis=1)
    t = jnp.arange(c, dtype=F32)
    diff = t[:, None] - t[None, :]
    gh = log_gamma[:, :, None, None]
    d_f = jnp.where(diff >= 0, jnp.exp(gh[0] * jnp.maximum(diff, 0.0)), 0.0)
    d_b = jnp.where(diff <= 0, jnp.exp(gh[1] * jnp.maximum(-diff, 0.0)), 0.0)
    d = jnp.stack([d_f, d_b]).reshape(2, N_REC_HEADS * c, c)
    q_f, k_f = jnp.exp(g[0] * (t[:, None] + 1.0)), jnp.exp(g[0] * (c - 1.0 - t[:, None]))
    q_b, k_b = jnp.exp(g[1] * (c - t[:, None])), jnp.exp(g[1] * t[:, None])
    f = jnp.stack([jnp.stack([q_f, k_f]), jnp.stack([q_b, k_b])])
    a = jnp.exp(g * c).reshape(2, 1, D_REC)
    return d, f, a


def _rec_kernel(*refs, state_in, state_out, aliased):
    refs = list(refs)
    rf_ref, rb_ref, rd_ref, rfac_ref, ra_ref, lb_ref, lv_ref = refs[:7]
    del refs[:7]
    if state_in:
        sret_ref, shg_ref, t4_ref = refs[:3]
        del refs[:3]
    if state_out:
        t4t_ref = refs.pop(0)
    del refs[:aliased]
    of_ref, ob_ref = refs[:2]
    del refs[:2]
    if state_out:
        oret_ref, ohg_ref = refs[:2]
        del refs[:2]
    (st_scr,) = refs

    ci = pl.program_id(1)
    c = rf_ref.shape[0]
    lane = lax.broadcasted_iota(I32, (1, D_REC), 1)
    head = lane >> 6
    row = lax.broadcasted_iota(I32, (c, 1), 0)
    srow = lax.broadcasted_iota(I32, (D_REC, 1), 0)
    same_head = (srow >> 6) == head
    eye = srow == lane
    head_is = [head == h for h in range(N_REC_HEADS)]
    head_m16 = [jnp.broadcast_to(jnp.where(m, 1.0, 0.0), (c, D_REC)).astype(BF16) for m in head_is]
    head_m16_half = [jnp.broadcast_to(jnp.where(m, 1.0, 0.0), (c // 2, D_REC)).astype(BF16) for m in head_is]

    @pl.when(ci == 0)
    def _():
        if state_in:
            for d in range(2):
                for idx, ref in ((d, sret_ref), (2 + d, shg_ref)):
                    tiled = functools.reduce(jnp.add, [_dot(p, t4_ref[...]) for p in _split_bf16(ref[0, 0, d], 3)])
                    st_scr[idx] = jnp.where(same_head, tiled, 0.0)
        else:
            st_scr[...] = jnp.zeros_like(st_scr)

    def expand(t16):
        return jnp.concatenate([t16 * m for m in head_m16], axis=0)

    def nt(a, b):
        return lax.dot_general(a, b, _NT, preferred_element_type=F32)

    def half_rows(t, h, second):
        o = h if second else 0
        return jnp.concatenate([t[j * 2 * h + o:j * 2 * h + o + h] for j in range(c // (2 * h))], axis=0)

    def spread_rows(t, h, second):
        z = jnp.zeros((h, t.shape[1]), t.dtype)
        parts = []
        for j in range(c // (2 * h)):
            parts += [z, t[j * h:(j + 1) * h]] if second else [t[j * h:(j + 1) * h], z]
        return jnp.concatenate(parts, axis=0)

    def finish(q, k, v16, scores, qfac, kfac, a_row, st_ref, own_term=False):
        st = st_ref[...]
        o = _dot((q * qfac).astype(BF16), st.astype(BF16))
        if own_term:
            o = o + _dot((q * k).astype(BF16), jnp.where(same_head, 1.0, 0.0).astype(BF16)) * v16.astype(F32)
        o_stack = _dot(scores.astype(BF16), v16)
        for h in range(N_REC_HEADS):
            o = o + jnp.where(head_is[h], o_stack[h * c:(h + 1) * c], 0.0)
        kv = lax.dot_general((k * kfac).astype(BF16), v16, _TN, preferred_element_type=F32)
        a_col = jnp.sum(jnp.where(eye, a_row, 0.0), axis=1, keepdims=True)
        st_ref[...] = st * a_col + jnp.where(same_head, kv, 0.0)
        return o

    def tree_scores(q, k, la, reverse):
        lv = lv_ref[...]
        scores = jnp.zeros(lv.shape, F32)
        pre, tot = la, la
        h, level = 1, 31
        while h < c:
            bit = (row & h) != 0
            query_side = jnp.logical_not(bit) if reverse else bit
            x = jnp.exp(jnp.minimum(jnp.where(query_side, pre, tot - pre), 0.0))
            qx = q * x
            kx = jnp.where(query_side, 0.0, k * x).astype(BF16)
            if h < 8:
                s_level = nt(expand(qx.astype(BF16)), kx)
            else:
                qc = half_rows(qx, h, not reverse).astype(BF16)
                sc = nt(jnp.concatenate([qc * m for m in head_m16_half], axis=0), kx)
                s_level = jnp.concatenate([spread_rows(sc[hh * c // 2:(hh + 1) * c // 2], h, not reverse)
                                           for hh in range(N_REC_HEADS)], axis=0)
            scores = jnp.where(lv == level, s_level, scores)
            partner = jnp.where(bit, pltpu.roll(tot, h, 0), pltpu.roll(tot, c - h, 0))
            pre = pre + jnp.where(query_side, partner, 0.0)
            tot = tot + partner
            h, level = 2 * h, level - 1
        return scores, pre, tot

    lb = lb_ref[...]
    col = lambda ref, j: ref[:, j * D_REC:(j + 1) * D_REC]
    for d, (r_ref, o_ref) in enumerate(((rf_ref, of_ref), (rb_ref, ob_ref))):
        q, k, v16 = col(r_ref, 0), col(r_ref, 1) * (RET_DK ** -0.5), col(r_ref, 2).astype(BF16)
        scores = nt(expand(q.astype(BF16)), k.astype(BF16)) * rd_ref[d]
        o_ref[:, 0:D_REC] = finish(q, k, v16, scores, rfac_ref[d, 0], rfac_ref[d, 1], ra_ref[d], st_scr.at[d])
        z = col(r_ref, 5 + d)
        la = jnp.log(jnp.maximum(lb + (1.0 - lb) * jax.nn.sigmoid(z), MIN_GATE))
        q, k, v16 = _silu(col(r_ref, 4)), (1.0 - lb) * jax.nn.sigmoid(-z), col(r_ref, 7).astype(BF16)
        scores, pre, tot = tree_scores(q, k, la, reverse=d == 1)
        o_ref[:, D_REC:2 * D_REC] = finish(q, k, v16, scores, jnp.exp(jnp.minimum(pre, 0.0)),
                                           jnp.exp(jnp.minimum(tot - pre, 0.0)), jnp.exp(tot[0:1, :]),
                                           st_scr.at[2 + d], own_term=True)

    if state_out:
        @pl.when(ci == pl.num_programs(1) - 1)
        def _():
            for d in range(2):
                for idx, ref in ((d, oret_ref), (2 + d, ohg_ref)):
                    ref[0, 0, d] = functools.reduce(
                        jnp.add, [_dot(p, t4t_ref[...]) for p in _split_bf16(st_scr[idx], 3)])
            if state_out == "first":
                oret_ref[0, 1:] = jnp.zeros_like(oret_ref[0, 1:])
                ohg_ref[0, 1:] = jnp.zeros_like(ohg_ref[0, 1:])


def _recurrence(rest, ret_tabs, lb_row, lv, row0, n_seq, seq_len, layer, *, states_in=None, want_states=False,
                states_out=None):
    c = REC_CHUNK
    nc = seq_len // c
    cb0 = row0 // c
    const2 = lambda b, i: (0, 0)
    const3 = lambda b, i: (0, 0, 0)
    rd, rfac, ra = ret_tabs
    in_specs = [pl.BlockSpec((c, D_REST), lambda b, i: (cb0 + b * nc + i, 0)),
                pl.BlockSpec((c, D_REST), lambda b, i: (cb0 + b * nc + nc - 1 - i, 0)),
                pl.BlockSpec((2, N_REC_HEADS * c, c), const3),
                pl.BlockSpec((2, 2, c, D_REC), lambda b, i: (0, 0, 0, 0)),
                pl.BlockSpec((2, 1, D_REC), const3),
                pl.BlockSpec((1, D_REC), const2),
                pl.BlockSpec((N_REC_HEADS * c, c), const2)]
    args = [rest, rest, rd, rfac, ra, lb_row, lv]
    tile4 = jnp.tile(jnp.eye(D_HEAD, dtype=BF16), (1, N_REC_HEADS))
    if states_in is not None:
        blk = pl.BlockSpec((1, 1, 2, D_REC, D_HEAD), lambda b, i: (b, layer, 0, 0, 0))
        in_specs += [blk, blk, pl.BlockSpec((D_HEAD, D_REC), const2)]
        args += [states_in[0], states_in[1], tile4]
    if want_states:
        in_specs += [pl.BlockSpec((D_REC, D_HEAD), const2)]
        args += [tile4.T]
    out_specs = [pl.BlockSpec((c, 2 * D_REC), lambda b, i: (b * nc + i, 0)),
                 pl.BlockSpec((c, 2 * D_REC), lambda b, i: (b * nc + nc - 1 - i, 0))]
    out_shape = [jax.ShapeDtypeStruct((n_seq * seq_len, 2 * D_REC), F32)] * 2
    aliases = {}
    state_mode = None
    if want_states:
        st_shape = jax.ShapeDtypeStruct((n_seq, DEPTH, 2, D_REC, D_HEAD), F32)
        out_shape += [st_shape, st_shape]
        if states_out is None:
            state_mode = "first"
            out_specs += [pl.BlockSpec((1, DEPTH, 2, D_REC, D_HEAD), lambda b, i: (b, 0, 0, 0, 0))] * 2
        else:
            state_mode = "next"
            out_specs += [pl.BlockSpec((1, 1, 2, D_REC, D_HEAD), lambda b, i: (b, layer, 0, 0, 0))] * 2
            aliases = {len(args): 2, len(args) + 1: 3}
            in_specs += [pl.BlockSpec(memory_space=pl.ANY)] * 2
            args += list(states_out)
    return pl.pallas_call(
        functools.partial(_rec_kernel, state_in=states_in is not None, state_out=state_mode,
                          aliased=len(aliases)),
        grid=(n_seq, nc),
        in_specs=in_specs,
        out_specs=out_specs,
        out_shape=out_shape,
        scratch_shapes=[pltpu.VMEM((4, D_REC, D_REC), F32)],
        input_output_aliases=aliases,
        compiler_params=_params("arbitrary", "arbitrary"),
        name="rec_lat" if states_in is not None else "rec_ctx",
    )(*args)


def _route(logits, bias):
    scores = jax.nn.sigmoid(logits)
    sel = scores + bias
    srow = [scores[e:e + 1, :] for e in range(N_EXPERTS)]
    rows = [sel[e:e + 1, :] for e in range(N_EXPERTS)]
    gs = []
    for g in range(N_GROUPS):
        a, b, c, d = rows[4 * g:4 * g + 4]
        gs.append(functools.reduce(jnp.maximum, [a + b, a + c, a + d, b + c, b + d, c + d]))
    best = jnp.zeros_like(gs[0], dtype=I32)
    best_v = gs[0]
    for g in range(1, N_GROUPS):
        upd = gs[g] > best_v
        best = jnp.where(upd, g, best)
        best_v = jnp.where(upd, gs[g], best_v)
    masked = [jnp.where(best == e // EXPERTS_PER_GROUP, rows[e], MASKED_SCORE) for e in range(N_EXPERTS)]
    i1 = jnp.zeros_like(best)
    v1 = masked[0]
    for e in range(1, N_EXPERTS):
        upd = masked[e] > v1
        i1 = jnp.where(upd, e, i1)
        v1 = jnp.where(upd, masked[e], v1)
    i2 = jnp.zeros_like(best)
    v2 = jnp.full_like(v1, -jnp.inf)
    for e in range(N_EXPERTS):
        upd = (masked[e] > v2) & (i1 != e)
        i2 = jnp.where(upd, e, i2)
        v2 = jnp.where(upd, masked[e], v2)
    w1 = functools.reduce(jnp.add, [jnp.where(i1 == e, srow[e], 0.0) for e in range(N_EXPERTS)])
    w2 = functools.reduce(jnp.add, [jnp.where(i2 == e, srow[e], 0.0) for e in range(N_EXPERTS)])
    tot = w1 + w2
    w1, w2 = w1 / tot, w2 / tot
    expert = lax.broadcasted_iota(I32, logits.shape, 0)
    return jnp.where(expert == i1, w1, 0.0) + jnp.where(expert == i2, w2, 0.0)


def _outproj_kernel(xa_ref, xb_ref, odaa_ref, odab_ref, ofa_ref, ofb_ref, oba_ref, obb_ref, rg_ref, gg_ref,
                    mod_ref, subln_ref, retn_ref, hgn_ref, g64_ref, wout_ref, nffn_ref, wrh_ref, wrl_ref, rb_ref,
                    xmid_ref, hn2_ref, gate_ref, rank_ref, total_ref, count_ref, *, n_ctx_tiles):
    g64 = g64_ref[...]
    is_ctx = pl.program_id(0) < n_ctx_tiles
    pick = lambda a_ref, b_ref: jnp.where(is_ctx, a_ref[...], b_ref[...])

    def gnorm(t, w):
        return t * lax.rsqrt(_group_mean_sq(t, g64) + EPS) * w

    o_da = gnorm(pick(odaa_ref, odab_ref), subln_ref[...])
    o_rec = pick(ofa_ref, ofb_ref) + pick(oba_ref, obb_ref)
    o_ret = gnorm(o_rec[:, 0:D_REC], retn_ref[...]) * _silu(rg_ref[...])
    o_hg = gnorm(o_rec[:, D_REC:2 * D_REC], hgn_ref[...]) * _silu(gg_ref[...])
    mixed = jnp.concatenate([o_da, o_ret, o_hg], axis=1).astype(BF16)
    mod = mod_ref[0]
    x = pick(xa_ref, xb_ref) + mod[:, 2 * D_MODEL:3 * D_MODEL] * _dot(mixed, wout_ref[...])
    xmid_ref[...] = x
    y = x * lax.rsqrt(jnp.mean(x * x, axis=-1, keepdims=True) + EPS) * nffn_ref[...]
    hn = y * (1.0 + mod[:, 4 * D_MODEL:5 * D_MODEL]) + mod[:, 3 * D_MODEL:4 * D_MODEL]
    hi, lo = _split_bf16(hn)
    hn2_ref[...] = hi
    wrh = wrh_ref[...]
    logits = (lax.dot_general(wrh, hi, _NT, preferred_element_type=F32)
              + lax.dot_general(wrh, lo, _NT, preferred_element_type=F32)
              + lax.dot_general(wrl_ref[...], hi, _NT, preferred_element_type=F32))
    gate = _route(logits, rb_ref[...])
    gate_ref[...] = gate

    tm = gate.shape[1]
    member = jnp.concatenate(
        [jnp.max(gate[EXPERTS_PER_GROUP * g:EXPERTS_PER_GROUP * (g + 1)], axis=0, keepdims=True)
         for g in range(N_GROUPS)] + [jnp.zeros((8 - N_GROUPS, tm), F32)], axis=0)
    member = jnp.where(member > 0, 1.0, 0.0)
    earlier = (lax.broadcasted_iota(I32, (tm, tm), 0) < lax.broadcasted_iota(I32, (tm, tm), 1))
    before = _dot(member.astype(BF16), jnp.where(earlier, 1.0, 0.0).astype(BF16))

    @pl.when(pl.program_id(0) % (MOE_BLOCK // tm) == 0)
    def _():
        count_ref[...] = jnp.zeros_like(count_ref)

    rank_ref[...] = jnp.where(member > 0, before + count_ref[:, 0:1], -1.0).astype(I32)
    count_ref[...] += jnp.sum(member, axis=1, keepdims=True)
    total_ref[...] = count_ref[...]


def _outproj(x, oda, o_f, o_b, rest, mod, subln, retn, hgn, g64, w_out16, norm_w, wr_hi, wr_lo, rbias, st):
    tm = st.tile
    t = st.n_tiles * tm
    row = lambda i: (i, 0)
    const = lambda i: (0, 0)
    pair = lambda width: [pl.BlockSpec((tm, width), lambda i: (st.ctx_idx(i), 0)),
                          pl.BlockSpec((tm, width), lambda i: (st.lat_idx(i), 0))]
    return pl.pallas_call(
        functools.partial(_outproj_kernel, n_ctx_tiles=st.n_ctx),
        grid=(st.n_tiles,),
        in_specs=pair(D_MODEL) + pair(D_V) + pair(2 * D_REC) + pair(2 * D_REC) + [
            pl.BlockSpec((tm, D_REC), lambda i: (i, 3)),
            pl.BlockSpec((tm, D_REC), lambda i: (i, 8)),
            pl.BlockSpec((1, 1, 6 * D_MODEL), lambda i: (st.mod_idx(i), 0, 0)),
            pl.BlockSpec((1, D_V), const),
            pl.BlockSpec((1, D_REC), const),
            pl.BlockSpec((1, D_REC), const),
            pl.BlockSpec((LANES, LANES), const),
            pl.BlockSpec((D_MODEL, D_MODEL), const, pipeline_mode=pl.Buffered(1)),
            pl.BlockSpec((1, D_MODEL), const),
            pl.BlockSpec((N_EXPERTS, D_MODEL), const),
            pl.BlockSpec((N_EXPERTS, D_MODEL), const),
            pl.BlockSpec((N_EXPERTS, 1), const)],
        out_specs=[pl.BlockSpec((tm, D_MODEL), row), pl.BlockSpec((tm, D_MODEL), row),
                   pl.BlockSpec((N_EXPERTS, tm), lambda i: (0, i)), pl.BlockSpec((8, tm), lambda i: (0, i)),
                   pl.BlockSpec((8, LANES), lambda i: (0, i // (MOE_BLOCK // tm)))],
        out_shape=[jax.ShapeDtypeStruct((t, D_MODEL), F32), jax.ShapeDtypeStruct((t, D_MODEL), BF16),
                   jax.ShapeDtypeStruct((N_EXPERTS, t), F32), jax.ShapeDtypeStruct((8, t), I32),
                   jax.ShapeDtypeStruct((8, t // MOE_BLOCK * LANES), F32)],
        scratch_shapes=[pltpu.VMEM((8, LANES), F32)],
        compiler_params=_params("arbitrary"),
        name="outproj",
    )(*x, *oda, *o_f, *o_b, rest, rest, mod, subln, retn, hgn, g64, w_out16, norm_w, wr_hi, wr_lo, rbias)


def _moe_plan(rank, totals, n_blocks):
    nb, tr, spb = MOE_BLOCK, MOE_TILE, MOE_SLOTS_PER_BLOCK
    n_slots = n_blocks * spb
    rank = rank[:N_GROUPS].reshape(N_GROUPS, n_blocks, nb)
    mb = rank >= 0
    mi = mb.astype(I32)
    count = totals[:N_GROUPS].reshape(N_GROUPS, n_blocks, LANES)[:, :, 0].astype(I32)
    tiles = (count + tr - 1) // tr
    t_end = jnp.cumsum(tiles, axis=0)
    t_off = t_end - tiles
    n_tiles = t_end[-1]
    dest = jnp.where(mb, t_off[..., None] * tr + rank, -1)
    order = jnp.cumsum(mi, axis=0) - mi
    n_mem = mi.sum(0)
    row_of = lambda k: jnp.where(n_mem > k, jnp.where(mb & (order == k), dest, 0).sum(0), -1)
    s = jnp.arange(n_slots, dtype=I32)
    s_blk, s_tile = s // spb, s % spb
    used = s_tile < jnp.repeat(n_tiles, spb)
    s_grp = (s_tile[None, :] >= jnp.repeat(t_end, spb, axis=1)).sum(0).astype(I32)
    key = jnp.where(used, s_grp, N_GROUPS) * n_slots + s
    pos = (key[None, :] < key[:, None]).sum(1).astype(I32)
    slot_of_item = jnp.where(pos[None, :] == s[:, None], s[None, :], 0).sum(1).astype(I32)
    n_valid = used.sum().astype(I32)
    src = slot_of_item[jnp.minimum(s, n_valid - 1)]
    item_g, item_b, item_tile = jnp.stack([s_grp, s_blk, s_tile])[:, src]
    return dict(dest=dest.reshape(N_GROUPS * n_blocks, 1, nb),
                row1=row_of(0).reshape(n_blocks, 1, nb), row2=row_of(1).reshape(n_blocks, 1, nb),
                n_tiles=n_tiles.astype(I32), item_g=item_g, item_b=item_b, item_tile=item_tile,
                item_valid=(s < n_valid).astype(I32), item_slot=slot_of_item)


def _moe_ffn_kernel(ig_ref, ib_ref, it_ref, iv_ref, is_ref, h_ref, dest_ref, gate_ref, wg_ref, wu_ref, wd_ref,
                    y_ref, wg16, wu16, wd16):
    i = pl.program_id(0)
    g = ig_ref[i]
    prev_g = ig_ref[jnp.maximum(i - 1, 0)]

    @pl.when((i == 0) | (g != prev_g))
    def _():
        wg16[...] = wg_ref[...].astype(BF16)
        wu16[...] = wu_ref[...].astype(BF16)
        wd16[...] = wd_ref[...].astype(BF16)

    @pl.when(iv_ref[i] == 0)
    def _():
        y_ref[...] = jnp.zeros_like(y_ref)

    @pl.when(iv_ref[i] == 1)
    def _():
        rows = lax.broadcasted_iota(I32, (MOE_TILE, 1), 0) + it_ref[i] * MOE_TILE
        sort = jnp.where(dest_ref[0] == rows, 1.0, 0.0).astype(BF16)
        x = _dot(sort, h_ref[...]).astype(BF16)
        gates = functools.reduce(jnp.add, [lax.dot_general(sort, p, _NT, preferred_element_type=F32)
                                           for p in _split_bf16(gate_ref[...])])
        lane = lax.broadcasted_iota(I32, (1, N_EXPERTS), 1)
        acc = jnp.zeros((MOE_TILE, D_MODEL), F32)
        for e in range(EXPERTS_PER_GROUP):
            ge = jnp.sum(jnp.where(lane == g * EXPERTS_PER_GROUP + e, gates, 0.0), axis=1, keepdims=True)
            act = _silu(_dot(x, wg16[e])) * _dot(x, wu16[e])
            acc = acc + _dot((act * ge).astype(BF16), wd16[e])
        y_ref[...] = acc.astype(BF16)


def _moe_unsort_kernel(nt_ref, y_ref, r1_ref, r2_ref, x_ref, mod_ref, oa_ref, ob_ref, acc_ref, *, n_ctx_blocks):
    b = pl.program_id(0)
    acc_ref[...] = jnp.zeros_like(acc_ref)
    chunk = max(MOE_TILE, MXU_DEPTH)
    tiles_per_chunk = chunk // MOE_TILE
    row1, row2 = r1_ref[0], r2_ref[0]

    def body(ci, carry):
        r0 = pl.multiple_of(ci * chunk, chunk)
        rows = lax.broadcasted_iota(I32, (chunk, 1), 0) + r0
        sort = jnp.where(row1 == rows, 1.0, jnp.where(row2 == rows, 1.0, 0.0)).astype(BF16)
        acc_ref[...] += lax.dot_general(sort, y_ref[pl.ds(r0, chunk), :], _TN, preferred_element_type=F32)
        return carry

    lax.fori_loop(0, (nt_ref[b] + tiles_per_chunk - 1) // tiles_per_chunk, body, 0)
    result = lambda: x_ref[...] + mod_ref[0][:, 5 * D_MODEL:6 * D_MODEL] * acc_ref[...]

    @pl.when(b < n_ctx_blocks)
    def _():
        oa_ref[...] = result()

    @pl.when(b >= n_ctx_blocks)
    def _():
        ob_ref[...] = result()


def _moe(hn2, gate_t, rank, totals, w_gate, w_up, w_down, x_mid, mod, st, layer):
    nb, tr, spb = st.tile, MOE_TILE, MOE_SLOTS_PER_BLOCK
    n_blocks = st.n_tiles
    plan = _moe_plan(rank, totals, n_blocks)

    wmap = lambda i, ig, ib, it, iv, isl: (layer * N_GROUPS + ig[i], 0, 0)
    once = pl.Buffered(1)
    y = pl.pallas_call(
        _moe_ffn_kernel,
        grid_spec=pltpu.PrefetchScalarGridSpec(
            num_scalar_prefetch=5,
            grid=(n_blocks * spb,),
            in_specs=[pl.BlockSpec((nb, D_MODEL), lambda i, ig, ib, it, iv, isl: (ib[i], 0)),
                      pl.BlockSpec((1, 1, nb), lambda i, ig, ib, it, iv, isl: (ig[i] * n_blocks + ib[i], 0, 0)),
                      pl.BlockSpec((N_EXPERTS, nb), lambda i, ig, ib, it, iv, isl: (0, ib[i])),
                      pl.BlockSpec((EXPERTS_PER_GROUP, D_MODEL, D_EXPERT), wmap, pipeline_mode=once),
                      pl.BlockSpec((EXPERTS_PER_GROUP, D_MODEL, D_EXPERT), wmap, pipeline_mode=once),
                      pl.BlockSpec((EXPERTS_PER_GROUP, D_EXPERT, D_MODEL), wmap, pipeline_mode=once)],
            out_specs=pl.BlockSpec((tr, D_MODEL), lambda i, ig, ib, it, iv, isl: (isl[i], 0)),
            scratch_shapes=[pltpu.VMEM((EXPERTS_PER_GROUP, D_MODEL, D_EXPERT), BF16),
                            pltpu.VMEM((EXPERTS_PER_GROUP, D_MODEL, D_EXPERT), BF16),
                            pltpu.VMEM((EXPERTS_PER_GROUP, D_EXPERT, D_MODEL), BF16)]),
        out_shape=jax.ShapeDtypeStruct((n_blocks * spb * tr, D_MODEL), BF16),
        compiler_params=_params("arbitrary"),
        name="moe_ffn",
    )(plan["item_g"], plan["item_b"], plan["item_tile"], plan["item_valid"], plan["item_slot"],
      hn2, plan["dest"], gate_t, w_gate, w_up, w_down)

    return pl.pallas_call(
        functools.partial(_moe_unsort_kernel, n_ctx_blocks=st.n_ctx),
        grid_spec=pltpu.PrefetchScalarGridSpec(
            num_scalar_prefetch=1,
            grid=(n_blocks,),
            in_specs=[pl.BlockSpec((spb * tr, D_MODEL), lambda b, nt: (b, 0)),
                      pl.BlockSpec((1, 1, nb), lambda b, nt: (b, 0, 0)),
                      pl.BlockSpec((1, 1, nb), lambda b, nt: (b, 0, 0)),
                      pl.BlockSpec((nb, D_MODEL), lambda b, nt: (b, 0)),
                      pl.BlockSpec((1, 1, 6 * D_MODEL), lambda b, nt: (st.mod_idx(b), 0, 0))],
            out_specs=[pl.BlockSpec((nb, D_MODEL), lambda b, nt: (st.ctx_idx(b), 0)),
                       pl.BlockSpec((nb, D_MODEL), lambda b, nt: (st.lat_idx(b), 0))],
            scratch_shapes=[pltpu.VMEM((nb, D_MODEL), F32)]),
        out_shape=[jax.ShapeDtypeStruct((st.n_ctx * nb, D_MODEL), F32),
                   jax.ShapeDtypeStruct(((n_blocks - st.n_ctx) * nb, D_MODEL), F32)],
        compiler_params=_params("arbitrary"),
        name="moe_unsort",
    )(plan["n_tiles"], y, plan["row1"], plan["row2"], x_mid, mod)


def _block_avg(group):
    i = np.arange(LANES)
    return jnp.asarray((i[:, None] // group == i[None, :] // group) / group, dtype=BF16)


def _rope_tables(n_tokens, n_identity):
    rows = n_tokens // GRID_W
    pos_r = jnp.repeat(jnp.arange(rows, dtype=F32), GRID_W)
    pos_c = jnp.tile(jnp.arange(GRID_W, dtype=F32), rows)
    n_freq = DA_QK // 4
    inv = ROPE_BASE ** (-jnp.arange(n_freq, dtype=F32) / n_freq)
    ang = jnp.concatenate([pos_r[:, None] * inv, pos_c[:, None] * inv], axis=-1)
    cos = jnp.repeat(jnp.cos(ang), 2, axis=-1)
    sin = jnp.repeat(jnp.sin(ang), 2, axis=-1) * jnp.tile(jnp.asarray([-1.0, 1.0], F32), DA_QK // 2)
    cos = jnp.concatenate([jnp.ones((n_identity, DA_QK), F32), cos], axis=0)
    sin = jnp.concatenate([jnp.zeros((n_identity, DA_QK), F32), sin], axis=0)
    return jnp.tile(cos, (1, 2 * H_DA)), jnp.tile(sin, (1, 2 * H_DA))


def kernel(x_prompt, x_sample, cache_k, cache_v, state_ret, state_hgrn, c, c_ctx, w_in, w_out, w_ada, b_ada,
           norm_mix, norm_ffn, da_qnorm, da_knorm, da_lambda, da_subln, ret_decay, ret_norm, hg_lb, hg_norm,
           w_router, router_bias, w_gate, w_up, w_down):
    n_ctx, l_ctx, _ = x_prompt.shape
    n_lat, l_lat, _ = x_sample.shape
    t_ctx = n_ctx * l_ctx
    past = cache_k.shape[2]
    assert TOKEN_TILE % l_ctx == 0 and MOE_BLOCK % TOKEN_TILE == 0
    assert l_lat % MOE_BLOCK == 0 and t_ctx % MOE_BLOCK == 0 and l_lat % GRID_W == 0

    ctx_row = n_lat
    n_cond = -(-(n_lat + 1) // 8) * 8
    cond = jnp.zeros((n_cond, D_MODEL), F32).at[:n_lat].set(c).at[ctx_row].set(c_ctx)
    mod_all = _ada(cond, w_ada, b_ada)

    st = _Stream(t_ctx, n_lat, l_lat, TOKEN_TILE, ctx_row)
    st_moe = _Stream(t_ctx, n_lat, l_lat, MOE_BLOCK, ctx_row)
    g32, g64 = _block_avg(DA_QK), _block_avg(DA_V)
    cos_t, sin_t = _rope_tables(l_lat, TOKEN_TILE)
    lv = jnp.asarray(_level_table(REC_CHUNK))
    p_lb = jax.nn.softmax(hg_lb.astype(F32), axis=0)
    lb_all = jnp.cumsum(p_lb, axis=0) - p_lb[0]
    wr_hi = w_router.T.astype(BF16)
    wr_lo = (w_router.T - wr_hi.astype(F32)).astype(BF16)
    rbias = router_bias.astype(F32).reshape(N_EXPERTS, 1)
    flat_w = lambda w: w.reshape(DEPTH * N_EXPERTS, w.shape[2], w.shape[3])
    w_gate, w_up, w_down = flat_w(w_gate), flat_w(w_up), flat_w(w_down)
    flat_s = lambda s: s.astype(F32).reshape(n_lat, DEPTH, 2, D_REC, D_HEAD)
    states_in = (flat_s(state_ret), flat_s(state_hgrn))
    cache_k16 = cache_k.reshape(n_lat, DEPTH, past, D_QK).astype(BF16)
    cache_v16 = cache_v.reshape(n_lat, DEPTH, past, D_V).astype(BF16)

    x = (x_prompt.reshape(t_ctx, D_MODEL), x_sample.reshape(n_lat * l_lat, D_MODEL))
    caches, states_out = None, None
    for l in range(DEPTH):
        lam_init = 0.8 - 0.6 * math.exp(-0.3 * l)
        lp = da_lambda[l].astype(F32)
        lam = (jnp.exp(jnp.sum(lp[0] * lp[1])) - jnp.exp(jnp.sum(lp[2] * lp[3])) + lam_init).reshape(1)
        mod = mod_all[l].reshape(n_cond, 1, 6 * D_MODEL)
        qw = (jnp.tile(da_qnorm[l].astype(F32), 2 * H_DA) * (DA_QK ** -0.5)).reshape(1, D_QK)
        kw = jnp.tile(da_knorm[l].astype(F32), 2 * H_DA).reshape(1, D_QK)

        qb, kb, vb, rest, kc, vc = _inproj(x[0], x[1], mod, norm_mix[l].reshape(1, D_MODEL),
                                           w_in[l].astype(BF16), qw, kw, g32, cos_t, sin_t, st, l, n_ctx, l_ctx,
                                           caches)
        caches = (kc, vc)

        oda = (_attention(lam, qb, kb, vb, 0, n_ctx, l_ctx),
               _attention(lam, qb, kb, vb, t_ctx, n_lat, l_lat, l, cache=(cache_k16, cache_v16)))

        ret_tabs = _retention_tables(-jnp.exp(ret_decay[l].astype(F32)), REC_CHUNK)
        lb_row = lb_all[l].reshape(1, D_REC)
        ofc, obc, s_ret, s_hg = _recurrence(rest, ret_tabs, lb_row, lv, 0, n_ctx, l_ctx, l,
                                            want_states=True, states_out=states_out)
        states_out = (s_ret, s_hg)
        ofl, obl = _recurrence(rest, ret_tabs, lb_row, lv, t_ctx, n_lat, l_lat, l, states_in=states_in)

        subln = (jnp.tile(da_subln[l].astype(F32), H_DA) * (1.0 - lam_init)).reshape(1, D_V)
        retn = jnp.tile(ret_norm[l].astype(F32), H_RET).reshape(1, D_REC)
        hgn = jnp.tile(hg_norm[l].astype(F32), H_HG).reshape(1, D_REC)
        x_mid, hn2, gate_t, rank, totals = _outproj(x, oda, (ofc, ofl), (obc, obl), rest, mod, subln, retn, hgn, g64,
                                      w_out[l].astype(BF16), norm_ffn[l].reshape(1, D_MODEL), wr_hi, wr_lo,
                                      rbias, st)
        x = _moe(hn2, gate_t, rank, totals, w_gate, w_up, w_down, x_mid, mod, st_moe, l)

    kc, vc = caches
    s_ret, s_hg = states_out
    return (x[0].reshape(n_ctx, l_ctx, D_MODEL), x[1].reshape(n_lat, l_lat, D_MODEL),
            kc.reshape(n_ctx, DEPTH, l_ctx, H_DA, 2, DA_QK), vc.reshape(n_ctx, DEPTH, l_ctx, H_DA, DA_V),
            s_ret.reshape(n_ctx, DEPTH, 2, H_RET, RET_DK, RET_DV),
            s_hg.reshape(n_ctx, DEPTH, 2, H_HG, HG_DK, HG_DV))
```

```python
import functools
import math

import numpy as np
import jax
import jax.numpy as jnp
from jax import lax
from jax.experimental import pallas as pl
from jax.experimental.pallas import tpu as pltpu

F32 = jnp.float32
BF16 = jnp.bfloat16
I32 = jnp.int32

D_MODEL = 1024
DEPTH = 2
GRID_W = 64
EPS = 1e-6
MIN_GATE = 1e-30
ROPE_BASE = 10000.0
H_DA, DA_QK, DA_V = 8, 32, 64
H_RET, RET_DK, RET_DV = 4, 64, 64
H_HG, HG_DK, HG_DV = 4, 64, 64
D_QK = H_DA * 2 * DA_QK
D_V = H_DA * DA_V
N_REC_HEADS = 4
D_HEAD = 64
D_REC = N_REC_HEADS * D_HEAD
D_REST = 9 * D_REC
D_IN = 2 * D_QK + D_V + D_REST
N_EXPERTS, N_GROUPS, EXPERTS_PER_GROUP = 16, 4, 4
D_EXPERT = 512
MASKED_SCORE = -2.0

LANES = 128
TOKEN_TILE = 512
Q_TILE = 1024
REC_CHUNK = 128
MOE_BLOCK = 1024
MOE_TILE = 256
MXU_DEPTH = 256
MOE_SLOTS_PER_BLOCK = -(-2 * MOE_BLOCK // MOE_TILE) + N_GROUPS
VMEM_LIMIT = 56 * 1024 * 1024

_NT = (((1,), (1,)), ((), ()))
_TN = (((0,), (0,)), ((), ()))


def _params(*sem):
    return pltpu.CompilerParams(dimension_semantics=sem, vmem_limit_bytes=VMEM_LIMIT)


def _dot(a, b):
    return jnp.dot(a, b, preferred_element_type=F32)


def _split_bf16(t, terms=2):
    out = []
    for _ in range(terms - 1):
        hi = t.astype(BF16)
        out.append(hi)
        t = t - hi.astype(F32)
    out.append(t.astype(BF16))
    return out


def _group_mean_sq(t, g):
    hi, lo = _split_bf16(t * t)
    outs = []
    for s in range(t.shape[1] // LANES):
        sl = slice(s * LANES, (s + 1) * LANES)
        outs.append(_dot(hi[:, sl], g) + _dot(lo[:, sl], g))
    return jnp.concatenate(outs, axis=1) if len(outs) > 1 else outs[0]


def _silu(t):
    return t * jax.nn.sigmoid(t)


class _Stream:
    def __init__(self, n_ctx_rows, n_lat_seq, lat_len, tile, ctx_row):
        self.tile = tile
        self.n_ctx = n_ctx_rows // tile
        self.per_seq = lat_len // tile
        self.n_tiles = self.n_ctx + n_lat_seq * self.per_seq
        self.ctx_row = ctx_row

    def ctx_idx(self, i):
        return jnp.minimum(i, self.n_ctx - 1)

    def lat_idx(self, i):
        return jnp.maximum(i - self.n_ctx, 0)

    def mod_idx(self, i):
        return jnp.where(i < self.n_ctx, self.ctx_row, (i - self.n_ctx) // self.per_seq)

    def pos_idx(self, i):
        return jnp.where(i < self.n_ctx, 0, 1 + (i - self.n_ctx) % self.per_seq)


def _ada_kernel(c_ref, w_ref, b_ref, o_ref):
    s = _silu(c_ref[...]).astype(BF16)
    o_ref[0] = _dot(s, w_ref[0].astype(BF16)) + b_ref[0]


def _ada(cond, w_ada, b_ada):
    rows = cond.shape[0]
    n_tile = 1536
    return pl.pallas_call(
        _ada_kernel,
        grid=(DEPTH, 6 * D_MODEL // n_tile),
        in_specs=[pl.BlockSpec((rows, D_MODEL), lambda l, j: (0, 0)),
                  pl.BlockSpec((1, D_MODEL, n_tile), lambda l, j: (l, 0, j)),
                  pl.BlockSpec((1, 1, n_tile), lambda l, j: (l, 0, j))],
        out_specs=pl.BlockSpec((1, rows, n_tile), lambda l, j: (l, 0, j)),
        out_shape=jax.ShapeDtypeStruct((DEPTH, rows, 6 * D_MODEL), F32),
        compiler_params=_params("arbitrary", "arbitrary"),
        name="ada",
    )(cond, w_ada, b_ada.reshape(DEPTH, 1, 6 * D_MODEL))


def _inproj_kernel(*refs, n_ctx_tiles, first):
    (xa_ref, xb_ref, mod_ref, nw_ref, w_ref, qw_ref, kw_ref, g32_ref, cos_ref, sin_ref) = refs[:10]
    q_ref, kb_ref, vb_ref, rest_ref, kc_ref, vc_ref = refs[10:] if first else refs[12:]
    is_ctx = pl.program_id(0) < n_ctx_tiles
    x = jnp.where(is_ctx, xa_ref[...], xb_ref[...])
    y = x * lax.rsqrt(jnp.mean(x * x, axis=-1, keepdims=True) + EPS) * nw_ref[...]
    mod = mod_ref[0]
    hn = (y * (1.0 + mod[:, D_MODEL:2 * D_MODEL]) + mod[:, 0:D_MODEL]).astype(BF16)
    g32 = g32_ref[...]
    lane = lax.broadcasted_iota(I32, (1, D_QK), 1)
    even = (lane & 1) == 0

    def qk_norm_rope(t, w):
        t = t * lax.rsqrt(_group_mean_sq(t, g32) + EPS) * w
        partner = jnp.where(even, pltpu.roll(t, D_QK - 1, 1), pltpu.roll(t, 1, 1))
        return t * cos_ref[...] + partner * sin_ref[...]

    q_ref[...] = qk_norm_rope(_dot(hn, w_ref[:, 0:D_QK]), qw_ref[...]).astype(BF16)
    k = qk_norm_rope(_dot(hn, w_ref[:, D_QK:2 * D_QK]), kw_ref[...])
    kb_ref[...] = k.astype(BF16)
    v = _dot(hn, w_ref[:, 2 * D_QK:2 * D_QK + D_V])
    vb_ref[...] = v.astype(BF16)
    rest_ref[...] = _dot(hn, w_ref[:, 2 * D_QK + D_V:D_IN])

    @pl.when(is_ctx)
    def _():
        n_seq, _, seq_len, _ = kc_ref.shape
        kc_ref[:, 0] = k.reshape(n_seq, seq_len, D_QK)
        vc_ref[:, 0] = v.reshape(n_seq, seq_len, D_V)
        if first:
            kc_ref[:, 1:] = jnp.zeros_like(kc_ref[:, 1:])
            vc_ref[:, 1:] = jnp.zeros_like(vc_ref[:, 1:])


def _inproj(xa, xb, mod, norm_w, w_in16, qw, kw, g32, cos_t, sin_t, st, layer, n_ctx_seq, l_ctx, caches):
    tm = st.tile
    t = st.n_tiles * tm
    first = caches is None
    row = lambda i: (i, 0)
    const = lambda i: (0, 0)
    in_specs = [pl.BlockSpec((tm, D_MODEL), lambda i: (st.ctx_idx(i), 0)),
                pl.BlockSpec((tm, D_MODEL), lambda i: (st.lat_idx(i), 0)),
                pl.BlockSpec((1, 1, 6 * D_MODEL), lambda i: (st.mod_idx(i), 0, 0)),
                pl.BlockSpec((1, D_MODEL), const),
                pl.BlockSpec((D_MODEL, D_IN), const, pipeline_mode=pl.Buffered(1)),
                pl.BlockSpec((1, D_QK), const),
                pl.BlockSpec((1, D_QK), const),
                pl.BlockSpec((LANES, LANES), const),
                pl.BlockSpec((tm, D_QK), lambda i: (st.pos_idx(i), 0)),
                pl.BlockSpec((tm, D_QK), lambda i: (st.pos_idx(i), 0))]
    args = [xa, xb, mod, norm_w, w_in16, qw, kw, g32, cos_t, sin_t]
    aliases = {}
    per_tile = tm // l_ctx
    if first:
        cache_spec = pl.BlockSpec((per_tile, DEPTH, l_ctx, D_QK), lambda i: (st.ctx_idx(i), 0, 0, 0))
    else:
        cache_spec = pl.BlockSpec((per_tile, 1, l_ctx, D_QK), lambda i: (st.ctx_idx(i), layer, 0, 0))
        in_specs += [pl.BlockSpec(memory_space=pl.ANY)] * 2
        args += list(caches)
        aliases = {10: 4, 11: 5}
    cache_shape = jax.ShapeDtypeStruct((n_ctx_seq, DEPTH, l_ctx, D_QK), F32)
    return pl.pallas_call(
        functools.partial(_inproj_kernel, n_ctx_tiles=st.n_ctx, first=first),
        grid=(st.n_tiles,),
        in_specs=in_specs,
        out_specs=[pl.BlockSpec((tm, D_QK), row), pl.BlockSpec((tm, D_QK), row),
                   pl.BlockSpec((tm, D_V), row), pl.BlockSpec((tm, D_REST), row), cache_spec, cache_spec],
        out_shape=[jax.ShapeDtypeStruct((t, D_QK), BF16), jax.ShapeDtypeStruct((t, D_QK), BF16),
                   jax.ShapeDtypeStruct((t, D_V), BF16), jax.ShapeDtypeStruct((t, D_REST), F32),
                   cache_shape, cache_shape],
        input_output_aliases=aliases,
        compiler_params=_params("arbitrary"),
        name="inproj",
    )(*args)


def _attn_kernel(*refs, with_cache):
    if with_cache:
        lam_ref, q_ref, k_ref, v_ref, ck_ref, cv_ref, o_ref = refs
    else:
        lam_ref, q_ref, k_ref, v_ref, o_ref = refs
    lam = lam_ref[0]
    lane = lax.broadcasted_iota(I32, (1, LANES), 1)
    left = lane < DA_V
    one = jnp.ones((), BF16)
    zero = jnp.zeros((), BF16)
    segs = [(k_ref[...], v_ref[...])]
    if with_cache:
        segs.append((ck_ref[0, 0], cv_ref[0, 0]))
    owns = [left, jnp.logical_not(left)]
    vms = [[jnp.where(own, v, one) for _, v in segs] for own in owns]
    tq = min(Q_TILE, q_ref.shape[0])

    def q_block(i, carry):
        r0 = pl.multiple_of(i * tq, tq)
        q = q_ref[pl.ds(r0, tq), :]
        acc = jnp.zeros((tq, LANES), F32)
        for side in range(2):
            for r in range(2):
                lo = side * DA_V + r * DA_QK
                qm = jnp.where((lane >= lo) & (lane < lo + DA_QK), q, zero)
                ss = [lax.dot_general(qm, k, _NT, preferred_element_type=F32) for k, _ in segs]
                m = functools.reduce(jnp.maximum, [jnp.max(s, axis=-1, keepdims=True) for s in ss])
                res = functools.reduce(
                    jnp.add, [_dot(jnp.exp(s - m).astype(BF16), vm) for s, vm in zip(ss, vms[side])])
                den = pltpu.roll(res, DA_V, 1)
                coef = 1.0 if r == 0 else -lam
                acc = acc + jnp.where(owns[side], coef * (res / den), 0.0)
        o_ref[pl.ds(r0, tq), :] = acc
        return carry

    lax.fori_loop(0, q_ref.shape[0] // tq, q_block, 0)


def _attention(lam, qb, kb, vb, row0, n_seq, seq_len, layer=0, cache=None):
    sb0 = row0 // seq_len
    n_pair = D_QK // LANES
    smap = lambda b, p: (sb0 + b, p)
    smem = pl.BlockSpec(memory_space=pltpu.SMEM)
    in_specs = [smem] + [pl.BlockSpec((seq_len, LANES), smap)] * 3
    args = [lam, qb, kb, vb]
    if cache is not None:
        ck, cv = cache
        cmap = lambda b, p: (b, layer, 0, p)
        in_specs += [pl.BlockSpec((1, 1, ck.shape[2], LANES), cmap),
                     pl.BlockSpec((1, 1, cv.shape[2], LANES), cmap)]
        args += [ck, cv]
    return pl.pallas_call(
        functools.partial(_attn_kernel, with_cache=cache is not None),
        grid=(n_seq, n_pair),
        in_specs=in_specs,
        out_specs=pl.BlockSpec((seq_len, LANES), lambda b, p: (b, p)),
        out_shape=jax.ShapeDtypeStruct((n_seq * seq_len, D_V), F32),
        compiler_params=_params("arbitrary", "arbitrary"),
        name="attn_lat" if cache is not None else "attn_ctx",
    )(*args)


def _level_table(c):
    t = np.arange(c)[:, None] ^ np.arange(c)[None, :]
    lv = np.where(t == 0, 32, 31 - np.floor(np.log2(np.maximum(t, 1))).astype(np.int64))
    return np.tile(lv, (N_REC_HEADS, 1)).astype(np.int32)


def _retention_tables(log_gamma, c):
    g = jnp.repeat(log_gamma, D_HEAD, axis=1)
    t = jnp.arange(c, dtype=F32)
    diff = t[:, None] - t[None, :]
    gh = log_gamma[:, :, None, None]
    d_f = jnp.where(diff >= 0, jnp.exp(gh[0] * jnp.maximum(diff, 0.0)), 0.0)
    d_b = jnp.where(diff <= 0, jnp.exp(gh[1] * jnp.maximum(-diff, 0.0)), 0.0)
    d = jnp.stack([d_f, d_b]).reshape(2, N_REC_HEADS * c, c)
    q_f, k_f = jnp.exp(g[0] * (t[:, None] + 1.0)), jnp.exp(g[0] * (c - 1.0 - t[:, None]))
    q_b, k_b = jnp.exp(g[1] * (c - t[:, None])), jnp.exp(g[1] * t[:, None])
    f = jnp.stack([jnp.stack([q_f, k_f]), jnp.stack([q_b, k_b])])
    a = jnp.exp(g * c).reshape(2, 1, D_REC)
    return d, f, a


def _rec_kernel(*refs, state_in, state_out, aliased):
    refs = list(refs)
    rf_ref, rb_ref, rd_ref, rfac_ref, ra_ref, lb_ref, lv_ref = refs[:7]
    del refs[:7]
    if state_in:
        sret_ref, shg_ref, t4_ref = refs[:3]
        del refs[:3]
    if state_out:
        t4t_ref = refs.pop(0)
    del refs[:aliased]
    of_ref, ob_ref = refs[:2]
    del refs[:2]
    if state_out:
        oret_ref, ohg_ref = refs[:2]
        del refs[:2]
    (st_scr,) = refs

    ci = pl.program_id(1)
    c = rf_ref.shape[0]
    lane = lax.broadcasted_iota(I32, (1, D_REC), 1)
    head = lane >> 6
    row = lax.broadcasted_iota(I32, (c, 1), 0)
    srow = lax.broadcasted_iota(I32, (D_REC, 1), 0)
    same_head = (srow >> 6) == head
    eye = srow == lane
    head_is = [head == h for h in range(N_REC_HEADS)]
    head_m16 = [jnp.broadcast_to(jnp.where(m, 1.0, 0.0), (c, D_REC)).astype(BF16) for m in head_is]
    head_m16_half = [jnp.broadcast_to(jnp.where(m, 1.0, 0.0), (c // 2, D_REC)).astype(BF16) for m in head_is]

    @pl.when(ci == 0)
    def _():
        if state_in:
            for d in range(2):
                for idx, ref in ((d, sret_ref), (2 + d, shg_ref)):
                    tiled = functools.reduce(jnp.add, [_dot(p, t4_ref[...]) for p in _split_bf16(ref[0, 0, d], 3)])
                    st_scr[idx] = jnp.where(same_head, tiled, 0.0)
        else:
            st_scr[...] = jnp.zeros_like(st_scr)

    def expand(t16):
        return jnp.concatenate([t16 * m for m in head_m16], axis=0)

    def nt(a, b):
        return lax.dot_general(a, b, _NT, preferred_element_type=F32)

    def half_rows(t, h, second):
        o = h if second else 0
        return jnp.concatenate([t[j * 2 * h + o:j * 2 * h + o + h] for j in range(c // (2 * h))], axis=0)

    def spread_rows(t, h, second):
        z = jnp.zeros((h, t.shape[1]), t.dtype)
        parts = []
        for j in range(c // (2 * h)):
            parts += [z, t[j * h:(j + 1) * h]] if second else [t[j * h:(j + 1) * h], z]
        return jnp.concatenate(parts, axis=0)

    def finish(q, k, v16, scores, qfac, kfac, a_row, st_ref, own_term=False):
        st = st_ref[...]
        o = _dot((q * qfac).astype(BF16), st.astype(BF16))
        if own_term:
            o = o + _dot((q * k).astype(BF16), jnp.where(same_head, 1.0, 0.0).astype(BF16)) * v16.astype(F32)
        o_stack = _dot(scores.astype(BF16), v16)
        for h in range(N_REC_HEADS):
            o = o + jnp.where(head_is[h], o_stack[h * c:(h + 1) * c], 0.0)
        kv = lax.dot_general((k * kfac).astype(BF16), v16, _TN, preferred_element_type=F32)
        a_col = jnp.sum(jnp.where(eye, a_row, 0.0), axis=1, keepdims=True)
        st_ref[...] = st * a_col + jnp.where(same_head, kv, 0.0)
        return o

    def tree_scores(q, k, la, reverse):
        lv = lv_ref[...]
        scores = jnp.zeros(lv.shape, F32)
        pre, tot = la, la
        h, level = 1, 31
        while h < c:
            bit = (row & h) != 0
            query_side = jnp.logical_not(bit) if reverse else bit
            x = jnp.exp(jnp.minimum(jnp.where(query_side, pre, tot - pre), 0.0))
            qx = q * x
            kx = jnp.where(query_side, 0.0, k * x).astype(BF16)
            if h < 8:
                s_level = nt(expand(qx.astype(BF16)), kx)
            else:
                qc = half_rows(qx, h, not reverse).astype(BF16)
                sc = nt(jnp.concatenate([qc * m for m in head_m16_half], axis=0), kx)
                s_level = jnp.concatenate([spread_rows(sc[hh * c // 2:(hh + 1) * c // 2], h, not reverse)
                                           for hh in range(N_REC_HEADS)], axis=0)
            scores = jnp.where(lv == level, s_level, scores)
            partner = jnp.where(bit, pltpu.roll(tot, h, 0), pltpu.roll(tot, c - h, 0))
            pre = pre + jnp.where(query_side, partner, 0.0)
            tot = tot + partner
            h, level = 2 * h, level - 1
        return scores, pre, tot

    lb = lb_ref[...]
    col = lambda ref, j: ref[:, j * D_REC:(j + 1) * D_REC]
    for d, (r_ref, o_ref) in enumerate(((rf_ref, of_ref), (rb_ref, ob_ref))):
        q, k, v16 = col(r_ref, 0), col(r_ref, 1) * (RET_DK ** -0.5), col(r_ref, 2).astype(BF16)
        scores = nt(expand(q.astype(BF16)), k.astype(BF16)) * rd_ref[d]
        o_ref[:, 0:D_REC] = finish(q, k, v16, scores, rfac_ref[d, 0], rfac_ref[d, 1], ra_ref[d], st_scr.at[d])
        z = col(r_ref, 5 + d)
        la = jnp.log(jnp.maximum(lb + (1.0 - lb) * jax.nn.sigmoid(z), MIN_GATE))
        q, k, v16 = _silu(col(r_ref, 4)), (1.0 - lb) * jax.nn.sigmoid(-z), col(r_ref, 7).astype(BF16)
        scores, pre, tot = tree_scores(q, k, la, reverse=d == 1)
        o_ref[:, D_REC:2 * D_REC] = finish(q, k, v16, scores, jnp.exp(jnp.minimum(pre, 0.0)),
                                           jnp.exp(jnp.minimum(tot - pre, 0.0)), jnp.exp(tot[0:1, :]),
                                           st_scr.at[2 + d], own_term=True)

    if state_out:
        @pl.when(ci == pl.num_programs(1) - 1)
        def _():
            for d in range(2):
                for idx, ref in ((d, oret_ref), (2 + d, ohg_ref)):
                    ref[0, 0, d] = functools.reduce(
                        jnp.add, [_dot(p, t4t_ref[...]) for p in _split_bf16(st_scr[idx], 3)])
            if state_out == "first":
                oret_ref[0, 1:] = jnp.zeros_like(oret_ref[0, 1:])
                ohg_ref[0, 1:] = jnp.zeros_like(ohg_ref[0, 1:])


def _recurrence(rest, ret_tabs, lb_row, lv, row0, n_seq, seq_len, layer, *, states_in=None, want_states=False,
                states_out=None):
    c = REC_CHUNK
    nc = seq_len // c
    cb0 = row0 // c
    const2 = lambda b, i: (0, 0)
    const3 = lambda b, i: (0, 0, 0)
    rd, rfac, ra = ret_tabs
    in_specs = [pl.BlockSpec((c, D_REST), lambda b, i: (cb0 + b * nc + i, 0)),
                pl.BlockSpec((c, D_REST), lambda b, i: (cb0 + b * nc + nc - 1 - i, 0)),
                pl.BlockSpec((2, N_REC_HEADS * c, c), const3),
                pl.BlockSpec((2, 2, c, D_REC), lambda b, i: (0, 0, 0, 0)),
                pl.BlockSpec((2, 1, D_REC), const3),
                pl.BlockSpec((1, D_REC), const2),
                pl.BlockSpec((N_REC_HEADS * c, c), const2)]
    args = [rest, rest, rd, rfac, ra, lb_row, lv]
    tile4 = jnp.tile(jnp.eye(D_HEAD, dtype=BF16), (1, N_REC_HEADS))
    if states_in is not None:
        blk = pl.BlockSpec((1, 1, 2, D_REC, D_HEAD), lambda b, i: (b, layer, 0, 0, 0))
        in_specs += [blk, blk, pl.BlockSpec((D_HEAD, D_REC), const2)]
        args += [states_in[0], states_in[1], tile4]
    if want_states:
        in_specs += [pl.BlockSpec((D_REC, D_HEAD), const2)]
        args += [tile4.T]
    out_specs = [pl.BlockSpec((c, 2 * D_REC), lambda b, i: (b * nc + i, 0)),
                 pl.BlockSpec((c, 2 * D_REC), lambda b, i: (b * nc + nc - 1 - i, 0))]
    out_shape = [jax.ShapeDtypeStruct((n_seq * seq_len, 2 * D_REC), F32)] * 2
    aliases = {}
    state_mode = None
    if want_states:
        st_shape = jax.ShapeDtypeStruct((n_seq, DEPTH, 2, D_REC, D_HEAD), F32)
        out_shape += [st_shape, st_shape]
        if states_out is None:
            state_mode = "first"
            out_specs += [pl.BlockSpec((1, DEPTH, 2, D_REC, D_HEAD), lambda b, i: (b, 0, 0, 0, 0))] * 2
        else:
            state_mode = "next"
            out_specs += [pl.BlockSpec((1, 1, 2, D_REC, D_HEAD), lambda b, i: (b, layer, 0, 0, 0))] * 2
            aliases = {len(args): 2, len(args) + 1: 3}
            in_specs += [pl.BlockSpec(memory_space=pl.ANY)] * 2
            args += list(states_out)
    return pl.pallas_call(
        functools.partial(_rec_kernel, state_in=states_in is not None, state_out=state_mode,
                          aliased=len(aliases)),
        grid=(n_seq, nc),
        in_specs=in_specs,
        out_specs=out_specs,
        out_shape=out_shape,
        scratch_shapes=[pltpu.VMEM((4, D_REC, D_REC), F32)],
        input_output_aliases=aliases,
        compiler_params=_params("arbitrary", "arbitrary"),
        name="rec_lat" if states_in is not None else "rec_ctx",
    )(*args)


def _route(logits, bias):
    scores = jax.nn.sigmoid(logits)
    sel = scores + bias
    srow = [scores[e:e + 1, :] for e in range(N_EXPERTS)]
    rows = [sel[e:e + 1, :] for e in range(N_EXPERTS)]
    gs = []
    for g in range(N_GROUPS):
        a, b, c, d = rows[4 * g:4 * g + 4]
        gs.append(functools.reduce(jnp.maximum, [a + b, a + c, a + d, b + c, b + d, c + d]))
    best = jnp.zeros_like(gs[0], dtype=I32)
    best_v = gs[0]
    for g in range(1, N_GROUPS):
        upd = gs[g] > best_v
        best = jnp.where(upd, g, best)
        best_v = jnp.where(upd, gs[g], best_v)
    masked = [jnp.where(best == e // EXPERTS_PER_GROUP, rows[e], MASKED_SCORE) for e in range(N_EXPERTS)]
    i1 = jnp.zeros_like(best)
    v1 = masked[0]
    for e in range(1, N_EXPERTS):
        upd = masked[e] > v1
        i1 = jnp.where(upd, e, i1)
        v1 = jnp.where(upd, masked[e], v1)
    i2 = jnp.zeros_like(best)
    v2 = jnp.full_like(v1, -jnp.inf)
    for e in range(N_EXPERTS):
        upd = (masked[e] > v2) & (i1 != e)
        i2 = jnp.where(upd, e, i2)
        v2 = jnp.where(upd, masked[e], v2)
    w1 = functools.reduce(jnp.add, [jnp.where(i1 == e, srow[e], 0.0) for e in range(N_EXPERTS)])
    w2 = functools.reduce(jnp.add, [jnp.where(i2 == e, srow[e], 0.0) for e in range(N_EXPERTS)])
    tot = w1 + w2
    w1, w2 = w1 / tot, w2 / tot
    expert = lax.broadcasted_iota(I32, logits.shape, 0)
    return jnp.where(expert == i1, w1, 0.0) + jnp.where(expert == i2, w2, 0.0)


def _outproj_kernel(xa_ref, xb_ref, odaa_ref, odab_ref, ofa_ref, ofb_ref, oba_ref, obb_ref, rg_ref, gg_ref,
                    mod_ref, subln_ref, retn_ref, hgn_ref, g64_ref, wout_ref, nffn_ref, wrh_ref, wrl_ref, rb_ref,
                    xmid_ref, hn2_ref, gate_ref, rank_ref, total_ref, count_ref, *, n_ctx_tiles):
    g64 = g64_ref[...]
    is_ctx = pl.program_id(0) < n_ctx_tiles
    pick = lambda a_ref, b_ref: jnp.where(is_ctx, a_ref[...], b_ref[...])

    def gnorm(t, w):
        return t * lax.rsqrt(_group_mean_sq(t, g64) + EPS) * w

    o_da = gnorm(pick(odaa_ref, odab_ref), subln_ref[...])
    o_rec = pick(ofa_ref, ofb_ref) + pick(oba_ref, obb_ref)
    o_ret = gnorm(o_rec[:, 0:D_REC], retn_ref[...]) * _silu(rg_ref[...])
    o_hg = gnorm(o_rec[:, D_REC:2 * D_REC], hgn_ref[...]) * _silu(gg_ref[...])
    mixed = jnp.concatenate([o_da, o_ret, o_hg], axis=1).astype(BF16)
    mod = mod_ref[0]
    x = pick(xa_ref, xb_ref) + mod[:, 2 * D_MODEL:3 * D_MODEL] * _dot(mixed, wout_ref[...])
    xmid_ref[...] = x
    y = x * lax.rsqrt(jnp.mean(x * x, axis=-1, keepdims=True) + EPS) * nffn_ref[...]
    hn = y * (1.0 + mod[:, 4 * D_MODEL:5 * D_MODEL]) + mod[:, 3 * D_MODEL:4 * D_MODEL]
    hi, lo = _split_bf16(hn)
    hn2_ref[...] = hi
    wrh = wrh_ref[...]
    logits = (lax.dot_general(wrh, hi, _NT, preferred_element_type=F32)
              + lax.dot_general(wrh, lo, _NT, preferred_element_type=F32)
              + lax.dot_general(wrl_ref[...], hi, _NT, preferred_element_type=F32))
    gate = _route(logits, rb_ref[...])
    gate_ref[...] = gate

    tm = gate.shape[1]
    member = jnp.concatenate(
        [jnp.max(gate[EXPERTS_PER_GROUP * g:EXPERTS_PER_GROUP * (g + 1)], axis=0, keepdims=True)
         for g in range(N_GROUPS)] + [jnp.zeros((8 - N_GROUPS, tm), F32)], axis=0)
    member = jnp.where(member > 0, 1.0, 0.0)
    earlier = (lax.broadcasted_iota(I32, (tm, tm), 0) < lax.broadcasted_iota(I32, (tm, tm), 1))
    before = _dot(member.astype(BF16), jnp.where(earlier, 1.0, 0.0).astype(BF16))

    @pl.when(pl.program_id(0) % (MOE_BLOCK // tm) == 0)
    def _():
        count_ref[...] = jnp.zeros_like(count_ref)

    rank_ref[...] = jnp.where(member > 0, before + count_ref[:, 0:1], -1.0).astype(I32)
    count_ref[...] += jnp.sum(member, axis=1, keepdims=True)
    total_ref[...] = count_ref[...]


def _outproj(x, oda, o_f, o_b, rest, mod, subln, retn, hgn, g64, w_out16, norm_w, wr_hi, wr_lo, rbias, st):
    tm = st.tile
    t = st.n_tiles * tm
    row = lambda i: (i, 0)
    const = lambda i: (0, 0)
    pair = lambda width: [pl.BlockSpec((tm, width), lambda i: (st.ctx_idx(i), 0)),
                          pl.BlockSpec((tm, width), lambda i: (st.lat_idx(i), 0))]
    return pl.pallas_call(
        functools.partial(_outproj_kernel, n_ctx_tiles=st.n_ctx),
        grid=(st.n_tiles,),
        in_specs=pair(D_MODEL) + pair(D_V) + pair(2 * D_REC) + pair(2 * D_REC) + [
            pl.BlockSpec((tm, D_REC), lambda i: (i, 3)),
            pl.BlockSpec((tm, D_REC), lambda i: (i, 8)),
            pl.BlockSpec((1, 1, 6 * D_MODEL), lambda i: (st.mod_idx(i), 0, 0)),
            pl.BlockSpec((1, D_V), const),
            pl.BlockSpec((1, D_REC), const),
            pl.BlockSpec((1, D_REC), const),
            pl.BlockSpec((LANES, LANES), const),
            pl.BlockSpec((D_MODEL, D_MODEL), const, pipeline_mode=pl.Buffered(1)),
            pl.BlockSpec((1, D_MODEL), const),
            pl.BlockSpec((N_EXPERTS, D_MODEL), const),
            pl.BlockSpec((N_EXPERTS, D_MODEL), const),
            pl.BlockSpec((N_EXPERTS, 1), const)],
        out_specs=[pl.BlockSpec((tm, D_MODEL), row), pl.BlockSpec((tm, D_MODEL), row),
                   pl.BlockSpec((N_EXPERTS, tm), lambda i: (0, i)), pl.BlockSpec((8, tm), lambda i: (0, i)),
                   pl.BlockSpec((8, LANES), lambda i: (0, i // (MOE_BLOCK // tm)))],
        out_shape=[jax.ShapeDtypeStruct((t, D_MODEL), F32), jax.ShapeDtypeStruct((t, D_MODEL), BF16),
                   jax.ShapeDtypeStruct((N_EXPERTS, t), F32), jax.ShapeDtypeStruct((8, t), I32),
                   jax.ShapeDtypeStruct((8, t // MOE_BLOCK * LANES), F32)],
        scratch_shapes=[pltpu.VMEM((8, LANES), F32)],
        compiler_params=_params("arbitrary"),
        name="outproj",
    )(*x, *oda, *o_f, *o_b, rest, rest, mod, subln, retn, hgn, g64, w_out16, norm_w, wr_hi, wr_lo, rbias)


def _moe_plan(rank, totals, n_blocks):
    nb, tr, spb = MOE_BLOCK, MOE_TILE, MOE_SLOTS_PER_BLOCK
    n_slots = n_blocks * spb
    rank = rank[:N_GROUPS].reshape(N_GROUPS, n_blocks, nb)
    mb = rank >= 0
    mi = mb.astype(I32)
    count = totals[:N_GROUPS].reshape(N_GROUPS, n_blocks, LANES)[:, :, 0].astype(I32)
    tiles = (count + tr - 1) // tr
    t_end = jnp.cumsum(tiles, axis=0)
    t_off = t_end - tiles
    n_tiles = t_end[-1]
    dest = jnp.where(mb, t_off[..., None] * tr + rank, -1)
    order = jnp.cumsum(mi, axis=0) - mi
    n_mem = mi.sum(0)
    row_of = lambda k: jnp.where(n_mem > k, jnp.where(mb & (order == k), dest, 0).sum(0), -1)
    s = jnp.arange(n_slots, dtype=I32)
    s_blk, s_tile = s // spb, s % spb
    used = s_tile < jnp.repeat(n_tiles, spb)
    s_grp = (s_tile[None, :] >= jnp.repeat(t_end, spb, axis=1)).sum(0).astype(I32)
    key = jnp.where(used, s_grp, N_GROUPS) * n_slots + s
    pos = (key[None, :] < key[:, None]).sum(1).astype(I32)
    slot_of_item = jnp.where(pos[None, :] == s[:, None], s[None, :], 0).sum(1).astype(I32)
    n_valid = used.sum().astype(I32)
    src = slot_of_item[jnp.minimum(s, n_valid - 1)]
    item_g, item_b, item_tile = jnp.stack([s_grp, s_blk, s_tile])[:, src]
    return dict(dest=dest.reshape(N_GROUPS * n_blocks, 1, nb),
                row1=row_of(0).reshape(n_blocks, 1, nb), row2=row_of(1).reshape(n_blocks, 1, nb),
                n_tiles=n_tiles.astype(I32), item_g=item_g, item_b=item_b, item_tile=item_tile,
                item_slot=slot_of_item, n_used=n_valid)


def _moe_ffn_kernel(ig_ref, ib_ref, it_ref, is_ref, h_ref, dest_ref, gate_ref, wg_ref, wu_ref, wd_ref, _,
                    y_ref, wg16, wu16, wd16):
    i = pl.program_id(0)
    g = ig_ref[i]
    prev_g = ig_ref[jnp.maximum(i - 1, 0)]

    @pl.when((i == 0) | (g != prev_g))
    def _():
        wg16[...] = wg_ref[...].astype(BF16)
        wu16[...] = wu_ref[...].astype(BF16)
        wd16[...] = wd_ref[...].astype(BF16)

    rows = lax.broadcasted_iota(I32, (MOE_TILE, 1), 0) + it_ref[i] * MOE_TILE
    sort = jnp.where(dest_ref[0] == rows, 1.0, 0.0).astype(BF16)
    x = _dot(sort, h_ref[...]).astype(BF16)
    gates = functools.reduce(jnp.add, [lax.dot_general(sort, p, _NT, preferred_element_type=F32)
                                       for p in _split_bf16(gate_ref[...])])
    lane = lax.broadcasted_iota(I32, (1, N_EXPERTS), 1)
    acc = jnp.zeros((MOE_TILE, D_MODEL), F32)
    for e in range(EXPERTS_PER_GROUP):
        ge = jnp.sum(jnp.where(lane == g * EXPERTS_PER_GROUP + e, gates, 0.0), axis=1, keepdims=True)
        act = _silu(_dot(x, wg16[e])) * _dot(x, wu16[e])
        acc = acc + _dot((act * ge).astype(BF16), wd16[e])
    y_ref[...] = acc.astype(BF16)


def _moe_unsort_kernel(nt_ref, y_ref, r1_ref, r2_ref, x_ref, mod_ref, oa_ref, ob_ref, acc_ref, *, n_ctx_blocks):
    b = pl.program_id(0)
    acc_ref[...] = jnp.zeros_like(acc_ref)
    chunk = max(MOE_TILE, MXU_DEPTH)
    tiles_per_chunk = chunk // MOE_TILE
    row1, row2 = r1_ref[0], r2_ref[0]

    def body(ci, carry):
        r0 = pl.multiple_of(ci * chunk, chunk)
        rows = lax.broadcasted_iota(I32, (chunk, 1), 0) + r0
        sort = jnp.where(row1 == rows, 1.0, jnp.where(row2 == rows, 1.0, 0.0)).astype(BF16)
        acc_ref[...] += lax.dot_general(sort, y_ref[pl.ds(r0, chunk), :], _TN, preferred_element_type=F32)
        return carry

    lax.fori_loop(0, (nt_ref[b] + tiles_per_chunk - 1) // tiles_per_chunk, body, 0)
    result = lambda: x_ref[...] + mod_ref[0][:, 5 * D_MODEL:6 * D_MODEL] * acc_ref[...]

    @pl.when(b < n_ctx_blocks)
    def _():
        oa_ref[...] = result()

    @pl.when(b >= n_ctx_blocks)
    def _():
        ob_ref[...] = result()


def _moe(hn2, gate_t, rank, totals, w_gate, w_up, w_down, x_mid, mod, st, layer):
    nb, tr, spb = st.tile, MOE_TILE, MOE_SLOTS_PER_BLOCK
    n_blocks = st.n_tiles
    plan = _moe_plan(rank, totals, n_blocks)

    wmap = lambda i, ig, ib, it, isl: (layer * N_GROUPS + ig[i], 0, 0)
    once = pl.Buffered(1)
    n_scalar, n_in = 4, 7
    y = pl.pallas_call(
        _moe_ffn_kernel,
        grid_spec=pltpu.PrefetchScalarGridSpec(
            num_scalar_prefetch=n_scalar,
            grid=(plan["n_used"],),
            in_specs=[pl.BlockSpec((nb, D_MODEL), lambda i, ig, ib, it, isl: (ib[i], 0)),
                      pl.BlockSpec((1, 1, nb), lambda i, ig, ib, it, isl: (ig[i] * n_blocks + ib[i], 0, 0)),
                      pl.BlockSpec((N_EXPERTS, nb), lambda i, ig, ib, it, isl: (0, ib[i])),
                      pl.BlockSpec((EXPERTS_PER_GROUP, D_MODEL, D_EXPERT), wmap, pipeline_mode=once),
                      pl.BlockSpec((EXPERTS_PER_GROUP, D_MODEL, D_EXPERT), wmap, pipeline_mode=once),
                      pl.BlockSpec((EXPERTS_PER_GROUP, D_EXPERT, D_MODEL), wmap, pipeline_mode=once),
                      pl.BlockSpec(memory_space=pl.ANY)],
            out_specs=pl.BlockSpec((tr, D_MODEL), lambda i, ig, ib, it, isl: (isl[i], 0)),
            scratch_shapes=[pltpu.VMEM((EXPERTS_PER_GROUP, D_MODEL, D_EXPERT), BF16),
                            pltpu.VMEM((EXPERTS_PER_GROUP, D_MODEL, D_EXPERT), BF16),
                            pltpu.VMEM((EXPERTS_PER_GROUP, D_EXPERT, D_MODEL), BF16)]),
        out_shape=jax.ShapeDtypeStruct((n_blocks * spb * tr, D_MODEL), BF16),
        input_output_aliases={n_scalar + n_in - 1: 0},
        compiler_params=_params("arbitrary"),
        name="moe_ffn",
    )(plan["item_g"], plan["item_b"], plan["item_tile"], plan["item_slot"],
      hn2, plan["dest"], gate_t, w_gate, w_up, w_down, jnp.zeros((n_blocks * spb * tr, D_MODEL), BF16))

    return pl.pallas_call(
        functools.partial(_moe_unsort_kernel, n_ctx_blocks=st.n_ctx),
        grid_spec=pltpu.PrefetchScalarGridSpec(
            num_scalar_prefetch=1,
            grid=(n_blocks,),
            in_specs=[pl.BlockSpec((spb * tr, D_MODEL), lambda b, nt: (b, 0)),
                      pl.BlockSpec((1, 1, nb), lambda b, nt: (b, 0, 0)),
                      pl.BlockSpec((1, 1, nb), lambda b, nt: (b, 0, 0)),
                      pl.BlockSpec((nb, D_MODEL), lambda b, nt: (b, 0)),
                      pl.BlockSpec((1, 1, 6 * D_MODEL), lambda b, nt: (st.mod_idx(b), 0, 0))],
            out_specs=[pl.BlockSpec((nb, D_MODEL), lambda b, nt: (st.ctx_idx(b), 0)),
                       pl.BlockSpec((nb, D_MODEL), lambda b, nt: (st.lat_idx(b), 0))],
            scratch_shapes=[pltpu.VMEM((nb, D_MODEL), F32)]),
        out_shape=[jax.ShapeDtypeStruct((st.n_ctx * nb, D_MODEL), F32),
                   jax.ShapeDtypeStruct(((n_blocks - st.n_ctx) * nb, D_MODEL), F32)],
        compiler_params=_params("arbitrary"),
        name="moe_unsort",
    )(plan["n_tiles"], y, plan["row1"], plan["row2"], x_mid, mod)


def _block_avg(group):
    i = np.arange(LANES)
    return jnp.asarray((i[:, None] // group == i[None, :] // group) / group, dtype=BF16)


def _rope_tables(n_tokens, n_identity):
    rows = n_tokens // GRID_W
    pos_r = jnp.repeat(jnp.arange(rows, dtype=F32), GRID_W)
    pos_c = jnp.tile(jnp.arange(GRID_W, dtype=F32), rows)
    n_freq = DA_QK // 4
    inv = ROPE_BASE ** (-jnp.arange(n_freq, dtype=F32) / n_freq)
    ang = jnp.concatenate([pos_r[:, None] * inv, pos_c[:, None] * inv], axis=-1)
    cos = jnp.repeat(jnp.cos(ang), 2, axis=-1)
    sin = jnp.repeat(jnp.sin(ang), 2, axis=-1) * jnp.tile(jnp.asarray([-1.0, 1.0], F32), DA_QK // 2)
    cos = jnp.concatenate([jnp.ones((n_identity, DA_QK), F32), cos], axis=0)
    sin = jnp.concatenate([jnp.zeros((n_identity, DA_QK), F32), sin], axis=0)
    return jnp.tile(cos, (1, 2 * H_DA)), jnp.tile(sin, (1, 2 * H_DA))


def kernel(x_prompt, x_sample, cache_k, cache_v, state_ret, state_hgrn, c, c_ctx, w_in, w_out, w_ada, b_ada,
           norm_mix, norm_ffn, da_qnorm, da_knorm, da_lambda, da_subln, ret_decay, ret_norm, hg_lb, hg_norm,
           w_router, router_bias, w_gate, w_up, w_down):
    n_ctx, l_ctx, _ = x_prompt.shape
    n_lat, l_lat, _ = x_sample.shape
    t_ctx = n_ctx * l_ctx
    past = cache_k.shape[2]
    assert TOKEN_TILE % l_ctx == 0 and MOE_BLOCK % TOKEN_TILE == 0
    assert l_lat % MOE_BLOCK == 0 and t_ctx % MOE_BLOCK == 0 and l_lat % GRID_W == 0

    ctx_row = n_lat
    n_cond = -(-(n_lat + 1) // 8) * 8
    cond = jnp.zeros((n_cond, D_MODEL), F32).at[:n_lat].set(c).at[ctx_row].set(c_ctx)
    mod_all = _ada(cond, w_ada, b_ada)

    st = _Stream(t_ctx, n_lat, l_lat, TOKEN_TILE, ctx_row)
    st_moe = _Stream(t_ctx, n_lat, l_lat, MOE_BLOCK, ctx_row)
    g32, g64 = _block_avg(DA_QK), _block_avg(DA_V)
    cos_t, sin_t = _rope_tables(l_lat, TOKEN_TILE)
    lv = jnp.asarray(_level_table(REC_CHUNK))
    p_lb = jax.nn.softmax(hg_lb.astype(F32), axis=0)
    lb_all = jnp.cumsum(p_lb, axis=0) - p_lb[0]
    wr_hi = w_router.T.astype(BF16)
    wr_lo = (w_router.T - wr_hi.astype(F32)).astype(BF16)
    rbias = router_bias.astype(F32).reshape(N_EXPERTS, 1)
    flat_w = lambda w: w.reshape(DEPTH * N_EXPERTS, w.shape[2], w.shape[3])
    w_gate, w_up, w_down = flat_w(w_gate), flat_w(w_up), flat_w(w_down)
    flat_s = lambda s: s.astype(F32).reshape(n_lat, DEPTH, 2, D_REC, D_HEAD)
    states_in = (flat_s(state_ret), flat_s(state_hgrn))
    cache_k16 = cache_k.reshape(n_lat, DEPTH, past, D_QK).astype(BF16)
    cache_v16 = cache_v.reshape(n_lat, DEPTH, past, D_V).astype(BF16)

    x = (x_prompt.reshape(t_ctx, D_MODEL), x_sample.reshape(n_lat * l_lat, D_MODEL))
    caches, states_out = None, None
    for l in range(DEPTH):
        lam_init = 0.8 - 0.6 * math.exp(-0.3 * l)
        lp = da_lambda[l].astype(F32)
        lam = (jnp.exp(jnp.sum(lp[0] * lp[1])) - jnp.exp(jnp.sum(lp[2] * lp[3])) + lam_init).reshape(1)
        mod = mod_all[l].reshape(n_cond, 1, 6 * D_MODEL)
        qw = (jnp.tile(da_qnorm[l].astype(F32), 2 * H_DA) * (DA_QK ** -0.5)).reshape(1, D_QK)
        kw = jnp.tile(da_knorm[l].astype(F32), 2 * H_DA).reshape(1, D_QK)

        qb, kb, vb, rest, kc, vc = _inproj(x[0], x[1], mod, norm_mix[l].reshape(1, D_MODEL),
                                           w_in[l].astype(BF16), qw, kw, g32, cos_t, sin_t, st, l, n_ctx, l_ctx,
                                           caches)
        caches = (kc, vc)

        oda = (_attention(lam, qb, kb, vb, 0, n_ctx, l_ctx),
               _attention(lam, qb, kb, vb, t_ctx, n_lat, l_lat, l, cache=(cache_k16, cache_v16)))

        ret_tabs = _retention_tables(-jnp.exp(ret_decay[l].astype(F32)), REC_CHUNK)
        lb_row = lb_all[l].reshape(1, D_REC)
        ofc, obc, s_ret, s_hg = _recurrence(rest, ret_tabs, lb_row, lv, 0, n_ctx, l_ctx, l,
                                            want_states=True, states_out=states_out)
        states_out = (s_ret, s_hg)
        ofl, obl = _recurrence(rest, ret_tabs, lb_row, lv, t_ctx, n_lat, l_lat, l, states_in=states_in)

        subln = (jnp.tile(da_subln[l].astype(F32), H_DA) * (1.0 - lam_init)).reshape(1, D_V)
        retn = jnp.tile(ret_norm[l].astype(F32), H_RET).reshape(1, D_REC)
        hgn = jnp.tile(hg_norm[l].astype(F32), H_HG).reshape(1, D_REC)
        x_mid, hn2, gate_t, rank, totals = _outproj(x, oda, (ofc, ofl), (obc, obl), rest, mod, subln, retn, hgn, g64,
                                      w_out[l].astype(BF16), norm_ffn[l].reshape(1, D_MODEL), wr_hi, wr_lo,
                                      rbias, st)
        x = _moe(hn2, gate_t, rank, totals, w_gate, w_up, w_down, x_mid, mod, st_moe, l)

    kc, vc = caches
    s_ret, s_hg = states_out
    return (x[0].reshape(n_ctx, l_ctx, D_MODEL), x[1].reshape(n_lat, l_lat, D_MODEL),
            kc.reshape(n_ctx, DEPTH, l_ctx, H_DA, 2, DA_QK), vc.reshape(n_ctx, DEPTH, l_ctx, H_DA, DA_V),
            s_ret.reshape(n_ctx, DEPTH, 2, H_RET, RET_DK, RET_DV),
            s_hg.reshape(n_ctx, DEPTH, 2, H_HG, HG_DK, HG_DV))
```

```python
import functools
import math

import numpy as np
import jax
import jax.numpy as jnp
from jax import lax
from jax.experimental import pallas as pl
from jax.experimental.pallas import tpu as pltpu

F32 = jnp.float32
BF16 = jnp.bfloat16
I32 = jnp.int32

D_MODEL = 1024
DEPTH = 2
GRID_W = 64
EPS = 1e-6
MIN_GATE = 1e-30
ROPE_BASE = 10000.0
H_DA, DA_QK, DA_V = 8, 32, 64
H_RET, RET_DK, RET_DV = 4, 64, 64
H_HG, HG_DK, HG_DV = 4, 64, 64
D_QK = H_DA * 2 * DA_QK
D_V = H_DA * DA_V
N_REC_HEADS = 4
D_HEAD = 64
D_REC = N_REC_HEADS * D_HEAD
D_REST = 9 * D_REC
D_IN = 2 * D_QK + D_V + D_REST
N_EXPERTS, N_GROUPS, EXPERTS_PER_GROUP = 16, 4, 4
D_EXPERT = 512
MASKED_SCORE = -2.0

LANES = 128
TOKEN_TILE = 512
Q_TILE = 1024
CTX_PAIRS_PER_STEP = 4
LAT_PAIRS_PER_STEP = 2
REC_CHUNK = 128
MOE_BLOCK = 1024
MOE_TILE = 256
MXU_DEPTH = 256
MOE_SLOTS_PER_BLOCK = -(-2 * MOE_BLOCK // MOE_TILE) + N_GROUPS
VMEM_LIMIT = 56 * 1024 * 1024

_NT = (((1,), (1,)), ((), ()))
_TN = (((0,), (0,)), ((), ()))


def _params(*sem):
    return pltpu.CompilerParams(dimension_semantics=sem, vmem_limit_bytes=VMEM_LIMIT)


def _dot(a, b):
    return jnp.dot(a, b, preferred_element_type=F32)


def _split_bf16(t, terms=2):
    out = []
    for _ in range(terms - 1):
        hi = t.astype(BF16)
        out.append(hi)
        t = t - hi.astype(F32)
    out.append(t.astype(BF16))
    return out


def _group_mean_sq(t, g):
    hi, lo = _split_bf16(t * t)
    outs = []
    for s in range(t.shape[1] // LANES):
        sl = slice(s * LANES, (s + 1) * LANES)
        outs.append(_dot(hi[:, sl], g) + _dot(lo[:, sl], g))
    return jnp.concatenate(outs, axis=1) if len(outs) > 1 else outs[0]


def _silu(t):
    return t * jax.nn.sigmoid(t)


class _Stream:
    def __init__(self, n_ctx_rows, n_lat_seq, lat_len, tile, ctx_row):
        self.tile = tile
        self.n_ctx = n_ctx_rows // tile
        self.per_seq = lat_len // tile
        self.n_tiles = self.n_ctx + n_lat_seq * self.per_seq
        self.ctx_row = ctx_row

    def ctx_idx(self, i):
        return jnp.minimum(i, self.n_ctx - 1)

    def lat_idx(self, i):
        return jnp.maximum(i - self.n_ctx, 0)

    def mod_idx(self, i):
        return jnp.where(i < self.n_ctx, self.ctx_row, (i - self.n_ctx) // self.per_seq)

    def pos_idx(self, i):
        return jnp.where(i < self.n_ctx, 0, 1 + (i - self.n_ctx) % self.per_seq)


def _ada_kernel(c_ref, w_ref, b_ref, o_ref):
    s = _silu(c_ref[...]).astype(BF16)
    o_ref[0] = _dot(s, w_ref[0].astype(BF16)) + b_ref[0]


def _ada(cond, w_ada, b_ada):
    rows = cond.shape[0]
    n_tile = 1536
    return pl.pallas_call(
        _ada_kernel,
        grid=(DEPTH, 6 * D_MODEL // n_tile),
        in_specs=[pl.BlockSpec((rows, D_MODEL), lambda l, j: (0, 0)),
                  pl.BlockSpec((1, D_MODEL, n_tile), lambda l, j: (l, 0, j)),
                  pl.BlockSpec((1, 1, n_tile), lambda l, j: (l, 0, j))],
        out_specs=pl.BlockSpec((1, rows, n_tile), lambda l, j: (l, 0, j)),
        out_shape=jax.ShapeDtypeStruct((DEPTH, rows, 6 * D_MODEL), F32),
        compiler_params=_params("arbitrary", "arbitrary"),
        name="ada",
    )(cond, w_ada, b_ada.reshape(DEPTH, 1, 6 * D_MODEL))


def _inproj_kernel(*refs, n_ctx_tiles, first):
    (xa_ref, xb_ref, mod_ref, nw_ref, w_ref, qw_ref, kw_ref, g32_ref, cos_ref, sin_ref) = refs[:10]
    q_ref, kb_ref, vb_ref, rest_ref, kc_ref, vc_ref = refs[10:] if first else refs[12:]
    is_ctx = pl.program_id(0) < n_ctx_tiles
    x = jnp.where(is_ctx, xa_ref[...], xb_ref[...])
    y = x * lax.rsqrt(jnp.mean(x * x, axis=-1, keepdims=True) + EPS) * nw_ref[...]
    mod = mod_ref[0]
    hn = (y * (1.0 + mod[:, D_MODEL:2 * D_MODEL]) + mod[:, 0:D_MODEL]).astype(BF16)
    g32 = g32_ref[...]
    lane = lax.broadcasted_iota(I32, (1, D_QK), 1)
    even = (lane & 1) == 0

    def qk_norm_rope(t, w):
        t = t * lax.rsqrt(_group_mean_sq(t, g32) + EPS) * w
        partner = jnp.where(even, pltpu.roll(t, D_QK - 1, 1), pltpu.roll(t, 1, 1))
        return t * cos_ref[...] + partner * sin_ref[...]

    q_ref[...] = qk_norm_rope(_dot(hn, w_ref[:, 0:D_QK]), qw_ref[...]).astype(BF16)
    k = qk_norm_rope(_dot(hn, w_ref[:, D_QK:2 * D_QK]), kw_ref[...])
    kb_ref[...] = k.astype(BF16)
    v = _dot(hn, w_ref[:, 2 * D_QK:2 * D_QK + D_V])
    vb_ref[...] = v.astype(BF16)
    rest_ref[...] = _dot(hn, w_ref[:, 2 * D_QK + D_V:D_IN])

    @pl.when(is_ctx)
    def _():
        n_seq, _, seq_len, _ = kc_ref.shape
        kc_ref[:, 0] = k.reshape(n_seq, seq_len, D_QK)
        vc_ref[:, 0] = v.reshape(n_seq, seq_len, D_V)
        if first:
            kc_ref[:, 1:] = jnp.zeros_like(kc_ref[:, 1:])
            vc_ref[:, 1:] = jnp.zeros_like(vc_ref[:, 1:])


def _inproj(xa, xb, mod, norm_w, w_in16, qw, kw, g32, cos_t, sin_t, st, layer, n_ctx_seq, l_ctx, caches):
    tm = st.tile
    t = st.n_tiles * tm
    first = caches is None
    row = lambda i: (i, 0)
    const = lambda i: (0, 0)
    in_specs = [pl.BlockSpec((tm, D_MODEL), lambda i: (st.ctx_idx(i), 0)),
                pl.BlockSpec((tm, D_MODEL), lambda i: (st.lat_idx(i), 0)),
                pl.BlockSpec((1, 1, 6 * D_MODEL), lambda i: (st.mod_idx(i), 0, 0)),
                pl.BlockSpec((1, D_MODEL), const),
                pl.BlockSpec((D_MODEL, D_IN), const, pipeline_mode=pl.Buffered(1)),
                pl.BlockSpec((1, D_QK), const),
                pl.BlockSpec((1, D_QK), const),
                pl.BlockSpec((LANES, LANES), const),
                pl.BlockSpec((tm, D_QK), lambda i: (st.pos_idx(i), 0)),
                pl.BlockSpec((tm, D_QK), lambda i: (st.pos_idx(i), 0))]
    args = [xa, xb, mod, norm_w, w_in16, qw, kw, g32, cos_t, sin_t]
    aliases = {}
    per_tile = tm // l_ctx
    if first:
        cache_spec = pl.BlockSpec((per_tile, DEPTH, l_ctx, D_QK), lambda i: (st.ctx_idx(i), 0, 0, 0))
    else:
        cache_spec = pl.BlockSpec((per_tile, 1, l_ctx, D_QK), lambda i: (st.ctx_idx(i), layer, 0, 0))
        in_specs += [pl.BlockSpec(memory_space=pl.ANY)] * 2
        args += list(caches)
        aliases = {10: 4, 11: 5}
    cache_shape = jax.ShapeDtypeStruct((n_ctx_seq, DEPTH, l_ctx, D_QK), F32)
    return pl.pallas_call(
        functools.partial(_inproj_kernel, n_ctx_tiles=st.n_ctx, first=first),
        grid=(st.n_tiles,),
        in_specs=in_specs,
        out_specs=[pl.BlockSpec((tm, D_QK), row), pl.BlockSpec((tm, D_QK), row),
                   pl.BlockSpec((tm, D_V), row), pl.BlockSpec((tm, D_REST), row), cache_spec, cache_spec],
        out_shape=[jax.ShapeDtypeStruct((t, D_QK), BF16), jax.ShapeDtypeStruct((t, D_QK), BF16),
                   jax.ShapeDtypeStruct((t, D_V), BF16), jax.ShapeDtypeStruct((t, D_REST), F32),
                   cache_shape, cache_shape],
        input_output_aliases=aliases,
        compiler_params=_params("arbitrary"),
        name="inproj",
    )(*args)


def _attn_kernel(*refs, with_cache):
    if with_cache:
        lam_ref, q_ref, k_ref, v_ref, ck_ref, cv_ref, o_ref = refs
    else:
        lam_ref, q_ref, k_ref, v_ref, o_ref = refs
    lam = lam_ref[0]
    lane = lax.broadcasted_iota(I32, (1, LANES), 1)
    left = lane < DA_V
    one = jnp.ones((), BF16)
    zero = jnp.zeros((), BF16)
    owns = [left, jnp.logical_not(left)]
    tq = min(Q_TILE, q_ref.shape[0])
    for p in range(q_ref.shape[1] // LANES):
        slab = slice(p * LANES, (p + 1) * LANES)
        segs = [(k_ref[:, slab], v_ref[:, slab])]
        if with_cache:
            segs.append((ck_ref[0, 0, :, slab], cv_ref[0, 0, :, slab]))
        vms = [[jnp.where(own, v, one) for _, v in segs] for own in owns]

        def q_block(i, carry, slab=slab, segs=segs, vms=vms):
            r0 = i * tq if isinstance(i, int) else pl.multiple_of(i * tq, tq)
            q = q_ref[pl.ds(r0, tq), slab]
            acc = jnp.zeros((tq, LANES), F32)
            for side in range(2):
                for r in range(2):
                    lo = side * DA_V + r * DA_QK
                    qm = jnp.where((lane >= lo) & (lane < lo + DA_QK), q, zero)
                    ss = [lax.dot_general(qm, k, _NT, preferred_element_type=F32) for k, _ in segs]
                    m = functools.reduce(jnp.maximum, [jnp.max(s, axis=-1, keepdims=True) for s in ss])
                    res = functools.reduce(
                        jnp.add, [_dot(jnp.exp(s - m).astype(BF16), vm) for s, vm in zip(ss, vms[side])])
                    den = pltpu.roll(res, DA_V, 1)
                    coef = 1.0 if r == 0 else -lam
                    acc = acc + jnp.where(owns[side], coef * (res / den), 0.0)
            o_ref[pl.ds(r0, tq), slab] = acc
            return carry

        if q_ref.shape[0] == tq:
            q_block(0, 0)
        else:
            lax.fori_loop(0, q_ref.shape[0] // tq, q_block, 0)


def _attention(lam, qb, kb, vb, row0, n_seq, seq_len, pairs_per_step, layer=0, cache=None):
    sb0 = row0 // seq_len
    width = pairs_per_step * LANES
    smap = lambda b, p: (sb0 + b, p)
    smem = pl.BlockSpec(memory_space=pltpu.SMEM)
    in_specs = [smem] + [pl.BlockSpec((seq_len, width), smap)] * 3
    args = [lam, qb, kb, vb]
    if cache is not None:
        ck, cv = cache
        cmap = lambda b, p: (b, layer, 0, p)
        in_specs += [pl.BlockSpec((1, 1, ck.shape[2], width), cmap),
                     pl.BlockSpec((1, 1, cv.shape[2], width), cmap)]
        args += [ck, cv]
    return pl.pallas_call(
        functools.partial(_attn_kernel, with_cache=cache is not None),
        grid=(n_seq, D_QK // width),
        in_specs=in_specs,
        out_specs=pl.BlockSpec((seq_len, width), lambda b, p: (b, p)),
        out_shape=jax.ShapeDtypeStruct((n_seq * seq_len, D_V), F32),
        compiler_params=_params("arbitrary", "arbitrary"),
        name="attn_lat" if cache is not None else "attn_ctx",
    )(*args)


def _level_table(c):
    t = np.arange(c)[:, None] ^ np.arange(c)[None, :]
    lv = np.where(t == 0, 32, 31 - np.floor(np.log2(np.maximum(t, 1))).astype(np.int64))
    return np.tile(lv, (N_REC_HEADS, 1)).astype(np.int32)


def _retention_tables(log_gamma, c):
    g = jnp.repeat(log_gamma, D_HEAD, axis=1)
    t = jnp.arange(c, dtype=F32)
    diff = t[:, None] - t[None, :]
    gh = log_gamma[:, :, None, None]
    d_f = jnp.where(diff >= 0, jnp.exp(gh[0] * jnp.maximum(diff, 0.0)), 0.0)
    d_b = jnp.where(diff <= 0, jnp.exp(gh[1] * jnp.maximum(-diff, 0.0)), 0.0)
    d = jnp.stack([d_f, d_b]).reshape(2, N_REC_HEADS * c, c)
    q_f, k_f = jnp.exp(g[0] * (t[:, None] + 1.0)), jnp.exp(g[0] * (c - 1.0 - t[:, None]))
    q_b, k_b = jnp.exp(g[1] * (c - t[:, None])), jnp.exp(g[1] * t[:, None])
    f = jnp.stack([jnp.stack([q_f, k_f]), jnp.stack([q_b, k_b])])
    a = jnp.exp(g * c).reshape(2, 1, D_REC)
    return d, f, a


def _rec_kernel(*refs, state_in, state_out, aliased):
    refs = list(refs)
    rf_ref, rb_ref, rd_ref, rfac_ref, ra_ref, lb_ref, lv_ref = refs[:7]
    del refs[:7]
    if state_in:
        sret_ref, shg_ref, t4_ref = refs[:3]
        del refs[:3]
    if state_out:
        t4t_ref = refs.pop(0)
    del refs[:aliased]
    of_ref, ob_ref = refs[:2]
    del refs[:2]
    if state_out:
        oret_ref, ohg_ref = refs[:2]
        del refs[:2]
    (st_scr,) = refs

    ci = pl.program_id(1)
    c = rf_ref.shape[0]
    lane = lax.broadcasted_iota(I32, (1, D_REC), 1)
    head = lane >> 6
    row = lax.broadcasted_iota(I32, (c, 1), 0)
    srow = lax.broadcasted_iota(I32, (D_REC, 1), 0)
    same_head = (srow >> 6) == head
    eye = srow == lane
    head_is = [head == h for h in range(N_REC_HEADS)]
    head_m16 = [jnp.broadcast_to(jnp.where(m, 1.0, 0.0), (c, D_REC)).astype(BF16) for m in head_is]
    head_m16_half = [jnp.broadcast_to(jnp.where(m, 1.0, 0.0), (c // 2, D_REC)).astype(BF16) for m in head_is]

    @pl.when(ci == 0)
    def _():
        if state_in:
            for d in range(2):
                for idx, ref in ((d, sret_ref), (2 + d, shg_ref)):
                    tiled = functools.reduce(jnp.add, [_dot(p, t4_ref[...]) for p in _split_bf16(ref[0, 0, d], 3)])
                    st_scr[idx] = jnp.where(same_head, tiled, 0.0)
        else:
            st_scr[...] = jnp.zeros_like(st_scr)

    def expand(t16):
        return jnp.concatenate([t16 * m for m in head_m16], axis=0)

    def nt(a, b):
        return lax.dot_general(a, b, _NT, preferred_element_type=F32)

    def half_rows(t, h, second):
        o = h if second else 0
        return jnp.concatenate([t[j * 2 * h + o:j * 2 * h + o + h] for j in range(c // (2 * h))], axis=0)

    def spread_rows(t, h, second):
        z = jnp.zeros((h, t.shape[1]), t.dtype)
        parts = []
        for j in range(c // (2 * h)):
            parts += [z, t[j * h:(j + 1) * h]] if second else [t[j * h:(j + 1) * h], z]
        return jnp.concatenate(parts, axis=0)

    def finish(q, k, v16, scores, qfac, kfac, a_row, st_ref, own_term=False):
        st = st_ref[...]
        o = _dot((q * qfac).astype(BF16), st.astype(BF16))
        if own_term:
            o = o + _dot((q * k).astype(BF16), jnp.where(same_head, 1.0, 0.0).astype(BF16)) * v16.astype(F32)
        o_stack = _dot(scores.astype(BF16), v16)
        for h in range(N_REC_HEADS):
            o = o + jnp.where(head_is[h], o_stack[h * c:(h + 1) * c], 0.0)
        kv = lax.dot_general((k * kfac).astype(BF16), v16, _TN, preferred_element_type=F32)
        a_col = jnp.sum(jnp.where(eye, a_row, 0.0), axis=1, keepdims=True)
        st_ref[...] = st * a_col + jnp.where(same_head, kv, 0.0)
        return o

    def tree_scores(q, k, la, reverse):
        lv = lv_ref[...]
        scores = jnp.zeros(lv.shape, F32)
        pre, tot = la, la
        h, level = 1, 31
        while h < c:
            bit = (row & h) != 0
            query_side = jnp.logical_not(bit) if reverse else bit
            x = jnp.exp(jnp.minimum(jnp.where(query_side, pre, tot - pre), 0.0))
            qx = q * x
            kx = jnp.where(query_side, 0.0, k * x).astype(BF16)
            if h < 8:
                s_level = nt(expand(qx.astype(BF16)), kx)
            else:
                qc = half_rows(qx, h, not reverse).astype(BF16)
                sc = nt(jnp.concatenate([qc * m for m in head_m16_half], axis=0), kx)
                s_level = jnp.concatenate([spread_rows(sc[hh * c // 2:(hh + 1) * c // 2], h, not reverse)
                                           for hh in range(N_REC_HEADS)], axis=0)
            scores = jnp.where(lv == level, s_level, scores)
            partner = jnp.where(bit, pltpu.roll(tot, h, 0), pltpu.roll(tot, c - h, 0))
            pre = pre + jnp.where(query_side, partner, 0.0)
            tot = tot + partner
            h, level = 2 * h, level - 1
        return scores, pre, tot

    lb = lb_ref[...]
    col = lambda ref, j: ref[:, j * D_REC:(j + 1) * D_REC]
    for d, (r_ref, o_ref) in enumerate(((rf_ref, of_ref), (rb_ref, ob_ref))):
        q, k, v16 = col(r_ref, 0), col(r_ref, 1) * (RET_DK ** -0.5), col(r_ref, 2).astype(BF16)
        scores = nt(expand(q.astype(BF16)), k.astype(BF16)) * rd_ref[d]
        o_ref[:, 0:D_REC] = finish(q, k, v16, scores, rfac_ref[d, 0], rfac_ref[d, 1], ra_ref[d], st_scr.at[d])
        z = col(r_ref, 5 + d)
        la = jnp.log(jnp.maximum(lb + (1.0 - lb) * jax.nn.sigmoid(z), MIN_GATE))
        q, k, v16 = _silu(col(r_ref, 4)), (1.0 - lb) * jax.nn.sigmoid(-z), col(r_ref, 7).astype(BF16)
        scores, pre, tot = tree_scores(q, k, la, reverse=d == 1)
        o_ref[:, D_REC:2 * D_REC] = finish(q, k, v16, scores, jnp.exp(jnp.minimum(pre, 0.0)),
                                           jnp.exp(jnp.minimum(tot - pre, 0.0)), jnp.exp(tot[0:1, :]),
                                           st_scr.at[2 + d], own_term=True)

    if state_out:
        @pl.when(ci == pl.num_programs(1) - 1)
        def _():
            for d in range(2):
                for idx, ref in ((d, oret_ref), (2 + d, ohg_ref)):
                    ref[0, 0, d] = functools.reduce(
                        jnp.add, [_dot(p, t4t_ref[...]) for p in _split_bf16(st_scr[idx], 3)])
            if state_out == "first":
                oret_ref[0, 1:] = jnp.zeros_like(oret_ref[0, 1:])
                ohg_ref[0, 1:] = jnp.zeros_like(ohg_ref[0, 1:])


def _recurrence(rest, ret_tabs, lb_row, lv, row0, n_seq, seq_len, layer, *, states_in=None, want_states=False,
                states_out=None):
    c = REC_CHUNK
    nc = seq_len // c
    cb0 = row0 // c
    const2 = lambda b, i: (0, 0)
    const3 = lambda b, i: (0, 0, 0)
    rd, rfac, ra = ret_tabs
    in_specs = [pl.BlockSpec((c, D_REST), lambda b, i: (cb0 + b * nc + i, 0)),
                pl.BlockSpec((c, D_REST), lambda b, i: (cb0 + b * nc + nc - 1 - i, 0)),
                pl.BlockSpec((2, N_REC_HEADS * c, c), const3),
                pl.BlockSpec((2, 2, c, D_REC), lambda b, i: (0, 0, 0, 0)),
                pl.BlockSpec((2, 1, D_REC), const3),
                pl.BlockSpec((1, D_REC), const2),
                pl.BlockSpec((N_REC_HEADS * c, c), const2)]
    args = [rest, rest, rd, rfac, ra, lb_row, lv]
    tile4 = jnp.tile(jnp.eye(D_HEAD, dtype=BF16), (1, N_REC_HEADS))
    if states_in is not None:
        blk = pl.BlockSpec((1, 1, 2, D_REC, D_HEAD), lambda b, i: (b, layer, 0, 0, 0))
        in_specs += [blk, blk, pl.BlockSpec((D_HEAD, D_REC), const2)]
        args += [states_in[0], states_in[1], tile4]
    if want_states:
        in_specs += [pl.BlockSpec((D_REC, D_HEAD), const2)]
        args += [tile4.T]
    out_specs = [pl.BlockSpec((c, 2 * D_REC), lambda b, i: (b * nc + i, 0)),
                 pl.BlockSpec((c, 2 * D_REC), lambda b, i: (b * nc + nc - 1 - i, 0))]
    out_shape = [jax.ShapeDtypeStruct((n_seq * seq_len, 2 * D_REC), F32)] * 2
    aliases = {}
    state_mode = None
    if want_states:
        st_shape = jax.ShapeDtypeStruct((n_seq, DEPTH, 2, D_REC, D_HEAD), F32)
        out_shape += [st_shape, st_shape]
        if states_out is None:
            state_mode = "first"
            out_specs += [pl.BlockSpec((1, DEPTH, 2, D_REC, D_HEAD), lambda b, i: (b, 0, 0, 0, 0))] * 2
        else:
            state_mode = "next"
            out_specs += [pl.BlockSpec((1, 1, 2, D_REC, D_HEAD), lambda b, i: (b, layer, 0, 0, 0))] * 2
            aliases = {len(args): 2, len(args) + 1: 3}
            in_specs += [pl.BlockSpec(memory_space=pl.ANY)] * 2
            args += list(states_out)
    return pl.pallas_call(
        functools.partial(_rec_kernel, state_in=states_in is not None, state_out=state_mode,
                          aliased=len(aliases)),
        grid=(n_seq, nc),
        in_specs=in_specs,
        out_specs=out_specs,
        out_shape=out_shape,
        scratch_shapes=[pltpu.VMEM((4, D_REC, D_REC), F32)],
        input_output_aliases=aliases,
        compiler_params=_params("arbitrary", "arbitrary"),
        name="rec_lat" if states_in is not None else "rec_ctx",
    )(*args)


def _route(logits, bias):
    scores = jax.nn.sigmoid(logits)
    sel = scores + bias
    srow = [scores[e:e + 1, :] for e in range(N_EXPERTS)]
    rows = [sel[e:e + 1, :] for e in range(N_EXPERTS)]
    gs = []
    for g in range(N_GROUPS):
        a, b, c, d = rows[4 * g:4 * g + 4]
        gs.append(functools.reduce(jnp.maximum, [a + b, a + c, a + d, b + c, b + d, c + d]))
    best = jnp.zeros_like(gs[0], dtype=I32)
    best_v = gs[0]
    for g in range(1, N_GROUPS):
        upd = gs[g] > best_v
        best = jnp.where(upd, g, best)
        best_v = jnp.where(upd, gs[g], best_v)
    masked = [jnp.where(best == e // EXPERTS_PER_GROUP, rows[e], MASKED_SCORE) for e in range(N_EXPERTS)]
    i1 = jnp.zeros_like(best)
    v1 = masked[0]
    for e in range(1, N_EXPERTS):
        upd = masked[e] > v1
        i1 = jnp.where(upd, e, i1)
        v1 = jnp.where(upd, masked[e], v1)
    i2 = jnp.zeros_like(best)
    v2 = jnp.full_like(v1, -jnp.inf)
    for e in range(N_EXPERTS):
        upd = (masked[e] > v2) & (i1 != e)
        i2 = jnp.where(upd, e, i2)
        v2 = jnp.where(upd, masked[e], v2)
    w1 = functools.reduce(jnp.add, [jnp.where(i1 == e, srow[e], 0.0) for e in range(N_EXPERTS)])
    w2 = functools.reduce(jnp.add, [jnp.where(i2 == e, srow[e], 0.0) for e in range(N_EXPERTS)])
    tot = w1 + w2
    w1, w2 = w1 / tot, w2 / tot
    expert = lax.broadcasted_iota(I32, logits.shape, 0)
    return jnp.where(expert == i1, w1, 0.0) + jnp.where(expert == i2, w2, 0.0)


def _outproj_kernel(xa_ref, xb_ref, odaa_ref, odab_ref, ofa_ref, ofb_ref, oba_ref, obb_ref, rg_ref, gg_ref,
                    mod_ref, subln_ref, retn_ref, hgn_ref, g64_ref, wout_ref, nffn_ref, wrh_ref, wrl_ref, rb_ref,
                    xmid_ref, hn2_ref, gate_ref, rank_ref, total_ref, count_ref, *, n_ctx_tiles):
    g64 = g64_ref[...]
    is_ctx = pl.program_id(0) < n_ctx_tiles
    pick = lambda a_ref, b_ref: jnp.where(is_ctx, a_ref[...], b_ref[...])

    def gnorm(t, w):
        return t * lax.rsqrt(_group_mean_sq(t, g64) + EPS) * w

    o_da = gnorm(pick(odaa_ref, odab_ref), subln_ref[...])
    o_rec = pick(ofa_ref, ofb_ref) + pick(oba_ref, obb_ref)
    o_ret = gnorm(o_rec[:, 0:D_REC], retn_ref[...]) * _silu(rg_ref[...])
    o_hg = gnorm(o_rec[:, D_REC:2 * D_REC], hgn_ref[...]) * _silu(gg_ref[...])
    mixed = jnp.concatenate([o_da, o_ret, o_hg], axis=1).astype(BF16)
    mod = mod_ref[0]
    x = pick(xa_ref, xb_ref) + mod[:, 2 * D_MODEL:3 * D_MODEL] * _dot(mixed, wout_ref[...])
    xmid_ref[...] = x
    y = x * lax.rsqrt(jnp.mean(x * x, axis=-1, keepdims=True) + EPS) * nffn_ref[...]
    hn = y * (1.0 + mod[:, 4 * D_MODEL:5 * D_MODEL]) + mod[:, 3 * D_MODEL:4 * D_MODEL]
    hi, lo = _split_bf16(hn)
    hn2_ref[...] = hi
    wrh = wrh_ref[...]
    logits = (lax.dot_general(wrh, hi, _NT, preferred_element_type=F32)
              + lax.dot_general(wrh, lo, _NT, preferred_element_type=F32)
              + lax.dot_general(wrl_ref[...], hi, _NT, preferred_element_type=F32))
    gate = _route(logits, rb_ref[...])
    gate_ref[...] = gate

    tm = gate.shape[1]
    member = jnp.concatenate(
        [jnp.max(gate[EXPERTS_PER_GROUP * g:EXPERTS_PER_GROUP * (g + 1)], axis=0, keepdims=True)
         for g in range(N_GROUPS)] + [jnp.zeros((8 - N_GROUPS, tm), F32)], axis=0)
    member = jnp.where(member > 0, 1.0, 0.0)
    earlier = (lax.broadcasted_iota(I32, (tm, tm), 0) < lax.broadcasted_iota(I32, (tm, tm), 1))
    before = _dot(member.astype(BF16), jnp.where(earlier, 1.0, 0.0).astype(BF16))

    @pl.when(pl.program_id(0) % (MOE_BLOCK // tm) == 0)
    def _():
        count_ref[...] = jnp.zeros_like(count_ref)

    rank_ref[...] = jnp.where(member > 0, before + count_ref[:, 0:1], -1.0).astype(I32)
    count_ref[...] += jnp.sum(member, axis=1, keepdims=True)
    total_ref[...] = count_ref[...]


def _outproj(x, oda, o_f, o_b, rest, mod, subln, retn, hgn, g64, w_out16, norm_w, wr_hi, wr_lo, rbias, st):
    tm = st.tile
    t = st.n_tiles * tm
    row = lambda i: (i, 0)
    const = lambda i: (0, 0)
    pair = lambda width: [pl.BlockSpec((tm, width), lambda i: (st.ctx_idx(i), 0)),
                          pl.BlockSpec((tm, width), lambda i: (st.lat_idx(i), 0))]
    return pl.pallas_call(
        functools.partial(_outproj_kernel, n_ctx_tiles=st.n_ctx),
        grid=(st.n_tiles,),
        in_specs=pair(D_MODEL) + pair(D_V) + pair(2 * D_REC) + pair(2 * D_REC) + [
            pl.BlockSpec((tm, D_REC), lambda i: (i, 3)),
            pl.BlockSpec((tm, D_REC), lambda i: (i, 8)),
            pl.BlockSpec((1, 1, 6 * D_MODEL), lambda i: (st.mod_idx(i), 0, 0)),
            pl.BlockSpec((1, D_V), const),
            pl.BlockSpec((1, D_REC), const),
            pl.BlockSpec((1, D_REC), const),
            pl.BlockSpec((LANES, LANES), const),
            pl.BlockSpec((D_MODEL, D_MODEL), const, pipeline_mode=pl.Buffered(1)),
            pl.BlockSpec((1, D_MODEL), const),
            pl.BlockSpec((N_EXPERTS, D_MODEL), const),
            pl.BlockSpec((N_EXPERTS, D_MODEL), const),
            pl.BlockSpec((N_EXPERTS, 1), const)],
        out_specs=[pl.BlockSpec((tm, D_MODEL), row), pl.BlockSpec((tm, D_MODEL), row),
                   pl.BlockSpec((N_EXPERTS, tm), lambda i: (0, i)), pl.BlockSpec((8, tm), lambda i: (0, i)),
                   pl.BlockSpec((8, LANES), lambda i: (0, i // (MOE_BLOCK // tm)))],
        out_shape=[jax.ShapeDtypeStruct((t, D_MODEL), F32), jax.ShapeDtypeStruct((t, D_MODEL), BF16),
                   jax.ShapeDtypeStruct((N_EXPERTS, t), F32), jax.ShapeDtypeStruct((8, t), I32),
                   jax.ShapeDtypeStruct((8, t // MOE_BLOCK * LANES), F32)],
        scratch_shapes=[pltpu.VMEM((8, LANES), F32)],
        compiler_params=_params("arbitrary"),
        name="outproj",
    )(*x, *oda, *o_f, *o_b, rest, rest, mod, subln, retn, hgn, g64, w_out16, norm_w, wr_hi, wr_lo, rbias)


def _moe_plan(rank, totals, n_blocks):
    nb, tr, spb = MOE_BLOCK, MOE_TILE, MOE_SLOTS_PER_BLOCK
    n_slots = n_blocks * spb
    rank = rank[:N_GROUPS].reshape(N_GROUPS, n_blocks, nb)
    mb = rank >= 0
    mi = mb.astype(I32)
    count = totals[:N_GROUPS].reshape(N_GROUPS, n_blocks, LANES)[:, :, 0].astype(I32)
    tiles = (count + tr - 1) // tr
    t_end = jnp.cumsum(tiles, axis=0)
    t_off = t_end - tiles
    n_tiles = t_end[-1]
    dest = jnp.where(mb, t_off[..., None] * tr + rank, -1)
    order = jnp.cumsum(mi, axis=0) - mi
    n_mem = mi.sum(0)
    row_of = lambda k: jnp.where(n_mem > k, jnp.where(mb & (order == k), dest, 0).sum(0), -1)
    s = jnp.arange(n_slots, dtype=I32)
    s_blk, s_tile = s // spb, s % spb
    used = s_tile < jnp.repeat(n_tiles, spb)
    s_grp = (s_tile[None, :] >= jnp.repeat(t_end, spb, axis=1)).sum(0).astype(I32)
    key = jnp.where(used, s_grp, N_GROUPS) * n_slots + s
    pos = (key[None, :] < key[:, None]).sum(1).astype(I32)
    slot_of_item = jnp.where(pos[None, :] == s[:, None], s[None, :], 0).sum(1).astype(I32)
    n_valid = used.sum().astype(I32)
    src = slot_of_item[jnp.minimum(s, n_valid - 1)]
    item_g, item_b, item_tile = jnp.stack([s_grp, s_blk, s_tile])[:, src]
    return dict(dest=dest.reshape(N_GROUPS * n_blocks, 1, nb),
                row1=row_of(0).reshape(n_blocks, 1, nb), row2=row_of(1).reshape(n_blocks, 1, nb),
                n_tiles=n_tiles.astype(I32), item_g=item_g, item_b=item_b, item_tile=item_tile,
                item_valid=(s < n_valid).astype(I32), item_slot=slot_of_item)


def _moe_ffn_kernel(ig_ref, ib_ref, it_ref, iv_ref, is_ref, h_ref, dest_ref, gate_ref, wg_ref, wu_ref, wd_ref,
                    y_ref, wg16, wu16, wd16):
    i = pl.program_id(0)
    g = ig_ref[i]
    prev_g = ig_ref[jnp.maximum(i - 1, 0)]

    @pl.when((i == 0) | (g != prev_g))
    def _():
        wg16[...] = wg_ref[...].astype(BF16)
        wu16[...] = wu_ref[...].astype(BF16)
        wd16[...] = wd_ref[...].astype(BF16)

    @pl.when(iv_ref[i] == 0)
    def _():
        y_ref[...] = jnp.zeros_like(y_ref)

    @pl.when(iv_ref[i] == 1)
    def _():
        rows = lax.broadcasted_iota(I32, (MOE_TILE, 1), 0) + it_ref[i] * MOE_TILE
        sort = jnp.where(dest_ref[0] == rows, 1.0, 0.0).astype(BF16)
        x = _dot(sort, h_ref[...]).astype(BF16)
        gates = functools.reduce(jnp.add, [lax.dot_general(sort, p, _NT, preferred_element_type=F32)
                                           for p in _split_bf16(gate_ref[...])])
        lane = lax.broadcasted_iota(I32, (1, N_EXPERTS), 1)
        acc = jnp.zeros((MOE_TILE, D_MODEL), F32)
        for e in range(EXPERTS_PER_GROUP):
            ge = jnp.sum(jnp.where(lane == g * EXPERTS_PER_GROUP + e, gates, 0.0), axis=1, keepdims=True)
            act = _silu(_dot(x, wg16[e])) * _dot(x, wu16[e])
            acc = acc + _dot((act * ge).astype(BF16), wd16[e])
        y_ref[...] = acc.astype(BF16)


def _moe_unsort_kernel(nt_ref, y_ref, r1_ref, r2_ref, x_ref, mod_ref, oa_ref, ob_ref, acc_ref, *, n_ctx_blocks):
    b = pl.program_id(0)
    acc_ref[...] = jnp.zeros_like(acc_ref)
    chunk = max(MOE_TILE, MXU_DEPTH)
    tiles_per_chunk = chunk // MOE_TILE
    row1, row2 = r1_ref[0], r2_ref[0]

    def body(ci, carry):
        r0 = pl.multiple_of(ci * chunk, chunk)
        rows = lax.broadcasted_iota(I32, (chunk, 1), 0) + r0
        sort = jnp.where(row1 == rows, 1.0, jnp.where(row2 == rows, 1.0, 0.0)).astype(BF16)
        acc_ref[...] += lax.dot_general(sort, y_ref[pl.ds(r0, chunk), :], _TN, preferred_element_type=F32)
        return carry

    lax.fori_loop(0, (nt_ref[b] + tiles_per_chunk - 1) // tiles_per_chunk, body, 0)
    result = lambda: x_ref[...] + mod_ref[0][:, 5 * D_MODEL:6 * D_MODEL] * acc_ref[...]

    @pl.when(b < n_ctx_blocks)
    def _():
        oa_ref[...] = result()

    @pl.when(b >= n_ctx_blocks)
    def _():
        ob_ref[...] = result()


def _moe(hn2, gate_t, rank, totals, w_gate, w_up, w_down, x_mid, mod, st, layer):
    nb, tr, spb = st.tile, MOE_TILE, MOE_SLOTS_PER_BLOCK
    n_blocks = st.n_tiles
    plan = _moe_plan(rank, totals, n_blocks)

    wmap = lambda i, ig, ib, it, iv, isl: (layer * N_GROUPS + ig[i], 0, 0)
    once = pl.Buffered(1)
    y = pl.pallas_call(
        _moe_ffn_kernel,
        grid_spec=pltpu.PrefetchScalarGridSpec(
            num_scalar_prefetch=5,
            grid=(n_blocks * spb,),
            in_specs=[pl.BlockSpec((nb, D_MODEL), lambda i, ig, ib, it, iv, isl: (ib[i], 0)),
                      pl.BlockSpec((1, 1, nb), lambda i, ig, ib, it, iv, isl: (ig[i] * n_blocks + ib[i], 0, 0)),
                      pl.BlockSpec((N_EXPERTS, nb), lambda i, ig, ib, it, iv, isl: (0, ib[i])),
                      pl.BlockSpec((EXPERTS_PER_GROUP, D_MODEL, D_EXPERT), wmap, pipeline_mode=once),
                      pl.BlockSpec((EXPERTS_PER_GROUP, D_MODEL, D_EXPERT), wmap, pipeline_mode=once),
                      pl.BlockSpec((EXPERTS_PER_GROUP, D_EXPERT, D_MODEL), wmap, pipeline_mode=once)],
            out_specs=pl.BlockSpec((tr, D_MODEL), lambda i, ig, ib, it, iv, isl: (isl[i], 0)),
            scratch_shapes=[pltpu.VMEM((EXPERTS_PER_GROUP, D_MODEL, D_EXPERT), BF16),
                            pltpu.VMEM((EXPERTS_PER_GROUP, D_MODEL, D_EXPERT), BF16),
                            pltpu.VMEM((EXPERTS_PER_GROUP, D_EXPERT, D_MODEL), BF16)]),
        out_shape=jax.ShapeDtypeStruct((n_blocks * spb * tr, D_MODEL), BF16),
        compiler_params=_params("arbitrary"),
        name="moe_ffn",
    )(plan["item_g"], plan["item_b"], plan["item_tile"], plan["item_valid"], plan["item_slot"],
      hn2, plan["dest"], gate_t, w_gate, w_up, w_down)

    return pl.pallas_call(
        functools.partial(_moe_unsort_kernel, n_ctx_blocks=st.n_ctx),
        grid_spec=pltpu.PrefetchScalarGridSpec(
            num_scalar_prefetch=1,
            grid=(n_blocks,),
            in_specs=[pl.BlockSpec((spb * tr, D_MODEL), lambda b, nt: (b, 0)),
                      pl.BlockSpec((1, 1, nb), lambda b, nt: (b, 0, 0)),
                      pl.BlockSpec((1, 1, nb), lambda b, nt: (b, 0, 0)),
                      pl.BlockSpec((nb, D_MODEL), lambda b, nt: (b, 0)),
                      pl.BlockSpec((1, 1, 6 * D_MODEL), lambda b, nt: (st.mod_idx(b), 0, 0))],
            out_specs=[pl.BlockSpec((nb, D_MODEL), lambda b, nt: (st.ctx_idx(b), 0)),
                       pl.BlockSpec((nb, D_MODEL), lambda b, nt: (st.lat_idx(b), 0))],
            scratch_shapes=[pltpu.VMEM((nb, D_MODEL), F32)]),
        out_shape=[jax.ShapeDtypeStruct((st.n_ctx * nb, D_MODEL), F32),
                   jax.ShapeDtypeStruct(((n_blocks - st.n_ctx) * nb, D_MODEL), F32)],
        compiler_params=_params("arbitrary"),
        name="moe_unsort",
    )(plan["n_tiles"], y, plan["row1"], plan["row2"], x_mid, mod)


def _block_avg(group):
    i = np.arange(LANES)
    return jnp.asarray((i[:, None] // group == i[None, :] // group) / group, dtype=BF16)


def _rope_tables(n_tokens, n_identity):
    rows = n_tokens // GRID_W
    pos_r = jnp.repeat(jnp.arange(rows, dtype=F32), GRID_W)
    pos_c = jnp.tile(jnp.arange(GRID_W, dtype=F32), rows)
    n_freq = DA_QK // 4
    inv = ROPE_BASE ** (-jnp.arange(n_freq, dtype=F32) / n_freq)
    ang = jnp.concatenate([pos_r[:, None] * inv, pos_c[:, None] * inv], axis=-1)
    cos = jnp.repeat(jnp.cos(ang), 2, axis=-1)
    sin = jnp.repeat(jnp.sin(ang), 2, axis=-1) * jnp.tile(jnp.asarray([-1.0, 1.0], F32), DA_QK // 2)
    cos = jnp.concatenate([jnp.ones((n_identity, DA_QK), F32), cos], axis=0)
    sin = jnp.concatenate([jnp.zeros((n_identity, DA_QK), F32), sin], axis=0)
    return jnp.tile(cos, (1, 2 * H_DA)), jnp.tile(sin, (1, 2 * H_DA))


def kernel(x_prompt, x_sample, cache_k, cache_v, state_ret, state_hgrn, c, c_ctx, w_in, w_out, w_ada, b_ada,
           norm_mix, norm_ffn, da_qnorm, da_knorm, da_lambda, da_subln, ret_decay, ret_norm, hg_lb, hg_norm,
           w_router, router_bias, w_gate, w_up, w_down):
    n_ctx, l_ctx, _ = x_prompt.shape
    n_lat, l_lat, _ = x_sample.shape
    t_ctx = n_ctx * l_ctx
    past = cache_k.shape[2]
    assert TOKEN_TILE % l_ctx == 0 and MOE_BLOCK % TOKEN_TILE == 0
    assert l_lat % MOE_BLOCK == 0 and t_ctx % MOE_BLOCK == 0 and l_lat % GRID_W == 0

    ctx_row = n_lat
    n_cond = -(-(n_lat + 1) // 8) * 8
    cond = jnp.zeros((n_cond, D_MODEL), F32).at[:n_lat].set(c).at[ctx_row].set(c_ctx)
    mod_all = _ada(cond, w_ada, b_ada)

    st = _Stream(t_ctx, n_lat, l_lat, TOKEN_TILE, ctx_row)
    st_moe = _Stream(t_ctx, n_lat, l_lat, MOE_BLOCK, ctx_row)
    g32, g64 = _block_avg(DA_QK), _block_avg(DA_V)
    cos_t, sin_t = _rope_tables(l_lat, TOKEN_TILE)
    lv = jnp.asarray(_level_table(REC_CHUNK))
    p_lb = jax.nn.softmax(hg_lb.astype(F32), axis=0)
    lb_all = jnp.cumsum(p_lb, axis=0) - p_lb[0]
    wr_hi = w_router.T.astype(BF16)
    wr_lo = (w_router.T - wr_hi.astype(F32)).astype(BF16)
    rbias = router_bias.astype(F32).reshape(N_EXPERTS, 1)
    flat_w = lambda w: w.reshape(DEPTH * N_EXPERTS, w.shape[2], w.shape[3])
    w_gate, w_up, w_down = flat_w(w_gate), flat_w(w_up), flat_w(w_down)
    flat_s = lambda s: s.astype(F32).reshape(n_lat, DEPTH, 2, D_REC, D_HEAD)
    states_in = (flat_s(state_ret), flat_s(state_hgrn))
    cache_k16 = cache_k.reshape(n_lat, DEPTH, past, D_QK).astype(BF16)
    cache_v16 = cache_v.reshape(n_lat, DEPTH, past, D_V).astype(BF16)

    x = (x_prompt.reshape(t_ctx, D_MODEL), x_sample.reshape(n_lat * l_lat, D_MODEL))
    caches, states_out = None, None
    for l in range(DEPTH):
        lam_init = 0.8 - 0.6 * math.exp(-0.3 * l)
        lp = da_lambda[l].astype(F32)
        lam = (jnp.exp(jnp.sum(lp[0] * lp[1])) - jnp.exp(jnp.sum(lp[2] * lp[3])) + lam_init).reshape(1)
        mod = mod_all[l].reshape(n_cond, 1, 6 * D_MODEL)
        qw = (jnp.tile(da_qnorm[l].astype(F32), 2 * H_DA) * (DA_QK ** -0.5)).reshape(1, D_QK)
        kw = jnp.tile(da_knorm[l].astype(F32), 2 * H_DA).reshape(1, D_QK)

        qb, kb, vb, rest, kc, vc = _inproj(x[0], x[1], mod, norm_mix[l].reshape(1, D_MODEL),
                                           w_in[l].astype(BF16), qw, kw, g32, cos_t, sin_t, st, l, n_ctx, l_ctx,
                                           caches)
        caches = (kc, vc)

        oda = (_attention(lam, qb, kb, vb, 0, n_ctx, l_ctx, CTX_PAIRS_PER_STEP),
               _attention(lam, qb, kb, vb, t_ctx, n_lat, l_lat, LAT_PAIRS_PER_STEP, l,
                          cache=(cache_k16, cache_v16)))

        ret_tabs = _retention_tables(-jnp.exp(ret_decay[l].astype(F32)), REC_CHUNK)
        lb_row = lb_all[l].reshape(1, D_REC)
        ofc, obc, s_ret, s_hg = _recurrence(rest, ret_tabs, lb_row, lv, 0, n_ctx, l_ctx, l,
                                            want_states=True, states_out=states_out)
        states_out = (s_ret, s_hg)
        ofl, obl = _recurrence(rest, ret_tabs, lb_row, lv, t_ctx, n_lat, l_lat, l, states_in=states_in)

        subln = (jnp.tile(da_subln[l].astype(F32), H_DA) * (1.0 - lam_init)).reshape(1, D_V)
        retn = jnp.tile(ret_norm[l].astype(F32), H_RET).reshape(1, D_REC)
        hgn = jnp.tile(hg_norm[l].astype(F32), H_HG).reshape(1, D_REC)
        x_mid, hn2, gate_t, rank, totals = _outproj(x, oda, (ofc, ofl), (obc, obl), rest, mod, subln, retn, hgn, g64,
                                      w_out[l].astype(BF16), norm_ffn[l].reshape(1, D_MODEL), wr_hi, wr_lo,
                                      rbias, st)
        x = _moe(hn2, gate_t, rank, totals, w_gate, w_up, w_down, x_mid, mod, st_moe, l)

    kc, vc = caches
    s_ret, s_hg = states_out
    return (x[0].reshape(n_ctx, l_ctx, D_MODEL), x[1].reshape(n_lat, l_lat, D_MODEL),
            kc.reshape(n_ctx, DEPTH, l_ctx, H_DA, 2, DA_QK), vc.reshape(n_ctx, DEPTH, l_ctx, H_DA, DA_V),
            s_ret.reshape(n_ctx, DEPTH, 2, H_RET, RET_DK, RET_DV),
            s_hg.reshape(n_ctx, DEPTH, 2, H_HG, HG_DK, HG_DV))
```

```python
import functools
import math

import numpy as np
import jax
import jax.numpy as jnp
from jax import lax
from jax.experimental import pallas as pl
from jax.experimental.pallas import tpu as pltpu

F32 = jnp.float32
BF16 = jnp.bfloat16
I32 = jnp.int32

D_MODEL = 1024
DEPTH = 2
GRID_W = 64
EPS = 1e-6
MIN_GATE = 1e-30
ROPE_BASE = 10000.0
H_DA, DA_QK, DA_V = 8, 32, 64
H_RET, RET_DK, RET_DV = 4, 64, 64
H_HG, HG_DK, HG_DV = 4, 64, 64
D_QK = H_DA * 2 * DA_QK
D_V = H_DA * DA_V
N_REC_HEADS = 4
D_HEAD = 64
D_REC = N_REC_HEADS * D_HEAD
D_REST = 9 * D_REC
D_IN = 2 * D_QK + D_V + D_REST
N_EXPERTS, N_GROUPS, EXPERTS_PER_GROUP = 16, 4, 4
D_EXPERT = 512
MASKED_SCORE = -2.0

LANES = 128
TOKEN_TILE = 512
Q_TILE = 1024
CTX_PAIRS_PER_STEP = 1
LAT_PAIRS_PER_STEP = 2
REC_CHUNK = 128
MOE_BLOCK = 1024
MOE_TILE = 256
MXU_DEPTH = 256
MOE_SLOTS_PER_BLOCK = -(-2 * MOE_BLOCK // MOE_TILE) + N_GROUPS
VMEM_LIMIT = 56 * 1024 * 1024

_NT = (((1,), (1,)), ((), ()))
_TN = (((0,), (0,)), ((), ()))


def _params(*sem):
    return pltpu.CompilerParams(dimension_semantics=sem, vmem_limit_bytes=VMEM_LIMIT)


def _dot(a, b):
    return jnp.dot(a, b, preferred_element_type=F32)


def _split_bf16(t, terms=2):
    out = []
    for _ in range(terms - 1):
        hi = t.astype(BF16)
        out.append(hi)
        t = t - hi.astype(F32)
    out.append(t.astype(BF16))
    return out


def _group_mean_sq(t, g):
    sq = (t * t).astype(BF16)
    outs = [_dot(sq[:, s * LANES:(s + 1) * LANES], g) for s in range(t.shape[1] // LANES)]
    return jnp.concatenate(outs, axis=1) if len(outs) > 1 else outs[0]


def _silu(t):
    return t * jax.nn.sigmoid(t)


class _Stream:
    def __init__(self, n_ctx_rows, n_lat_seq, lat_len, tile, ctx_row):
        self.tile = tile
        self.n_ctx = n_ctx_rows // tile
        self.per_seq = lat_len // tile
        self.n_tiles = self.n_ctx + n_lat_seq * self.per_seq
        self.ctx_row = ctx_row

    def ctx_idx(self, i):
        return jnp.minimum(i, self.n_ctx - 1)

    def lat_idx(self, i):
        return jnp.maximum(i - self.n_ctx, 0)

    def mod_idx(self, i):
        return jnp.where(i < self.n_ctx, self.ctx_row, (i - self.n_ctx) // self.per_seq)

    def pos_idx(self, i):
        return jnp.where(i < self.n_ctx, 0, 1 + (i - self.n_ctx) % self.per_seq)


def _ada_kernel(c_ref, w_ref, b_ref, o_ref):
    s = _silu(c_ref[...]).astype(BF16)
    o_ref[0] = _dot(s, w_ref[0].astype(BF16)) + b_ref[0]


def _ada(cond, w_ada, b_ada):
    rows = cond.shape[0]
    n_tile = 1536
    return pl.pallas_call(
        _ada_kernel,
        grid=(DEPTH, 6 * D_MODEL // n_tile),
        in_specs=[pl.BlockSpec((rows, D_MODEL), lambda l, j: (0, 0)),
                  pl.BlockSpec((1, D_MODEL, n_tile), lambda l, j: (l, 0, j)),
                  pl.BlockSpec((1, 1, n_tile), lambda l, j: (l, 0, j))],
        out_specs=pl.BlockSpec((1, rows, n_tile), lambda l, j: (l, 0, j)),
        out_shape=jax.ShapeDtypeStruct((DEPTH, rows, 6 * D_MODEL), F32),
        compiler_params=_params("arbitrary", "arbitrary"),
        name="ada",
    )(cond, w_ada, b_ada.reshape(DEPTH, 1, 6 * D_MODEL))


def _inproj_kernel(*refs, n_ctx_tiles, first):
    (xa_ref, xb_ref, mod_ref, nw_ref, w_ref, qw_ref, kw_ref, g32_ref, cos_ref, sin_ref) = refs[:10]
    q_ref, kb_ref, vb_ref, rest_ref, kc_ref, vc_ref = refs[10:] if first else refs[12:]
    is_ctx = pl.program_id(0) < n_ctx_tiles
    x = jnp.where(is_ctx, xa_ref[...], xb_ref[...])
    y = x * lax.rsqrt(jnp.mean(x * x, axis=-1, keepdims=True) + EPS) * nw_ref[...]
    mod = mod_ref[0]
    hn = (y * (1.0 + mod[:, D_MODEL:2 * D_MODEL]) + mod[:, 0:D_MODEL]).astype(BF16)
    g32 = g32_ref[...]
    lane = lax.broadcasted_iota(I32, (1, D_QK), 1)
    even = (lane & 1) == 0

    def qk_norm_rope(t, w):
        t = t * lax.rsqrt(_group_mean_sq(t, g32) + EPS) * w
        partner = jnp.where(even, pltpu.roll(t, D_QK - 1, 1), pltpu.roll(t, 1, 1))
        return t * cos_ref[...] + partner * sin_ref[...]

    q_ref[...] = qk_norm_rope(_dot(hn, w_ref[:, 0:D_QK]), qw_ref[...]).astype(BF16)
    k = qk_norm_rope(_dot(hn, w_ref[:, D_QK:2 * D_QK]), kw_ref[...])
    kb_ref[...] = k.astype(BF16)
    v = _dot(hn, w_ref[:, 2 * D_QK:2 * D_QK + D_V])
    vb_ref[...] = v.astype(BF16)
    rest_ref[...] = _dot(hn, w_ref[:, 2 * D_QK + D_V:D_IN])

    @pl.when(is_ctx)
    def _():
        n_seq, _, seq_len, _ = kc_ref.shape
        kc_ref[:, 0] = k.reshape(n_seq, seq_len, D_QK)
        vc_ref[:, 0] = v.reshape(n_seq, seq_len, D_V)
        if first:
            kc_ref[:, 1:] = jnp.zeros_like(kc_ref[:, 1:])
            vc_ref[:, 1:] = jnp.zeros_like(vc_ref[:, 1:])


def _inproj(xa, xb, mod, norm_w, w_in16, qw, kw, g32, cos_t, sin_t, st, layer, n_ctx_seq, l_ctx, caches):
    tm = st.tile
    t = st.n_tiles * tm
    first = caches is None
    row = lambda i: (i, 0)
    const = lambda i: (0, 0)
    in_specs = [pl.BlockSpec((tm, D_MODEL), lambda i: (st.ctx_idx(i), 0)),
                pl.BlockSpec((tm, D_MODEL), lambda i: (st.lat_idx(i), 0)),
                pl.BlockSpec((1, 1, 6 * D_MODEL), lambda i: (st.mod_idx(i), 0, 0)),
                pl.BlockSpec((1, D_MODEL), const),
                pl.BlockSpec((D_MODEL, D_IN), const, pipeline_mode=pl.Buffered(1)),
                pl.BlockSpec((1, D_QK), const),
                pl.BlockSpec((1, D_QK), const),
                pl.BlockSpec((LANES, LANES), const),
                pl.BlockSpec((tm, D_QK), lambda i: (st.pos_idx(i), 0)),
                pl.BlockSpec((tm, D_QK), lambda i: (st.pos_idx(i), 0))]
    args = [xa, xb, mod, norm_w, w_in16, qw, kw, g32, cos_t, sin_t]
    aliases = {}
    per_tile = tm // l_ctx
    if first:
        cache_spec = pl.BlockSpec((per_tile, DEPTH, l_ctx, D_QK), lambda i: (st.ctx_idx(i), 0, 0, 0))
    else:
        cache_spec = pl.BlockSpec((per_tile, 1, l_ctx, D_QK), lambda i: (st.ctx_idx(i), layer, 0, 0))
        in_specs += [pl.BlockSpec(memory_space=pl.ANY)] * 2
        args += list(caches)
        aliases = {10: 4, 11: 5}
    cache_shape = jax.ShapeDtypeStruct((n_ctx_seq, DEPTH, l_ctx, D_QK), F32)
    return pl.pallas_call(
        functools.partial(_inproj_kernel, n_ctx_tiles=st.n_ctx, first=first),
        grid=(st.n_tiles,),
        in_specs=in_specs,
        out_specs=[pl.BlockSpec((tm, D_QK), row), pl.BlockSpec((tm, D_QK), row),
                   pl.BlockSpec((tm, D_V), row), pl.BlockSpec((tm, D_REST), row), cache_spec, cache_spec],
        out_shape=[jax.ShapeDtypeStruct((t, D_QK), BF16), jax.ShapeDtypeStruct((t, D_QK), BF16),
                   jax.ShapeDtypeStruct((t, D_V), BF16), jax.ShapeDtypeStruct((t, D_REST), F32),
                   cache_shape, cache_shape],
        input_output_aliases=aliases,
        compiler_params=_params("arbitrary"),
        name="inproj",
    )(*args)


def _attn_kernel(*refs, with_cache):
    if with_cache:
        lam_ref, q_ref, k_ref, v_ref, ck_ref, cv_ref, o_ref = refs
    else:
        lam_ref, q_ref, k_ref, v_ref, o_ref = refs
    lam = lam_ref[0]
    lane = lax.broadcasted_iota(I32, (1, LANES), 1)
    left = lane < DA_V
    one = jnp.ones((), BF16)
    zero = jnp.zeros((), BF16)
    owns = [left, jnp.logical_not(left)]
    tq = min(Q_TILE, q_ref.shape[0])
    for p in range(q_ref.shape[1] // LANES):
        slab = slice(p * LANES, (p + 1) * LANES)
        segs = [(k_ref[:, slab], v_ref[:, slab])]
        if with_cache:
            segs.append((ck_ref[0, 0, :, slab], cv_ref[0, 0, :, slab]))
        vms = [[jnp.where(own, v, one) for _, v in segs] for own in owns]

        def q_block(i, carry, slab=slab, segs=segs, vms=vms):
            r0 = i * tq if isinstance(i, int) else pl.multiple_of(i * tq, tq)
            q = q_ref[pl.ds(r0, tq), slab]
            acc = jnp.zeros((tq, LANES), F32)
            for side in range(2):
                for r in range(2):
                    lo = side * DA_V + r * DA_QK
                    qm = jnp.where((lane >= lo) & (lane < lo + DA_QK), q, zero)
                    ss = [lax.dot_general(qm, k, _NT, preferred_element_type=F32) for k, _ in segs]
                    m = functools.reduce(jnp.maximum, [jnp.max(s, axis=-1, keepdims=True) for s in ss])
                    res = functools.reduce(
                        jnp.add, [_dot(jnp.exp(s - m).astype(BF16), vm) for s, vm in zip(ss, vms[side])])
                    den = pltpu.roll(res, DA_V, 1)
                    coef = 1.0 if r == 0 else -lam
                    acc = acc + jnp.where(owns[side], coef * (res / den), 0.0)
            o_ref[pl.ds(r0, tq), slab] = acc
            return carry

        if q_ref.shape[0] == tq:
            q_block(0, 0)
        else:
            lax.fori_loop(0, q_ref.shape[0] // tq, q_block, 0)


def _attention(lam, qb, kb, vb, row0, n_seq, seq_len, pairs_per_step, layer=0, cache=None):
    sb0 = row0 // seq_len
    width = pairs_per_step * LANES
    smap = lambda b, p: (sb0 + b, p)
    smem = pl.BlockSpec(memory_space=pltpu.SMEM)
    in_specs = [smem] + [pl.BlockSpec((seq_len, width), smap)] * 3
    args = [lam, qb, kb, vb]
    if cache is not None:
        ck, cv = cache
        cmap = lambda b, p: (b, layer, 0, p)
        in_specs += [pl.BlockSpec((1, 1, ck.shape[2], width), cmap),
                     pl.BlockSpec((1, 1, cv.shape[2], width), cmap)]
        args += [ck, cv]
    return pl.pallas_call(
        functools.partial(_attn_kernel, with_cache=cache is not None),
        grid=(n_seq, D_QK // width),
        in_specs=in_specs,
        out_specs=pl.BlockSpec((seq_len, width), lambda b, p: (b, p)),
        out_shape=jax.ShapeDtypeStruct((n_seq * seq_len, D_V), F32),
        compiler_params=_params("arbitrary", "arbitrary"),
        name="attn_lat" if cache is not None else "attn_ctx",
    )(*args)


def _level_table(c):
    t = np.arange(c)[:, None] ^ np.arange(c)[None, :]
    lv = np.where(t == 0, 32, 31 - np.floor(np.log2(np.maximum(t, 1))).astype(np.int64))
    return np.tile(lv, (N_REC_HEADS, 1)).astype(np.int32)


def _retention_tables(log_gamma, c):
    g = jnp.repeat(log_gamma, D_HEAD, axis=1)
    t = jnp.arange(c, dtype=F32)
    diff = t[:, None] - t[None, :]
    gh = log_gamma[:, :, None, None]
    d_f = jnp.where(diff >= 0, jnp.exp(gh[0] * jnp.maximum(diff, 0.0)), 0.0)
    d_b = jnp.where(diff <= 0, jnp.exp(gh[1] * jnp.maximum(-diff, 0.0)), 0.0)
    d = jnp.stack([d_f, d_b]).reshape(2, N_REC_HEADS * c, c)
    q_f, k_f = jnp.exp(g[0] * (t[:, None] + 1.0)), jnp.exp(g[0] * (c - 1.0 - t[:, None]))
    q_b, k_b = jnp.exp(g[1] * (c - t[:, None])), jnp.exp(g[1] * t[:, None])
    f = jnp.stack([jnp.stack([q_f, k_f]), jnp.stack([q_b, k_b])])
    a = jnp.exp(g * c).reshape(2, 1, D_REC)
    return d, f, a


def _rec_kernel(*refs, state_in, state_out, aliased):
    refs = list(refs)
    rf_ref, rb_ref, rd_ref, rfac_ref, ra_ref, lb_ref, lv_ref = refs[:7]
    del refs[:7]
    if state_in:
        sret_ref, shg_ref, t4_ref = refs[:3]
        del refs[:3]
    if state_out:
        t4t_ref = refs.pop(0)
    del refs[:aliased]
    of_ref, ob_ref = refs[:2]
    del refs[:2]
    if state_out:
        oret_ref, ohg_ref = refs[:2]
        del refs[:2]
    (st_scr,) = refs

    ci = pl.program_id(1)
    c = rf_ref.shape[0]
    lane = lax.broadcasted_iota(I32, (1, D_REC), 1)
    head = lane >> 6
    row = lax.broadcasted_iota(I32, (c, 1), 0)
    srow = lax.broadcasted_iota(I32, (D_REC, 1), 0)
    same_head = (srow >> 6) == head
    eye = srow == lane
    head_is = [head == h for h in range(N_REC_HEADS)]
    head_m16 = [jnp.broadcast_to(jnp.where(m, 1.0, 0.0), (c, D_REC)).astype(BF16) for m in head_is]
    head_m16_half = [jnp.broadcast_to(jnp.where(m, 1.0, 0.0), (c // 2, D_REC)).astype(BF16) for m in head_is]

    @pl.when(ci == 0)
    def _():
        if state_in:
            for d in range(2):
                for idx, ref in ((d, sret_ref), (2 + d, shg_ref)):
                    tiled = functools.reduce(jnp.add, [_dot(p, t4_ref[...]) for p in _split_bf16(ref[0, 0, d], 3)])
                    st_scr[idx] = jnp.where(same_head, tiled, 0.0)
        else:
            st_scr[...] = jnp.zeros_like(st_scr)

    def expand(t16):
        return jnp.concatenate([t16 * m for m in head_m16], axis=0)

    def nt(a, b):
        return lax.dot_general(a, b, _NT, preferred_element_type=F32)

    def half_rows(t, h, second):
        o = h if second else 0
        return jnp.concatenate([t[j * 2 * h + o:j * 2 * h + o + h] for j in range(c // (2 * h))], axis=0)

    def spread_rows(t, h, second):
        z = jnp.zeros((h, t.shape[1]), t.dtype)
        parts = []
        for j in range(c // (2 * h)):
            parts += [z, t[j * h:(j + 1) * h]] if second else [t[j * h:(j + 1) * h], z]
        return jnp.concatenate(parts, axis=0)

    def finish(q, k, v16, scores, qfac, kfac, a_row, st_ref, own_term=False):
        st = st_ref[...]
        o = _dot((q * qfac).astype(BF16), st.astype(BF16))
        if own_term:
            o = o + _dot((q * k).astype(BF16), jnp.where(same_head, 1.0, 0.0).astype(BF16)) * v16.astype(F32)
        o_stack = _dot(scores.astype(BF16), v16)
        for h in range(N_REC_HEADS):
            o = o + jnp.where(head_is[h], o_stack[h * c:(h + 1) * c], 0.0)
        kv = lax.dot_general((k * kfac).astype(BF16), v16, _TN, preferred_element_type=F32)
        a_col = jnp.sum(jnp.where(eye, a_row, 0.0), axis=1, keepdims=True)
        st_ref[...] = st * a_col + jnp.where(same_head, kv, 0.0)
        return o

    def tree_scores(q, k, la, reverse):
        lv = lv_ref[...]
        scores = jnp.zeros(lv.shape, F32)
        pre, tot = la, la
        h, level = 1, 31
        while h < c:
            bit = (row & h) != 0
            query_side = jnp.logical_not(bit) if reverse else bit
            x = jnp.exp(jnp.minimum(jnp.where(query_side, pre, tot - pre), 0.0))
            qx = q * x
            kx = jnp.where(query_side, 0.0, k * x).astype(BF16)
            if h < 8:
                s_level = nt(expand(qx.astype(BF16)), kx)
            else:
                qc = half_rows(qx, h, not reverse).astype(BF16)
                sc = nt(jnp.concatenate([qc * m for m in head_m16_half], axis=0), kx)
                s_level = jnp.concatenate([spread_rows(sc[hh * c // 2:(hh + 1) * c // 2], h, not reverse)
                                           for hh in range(N_REC_HEADS)], axis=0)
            scores = jnp.where(lv == level, s_level, scores)
            partner = jnp.where(bit, pltpu.roll(tot, h, 0), pltpu.roll(tot, c - h, 0))
            pre = pre + jnp.where(query_side, partner, 0.0)
            tot = tot + partner
            h, level = 2 * h, level - 1
        return scores, pre, tot

    lb = lb_ref[...]
    col = lambda ref, j: ref[:, j * D_REC:(j + 1) * D_REC]
    for d, (r_ref, o_ref) in enumerate(((rf_ref, of_ref), (rb_ref, ob_ref))):
        q, k, v16 = col(r_ref, 0), col(r_ref, 1) * (RET_DK ** -0.5), col(r_ref, 2).astype(BF16)
        scores = nt(expand(q.astype(BF16)), k.astype(BF16)) * rd_ref[d]
        o_ref[:, 0:D_REC] = finish(q, k, v16, scores, rfac_ref[d, 0], rfac_ref[d, 1], ra_ref[d], st_scr.at[d])
        z = col(r_ref, 5 + d)
        la = jnp.log(jnp.maximum(lb + (1.0 - lb) * jax.nn.sigmoid(z), MIN_GATE))
        q, k, v16 = _silu(col(r_ref, 4)), (1.0 - lb) * jax.nn.sigmoid(-z), col(r_ref, 7).astype(BF16)
        scores, pre, tot = tree_scores(q, k, la, reverse=d == 1)
        o_ref[:, D_REC:2 * D_REC] = finish(q, k, v16, scores, jnp.exp(jnp.minimum(pre, 0.0)),
                                           jnp.exp(jnp.minimum(tot - pre, 0.0)), jnp.exp(tot[0:1, :]),
                                           st_scr.at[2 + d], own_term=True)

    if state_out:
        @pl.when(ci == pl.num_programs(1) - 1)
        def _():
            for d in range(2):
                for idx, ref in ((d, oret_ref), (2 + d, ohg_ref)):
                    ref[0, 0, d] = functools.reduce(
                        jnp.add, [_dot(p, t4t_ref[...]) for p in _split_bf16(st_scr[idx], 3)])
            if state_out == "first":
                oret_ref[0, 1:] = jnp.zeros_like(oret_ref[0, 1:])
                ohg_ref[0, 1:] = jnp.zeros_like(ohg_ref[0, 1:])


def _recurrence(rest, ret_tabs, lb_row, lv, row0, n_seq, seq_len, layer, *, states_in=None, want_states=False,
                states_out=None):
    c = REC_CHUNK
    nc = seq_len // c
    cb0 = row0 // c
    const2 = lambda b, i: (0, 0)
    const3 = lambda b, i: (0, 0, 0)
    rd, rfac, ra = ret_tabs
    in_specs = [pl.BlockSpec((c, D_REST), lambda b, i: (cb0 + b * nc + i, 0)),
                pl.BlockSpec((c, D_REST), lambda b, i: (cb0 + b * nc + nc - 1 - i, 0)),
                pl.BlockSpec((2, N_REC_HEADS * c, c), const3),
                pl.BlockSpec((2, 2, c, D_REC), lambda b, i: (0, 0, 0, 0)),
                pl.BlockSpec((2, 1, D_REC), const3),
                pl.BlockSpec((1, D_REC), const2),
                pl.BlockSpec((N_REC_HEADS * c, c), const2)]
    args = [rest, rest, rd, rfac, ra, lb_row, lv]
    tile4 = jnp.tile(jnp.eye(D_HEAD, dtype=BF16), (1, N_REC_HEADS))
    if states_in is not None:
        blk = pl.BlockSpec((1, 1, 2, D_REC, D_HEAD), lambda b, i: (b, layer, 0, 0, 0))
        in_specs += [blk, blk, pl.BlockSpec((D_HEAD, D_REC), const2)]
        args += [states_in[0], states_in[1], tile4]
    if want_states:
        in_specs += [pl.BlockSpec((D_REC, D_HEAD), const2)]
        args += [tile4.T]
    out_specs = [pl.BlockSpec((c, 2 * D_REC), lambda b, i: (b * nc + i, 0)),
                 pl.BlockSpec((c, 2 * D_REC), lambda b, i: (b * nc + nc - 1 - i, 0))]
    out_shape = [jax.ShapeDtypeStruct((n_seq * seq_len, 2 * D_REC), F32)] * 2
    aliases = {}
    state_mode = None
    if want_states:
        st_shape = jax.ShapeDtypeStruct((n_seq, DEPTH, 2, D_REC, D_HEAD), F32)
        out_shape += [st_shape, st_shape]
        if states_out is None:
            state_mode = "first"
            out_specs += [pl.BlockSpec((1, DEPTH, 2, D_REC, D_HEAD), lambda b, i: (b, 0, 0, 0, 0))] * 2
        else:
            state_mode = "next"
            out_specs += [pl.BlockSpec((1, 1, 2, D_REC, D_HEAD), lambda b, i: (b, layer, 0, 0, 0))] * 2
            aliases = {len(args): 2, len(args) + 1: 3}
            in_specs += [pl.BlockSpec(memory_space=pl.ANY)] * 2
            args += list(states_out)
    return pl.pallas_call(
        functools.partial(_rec_kernel, state_in=states_in is not None, state_out=state_mode,
                          aliased=len(aliases)),
        grid=(n_seq, nc),
        in_specs=in_specs,
        out_specs=out_specs,
        out_shape=out_shape,
        scratch_shapes=[pltpu.VMEM((4, D_REC, D_REC), F32)],
        input_output_aliases=aliases,
        compiler_params=_params("arbitrary", "arbitrary"),
        name="rec_lat" if states_in is not None else "rec_ctx",
    )(*args)


def _route(logits, bias):
    scores = jax.nn.sigmoid(logits)
    sel = scores + bias
    srow = [scores[e:e + 1, :] for e in range(N_EXPERTS)]
    rows = [sel[e:e + 1, :] for e in range(N_EXPERTS)]
    gs = []
    for g in range(N_GROUPS):
        a, b, c, d = rows[4 * g:4 * g + 4]
        gs.append(functools.reduce(jnp.maximum, [a + b, a + c, a + d, b + c, b + d, c + d]))
    best = jnp.zeros_like(gs[0], dtype=I32)
    best_v = gs[0]
    for g in range(1, N_GROUPS):
        upd = gs[g] > best_v
        best = jnp.where(upd, g, best)
        best_v = jnp.where(upd, gs[g], best_v)
    masked = [jnp.where(best == e // EXPERTS_PER_GROUP, rows[e], MASKED_SCORE) for e in range(N_EXPERTS)]
    i1 = jnp.zeros_like(best)
    v1 = masked[0]
    for e in range(1, N_EXPERTS):
        upd = masked[e] > v1
        i1 = jnp.where(upd, e, i1)
        v1 = jnp.where(upd, masked[e], v1)
    i2 = jnp.zeros_like(best)
    v2 = jnp.full_like(v1, -jnp.inf)
    for e in range(N_EXPERTS):
        upd = (masked[e] > v2) & (i1 != e)
        i2 = jnp.where(upd, e, i2)
        v2 = jnp.where(upd, masked[e], v2)
    w1 = functools.reduce(jnp.add, [jnp.where(i1 == e, srow[e], 0.0) for e in range(N_EXPERTS)])
    w2 = functools.reduce(jnp.add, [jnp.where(i2 == e, srow[e], 0.0) for e in range(N_EXPERTS)])
    tot = w1 + w2
    w1, w2 = w1 / tot, w2 / tot
    expert = lax.broadcasted_iota(I32, logits.shape, 0)
    return jnp.where(expert == i1, w1, 0.0) + jnp.where(expert == i2, w2, 0.0)


def _outproj_kernel(xa_ref, xb_ref, odaa_ref, odab_ref, ofa_ref, ofb_ref, oba_ref, obb_ref, rg_ref, gg_ref,
                    mod_ref, subln_ref, retn_ref, hgn_ref, g64_ref, wout_ref, nffn_ref, wrh_ref, wrl_ref, rb_ref,
                    xmid_ref, hn2_ref, gate_ref, rank_ref, total_ref, count_ref, *, n_ctx_tiles):
    g64 = g64_ref[...]
    is_ctx = pl.program_id(0) < n_ctx_tiles
    pick = lambda a_ref, b_ref: jnp.where(is_ctx, a_ref[...], b_ref[...])

    def gnorm(t, w):
        return t * lax.rsqrt(_group_mean_sq(t, g64) + EPS) * w

    o_da = gnorm(pick(odaa_ref, odab_ref), subln_ref[...])
    o_rec = pick(ofa_ref, ofb_ref) + pick(oba_ref, obb_ref)
    o_ret = gnorm(o_rec[:, 0:D_REC], retn_ref[...]) * _silu(rg_ref[...])
    o_hg = gnorm(o_rec[:, D_REC:2 * D_REC], hgn_ref[...]) * _silu(gg_ref[...])
    mixed = jnp.concatenate([o_da, o_ret, o_hg], axis=1).astype(BF16)
    mod = mod_ref[0]
    x = pick(xa_ref, xb_ref) + mod[:, 2 * D_MODEL:3 * D_MODEL] * _dot(mixed, wout_ref[...])
    xmid_ref[...] = x
    y = x * lax.rsqrt(jnp.mean(x * x, axis=-1, keepdims=True) + EPS) * nffn_ref[...]
    hn = y * (1.0 + mod[:, 4 * D_MODEL:5 * D_MODEL]) + mod[:, 3 * D_MODEL:4 * D_MODEL]
    hi, lo = _split_bf16(hn)
    hn2_ref[...] = hi
    wrh = wrh_ref[...]
    logits = (lax.dot_general(wrh, hi, _NT, preferred_element_type=F32)
              + lax.dot_general(wrh, lo, _NT, preferred_element_type=F32)
              + lax.dot_general(wrl_ref[...], hi, _NT, preferred_element_type=F32))
    gate = _route(logits, rb_ref[...])
    gate_ref[...] = gate

    tm = gate.shape[1]
    member = jnp.concatenate(
        [jnp.max(gate[EXPERTS_PER_GROUP * g:EXPERTS_PER_GROUP * (g + 1)], axis=0, keepdims=True)
         for g in range(N_GROUPS)] + [jnp.zeros((8 - N_GROUPS, tm), F32)], axis=0)
    member = jnp.where(member > 0, 1.0, 0.0)
    earlier = (lax.broadcasted_iota(I32, (tm, tm), 0) < lax.broadcasted_iota(I32, (tm, tm), 1))
    before = _dot(member.astype(BF16), jnp.where(earlier, 1.0, 0.0).astype(BF16))

    @pl.when(pl.program_id(0) % (MOE_BLOCK // tm) == 0)
    def _():
        count_ref[...] = jnp.zeros_like(count_ref)

    rank_ref[...] = jnp.where(member > 0, before + count_ref[:, 0:1], -1.0).astype(I32)
    count_ref[...] += jnp.sum(member, axis=1, keepdims=True)
    total_ref[...] = count_ref[...]


def _outproj(x, oda, o_f, o_b, rest, mod, subln, retn, hgn, g64, w_out16, norm_w, wr_hi, wr_lo, rbias, st):
    tm = st.tile
    t = st.n_tiles * tm
    row = lambda i: (i, 0)
    const = lambda i: (0, 0)
    pair = lambda width: [pl.BlockSpec((tm, width), lambda i: (st.ctx_idx(i), 0)),
                          pl.BlockSpec((tm, width), lambda i: (st.lat_idx(i), 0))]
    return pl.pallas_call(
        functools.partial(_outproj_kernel, n_ctx_tiles=st.n_ctx),
        grid=(st.n_tiles,),
        in_specs=pair(D_MODEL) + pair(D_V) + pair(2 * D_REC) + pair(2 * D_REC) + [
            pl.BlockSpec((tm, D_REC), lambda i: (i, 3)),
            pl.BlockSpec((tm, D_REC), lambda i: (i, 8)),
            pl.BlockSpec((1, 1, 6 * D_MODEL), lambda i: (st.mod_idx(i), 0, 0)),
            pl.BlockSpec((1, D_V), const),
            pl.BlockSpec((1, D_REC), const),
            pl.BlockSpec((1, D_REC), const),
            pl.BlockSpec((LANES, LANES), const),
            pl.BlockSpec((D_MODEL, D_MODEL), const, pipeline_mode=pl.Buffered(1)),
            pl.BlockSpec((1, D_MODEL), const),
            pl.BlockSpec((N_EXPERTS, D_MODEL), const),
            pl.BlockSpec((N_EXPERTS, D_MODEL), const),
            pl.BlockSpec((N_EXPERTS, 1), const)],
        out_specs=[pl.BlockSpec((tm, D_MODEL), row), pl.BlockSpec((tm, D_MODEL), row),
                   pl.BlockSpec((N_EXPERTS, tm), lambda i: (0, i)), pl.BlockSpec((8, tm), lambda i: (0, i)),
                   pl.BlockSpec((8, LANES), lambda i: (0, i // (MOE_BLOCK // tm)))],
        out_shape=[jax.ShapeDtypeStruct((t, D_MODEL), F32), jax.ShapeDtypeStruct((t, D_MODEL), BF16),
                   jax.ShapeDtypeStruct((N_EXPERTS, t), F32), jax.ShapeDtypeStruct((8, t), I32),
                   jax.ShapeDtypeStruct((8, t // MOE_BLOCK * LANES), F32)],
        scratch_shapes=[pltpu.VMEM((8, LANES), F32)],
        compiler_params=_params("arbitrary"),
        name="outproj",
    )(*x, *oda, *o_f, *o_b, rest, rest, mod, subln, retn, hgn, g64, w_out16, norm_w, wr_hi, wr_lo, rbias)


def _moe_plan(rank, totals, n_blocks):
    nb, tr, spb = MOE_BLOCK, MOE_TILE, MOE_SLOTS_PER_BLOCK
    n_slots = n_blocks * spb
    rank = rank[:N_GROUPS].reshape(N_GROUPS, n_blocks, nb)
    mb = rank >= 0
    mi = mb.astype(I32)
    count = totals[:N_GROUPS].reshape(N_GROUPS, n_blocks, LANES)[:, :, 0].astype(I32)
    tiles = (count + tr - 1) // tr
    t_end = jnp.cumsum(tiles, axis=0)
    t_off = t_end - tiles
    n_tiles = t_end[-1]
    dest = jnp.where(mb, t_off[..., None] * tr + rank, -1)
    order = jnp.cumsum(mi, axis=0) - mi
    n_mem = mi.sum(0)
    row_of = lambda k: jnp.where(n_mem > k, jnp.where(mb & (order == k), dest, 0).sum(0), -1)
    s = jnp.arange(n_slots, dtype=I32)
    s_blk, s_tile = s // spb, s % spb
    used = s_tile < jnp.repeat(n_tiles, spb)
    s_grp = (s_tile[None, :] >= jnp.repeat(t_end, spb, axis=1)).sum(0).astype(I32)
    key = jnp.where(used, s_grp, N_GROUPS) * n_slots + s
    pos = (key[None, :] < key[:, None]).sum(1).astype(I32)
    slot_of_item = jnp.where(pos[None, :] == s[:, None], s[None, :], 0).sum(1).astype(I32)
    n_valid = used.sum().astype(I32)
    src = slot_of_item[jnp.minimum(s, n_valid - 1)]
    item_g, item_b, item_tile = jnp.stack([s_grp, s_blk, s_tile])[:, src]
    blk = jnp.arange(n_blocks, dtype=I32)
    needs_hi = n_tiles > spb // 2
    hi_block = jnp.where((blk[None, :] <= blk[:, None]) & needs_hi[None, :], blk[None, :], 0).max(axis=1)
    return dict(dest=dest.reshape(N_GROUPS * n_blocks, 1, nb),
                row1=row_of(0).reshape(n_blocks, 1, nb), row2=row_of(1).reshape(n_blocks, 1, nb),
                n_tiles=n_tiles.astype(I32), hi_block=hi_block.astype(I32),
                item_g=item_g, item_b=item_b, item_tile=item_tile,
                item_valid=(s < n_valid).astype(I32), item_slot=slot_of_item)


def _moe_ffn_kernel(ig_ref, ib_ref, it_ref, iv_ref, is_ref, h_ref, dest_ref, gate_ref, wg_ref, wu_ref, wd_ref,
                    y_ref, wg16, wu16, wd16):
    i = pl.program_id(0)
    g = ig_ref[i]
    prev_g = ig_ref[jnp.maximum(i - 1, 0)]

    @pl.when((i == 0) | (g != prev_g))
    def _():
        wg16[...] = wg_ref[...].astype(BF16)
        wu16[...] = wu_ref[...].astype(BF16)
        wd16[...] = wd_ref[...].astype(BF16)

    @pl.when(iv_ref[i] == 0)
    def _():
        y_ref[...] = jnp.zeros_like(y_ref)

    @pl.when(iv_ref[i] == 1)
    def _():
        rows = lax.broadcasted_iota(I32, (MOE_TILE, 1), 0) + it_ref[i] * MOE_TILE
        sort = jnp.where(dest_ref[0] == rows, 1.0, 0.0).astype(BF16)
        x = _dot(sort, h_ref[...]).astype(BF16)
        gates = functools.reduce(jnp.add, [lax.dot_general(sort, p, _NT, preferred_element_type=F32)
                                           for p in _split_bf16(gate_ref[...])])
        lane = lax.broadcasted_iota(I32, (1, N_EXPERTS), 1)
        acc = jnp.zeros((MOE_TILE, D_MODEL), F32)
        for e in range(EXPERTS_PER_GROUP):
            ge = jnp.sum(jnp.where(lane == g * EXPERTS_PER_GROUP + e, gates, 0.0), axis=1, keepdims=True)
            act = _silu(_dot(x, wg16[e])) * _dot(x, wu16[e])
            acc = acc + _dot((act * ge).astype(BF16), wd16[e])
        y_ref[...] = acc.astype(BF16)


def _moe_unsort_kernel(nt_ref, hb_ref, ylo_ref, yhi_ref, r1_ref, r2_ref, x_ref, mod_ref, oa_ref, ob_ref, acc_ref, *,
                       n_ctx_blocks):
    b = pl.program_id(0)
    acc_ref[...] = jnp.zeros_like(acc_ref)
    chunk = max(MOE_TILE, MXU_DEPTH)
    tiles_per_chunk = chunk // MOE_TILE
    row1, row2 = r1_ref[0], r2_ref[0]

    def accumulate(y_part, first):
        def body(ci, carry):
            r0 = pl.multiple_of((ci - first) * chunk, chunk)
            rows = lax.broadcasted_iota(I32, (chunk, 1), 0) + ci * chunk
            sort = jnp.where(row1 == rows, 1.0, jnp.where(row2 == rows, 1.0, 0.0)).astype(BF16)
            acc_ref[...] += lax.dot_general(sort, y_part[pl.ds(r0, chunk), :], _TN, preferred_element_type=F32)
            return carry
        return body

    half = ylo_ref.shape[0] // chunk
    n_chunks = (nt_ref[b] + tiles_per_chunk - 1) // tiles_per_chunk
    lax.fori_loop(0, jnp.minimum(n_chunks, half), accumulate(ylo_ref, 0), 0)
    lax.fori_loop(half, jnp.maximum(n_chunks, half), accumulate(yhi_ref, half), 0)
    result = lambda: x_ref[...] + mod_ref[0][:, 5 * D_MODEL:6 * D_MODEL] * acc_ref[...]

    @pl.when(b < n_ctx_blocks)
    def _():
        oa_ref[...] = result()

    @pl.when(b >= n_ctx_blocks)
    def _():
        ob_ref[...] = result()


def _moe(hn2, gate_t, rank, totals, w_gate, w_up, w_down, x_mid, mod, st, layer):
    nb, tr, spb = st.tile, MOE_TILE, MOE_SLOTS_PER_BLOCK
    n_blocks = st.n_tiles
    plan = _moe_plan(rank, totals, n_blocks)

    wmap = lambda i, ig, ib, it, iv, isl: (layer * N_GROUPS + ig[i], 0, 0)
    once = pl.Buffered(1)
    y = pl.pallas_call(
        _moe_ffn_kernel,
        grid_spec=pltpu.PrefetchScalarGridSpec(
            num_scalar_prefetch=5,
            grid=(n_blocks * spb,),
            in_specs=[pl.BlockSpec((nb, D_MODEL), lambda i, ig, ib, it, iv, isl: (ib[i], 0)),
                      pl.BlockSpec((1, 1, nb), lambda i, ig, ib, it, iv, isl: (ig[i] * n_blocks + ib[i], 0, 0)),
                      pl.BlockSpec((N_EXPERTS, nb), lambda i, ig, ib, it, iv, isl: (0, ib[i])),
                      pl.BlockSpec((EXPERTS_PER_GROUP, D_MODEL, D_EXPERT), wmap, pipeline_mode=once),
                      pl.BlockSpec((EXPERTS_PER_GROUP, D_MODEL, D_EXPERT), wmap, pipeline_mode=once),
                      pl.BlockSpec((EXPERTS_PER_GROUP, D_EXPERT, D_MODEL), wmap, pipeline_mode=once)],
            out_specs=pl.BlockSpec((tr, D_MODEL), lambda i, ig, ib, it, iv, isl: (isl[i], 0)),
            scratch_shapes=[pltpu.VMEM((EXPERTS_PER_GROUP, D_MODEL, D_EXPERT), BF16),
                            pltpu.VMEM((EXPERTS_PER_GROUP, D_MODEL, D_EXPERT), BF16),
                            pltpu.VMEM((EXPERTS_PER_GROUP, D_EXPERT, D_MODEL), BF16)]),
        out_shape=jax.ShapeDtypeStruct((n_blocks * spb * tr, D_MODEL), BF16),
        compiler_params=_params("arbitrary"),
        name="moe_ffn",
    )(plan["item_g"], plan["item_b"], plan["item_tile"], plan["item_valid"], plan["item_slot"],
      hn2, plan["dest"], gate_t, w_gate, w_up, w_down)

    return pl.pallas_call(
        functools.partial(_moe_unsort_kernel, n_ctx_blocks=st.n_ctx),
        grid_spec=pltpu.PrefetchScalarGridSpec(
            num_scalar_prefetch=2,
            grid=(n_blocks,),
            in_specs=[pl.BlockSpec((spb // 2 * tr, D_MODEL), lambda b, nt, hb: (2 * b, 0)),
                      pl.BlockSpec((spb // 2 * tr, D_MODEL), lambda b, nt, hb: (2 * hb[b] + 1, 0)),
                      pl.BlockSpec((1, 1, nb), lambda b, nt, hb: (b, 0, 0)),
                      pl.BlockSpec((1, 1, nb), lambda b, nt, hb: (b, 0, 0)),
                      pl.BlockSpec((nb, D_MODEL), lambda b, nt, hb: (b, 0)),
                      pl.BlockSpec((1, 1, 6 * D_MODEL), lambda b, nt, hb: (st.mod_idx(b), 0, 0))],
            out_specs=[pl.BlockSpec((nb, D_MODEL), lambda b, nt, hb: (st.ctx_idx(b), 0)),
                       pl.BlockSpec((nb, D_MODEL), lambda b, nt, hb: (st.lat_idx(b), 0))],
            scratch_shapes=[pltpu.VMEM((nb, D_MODEL), F32)]),
        out_shape=[jax.ShapeDtypeStruct((st.n_ctx * nb, D_MODEL), F32),
                   jax.ShapeDtypeStruct(((n_blocks - st.n_ctx) * nb, D_MODEL), F32)],
        compiler_params=_params("arbitrary"),
        name="moe_unsort",
    )(plan["n_tiles"], plan["hi_block"], y, y, plan["row1"], plan["row2"], x_mid, mod)


def _block_avg(group):
    i = np.arange(LANES)
    return jnp.asarray((i[:, None] // group == i[None, :] // group) / group, dtype=BF16)


def _rope_tables(n_tokens, n_identity):
    rows = n_tokens // GRID_W
    pos_r = jnp.repeat(jnp.arange(rows, dtype=F32), GRID_W)
    pos_c = jnp.tile(jnp.arange(GRID_W, dtype=F32), rows)
    n_freq = DA_QK // 4
    inv = ROPE_BASE ** (-jnp.arange(n_freq, dtype=F32) / n_freq)
    ang = jnp.concatenate([pos_r[:, None] * inv, pos_c[:, None] * inv], axis=-1)
    cos = jnp.repeat(jnp.cos(ang), 2, axis=-1)
    sin = jnp.repeat(jnp.sin(ang), 2, axis=-1) * jnp.tile(jnp.asarray([-1.0, 1.0], F32), DA_QK // 2)
    cos = jnp.concatenate([jnp.ones((n_identity, DA_QK), F32), cos], axis=0)
    sin = jnp.concatenate([jnp.zeros((n_identity, DA_QK), F32), sin], axis=0)
    return jnp.tile(cos, (1, 2 * H_DA)), jnp.tile(sin, (1, 2 * H_DA))


def kernel(x_prompt, x_sample, cache_k, cache_v, state_ret, state_hgrn, c, c_ctx, w_in, w_out, w_ada, b_ada,
           norm_mix, norm_ffn, da_qnorm, da_knorm, da_lambda, da_subln, ret_decay, ret_norm, hg_lb, hg_norm,
           w_router, router_bias, w_gate, w_up, w_down):
    n_ctx, l_ctx, _ = x_prompt.shape
    n_lat, l_lat, _ = x_sample.shape
    t_ctx = n_ctx * l_ctx
    past = cache_k.shape[2]
    assert TOKEN_TILE % l_ctx == 0 and MOE_BLOCK % TOKEN_TILE == 0
    assert l_lat % MOE_BLOCK == 0 and t_ctx % MOE_BLOCK == 0 and l_lat % GRID_W == 0

    ctx_row = n_lat
    n_cond = -(-(n_lat + 1) // 8) * 8
    cond = jnp.zeros((n_cond, D_MODEL), F32).at[:n_lat].set(c).at[ctx_row].set(c_ctx)
    mod_all = _ada(cond, w_ada, b_ada)

    st = _Stream(t_ctx, n_lat, l_lat, TOKEN_TILE, ctx_row)
    st_moe = _Stream(t_ctx, n_lat, l_lat, MOE_BLOCK, ctx_row)
    g32, g64 = _block_avg(DA_QK), _block_avg(DA_V)
    cos_t, sin_t = _rope_tables(l_lat, TOKEN_TILE)
    lv = jnp.asarray(_level_table(REC_CHUNK))
    p_lb = jax.nn.softmax(hg_lb.astype(F32), axis=0)
    lb_all = jnp.cumsum(p_lb, axis=0) - p_lb[0]
    wr_hi = w_router.T.astype(BF16)
    wr_lo = (w_router.T - wr_hi.astype(F32)).astype(BF16)
    rbias = router_bias.astype(F32).reshape(N_EXPERTS, 1)
    flat_w = lambda w: w.reshape(DEPTH * N_EXPERTS, w.shape[2], w.shape[3])
    w_gate, w_up, w_down = flat_w(w_gate), flat_w(w_up), flat_w(w_down)
    flat_s = lambda s: s.astype(F32).reshape(n_lat, DEPTH, 2, D_REC, D_HEAD)
    states_in = (flat_s(state_ret), flat_s(state_hgrn))
    cache_k16 = cache_k.reshape(n_lat, DEPTH, past, D_QK).astype(BF16)
    cache_v16 = cache_v.reshape(n_lat, DEPTH, past, D_V).astype(BF16)

    x = (x_prompt.reshape(t_ctx, D_MODEL), x_sample.reshape(n_lat * l_lat, D_MODEL))
    caches, states_out = None, None
    for l in range(DEPTH):
        lam_init = 0.8 - 0.6 * math.exp(-0.3 * l)
        lp = da_lambda[l].astype(F32)
        lam = (jnp.exp(jnp.sum(lp[0] * lp[1])) - jnp.exp(jnp.sum(lp[2] * lp[3])) + lam_init).reshape(1)
        mod = mod_all[l].reshape(n_cond, 1, 6 * D_MODEL)
        qw = (jnp.tile(da_qnorm[l].astype(F32), 2 * H_DA) * (DA_QK ** -0.5)).reshape(1, D_QK)
        kw = jnp.tile(da_knorm[l].astype(F32), 2 * H_DA).reshape(1, D_QK)

        qb, kb, vb, rest, kc, vc = _inproj(x[0], x[1], mod, norm_mix[l].reshape(1, D_MODEL),
                                           w_in[l].astype(BF16), qw, kw, g32, cos_t, sin_t, st, l, n_ctx, l_ctx,
                                           caches)
        caches = (kc, vc)

        oda = (_attention(lam, qb, kb, vb, 0, n_ctx, l_ctx, CTX_PAIRS_PER_STEP),
               _attention(lam, qb, kb, vb, t_ctx, n_lat, l_lat, LAT_PAIRS_PER_STEP, l,
                          cache=(cache_k16, cache_v16)))

        ret_tabs = _retention_tables(-jnp.exp(ret_decay[l].astype(F32)), REC_CHUNK)
        lb_row = lb_all[l].reshape(1, D_REC)
        ofc, obc, s_ret, s_hg = _recurrence(rest, ret_tabs, lb_row, lv, 0, n_ctx, l_ctx, l,
                                            want_states=True, states_out=states_out)
        states_out = (s_ret, s_hg)
        ofl, obl = _recurrence(rest, ret_tabs, lb_row, lv, t_ctx, n_lat, l_lat, l, states_in=states_in)

        subln = (jnp.tile(da_subln[l].astype(F32), H_DA) * (1.0 - lam_init)).reshape(1, D_V)
        retn = jnp.tile(ret_norm[l].astype(F32), H_RET).reshape(1, D_REC)
        hgn = jnp.tile(hg_norm[l].astype(F32), H_HG).reshape(1, D_REC)
        x_mid, hn2, gate_t, rank, totals = _outproj(x, oda, (ofc, ofl), (obc, obl), rest, mod, subln, retn, hgn, g64,
                                      w_out[l].astype(BF16), norm_ffn[l].reshape(1, D_MODEL), wr_hi, wr_lo,
                                      rbias, st)
        x = _moe(hn2, gate_t, rank, totals, w_gate, w_up, w_down, x_mid, mod, st_moe, l)

    kc, vc = caches
    s_ret, s_hg = states_out
    return (x[0].reshape(n_ctx, l_ctx, D_MODEL), x[1].reshape(n_lat, l_lat, D_MODEL),
            kc.reshape(n_ctx, DEPTH, l_ctx, H_DA, 2, DA_QK), vc.reshape(n_ctx, DEPTH, l_ctx, H_DA, DA_V),
            s_ret.reshape(n_ctx, DEPTH, 2, H_RET, RET_DK, RET_DV),
            s_hg.reshape(n_ctx, DEPTH, 2, H_HG, HG_DK, HG_DV))
```

```python
import functools
import math

import numpy as np
import jax
import jax.numpy as jnp
from jax import lax
from jax.experimental import pallas as pl
from jax.experimental.pallas import tpu as pltpu

F32 = jnp.float32
BF16 = jnp.bfloat16
I32 = jnp.int32

D_MODEL = 1024
DEPTH = 2
GRID_W = 64
EPS = 1e-6
MIN_GATE = 1e-30
ROPE_BASE = 10000.0
H_DA, DA_QK, DA_V = 8, 32, 64
H_RET, RET_DK, RET_DV = 4, 64, 64
H_HG, HG_DK, HG_DV = 4, 64, 64
D_QK = H_DA * 2 * DA_QK
D_V = H_DA * DA_V
N_REC_HEADS = 4
D_HEAD = 64
D_REC = N_REC_HEADS * D_HEAD
D_REST = 9 * D_REC
D_IN = 2 * D_QK + D_V + D_REST
N_EXPERTS, N_GROUPS, EXPERTS_PER_GROUP = 16, 4, 4
D_EXPERT = 512
MASKED_SCORE = -2.0

LANES = 128
TOKEN_TILE = 512
Q_TILE = 1024
CTX_PAIRS_PER_STEP = 1
LAT_PAIRS_PER_STEP = 4
REC_CHUNK = 128
MOE_BLOCK = 1024
MOE_TILE = 256
MXU_DEPTH = 256
MOE_SLOTS_PER_BLOCK = -(-2 * MOE_BLOCK // MOE_TILE) + N_GROUPS
VMEM_LIMIT = 56 * 1024 * 1024

_NT = (((1,), (1,)), ((), ()))
_TN = (((0,), (0,)), ((), ()))


def _params(*sem):
    return pltpu.CompilerParams(dimension_semantics=sem, vmem_limit_bytes=VMEM_LIMIT)


def _dot(a, b):
    return jnp.dot(a, b, preferred_element_type=F32)


def _split_bf16(t, terms=2):
    out = []
    for _ in range(terms - 1):
        hi = t.astype(BF16)
        out.append(hi)
        t = t - hi.astype(F32)
    out.append(t.astype(BF16))
    return out


def _group_mean_sq(t, g):
    sq = (t * t).astype(BF16)
    outs = [_dot(sq[:, s * LANES:(s + 1) * LANES], g) for s in range(t.shape[1] // LANES)]
    return jnp.concatenate(outs, axis=1) if len(outs) > 1 else outs[0]


def _silu(t):
    return t * jax.nn.sigmoid(t)


class _Stream:
    def __init__(self, n_ctx_rows, n_lat_seq, lat_len, tile, ctx_row):
        self.tile = tile
        self.n_ctx = n_ctx_rows // tile
        self.per_seq = lat_len // tile
        self.n_tiles = self.n_ctx + n_lat_seq * self.per_seq
        self.ctx_row = ctx_row

    def ctx_idx(self, i):
        return jnp.minimum(i, self.n_ctx - 1)

    def lat_idx(self, i):
        return jnp.maximum(i - self.n_ctx, 0)

    def mod_idx(self, i):
        return jnp.where(i < self.n_ctx, self.ctx_row, (i - self.n_ctx) // self.per_seq)

    def pos_idx(self, i):
        return jnp.where(i < self.n_ctx, 0, 1 + (i - self.n_ctx) % self.per_seq)


def _ada_kernel(c_ref, w_ref, b_ref, o_ref):
    s = _silu(c_ref[...]).astype(BF16)
    o_ref[0] = _dot(s, w_ref[0].astype(BF16)) + b_ref[0]


def _ada(cond, w_ada, b_ada):
    rows = cond.shape[0]
    n_tile = 1536
    return pl.pallas_call(
        _ada_kernel,
        grid=(DEPTH, 6 * D_MODEL // n_tile),
        in_specs=[pl.BlockSpec((rows, D_MODEL), lambda l, j: (0, 0)),
                  pl.BlockSpec((1, D_MODEL, n_tile), lambda l, j: (l, 0, j)),
                  pl.BlockSpec((1, 1, n_tile), lambda l, j: (l, 0, j))],
        out_specs=pl.BlockSpec((1, rows, n_tile), lambda l, j: (l, 0, j)),
        out_shape=jax.ShapeDtypeStruct((DEPTH, rows, 6 * D_MODEL), F32),
        compiler_params=_params("arbitrary", "arbitrary"),
        name="ada",
    )(cond, w_ada, b_ada.reshape(DEPTH, 1, 6 * D_MODEL))


def _inproj_kernel(*refs, n_ctx_tiles, first):
    (xa_ref, xb_ref, mod_ref, nw_ref, w_ref, qw_ref, kw_ref, g32_ref, cos_ref, sin_ref) = refs[:10]
    q_ref, kb_ref, vb_ref, rest_ref, kc_ref, vc_ref = refs[10:] if first else refs[12:]
    is_ctx = pl.program_id(0) < n_ctx_tiles
    x = jnp.where(is_ctx, xa_ref[...], xb_ref[...])
    y = x * lax.rsqrt(jnp.mean(x * x, axis=-1, keepdims=True) + EPS) * nw_ref[...]
    mod = mod_ref[0]
    hn = (y * (1.0 + mod[:, D_MODEL:2 * D_MODEL]) + mod[:, 0:D_MODEL]).astype(BF16)
    g32 = g32_ref[...]
    lane = lax.broadcasted_iota(I32, (1, D_QK), 1)
    even = (lane & 1) == 0

    def qk_norm_rope(t, w):
        t = t * lax.rsqrt(_group_mean_sq(t, g32) + EPS) * w
        partner = jnp.where(even, pltpu.roll(t, D_QK - 1, 1), pltpu.roll(t, 1, 1))
        return t * cos_ref[...] + partner * sin_ref[...]

    q_ref[...] = qk_norm_rope(_dot(hn, w_ref[:, 0:D_QK]), qw_ref[...]).astype(BF16)
    k = qk_norm_rope(_dot(hn, w_ref[:, D_QK:2 * D_QK]), kw_ref[...])
    kb_ref[...] = k.astype(BF16)
    v = _dot(hn, w_ref[:, 2 * D_QK:2 * D_QK + D_V])
    vb_ref[...] = v.astype(BF16)
    rest_ref[...] = _dot(hn, w_ref[:, 2 * D_QK + D_V:D_IN])

    @pl.when(is_ctx)
    def _():
        n_seq, _, seq_len, _ = kc_ref.shape
        kc_ref[:, 0] = k.reshape(n_seq, seq_len, D_QK)
        vc_ref[:, 0] = v.reshape(n_seq, seq_len, D_V)
        if first:
            kc_ref[:, 1:] = jnp.zeros_like(kc_ref[:, 1:])
            vc_ref[:, 1:] = jnp.zeros_like(vc_ref[:, 1:])


def _inproj(xa, xb, mod, norm_w, w_in16, qw, kw, g32, cos_t, sin_t, st, layer, n_ctx_seq, l_ctx, caches):
    tm = st.tile
    t = st.n_tiles * tm
    first = caches is None
    row = lambda i: (i, 0)
    const = lambda i: (0, 0)
    in_specs = [pl.BlockSpec((tm, D_MODEL), lambda i: (st.ctx_idx(i), 0)),
                pl.BlockSpec((tm, D_MODEL), lambda i: (st.lat_idx(i), 0)),
                pl.BlockSpec((1, 1, 6 * D_MODEL), lambda i: (st.mod_idx(i), 0, 0)),
                pl.BlockSpec((1, D_MODEL), const),
                pl.BlockSpec((D_MODEL, D_IN), const, pipeline_mode=pl.Buffered(1)),
                pl.BlockSpec((1, D_QK), const),
                pl.BlockSpec((1, D_QK), const),
                pl.BlockSpec((LANES, LANES), const),
                pl.BlockSpec((tm, D_QK), lambda i: (st.pos_idx(i), 0)),
                pl.BlockSpec((tm, D_QK), lambda i: (st.pos_idx(i), 0))]
    args = [xa, xb, mod, norm_w, w_in16, qw, kw, g32, cos_t, sin_t]
    aliases = {}
    per_tile = tm // l_ctx
    if first:
        cache_spec = pl.BlockSpec((per_tile, DEPTH, l_ctx, D_QK), lambda i: (st.ctx_idx(i), 0, 0, 0))
    else:
        cache_spec = pl.BlockSpec((per_tile, 1, l_ctx, D_QK), lambda i: (st.ctx_idx(i), layer, 0, 0))
        in_specs += [pl.BlockSpec(memory_space=pl.ANY)] * 2
        args += list(caches)
        aliases = {10: 4, 11: 5}
    cache_shape = jax.ShapeDtypeStruct((n_ctx_seq, DEPTH, l_ctx, D_QK), F32)
    return pl.pallas_call(
        functools.partial(_inproj_kernel, n_ctx_tiles=st.n_ctx, first=first),
        grid=(st.n_tiles,),
        in_specs=in_specs,
        out_specs=[pl.BlockSpec((tm, D_QK), row), pl.BlockSpec((tm, D_QK), row),
                   pl.BlockSpec((tm, D_V), row), pl.BlockSpec((tm, D_REST), row), cache_spec, cache_spec],
        out_shape=[jax.ShapeDtypeStruct((t, D_QK), BF16), jax.ShapeDtypeStruct((t, D_QK), BF16),
                   jax.ShapeDtypeStruct((t, D_V), BF16), jax.ShapeDtypeStruct((t, D_REST), F32),
                   cache_shape, cache_shape],
        input_output_aliases=aliases,
        compiler_params=_params("arbitrary"),
        name="inproj",
    )(*args)


def _attn_kernel(*refs, with_cache):
    if with_cache:
        lam_ref, q_ref, k_ref, v_ref, ck_ref, cv_ref, o_ref = refs
    else:
        lam_ref, q_ref, k_ref, v_ref, o_ref = refs
    lam = lam_ref[0]
    lane = lax.broadcasted_iota(I32, (1, LANES), 1)
    left = lane < DA_V
    one = jnp.ones((), BF16)
    zero = jnp.zeros((), BF16)
    owns = [left, jnp.logical_not(left)]
    tq = min(Q_TILE, q_ref.shape[0])
    for p in range(q_ref.shape[1] // LANES):
        slab = slice(p * LANES, (p + 1) * LANES)
        segs = [(k_ref[:, slab], v_ref[:, slab])]
        if with_cache:
            segs.append((ck_ref[0, 0, :, slab], cv_ref[0, 0, :, slab]))
        vms = [[jnp.where(own, v, one) for _, v in segs] for own in owns]

        def q_block(i, carry, slab=slab, segs=segs, vms=vms):
            r0 = i * tq if isinstance(i, int) else pl.multiple_of(i * tq, tq)
            q = q_ref[pl.ds(r0, tq), slab]
            acc = jnp.zeros((tq, LANES), F32)
            for side in range(2):
                for r in range(2):
                    lo = side * DA_V + r * DA_QK
                    qm = jnp.where((lane >= lo) & (lane < lo + DA_QK), q, zero)
                    ss = [lax.dot_general(qm, k, _NT, preferred_element_type=F32) for k, _ in segs]
                    m = functools.reduce(jnp.maximum, [jnp.max(s, axis=-1, keepdims=True) for s in ss])
                    res = functools.reduce(
                        jnp.add, [_dot(jnp.exp(s - m).astype(BF16), vm) for s, vm in zip(ss, vms[side])])
                    den = pltpu.roll(res, DA_V, 1)
                    coef = 1.0 if r == 0 else -lam
                    acc = acc + jnp.where(owns[side], coef * (res / den), 0.0)
            o_ref[pl.ds(r0, tq), slab] = acc
            return carry

        if q_ref.shape[0] == tq:
            q_block(0, 0)
        else:
            lax.fori_loop(0, q_ref.shape[0] // tq, q_block, 0)


def _attention(lam, qb, kb, vb, row0, n_seq, seq_len, pairs_per_step, layer=0, cache=None):
    sb0 = row0 // seq_len
    width = pairs_per_step * LANES
    smap = lambda b, p: (sb0 + b, p)
    smem = pl.BlockSpec(memory_space=pltpu.SMEM)
    in_specs = [smem] + [pl.BlockSpec((seq_len, width), smap)] * 3
    args = [lam, qb, kb, vb]
    if cache is not None:
        ck, cv = cache
        cmap = lambda b, p: (b, layer, 0, p)
        in_specs += [pl.BlockSpec((1, 1, ck.shape[2], width), cmap),
                     pl.BlockSpec((1, 1, cv.shape[2], width), cmap)]
        args += [ck, cv]
    return pl.pallas_call(
        functools.partial(_attn_kernel, with_cache=cache is not None),
        grid=(n_seq, D_QK // width),
        in_specs=in_specs,
        out_specs=pl.BlockSpec((seq_len, width), lambda b, p: (b, p)),
        out_shape=jax.ShapeDtypeStruct((n_seq * seq_len, D_V), F32),
        compiler_params=_params("arbitrary", "arbitrary"),
        name="attn_lat" if cache is not None else "attn_ctx",
    )(*args)


def _level_table(c):
    t = np.arange(c)[:, None] ^ np.arange(c)[None, :]
    lv = np.where(t == 0, 32, 31 - np.floor(np.log2(np.maximum(t, 1))).astype(np.int64))
    return np.tile(lv, (N_REC_HEADS, 1)).astype(np.int32)


def _retention_tables(log_gamma, c):
    g = jnp.repeat(log_gamma, D_HEAD, axis=1)
    t = jnp.arange(c, dtype=F32)
    diff = t[:, None] - t[None, :]
    gh = log_gamma[:, :, None, None]
    d_f = jnp.where(diff >= 0, jnp.exp(gh[0] * jnp.maximum(diff, 0.0)), 0.0)
    d_b = jnp.where(diff <= 0, jnp.exp(gh[1] * jnp.maximum(-diff, 0.0)), 0.0)
    d = jnp.stack([d_f, d_b]).reshape(2, N_REC_HEADS * c, c)
    q_f, k_f = jnp.exp(g[0] * (t[:, None] + 1.0)), jnp.exp(g[0] * (c - 1.0 - t[:, None]))
    q_b, k_b = jnp.exp(g[1] * (c - t[:, None])), jnp.exp(g[1] * t[:, None])
    f = jnp.stack([jnp.stack([q_f, k_f]), jnp.stack([q_b, k_b])])
    a = jnp.exp(g * c).reshape(2, 1, D_REC)
    return d, f, a


def _rec_kernel(*refs, state_in, state_out, aliased):
    refs = list(refs)
    rf_ref, rb_ref, rd_ref, rfac_ref, ra_ref, lb_ref, lv_ref = refs[:7]
    del refs[:7]
    if state_in:
        sret_ref, shg_ref, t4_ref = refs[:3]
        del refs[:3]
    if state_out:
        t4t_ref = refs.pop(0)
    del refs[:aliased]
    of_ref, ob_ref = refs[:2]
    del refs[:2]
    if state_out:
        oret_ref, ohg_ref = refs[:2]
        del refs[:2]
    (st_scr,) = refs

    ci = pl.program_id(1)
    c = rf_ref.shape[0]
    lane = lax.broadcasted_iota(I32, (1, D_REC), 1)
    head = lane >> 6
    row = lax.broadcasted_iota(I32, (c, 1), 0)
    srow = lax.broadcasted_iota(I32, (D_REC, 1), 0)
    same_head = (srow >> 6) == head
    eye = srow == lane
    head_is = [head == h for h in range(N_REC_HEADS)]
    head_m16 = [jnp.broadcast_to(jnp.where(m, 1.0, 0.0), (c, D_REC)).astype(BF16) for m in head_is]
    head_m16_half = [jnp.broadcast_to(jnp.where(m, 1.0, 0.0), (c // 2, D_REC)).astype(BF16) for m in head_is]

    @pl.when(ci == 0)
    def _():
        if state_in:
            for d in range(2):
                for idx, ref in ((d, sret_ref), (2 + d, shg_ref)):
                    tiled = functools.reduce(jnp.add, [_dot(p, t4_ref[...]) for p in _split_bf16(ref[0, 0, d], 3)])
                    st_scr[idx] = jnp.where(same_head, tiled, 0.0)
        else:
            st_scr[...] = jnp.zeros_like(st_scr)

    def expand(t16):
        return jnp.concatenate([t16 * m for m in head_m16], axis=0)

    def nt(a, b):
        return lax.dot_general(a, b, _NT, preferred_element_type=F32)

    def half_rows(t, h, second):
        o = h if second else 0
        return jnp.concatenate([t[j * 2 * h + o:j * 2 * h + o + h] for j in range(c // (2 * h))], axis=0)

    def spread_rows(t, h, second):
        z = jnp.zeros((h, t.shape[1]), t.dtype)
        parts = []
        for j in range(c // (2 * h)):
            parts += [z, t[j * h:(j + 1) * h]] if second else [t[j * h:(j + 1) * h], z]
        return jnp.concatenate(parts, axis=0)

    def finish(q, k, v16, scores, qfac, kfac, a_row, st_ref, own_term=False):
        st = st_ref[...]
        o = _dot((q * qfac).astype(BF16), st.astype(BF16))
        if own_term:
            o = o + _dot((q * k).astype(BF16), jnp.where(same_head, 1.0, 0.0).astype(BF16)) * v16.astype(F32)
        o_stack = _dot(scores.astype(BF16), v16)
        for h in range(N_REC_HEADS):
            o = o + jnp.where(head_is[h], o_stack[h * c:(h + 1) * c], 0.0)
        kv = lax.dot_general((k * kfac).astype(BF16), v16, _TN, preferred_element_type=F32)
        a_col = jnp.sum(jnp.where(eye, a_row, 0.0), axis=1, keepdims=True)
        st_ref[...] = st * a_col + jnp.where(same_head, kv, 0.0)
        return o

    def tree_scores(q, k, la, reverse):
        lv = lv_ref[...]
        scores = jnp.zeros(lv.shape, F32)
        pre, tot = la, la
        h, level = 1, 31
        while h < c:
            bit = (row & h) != 0
            query_side = jnp.logical_not(bit) if reverse else bit
            x = jnp.exp(jnp.minimum(jnp.where(query_side, pre, tot - pre), 0.0))
            qx = q * x
            kx = jnp.where(query_side, 0.0, k * x).astype(BF16)
            if h < 8:
                s_level = nt(expand(qx.astype(BF16)), kx)
            else:
                qc = half_rows(qx, h, not reverse).astype(BF16)
                sc = nt(jnp.concatenate([qc * m for m in head_m16_half], axis=0), kx)
                s_level = jnp.concatenate([spread_rows(sc[hh * c // 2:(hh + 1) * c // 2], h, not reverse)
                                           for hh in range(N_REC_HEADS)], axis=0)
            scores = jnp.where(lv == level, s_level, scores)
            partner = jnp.where(bit, pltpu.roll(tot, h, 0), pltpu.roll(tot, c - h, 0))
            pre = pre + jnp.where(query_side, partner, 0.0)
            tot = tot + partner
            h, level = 2 * h, level - 1
        return scores, pre, tot

    lb = lb_ref[...]
    col = lambda ref, j: ref[:, j * D_REC:(j + 1) * D_REC]
    for d, (r_ref, o_ref) in enumerate(((rf_ref, of_ref), (rb_ref, ob_ref))):
        q, k, v16 = col(r_ref, 0), col(r_ref, 1) * (RET_DK ** -0.5), col(r_ref, 2).astype(BF16)
        scores = nt(expand(q.astype(BF16)), k.astype(BF16)) * rd_ref[d]
        o_ref[:, 0:D_REC] = finish(q, k, v16, scores, rfac_ref[d, 0], rfac_ref[d, 1], ra_ref[d], st_scr.at[d])
        z = col(r_ref, 5 + d)
        la = jnp.log(jnp.maximum(lb + (1.0 - lb) * jax.nn.sigmoid(z), MIN_GATE))
        q, k, v16 = _silu(col(r_ref, 4)), (1.0 - lb) * jax.nn.sigmoid(-z), col(r_ref, 7).astype(BF16)
        scores, pre, tot = tree_scores(q, k, la, reverse=d == 1)
        o_ref[:, D_REC:2 * D_REC] = finish(q, k, v16, scores, jnp.exp(jnp.minimum(pre, 0.0)),
                                           jnp.exp(jnp.minimum(tot - pre, 0.0)), jnp.exp(tot[0:1, :]),
                                           st_scr.at[2 + d], own_term=True)

    if state_out:
        @pl.when(ci == pl.num_programs(1) - 1)
        def _():
            for d in range(2):
                for idx, ref in ((d, oret_ref), (2 + d, ohg_ref)):
                    ref[0, 0, d] = functools.reduce(
                        jnp.add, [_dot(p, t4t_ref[...]) for p in _split_bf16(st_scr[idx], 3)])
            if state_out == "first":
                oret_ref[0, 1:] = jnp.zeros_like(oret_ref[0, 1:])
                ohg_ref[0, 1:] = jnp.zeros_like(ohg_ref[0, 1:])


def _recurrence(rest, ret_tabs, lb_row, lv, row0, n_seq, seq_len, layer, *, states_in=None, want_states=False,
                states_out=None):
    c = REC_CHUNK
    nc = seq_len // c
    cb0 = row0 // c
    const2 = lambda b, i: (0, 0)
    const3 = lambda b, i: (0, 0, 0)
    rd, rfac, ra = ret_tabs
    in_specs = [pl.BlockSpec((c, D_REST), lambda b, i: (cb0 + b * nc + i, 0)),
                pl.BlockSpec((c, D_REST), lambda b, i: (cb0 + b * nc + nc - 1 - i, 0)),
                pl.BlockSpec((2, N_REC_HEADS * c, c), const3),
                pl.BlockSpec((2, 2, c, D_REC), lambda b, i: (0, 0, 0, 0)),
                pl.BlockSpec((2, 1, D_REC), const3),
                pl.BlockSpec((1, D_REC), const2),
                pl.BlockSpec((N_REC_HEADS * c, c), const2)]
    args = [rest, rest, rd, rfac, ra, lb_row, lv]
    tile4 = jnp.tile(jnp.eye(D_HEAD, dtype=BF16), (1, N_REC_HEADS))
    if states_in is not None:
        blk = pl.BlockSpec((1, 1, 2, D_REC, D_HEAD), lambda b, i: (b, layer, 0, 0, 0))
        in_specs += [blk, blk, pl.BlockSpec((D_HEAD, D_REC), const2)]
        args += [states_in[0], states_in[1], tile4]
    if want_states:
        in_specs += [pl.BlockSpec((D_REC, D_HEAD), const2)]
        args += [tile4.T]
    out_specs = [pl.BlockSpec((c, 2 * D_REC), lambda b, i: (b * nc + i, 0)),
                 pl.BlockSpec((c, 2 * D_REC), lambda b, i: (b * nc + nc - 1 - i, 0))]
    out_shape = [jax.ShapeDtypeStruct((n_seq * seq_len, 2 * D_REC), F32)] * 2
    aliases = {}
    state_mode = None
    if want_states:
        st_shape = jax.ShapeDtypeStruct((n_seq, DEPTH, 2, D_REC, D_HEAD), F32)
        out_shape += [st_shape, st_shape]
        if states_out is None:
            state_mode = "first"
            out_specs += [pl.BlockSpec((1, DEPTH, 2, D_REC, D_HEAD), lambda b, i: (b, 0, 0, 0, 0))] * 2
        else:
            state_mode = "next"
            out_specs += [pl.BlockSpec((1, 1, 2, D_REC, D_HEAD), lambda b, i: (b, layer, 0, 0, 0))] * 2
            aliases = {len(args): 2, len(args) + 1: 3}
            in_specs += [pl.BlockSpec(memory_space=pl.ANY)] * 2
            args += list(states_out)
    return pl.pallas_call(
        functools.partial(_rec_kernel, state_in=states_in is not None, state_out=state_mode,
                          aliased=len(aliases)),
        grid=(n_seq, nc),
        in_specs=in_specs,
        out_specs=out_specs,
        out_shape=out_shape,
        scratch_shapes=[pltpu.VMEM((4, D_REC, D_REC), F32)],
        input_output_aliases=aliases,
        compiler_params=_params("arbitrary", "arbitrary"),
        name="rec_lat" if states_in is not None else "rec_ctx",
    )(*args)


def _route(logits, bias):
    scores = jax.nn.sigmoid(logits)
    sel = scores + bias
    srow = [scores[e:e + 1, :] for e in range(N_EXPERTS)]
    rows = [sel[e:e + 1, :] for e in range(N_EXPERTS)]
    gs = []
    for g in range(N_GROUPS):
        a, b, c, d = rows[4 * g:4 * g + 4]
        gs.append(functools.reduce(jnp.maximum, [a + b, a + c, a + d, b + c, b + d, c + d]))
    best = jnp.zeros_like(gs[0], dtype=I32)
    best_v = gs[0]
    for g in range(1, N_GROUPS):
        upd = gs[g] > best_v
        best = jnp.where(upd, g, best)
        best_v = jnp.where(upd, gs[g], best_v)
    masked = [jnp.where(best == e // EXPERTS_PER_GROUP, rows[e], MASKED_SCORE) for e in range(N_EXPERTS)]
    i1 = jnp.zeros_like(best)
    v1 = masked[0]
    for e in range(1, N_EXPERTS):
        upd = masked[e] > v1
        i1 = jnp.where(upd, e, i1)
        v1 = jnp.where(upd, masked[e], v1)
    i2 = jnp.zeros_like(best)
    v2 = jnp.full_like(v1, -jnp.inf)
    for e in range(N_EXPERTS):
        upd = (masked[e] > v2) & (i1 != e)
        i2 = jnp.where(upd, e, i2)
        v2 = jnp.where(upd, masked[e], v2)
    w1 = functools.reduce(jnp.add, [jnp.where(i1 == e, srow[e], 0.0) for e in range(N_EXPERTS)])
    w2 = functools.reduce(jnp.add, [jnp.where(i2 == e, srow[e], 0.0) for e in range(N_EXPERTS)])
    tot = w1 + w2
    w1, w2 = w1 / tot, w2 / tot
    expert = lax.broadcasted_iota(I32, logits.shape, 0)
    return jnp.where(expert == i1, w1, 0.0) + jnp.where(expert == i2, w2, 0.0)


def _outproj_kernel(xa_ref, xb_ref, odaa_ref, odab_ref, ofa_ref, ofb_ref, oba_ref, obb_ref, rg_ref, gg_ref,
                    mod_ref, subln_ref, retn_ref, hgn_ref, g64_ref, wout_ref, nffn_ref, wrh_ref, wrl_ref, rb_ref,
                    tri_ref, xmid_ref, hn2_ref, gate_ref, rank_ref, total_ref, count_ref, *, n_ctx_tiles):
    g64 = g64_ref[...]
    is_ctx = pl.program_id(0) < n_ctx_tiles
    pick = lambda a_ref, b_ref: jnp.where(is_ctx, a_ref[...], b_ref[...])

    def gnorm(t, w):
        return t * lax.rsqrt(_group_mean_sq(t, g64) + EPS) * w

    o_da = gnorm(pick(odaa_ref, odab_ref), subln_ref[...])
    o_rec = pick(ofa_ref, ofb_ref) + pick(oba_ref, obb_ref)
    o_ret = gnorm(o_rec[:, 0:D_REC], retn_ref[...]) * _silu(rg_ref[...])
    o_hg = gnorm(o_rec[:, D_REC:2 * D_REC], hgn_ref[...]) * _silu(gg_ref[...])
    mixed = jnp.concatenate([o_da, o_ret, o_hg], axis=1).astype(BF16)
    mod = mod_ref[0]
    x = pick(xa_ref, xb_ref) + mod[:, 2 * D_MODEL:3 * D_MODEL] * _dot(mixed, wout_ref[...])
    xmid_ref[...] = x
    y = x * lax.rsqrt(jnp.mean(x * x, axis=-1, keepdims=True) + EPS) * nffn_ref[...]
    hn = y * (1.0 + mod[:, 4 * D_MODEL:5 * D_MODEL]) + mod[:, 3 * D_MODEL:4 * D_MODEL]
    hi, lo = _split_bf16(hn)
    hn2_ref[...] = hi
    wrh = wrh_ref[...]
    logits = (lax.dot_general(wrh, hi, _NT, preferred_element_type=F32)
              + lax.dot_general(wrh, lo, _NT, preferred_element_type=F32)
              + lax.dot_general(wrl_ref[...], hi, _NT, preferred_element_type=F32))
    gate = _route(logits, rb_ref[...])
    gate_ref[...] = gate

    tm = gate.shape[1]
    member = jnp.concatenate(
        [jnp.max(gate[EXPERTS_PER_GROUP * g:EXPERTS_PER_GROUP * (g + 1)], axis=0, keepdims=True)
         for g in range(N_GROUPS)] + [jnp.zeros((8 - N_GROUPS, tm), F32)], axis=0)
    member = jnp.where(member > 0, 1.0, 0.0)
    before = _dot(member.astype(BF16), tri_ref[...])

    @pl.when(pl.program_id(0) % (MOE_BLOCK // tm) == 0)
    def _():
        count_ref[...] = jnp.zeros_like(count_ref)

    rank_ref[...] = jnp.where(member > 0, before + count_ref[:, 0:1], -1.0).astype(I32)
    count_ref[...] += jnp.sum(member, axis=1, keepdims=True)
    total_ref[...] = count_ref[...]


def _outproj(x, oda, o_f, o_b, rest, mod, subln, retn, hgn, g64, w_out16, norm_w, wr_hi, wr_lo, rbias, st):
    tm = st.tile
    t = st.n_tiles * tm
    row = lambda i: (i, 0)
    const = lambda i: (0, 0)
    pair = lambda width: [pl.BlockSpec((tm, width), lambda i: (st.ctx_idx(i), 0)),
                          pl.BlockSpec((tm, width), lambda i: (st.lat_idx(i), 0))]
    return pl.pallas_call(
        functools.partial(_outproj_kernel, n_ctx_tiles=st.n_ctx),
        grid=(st.n_tiles,),
        in_specs=pair(D_MODEL) + pair(D_V) + pair(2 * D_REC) + pair(2 * D_REC) + [
            pl.BlockSpec((tm, D_REC), lambda i: (i, 3)),
            pl.BlockSpec((tm, D_REC), lambda i: (i, 8)),
            pl.BlockSpec((1, 1, 6 * D_MODEL), lambda i: (st.mod_idx(i), 0, 0)),
            pl.BlockSpec((1, D_V), const),
            pl.BlockSpec((1, D_REC), const),
            pl.BlockSpec((1, D_REC), const),
            pl.BlockSpec((LANES, LANES), const),
            pl.BlockSpec((D_MODEL, D_MODEL), const, pipeline_mode=pl.Buffered(1)),
            pl.BlockSpec((1, D_MODEL), const),
            pl.BlockSpec((N_EXPERTS, D_MODEL), const),
            pl.BlockSpec((N_EXPERTS, D_MODEL), const),
            pl.BlockSpec((N_EXPERTS, 1), const),
            pl.BlockSpec((tm, tm), const)],
        out_specs=[pl.BlockSpec((tm, D_MODEL), row), pl.BlockSpec((tm, D_MODEL), row),
                   pl.BlockSpec((N_EXPERTS, tm), lambda i: (0, i)), pl.BlockSpec((8, tm), lambda i: (0, i)),
                   pl.BlockSpec((8, LANES), lambda i: (0, i // (MOE_BLOCK // tm)))],
        out_shape=[jax.ShapeDtypeStruct((t, D_MODEL), F32), jax.ShapeDtypeStruct((t, D_MODEL), BF16),
                   jax.ShapeDtypeStruct((N_EXPERTS, t), F32), jax.ShapeDtypeStruct((8, t), I32),
                   jax.ShapeDtypeStruct((8, t // MOE_BLOCK * LANES), F32)],
        scratch_shapes=[pltpu.VMEM((8, LANES), F32)],
        compiler_params=_params("arbitrary"),
        name="outproj",
    )(*x, *oda, *o_f, *o_b, rest, rest, mod, subln, retn, hgn, g64, w_out16, norm_w, wr_hi, wr_lo, rbias,
      jnp.asarray(np.triu(np.ones((tm, tm), np.float32), 1), dtype=BF16))


def _moe_plan(rank, totals, n_blocks):
    nb, tr, spb = MOE_BLOCK, MOE_TILE, MOE_SLOTS_PER_BLOCK
    n_slots = n_blocks * spb
    rank = rank[:N_GROUPS].reshape(N_GROUPS, n_blocks, nb)
    mb = rank >= 0
    mi = mb.astype(I32)
    count = totals[:N_GROUPS].reshape(N_GROUPS, n_blocks, LANES)[:, :, 0].astype(I32)
    tiles = (count + tr - 1) // tr
    t_end = jnp.cumsum(tiles, axis=0)
    t_off = t_end - tiles
    n_tiles = t_end[-1]
    dest = jnp.where(mb, t_off[..., None] * tr + rank, -1)
    order = jnp.cumsum(mi, axis=0) - mi
    n_mem = mi.sum(0)
    row_of = lambda k: jnp.where(n_mem > k, jnp.where(mb & (order == k), dest, 0).sum(0), -1)
    s = jnp.arange(n_slots, dtype=I32)
    s_blk, s_tile = s // spb, s % spb
    used = s_tile < jnp.repeat(n_tiles, spb)
    s_grp = (s_tile[None, :] >= jnp.repeat(t_end, spb, axis=1)).sum(0).astype(I32)
    key = jnp.where(used, s_grp, N_GROUPS) * n_slots + s
    pos = (key[None, :] < key[:, None]).sum(1).astype(I32)
    slot_of_item = jnp.where(pos[None, :] == s[:, None], s[None, :], 0).sum(1).astype(I32)
    n_valid = used.sum().astype(I32)
    src = slot_of_item[jnp.minimum(s, n_valid - 1)]
    item_g, item_b, item_tile = jnp.stack([s_grp, s_blk, s_tile])[:, src]
    blk = jnp.arange(n_blocks, dtype=I32)
    needs_hi = n_tiles > spb // 2
    hi_block = jnp.where((blk[None, :] <= blk[:, None]) & needs_hi[None, :], blk[None, :], 0).max(axis=1)
    return dict(dest=dest.reshape(N_GROUPS * n_blocks, 1, nb),
                row1=row_of(0).reshape(n_blocks, 1, nb), row2=row_of(1).reshape(n_blocks, 1, nb),
                n_tiles=n_tiles.astype(I32), hi_block=hi_block.astype(I32),
                item_g=item_g, item_b=item_b, item_tile=item_tile,
                item_valid=(s < n_valid).astype(I32), item_slot=slot_of_item)


def _moe_ffn_kernel(ig_ref, ib_ref, it_ref, iv_ref, is_ref, h_ref, dest_ref, gate_ref, wg_ref, wu_ref, wd_ref,
                    y_ref, wg16, wu16, wd16):
    i = pl.program_id(0)
    g = ig_ref[i]
    prev_g = ig_ref[jnp.maximum(i - 1, 0)]

    @pl.when((i == 0) | (g != prev_g))
    def _():
        wg16[...] = wg_ref[...].astype(BF16)
        wu16[...] = wu_ref[...].astype(BF16)
        wd16[...] = wd_ref[...].astype(BF16)

    @pl.when(iv_ref[i] == 0)
    def _():
        y_ref[...] = jnp.zeros_like(y_ref)

    @pl.when(iv_ref[i] == 1)
    def _():
        rows = lax.broadcasted_iota(I32, (MOE_TILE, 1), 0) + it_ref[i] * MOE_TILE
        sort = jnp.where(dest_ref[0] == rows, 1.0, 0.0).astype(BF16)
        x = _dot(sort, h_ref[...]).astype(BF16)
        gates = functools.reduce(jnp.add, [lax.dot_general(sort, p, _NT, preferred_element_type=F32)
                                           for p in _split_bf16(gate_ref[...])])
        lane = lax.broadcasted_iota(I32, (1, N_EXPERTS), 1)
        acc = jnp.zeros((MOE_TILE, D_MODEL), F32)
        for e in range(EXPERTS_PER_GROUP):
            ge = jnp.sum(jnp.where(lane == g * EXPERTS_PER_GROUP + e, gates, 0.0), axis=1, keepdims=True)
            act = _silu(_dot(x, wg16[e])) * _dot(x, wu16[e])
            acc = acc + _dot((act * ge).astype(BF16), wd16[e])
        y_ref[...] = acc.astype(BF16)


def _moe_unsort_kernel(nt_ref, hb_ref, ylo_ref, yhi_ref, r1_ref, r2_ref, x_ref, mod_ref, oa_ref, ob_ref, acc_ref, *,
                       n_ctx_blocks):
    b = pl.program_id(0)
    acc_ref[...] = jnp.zeros_like(acc_ref)
    chunk = max(MOE_TILE, MXU_DEPTH)
    tiles_per_chunk = chunk // MOE_TILE
    row1, row2 = r1_ref[0], r2_ref[0]

    def accumulate(y_part, first):
        def body(ci, carry):
            r0 = pl.multiple_of((ci - first) * chunk, chunk)
            rows = lax.broadcasted_iota(I32, (chunk, 1), 0) + ci * chunk
            sort = jnp.where(row1 == rows, 1.0, jnp.where(row2 == rows, 1.0, 0.0)).astype(BF16)
            acc_ref[...] += lax.dot_general(sort, y_part[pl.ds(r0, chunk), :], _TN, preferred_element_type=F32)
            return carry
        return body

    half = ylo_ref.shape[0] // chunk
    n_chunks = (nt_ref[b] + tiles_per_chunk - 1) // tiles_per_chunk
    lax.fori_loop(0, jnp.minimum(n_chunks, half), accumulate(ylo_ref, 0), 0)
    lax.fori_loop(half, jnp.maximum(n_chunks, half), accumulate(yhi_ref, half), 0)
    result = lambda: x_ref[...] + mod_ref[0][:, 5 * D_MODEL:6 * D_MODEL] * acc_ref[...]

    @pl.when(b < n_ctx_blocks)
    def _():
        oa_ref[...] = result()

    @pl.when(b >= n_ctx_blocks)
    def _():
        ob_ref[...] = result()


def _moe(hn2, gate_t, rank, totals, w_gate, w_up, w_down, x_mid, mod, st, layer):
    nb, tr, spb = st.tile, MOE_TILE, MOE_SLOTS_PER_BLOCK
    n_blocks = st.n_tiles
    plan = _moe_plan(rank, totals, n_blocks)

    wmap = lambda i, ig, ib, it, iv, isl: (layer * N_GROUPS + ig[i], 0, 0)
    once = pl.Buffered(1)
    y = pl.pallas_call(
        _moe_ffn_kernel,
        grid_spec=pltpu.PrefetchScalarGridSpec(
            num_scalar_prefetch=5,
            grid=(n_blocks * spb,),
            in_specs=[pl.BlockSpec((nb, D_MODEL), lambda i, ig, ib, it, iv, isl: (ib[i], 0)),
                      pl.BlockSpec((1, 1, nb), lambda i, ig, ib, it, iv, isl: (ig[i] * n_blocks + ib[i], 0, 0)),
                      pl.BlockSpec((N_EXPERTS, nb), lambda i, ig, ib, it, iv, isl: (0, ib[i])),
                      pl.BlockSpec((EXPERTS_PER_GROUP, D_MODEL, D_EXPERT), wmap, pipeline_mode=once),
                      pl.BlockSpec((EXPERTS_PER_GROUP, D_MODEL, D_EXPERT), wmap, pipeline_mode=once),
                      pl.BlockSpec((EXPERTS_PER_GROUP, D_EXPERT, D_MODEL), wmap, pipeline_mode=once)],
            out_specs=pl.BlockSpec((tr, D_MODEL), lambda i, ig, ib, it, iv, isl: (isl[i], 0)),
            scratch_shapes=[pltpu.VMEM((EXPERTS_PER_GROUP, D_MODEL, D_EXPERT), BF16),
                            pltpu.VMEM((EXPERTS_PER_GROUP, D_MODEL, D_EXPERT), BF16),
                            pltpu.VMEM((EXPERTS_PER_GROUP, D_EXPERT, D_MODEL), BF16)]),
        out_shape=jax.ShapeDtypeStruct((n_blocks * spb * tr, D_MODEL), BF16),
        compiler_params=_params("arbitrary"),
        name="moe_ffn",
    )(plan["item_g"], plan["item_b"], plan["item_tile"], plan["item_valid"], plan["item_slot"],
      hn2, plan["dest"], gate_t, w_gate, w_up, w_down)

    return pl.pallas_call(
        functools.partial(_moe_unsort_kernel, n_ctx_blocks=st.n_ctx),
        grid_spec=pltpu.PrefetchScalarGridSpec(
            num_scalar_prefetch=2,
            grid=(n_blocks,),
            in_specs=[pl.BlockSpec((spb // 2 * tr, D_MODEL), lambda b, nt, hb: (2 * b, 0)),
                      pl.BlockSpec((spb // 2 * tr, D_MODEL), lambda b, nt, hb: (2 * hb[b] + 1, 0)),
                      pl.BlockSpec((1, 1, nb), lambda b, nt, hb: (b, 0, 0)),
                      pl.BlockSpec((1, 1, nb), lambda b, nt, hb: (b, 0, 0)),
                      pl.BlockSpec((nb, D_MODEL), lambda b, nt, hb: (b, 0)),
                      pl.BlockSpec((1, 1, 6 * D_MODEL), lambda b, nt, hb: (st.mod_idx(b), 0, 0))],
            out_specs=[pl.BlockSpec((nb, D_MODEL), lambda b, nt, hb: (st.ctx_idx(b), 0)),
                       pl.BlockSpec((nb, D_MODEL), lambda b, nt, hb: (st.lat_idx(b), 0))],
            scratch_shapes=[pltpu.VMEM((nb, D_MODEL), F32)]),
        out_shape=[jax.ShapeDtypeStruct((st.n_ctx * nb, D_MODEL), F32),
                   jax.ShapeDtypeStruct(((n_blocks - st.n_ctx) * nb, D_MODEL), F32)],
        compiler_params=_params("arbitrary"),
        name="moe_unsort",
    )(plan["n_tiles"], plan["hi_block"], y, y, plan["row1"], plan["row2"], x_mid, mod)


def _block_avg(group):
    i = np.arange(LANES)
    return jnp.asarray((i[:, None] // group == i[None, :] // group) / group, dtype=BF16)


def _rope_tables(n_tokens, n_identity):
    rows = n_tokens // GRID_W
    pos_r = jnp.repeat(jnp.arange(rows, dtype=F32), GRID_W)
    pos_c = jnp.tile(jnp.arange(GRID_W, dtype=F32), rows)
    n_freq = DA_QK // 4
    inv = ROPE_BASE ** (-jnp.arange(n_freq, dtype=F32) / n_freq)
    ang = jnp.concatenate([pos_r[:, None] * inv, pos_c[:, None] * inv], axis=-1)
    cos = jnp.repeat(jnp.cos(ang), 2, axis=-1)
    sin = jnp.repeat(jnp.sin(ang), 2, axis=-1) * jnp.tile(jnp.asarray([-1.0, 1.0], F32), DA_QK // 2)
    cos = jnp.concatenate([jnp.ones((n_identity, DA_QK), F32), cos], axis=0)
    sin = jnp.concatenate([jnp.zeros((n_identity, DA_QK), F32), sin], axis=0)
    return jnp.tile(cos, (1, 2 * H_DA)), jnp.tile(sin, (1, 2 * H_DA))


def kernel(x_prompt, x_sample, cache_k, cache_v, state_ret, state_hgrn, c, c_ctx, w_in, w_out, w_ada, b_ada,
           norm_mix, norm_ffn, da_qnorm, da_knorm, da_lambda, da_subln, ret_decay, ret_norm, hg_lb, hg_norm,
           w_router, router_bias, w_gate, w_up, w_down):
    n_ctx, l_ctx, _ = x_prompt.shape
    n_lat, l_lat, _ = x_sample.shape
    t_ctx = n_ctx * l_ctx
    past = cache_k.shape[2]
    assert TOKEN_TILE % l_ctx == 0 and MOE_BLOCK % TOKEN_TILE == 0
    assert l_lat % MOE_BLOCK == 0 and t_ctx % MOE_BLOCK == 0 and l_lat % GRID_W == 0

    ctx_row = n_lat
    n_cond = -(-(n_lat + 1) // 8) * 8
    cond = jnp.zeros((n_cond, D_MODEL), F32).at[:n_lat].set(c).at[ctx_row].set(c_ctx)
    mod_all = _ada(cond, w_ada, b_ada)

    st = _Stream(t_ctx, n_lat, l_lat, TOKEN_TILE, ctx_row)
    st_moe = _Stream(t_ctx, n_lat, l_lat, MOE_BLOCK, ctx_row)
    g32, g64 = _block_avg(DA_QK), _block_avg(DA_V)
    cos_t, sin_t = _rope_tables(l_lat, TOKEN_TILE)
    lv = jnp.asarray(_level_table(REC_CHUNK))
    p_lb = jax.nn.softmax(hg_lb.astype(F32), axis=0)
    lb_all = jnp.cumsum(p_lb, axis=0) - p_lb[0]
    wr_hi = w_router.T.astype(BF16)
    wr_lo = (w_router.T - wr_hi.astype(F32)).astype(BF16)
    rbias = router_bias.astype(F32).reshape(N_EXPERTS, 1)
    flat_w = lambda w: w.reshape(DEPTH * N_EXPERTS, w.shape[2], w.shape[3])
    w_gate, w_up, w_down = flat_w(w_gate), flat_w(w_up), flat_w(w_down)
    flat_s = lambda s: s.astype(F32).reshape(n_lat, DEPTH, 2, D_REC, D_HEAD)
    states_in = (flat_s(state_ret), flat_s(state_hgrn))
    cache_k16 = cache_k.reshape(n_lat, DEPTH, past, D_QK).astype(BF16)
    cache_v16 = cache_v.reshape(n_lat, DEPTH, past, D_V).astype(BF16)

    x = (x_prompt.reshape(t_ctx, D_MODEL), x_sample.reshape(n_lat * l_lat, D_MODEL))
    caches, states_out = None, None
    for l in range(DEPTH):
        lam_init = 0.8 - 0.6 * math.exp(-0.3 * l)
        lp = da_lambda[l].astype(F32)
        lam = (jnp.exp(jnp.sum(lp[0] * lp[1])) - jnp.exp(jnp.sum(lp[2] * lp[3])) + lam_init).reshape(1)
        mod = mod_all[l].reshape(n_cond, 1, 6 * D_MODEL)
        qw = (jnp.tile(da_qnorm[l].astype(F32), 2 * H_DA) * (DA_QK ** -0.5)).reshape(1, D_QK)
        kw = jnp.tile(da_knorm[l].astype(F32), 2 * H_DA).reshape(1, D_QK)

        qb, kb, vb, rest, kc, vc = _inproj(x[0], x[1], mod, norm_mix[l].reshape(1, D_MODEL),
                                           w_in[l].astype(BF16), qw, kw, g32, cos_t, sin_t, st, l, n_ctx, l_ctx,
                                           caches)
        caches = (kc, vc)

        oda = (_attention(lam, qb, kb, vb, 0, n_ctx, l_ctx, CTX_PAIRS_PER_STEP),
               _attention(lam, qb, kb, vb, t_ctx, n_lat, l_lat, LAT_PAIRS_PER_STEP, l,
                          cache=(cache_k16, cache_v16)))

        ret_tabs = _retention_tables(-jnp.exp(ret_decay[l].astype(F32)), REC_CHUNK)
        lb_row = lb_all[l].reshape(1, D_REC)
        ofc, obc, s_ret, s_hg = _recurrence(rest, ret_tabs, lb_row, lv, 0, n_ctx, l_ctx, l,
                                            want_states=True, states_out=states_out)
        states_out = (s_ret, s_hg)
        ofl, obl = _recurrence(rest, ret_tabs, lb_row, lv, t_ctx, n_lat, l_lat, l, states_in=states_in)

        subln = (jnp.tile(da_subln[l].astype(F32), H_DA) * (1.0 - lam_init)).reshape(1, D_V)
        retn = jnp.tile(ret_norm[l].astype(F32), H_RET).reshape(1, D_REC)
        hgn = jnp.tile(hg_norm[l].astype(F32), H_HG).reshape(1, D_REC)
        x_mid, hn2, gate_t, rank, totals = _outproj(x, oda, (ofc, ofl), (obc, obl), rest, mod, subln, retn, hgn, g64,
                                      w_out[l].astype(BF16), norm_ffn[l].reshape(1, D_MODEL), wr_hi, wr_lo,
                                      rbias, st)
        x = _moe(hn2, gate_t, rank, totals, w_gate, w_up, w_down, x_mid, mod, st_moe, l)

    kc, vc = caches
    s_ret, s_hg = states_out
    return (x[0].reshape(n_ctx, l_ctx, D_MODEL), x[1].reshape(n_lat, l_lat, D_MODEL),
            kc.reshape(n_ctx, DEPTH, l_ctx, H_DA, 2, DA_QK), vc.reshape(n_ctx, DEPTH, l_ctx, H_DA, DA_V),
            s_ret.reshape(n_ctx, DEPTH, 2, H_RET, RET_DK, RET_DV),
            s_hg.reshape(n_ctx, DEPTH, 2, H_HG, HG_DK, HG_DV))
```

```python
import functools
import math

import numpy as np
import jax
import jax.numpy as jnp
from jax import lax
from jax.experimental import pallas as pl
from jax.experimental.pallas import tpu as pltpu

F32 = jnp.float32
BF16 = jnp.bfloat16
I32 = jnp.int32

D_MODEL = 1024
DEPTH = 2
GRID_W = 64
EPS = 1e-6
MIN_GATE = 1e-30
ROPE_BASE = 10000.0
H_DA, DA_QK, DA_V = 8, 32, 64
H_RET, RET_DK, RET_DV = 4, 64, 64
H_HG, HG_DK, HG_DV = 4, 64, 64
D_QK = H_DA * 2 * DA_QK
D_V = H_DA * DA_V
N_REC_HEADS = 4
D_HEAD = 64
D_REC = N_REC_HEADS * D_HEAD
D_REST = 9 * D_REC
D_IN = 2 * D_QK + D_V + D_REST
N_EXPERTS, N_GROUPS, EXPERTS_PER_GROUP = 16, 4, 4
D_EXPERT = 512
MASKED_SCORE = -2.0

LANES = 128
TOKEN_TILE = 512
Q_TILE = 1024
CTX_PAIRS_PER_STEP = 1
LAT_PAIRS_PER_STEP = 2
REC_CHUNK = 128
MOE_BLOCK = 1024
MOE_TILE = 256
MXU_DEPTH = 256
MOE_SLOTS_PER_BLOCK = -(-2 * MOE_BLOCK // MOE_TILE) + N_GROUPS
VMEM_LIMIT = 56 * 1024 * 1024

_NT = (((1,), (1,)), ((), ()))
_TN = (((0,), (0,)), ((), ()))


def _params(*sem):
    return pltpu.CompilerParams(dimension_semantics=sem, vmem_limit_bytes=VMEM_LIMIT)


def _dot(a, b):
    return jnp.dot(a, b, preferred_element_type=F32)


def _split_bf16(t, terms=2):
    out = []
    for _ in range(terms - 1):
        hi = t.astype(BF16)
        out.append(hi)
        t = t - hi.astype(F32)
    out.append(t.astype(BF16))
    return out


def _group_mean_sq(t, g):
    sq = (t * t).astype(BF16)
    outs = [_dot(sq[:, s * LANES:(s + 1) * LANES], g) for s in range(t.shape[1] // LANES)]
    return jnp.concatenate(outs, axis=1) if len(outs) > 1 else outs[0]


def _silu(t):
    return t * jax.nn.sigmoid(t)


class _Stream:
    def __init__(self, n_ctx_rows, n_lat_seq, lat_len, tile, ctx_row):
        self.tile = tile
        self.n_ctx = n_ctx_rows // tile
        self.per_seq = lat_len // tile
        self.n_tiles = self.n_ctx + n_lat_seq * self.per_seq
        self.ctx_row = ctx_row

    def ctx_idx(self, i):
        return jnp.minimum(i, self.n_ctx - 1)

    def lat_idx(self, i):
        return jnp.maximum(i - self.n_ctx, 0)

    def mod_idx(self, i):
        return jnp.where(i < self.n_ctx, self.ctx_row, (i - self.n_ctx) // self.per_seq)

    def pos_idx(self, i):
        return jnp.where(i < self.n_ctx, 0, 1 + (i - self.n_ctx) % self.per_seq)


def _ada_kernel(c_ref, w_ref, b_ref, o_ref):
    s = _silu(c_ref[...]).astype(BF16)
    o_ref[0] = _dot(s, w_ref[0].astype(BF16)) + b_ref[0]


def _ada(cond, w_ada, b_ada):
    rows = cond.shape[0]
    n_tile = 1536
    return pl.pallas_call(
        _ada_kernel,
        grid=(DEPTH, 6 * D_MODEL // n_tile),
        in_specs=[pl.BlockSpec((rows, D_MODEL), lambda l, j: (0, 0)),
                  pl.BlockSpec((1, D_MODEL, n_tile), lambda l, j: (l, 0, j)),
                  pl.BlockSpec((1, 1, n_tile), lambda l, j: (l, 0, j))],
        out_specs=pl.BlockSpec((1, rows, n_tile), lambda l, j: (l, 0, j)),
        out_shape=jax.ShapeDtypeStruct((DEPTH, rows, 6 * D_MODEL), F32),
        compiler_params=_params("arbitrary", "arbitrary"),
        name="ada",
    )(cond, w_ada, b_ada.reshape(DEPTH, 1, 6 * D_MODEL))


def _inproj_kernel(*refs, n_ctx_tiles, first):
    (xa_ref, xb_ref, mod_ref, nw_ref, w_ref, qw_ref, kw_ref, g32_ref, cos_ref, sin_ref) = refs[:10]
    q_ref, kb_ref, vb_ref, rest_ref, kc_ref, vc_ref = refs[10:] if first else refs[12:]
    is_ctx = pl.program_id(0) < n_ctx_tiles
    x = jnp.where(is_ctx, xa_ref[...], xb_ref[...])
    y = x * lax.rsqrt(jnp.mean(x * x, axis=-1, keepdims=True) + EPS) * nw_ref[...]
    mod = mod_ref[0]
    hn = (y * (1.0 + mod[:, D_MODEL:2 * D_MODEL]) + mod[:, 0:D_MODEL]).astype(BF16)
    g32 = g32_ref[...]
    lane = lax.broadcasted_iota(I32, (1, D_QK), 1)
    even = (lane & 1) == 0

    def qk_norm_rope(t, w):
        t = t * lax.rsqrt(_group_mean_sq(t, g32) + EPS) * w
        partner = jnp.where(even, pltpu.roll(t, D_QK - 1, 1), pltpu.roll(t, 1, 1))
        return t * cos_ref[...] + partner * sin_ref[...]

    q_ref[...] = qk_norm_rope(_dot(hn, w_ref[:, 0:D_QK]), qw_ref[...]).astype(BF16)
    k = qk_norm_rope(_dot(hn, w_ref[:, D_QK:2 * D_QK]), kw_ref[...])
    kb_ref[...] = k.astype(BF16)
    v = _dot(hn, w_ref[:, 2 * D_QK:2 * D_QK + D_V])
    vb_ref[...] = v.astype(BF16)
    rest_ref[...] = _dot(hn, w_ref[:, 2 * D_QK + D_V:D_IN])

    @pl.when(is_ctx)
    def _():
        n_seq, _, seq_len, _ = kc_ref.shape
        kc_ref[:, 0] = k.reshape(n_seq, seq_len, D_QK)
        vc_ref[:, 0] = v.reshape(n_seq, seq_len, D_V)
        if first:
            kc_ref[:, 1:] = jnp.zeros_like(kc_ref[:, 1:])
            vc_ref[:, 1:] = jnp.zeros_like(vc_ref[:, 1:])


def _inproj(xa, xb, mod, norm_w, w_in16, qw, kw, g32, cos_t, sin_t, st, layer, n_ctx_seq, l_ctx, caches):
    tm = st.tile
    t = st.n_tiles * tm
    first = caches is None
    row = lambda i: (i, 0)
    const = lambda i: (0, 0)
    in_specs = [pl.BlockSpec((tm, D_MODEL), lambda i: (st.ctx_idx(i), 0)),
                pl.BlockSpec((tm, D_MODEL), lambda i: (st.lat_idx(i), 0)),
                pl.BlockSpec((1, 1, 6 * D_MODEL), lambda i: (st.mod_idx(i), 0, 0)),
                pl.BlockSpec((1, D_MODEL), const),
                pl.BlockSpec((D_MODEL, D_IN), const, pipeline_mode=pl.Buffered(1)),
                pl.BlockSpec((1, D_QK), const),
                pl.BlockSpec((1, D_QK), const),
                pl.BlockSpec((LANES, LANES), const),
                pl.BlockSpec((tm, D_QK), lambda i: (st.pos_idx(i), 0)),
                pl.BlockSpec((tm, D_QK), lambda i: (st.pos_idx(i), 0))]
    args = [xa, xb, mod, norm_w, w_in16, qw, kw, g32, cos_t, sin_t]
    aliases = {}
    per_tile = tm // l_ctx
    if first:
        cache_spec = pl.BlockSpec((per_tile, DEPTH, l_ctx, D_QK), lambda i: (st.ctx_idx(i), 0, 0, 0))
    else:
        cache_spec = pl.BlockSpec((per_tile, 1, l_ctx, D_QK), lambda i: (st.ctx_idx(i), layer, 0, 0))
        in_specs += [pl.BlockSpec(memory_space=pl.ANY)] * 2
        args += list(caches)
        aliases = {10: 4, 11: 5}
    cache_shape = jax.ShapeDtypeStruct((n_ctx_seq, DEPTH, l_ctx, D_QK), F32)
    return pl.pallas_call(
        functools.partial(_inproj_kernel, n_ctx_tiles=st.n_ctx, first=first),
        grid=(st.n_tiles,),
        in_specs=in_specs,
        out_specs=[pl.BlockSpec((tm, D_QK), row), pl.BlockSpec((tm, D_QK), row),
                   pl.BlockSpec((tm, D_V), row), pl.BlockSpec((tm, D_REST), row), cache_spec, cache_spec],
        out_shape=[jax.ShapeDtypeStruct((t, D_QK), BF16), jax.ShapeDtypeStruct((t, D_QK), BF16),
                   jax.ShapeDtypeStruct((t, D_V), BF16), jax.ShapeDtypeStruct((t, D_REST), F32),
                   cache_shape, cache_shape],
        input_output_aliases=aliases,
        compiler_params=_params("arbitrary"),
        name="inproj",
    )(*args)


def _attn_kernel(*refs, with_cache):
    if with_cache:
        lam_ref, q_ref, k_ref, v_ref, ck_ref, cv_ref, o_ref = refs
    else:
        lam_ref, q_ref, k_ref, v_ref, o_ref = refs
    lam = lam_ref[0]
    lane = lax.broadcasted_iota(I32, (1, LANES), 1)
    left = lane < DA_V
    one = jnp.ones((), BF16)
    zero = jnp.zeros((), BF16)
    owns = [left, jnp.logical_not(left)]
    tq = min(Q_TILE, q_ref.shape[0])
    for p in range(q_ref.shape[1] // LANES):
        slab = slice(p * LANES, (p + 1) * LANES)
        segs = [(k_ref[:, slab], v_ref[:, slab])]
        if with_cache:
            segs.append((ck_ref[0, 0, :, slab], cv_ref[0, 0, :, slab]))
        vms = [[jnp.where(own, v, one) for _, v in segs] for own in owns]

        def q_block(i, carry, slab=slab, segs=segs, vms=vms):
            r0 = i * tq if isinstance(i, int) else pl.multiple_of(i * tq, tq)
            q = q_ref[pl.ds(r0, tq), slab]
            acc = jnp.zeros((tq, LANES), F32)
            for side in range(2):
                for r in range(2):
                    lo = side * DA_V + r * DA_QK
                    qm = jnp.where((lane >= lo) & (lane < lo + DA_QK), q, zero)
                    ss = [lax.dot_general(qm, k, _NT, preferred_element_type=F32) for k, _ in segs]
                    m = functools.reduce(jnp.maximum, [jnp.max(s, axis=-1, keepdims=True) for s in ss])
                    res = functools.reduce(
                        jnp.add, [_dot(jnp.exp(s - m).astype(BF16), vm) for s, vm in zip(ss, vms[side])])
                    den = pltpu.roll(res, DA_V, 1)
                    coef = 1.0 if r == 0 else -lam
                    acc = acc + jnp.where(owns[side], coef * (res / den), 0.0)
            o_ref[pl.ds(r0, tq), slab] = acc
            return carry

        if q_ref.shape[0] == tq:
            q_block(0, 0)
        else:
            lax.fori_loop(0, q_ref.shape[0] // tq, q_block, 0)


def _attention(lam, qb, kb, vb, row0, n_seq, seq_len, pairs_per_step, layer=0, cache=None):
    sb0 = row0 // seq_len
    width = pairs_per_step * LANES
    smap = lambda b, p: (sb0 + b, p)
    smem = pl.BlockSpec(memory_space=pltpu.SMEM)
    in_specs = [smem] + [pl.BlockSpec((seq_len, width), smap)] * 3
    args = [lam, qb, kb, vb]
    if cache is not None:
        ck, cv = cache
        cmap = lambda b, p: (b, layer, 0, p)
        in_specs += [pl.BlockSpec((1, 1, ck.shape[2], width), cmap),
                     pl.BlockSpec((1, 1, cv.shape[2], width), cmap)]
        args += [ck, cv]
    return pl.pallas_call(
        functools.partial(_attn_kernel, with_cache=cache is not None),
        grid=(n_seq, D_QK // width),
        in_specs=in_specs,
        out_specs=pl.BlockSpec((seq_len, width), lambda b, p: (b, p)),
        out_shape=jax.ShapeDtypeStruct((n_seq * seq_len, D_V), F32),
        compiler_params=_params("arbitrary", "arbitrary"),
        name="attn_lat" if cache is not None else "attn_ctx",
    )(*args)


def _level_table(c):
    t = np.arange(c)[:, None] ^ np.arange(c)[None, :]
    lv = np.where(t == 0, 32, 31 - np.floor(np.log2(np.maximum(t, 1))).astype(np.int64))
    return np.tile(lv, (N_REC_HEADS, 1)).astype(np.int32)


def _retention_tables(log_gamma, c):
    g = jnp.repeat(log_gamma, D_HEAD, axis=1)
    t = jnp.arange(c, dtype=F32)
    diff = t[:, None] - t[None, :]
    gh = log_gamma[:, :, None, None]
    d_f = jnp.where(diff >= 0, jnp.exp(gh[0] * jnp.maximum(diff, 0.0)), 0.0)
    d_b = jnp.where(diff <= 0, jnp.exp(gh[1] * jnp.maximum(-diff, 0.0)), 0.0)
    d = jnp.stack([d_f, d_b]).reshape(2, N_REC_HEADS * c, c)
    q_f, k_f = jnp.exp(g[0] * (t[:, None] + 1.0)), jnp.exp(g[0] * (c - 1.0 - t[:, None]))
    q_b, k_b = jnp.exp(g[1] * (c - t[:, None])), jnp.exp(g[1] * t[:, None])
    f = jnp.stack([jnp.stack([q_f, k_f]), jnp.stack([q_b, k_b])])
    a = jnp.exp(g * c).reshape(2, 1, D_REC)
    return d, f, a


def _rec_kernel(*refs, state_in, state_out, aliased):
    refs = list(refs)
    rf_ref, rb_ref, rd_ref, rfac_ref, ra_ref, lb_ref, lv_ref = refs[:7]
    del refs[:7]
    if state_in:
        sret_ref, shg_ref, t4_ref = refs[:3]
        del refs[:3]
    if state_out:
        t4t_ref = refs.pop(0)
    del refs[:aliased]
    of_ref, ob_ref = refs[:2]
    del refs[:2]
    if state_out:
        oret_ref, ohg_ref = refs[:2]
        del refs[:2]
    (st_scr,) = refs

    ci = pl.program_id(1)
    c = rf_ref.shape[0]
    lane = lax.broadcasted_iota(I32, (1, D_REC), 1)
    head = lane >> 6
    row = lax.broadcasted_iota(I32, (c, 1), 0)
    srow = lax.broadcasted_iota(I32, (D_REC, 1), 0)
    same_head = (srow >> 6) == head
    eye = srow == lane
    head_is = [head == h for h in range(N_REC_HEADS)]
    head_m16 = [jnp.broadcast_to(jnp.where(m, 1.0, 0.0), (c, D_REC)).astype(BF16) for m in head_is]
    head_m16_half = [jnp.broadcast_to(jnp.where(m, 1.0, 0.0), (c // 2, D_REC)).astype(BF16) for m in head_is]

    @pl.when(ci == 0)
    def _():
        if state_in:
            for d in range(2):
                for idx, ref in ((d, sret_ref), (2 + d, shg_ref)):
                    tiled = functools.reduce(jnp.add, [_dot(p, t4_ref[...]) for p in _split_bf16(ref[0, 0, d], 3)])
                    st_scr[idx] = jnp.where(same_head, tiled, 0.0)
        else:
            st_scr[...] = jnp.zeros_like(st_scr)

    def expand(t16):
        return jnp.concatenate([t16 * m for m in head_m16], axis=0)

    def nt(a, b):
        return lax.dot_general(a, b, _NT, preferred_element_type=F32)

    def half_rows(t, h, second):
        o = h if second else 0
        return jnp.concatenate([t[j * 2 * h + o:j * 2 * h + o + h] for j in range(c // (2 * h))], axis=0)

    def spread_rows(t, h, second):
        z = jnp.zeros((h, t.shape[1]), t.dtype)
        parts = []
        for j in range(c // (2 * h)):
            parts += [z, t[j * h:(j + 1) * h]] if second else [t[j * h:(j + 1) * h], z]
        return jnp.concatenate(parts, axis=0)

    def finish(q, k, v16, scores, qfac, kfac, a_row, st_ref, own_term=False):
        st = st_ref[...]
        o = _dot((q * qfac).astype(BF16), st.astype(BF16))
        if own_term:
            o = o + _dot((q * k).astype(BF16), jnp.where(same_head, 1.0, 0.0).astype(BF16)) * v16.astype(F32)
        o_stack = _dot(scores.astype(BF16), v16)
        for h in range(N_REC_HEADS):
            o = o + jnp.where(head_is[h], o_stack[h * c:(h + 1) * c], 0.0)
        kv = lax.dot_general((k * kfac).astype(BF16), v16, _TN, preferred_element_type=F32)
        a_col = jnp.sum(jnp.where(eye, a_row, 0.0), axis=1, keepdims=True)
        st_ref[...] = st * a_col + jnp.where(same_head, kv, 0.0)
        return o

    def tree_scores(q, k, la, reverse):
        lv = lv_ref[...]
        scores = jnp.zeros(lv.shape, F32)
        pre, tot = la, la
        h, level = 1, 31
        while h < c:
            bit = (row & h) != 0
            query_side = jnp.logical_not(bit) if reverse else bit
            x = jnp.exp(jnp.minimum(jnp.where(query_side, pre, tot - pre), 0.0))
            qx = q * x
            kx = jnp.where(query_side, 0.0, k * x).astype(BF16)
            if h < 8:
                s_level = nt(expand(qx.astype(BF16)), kx)
            else:
                qc = half_rows(qx, h, not reverse).astype(BF16)
                sc = nt(jnp.concatenate([qc * m for m in head_m16_half], axis=0), kx)
                s_level = jnp.concatenate([spread_rows(sc[hh * c // 2:(hh + 1) * c // 2], h, not reverse)
                                           for hh in range(N_REC_HEADS)], axis=0)
            scores = jnp.where(lv == level, s_level, scores)
            partner = jnp.where(bit, pltpu.roll(tot, h, 0), pltpu.roll(tot, c - h, 0))
            pre = pre + jnp.where(query_side, partner, 0.0)
            tot = tot + partner
            h, level = 2 * h, level - 1
        return scores, pre, tot

    lb = lb_ref[...]
    col = lambda ref, j: ref[:, j * D_REC:(j + 1) * D_REC]
    for d, (r_ref, o_ref) in enumerate(((rf_ref, of_ref), (rb_ref, ob_ref))):
        q, k, v16 = col(r_ref, 0), col(r_ref, 1) * (RET_DK ** -0.5), col(r_ref, 2).astype(BF16)
        scores = nt(expand(q.astype(BF16)), k.astype(BF16)) * rd_ref[d]
        o_ref[:, 0:D_REC] = finish(q, k, v16, scores, rfac_ref[d, 0], rfac_ref[d, 1], ra_ref[d], st_scr.at[d])
        z = col(r_ref, 5 + d)
        la = jnp.log(jnp.maximum(lb + (1.0 - lb) * jax.nn.sigmoid(z), MIN_GATE))
        q, k, v16 = _silu(col(r_ref, 4)), (1.0 - lb) * jax.nn.sigmoid(-z), col(r_ref, 7).astype(BF16)
        scores, pre, tot = tree_scores(q, k, la, reverse=d == 1)
        o_ref[:, D_REC:2 * D_REC] = finish(q, k, v16, scores, jnp.exp(jnp.minimum(pre, 0.0)),
                                           jnp.exp(jnp.minimum(tot - pre, 0.0)), jnp.exp(tot[0:1, :]),
                                           st_scr.at[2 + d], own_term=True)

    if state_out:
        @pl.when(ci == pl.num_programs(1) - 1)
        def _():
            for d in range(2):
                for idx, ref in ((d, oret_ref), (2 + d, ohg_ref)):
                    ref[0, 0, d] = functools.reduce(
                        jnp.add, [_dot(p, t4t_ref[...]) for p in _split_bf16(st_scr[idx], 3)])
            if state_out == "first":
                oret_ref[0, 1:] = jnp.zeros_like(oret_ref[0, 1:])
                ohg_ref[0, 1:] = jnp.zeros_like(ohg_ref[0, 1:])


def _recurrence(rest, ret_tabs, lb_row, lv, row0, n_seq, seq_len, layer, *, states_in=None, want_states=False,
                states_out=None):
    c = REC_CHUNK
    nc = seq_len // c
    cb0 = row0 // c
    const2 = lambda b, i: (0, 0)
    const3 = lambda b, i: (0, 0, 0)
    rd, rfac, ra = ret_tabs
    in_specs = [pl.BlockSpec((c, D_REST), lambda b, i: (cb0 + b * nc + i, 0)),
                pl.BlockSpec((c, D_REST), lambda b, i: (cb0 + b * nc + nc - 1 - i, 0)),
                pl.BlockSpec((2, N_REC_HEADS * c, c), const3),
                pl.BlockSpec((2, 2, c, D_REC), lambda b, i: (0, 0, 0, 0)),
                pl.BlockSpec((2, 1, D_REC), const3),
                pl.BlockSpec((1, D_REC), const2),
                pl.BlockSpec((N_REC_HEADS * c, c), const2)]
    args = [rest, rest, rd, rfac, ra, lb_row, lv]
    tile4 = jnp.tile(jnp.eye(D_HEAD, dtype=BF16), (1, N_REC_HEADS))
    if states_in is not None:
        blk = pl.BlockSpec((1, 1, 2, D_REC, D_HEAD), lambda b, i: (b, layer, 0, 0, 0))
        in_specs += [blk, blk, pl.BlockSpec((D_HEAD, D_REC), const2)]
        args += [states_in[0], states_in[1], tile4]
    if want_states:
        in_specs += [pl.BlockSpec((D_REC, D_HEAD), const2)]
        args += [tile4.T]
    out_specs = [pl.BlockSpec((c, 2 * D_REC), lambda b, i: (b * nc + i, 0)),
                 pl.BlockSpec((c, 2 * D_REC), lambda b, i: (b * nc + nc - 1 - i, 0))]
    out_shape = [jax.ShapeDtypeStruct((n_seq * seq_len, 2 * D_REC), F32)] * 2
    aliases = {}
    state_mode = None
    if want_states:
        st_shape = jax.ShapeDtypeStruct((n_seq, DEPTH, 2, D_REC, D_HEAD), F32)
        out_shape += [st_shape, st_shape]
        if states_out is None:
            state_mode = "first"
            out_specs += [pl.BlockSpec((1, DEPTH, 2, D_REC, D_HEAD), lambda b, i: (b, 0, 0, 0, 0))] * 2
        else:
            state_mode = "next"
            out_specs += [pl.BlockSpec((1, 1, 2, D_REC, D_HEAD), lambda b, i: (b, layer, 0, 0, 0))] * 2
            aliases = {len(args): 2, len(args) + 1: 3}
            in_specs += [pl.BlockSpec(memory_space=pl.ANY)] * 2
            args += list(states_out)
    return pl.pallas_call(
        functools.partial(_rec_kernel, state_in=states_in is not None, state_out=state_mode,
                          aliased=len(aliases)),
        grid=(n_seq, nc),
        in_specs=in_specs,
        out_specs=out_specs,
        out_shape=out_shape,
        scratch_shapes=[pltpu.VMEM((4, D_REC, D_REC), F32)],
        input_output_aliases=aliases,
        compiler_params=_params("arbitrary", "arbitrary"),
        name="rec_lat" if states_in is not None else "rec_ctx",
    )(*args)


def _route(logits, bias):
    scores = jax.nn.sigmoid(logits)
    sel = scores + bias
    srow = [scores[e:e + 1, :] for e in range(N_EXPERTS)]
    rows = [sel[e:e + 1, :] for e in range(N_EXPERTS)]
    gs = []
    for g in range(N_GROUPS):
        a, b, c, d = rows[4 * g:4 * g + 4]
        gs.append(functools.reduce(jnp.maximum, [a + b, a + c, a + d, b + c, b + d, c + d]))
    best = jnp.zeros_like(gs[0], dtype=I32)
    best_v = gs[0]
    for g in range(1, N_GROUPS):
        upd = gs[g] > best_v
        best = jnp.where(upd, g, best)
        best_v = jnp.where(upd, gs[g], best_v)
    masked = [jnp.where(best == e // EXPERTS_PER_GROUP, rows[e], MASKED_SCORE) for e in range(N_EXPERTS)]
    i1 = jnp.zeros_like(best)
    v1 = masked[0]
    for e in range(1, N_EXPERTS):
        upd = masked[e] > v1
        i1 = jnp.where(upd, e, i1)
        v1 = jnp.where(upd, masked[e], v1)
    i2 = jnp.zeros_like(best)
    v2 = jnp.full_like(v1, -jnp.inf)
    for e in range(N_EXPERTS):
        upd = (masked[e] > v2) & (i1 != e)
        i2 = jnp.where(upd, e, i2)
        v2 = jnp.where(upd, masked[e], v2)
    w1 = functools.reduce(jnp.add, [jnp.where(i1 == e, srow[e], 0.0) for e in range(N_EXPERTS)])
    w2 = functools.reduce(jnp.add, [jnp.where(i2 == e, srow[e], 0.0) for e in range(N_EXPERTS)])
    tot = w1 + w2
    w1, w2 = w1 / tot, w2 / tot
    expert = lax.broadcasted_iota(I32, logits.shape, 0)
    return jnp.where(expert == i1, w1, 0.0) + jnp.where(expert == i2, w2, 0.0)


def _outproj_kernel(xa_ref, xb_ref, odaa_ref, odab_ref, ofa_ref, ofb_ref, oba_ref, obb_ref, rg_ref, gg_ref,
                    mod_ref, subln_ref, retn_ref, hgn_ref, g64_ref, wout_ref, nffn_ref, wrh_ref, wrl_ref, rb_ref,
                    xmid_ref, hn2_ref, gate_ref, rank_ref, total_ref, count_ref, *, n_ctx_tiles):
    g64 = g64_ref[...]
    is_ctx = pl.program_id(0) < n_ctx_tiles
    pick = lambda a_ref, b_ref: jnp.where(is_ctx, a_ref[...], b_ref[...])

    def gnorm(t, w):
        return t * lax.rsqrt(_group_mean_sq(t, g64) + EPS) * w

    o_da = gnorm(pick(odaa_ref, odab_ref), subln_ref[...])
    o_rec = pick(ofa_ref, ofb_ref) + pick(oba_ref, obb_ref)
    o_ret = gnorm(o_rec[:, 0:D_REC], retn_ref[...]) * _silu(rg_ref[...])
    o_hg = gnorm(o_rec[:, D_REC:2 * D_REC], hgn_ref[...]) * _silu(gg_ref[...])
    mixed = jnp.concatenate([o_da, o_ret, o_hg], axis=1).astype(BF16)
    mod = mod_ref[0]
    x = pick(xa_ref, xb_ref) + mod[:, 2 * D_MODEL:3 * D_MODEL] * _dot(mixed, wout_ref[...])
    xmid_ref[...] = x
    y = x * lax.rsqrt(jnp.mean(x * x, axis=-1, keepdims=True) + EPS) * nffn_ref[...]
    hn = y * (1.0 + mod[:, 4 * D_MODEL:5 * D_MODEL]) + mod[:, 3 * D_MODEL:4 * D_MODEL]
    hi, lo = _split_bf16(hn)
    hn2_ref[...] = hi
    wrh = wrh_ref[...]
    logits = (lax.dot_general(wrh, hi, _NT, preferred_element_type=F32)
              + lax.dot_general(wrh, lo, _NT, preferred_element_type=F32)
              + lax.dot_general(wrl_ref[...], hi, _NT, preferred_element_type=F32))
    gate = _route(logits, rb_ref[...])
    gate_ref[...] = gate

    tm = gate.shape[1]
    member = jnp.concatenate(
        [jnp.max(gate[EXPERTS_PER_GROUP * g:EXPERTS_PER_GROUP * (g + 1)], axis=0, keepdims=True)
         for g in range(N_GROUPS)] + [jnp.zeros((8 - N_GROUPS, tm), F32)], axis=0)
    member = jnp.where(member > 0, 1.0, 0.0)
    earlier = (lax.broadcasted_iota(I32, (tm, tm), 0) < lax.broadcasted_iota(I32, (tm, tm), 1))
    before = _dot(member.astype(BF16), jnp.where(earlier, 1.0, 0.0).astype(BF16))

    @pl.when(pl.program_id(0) % (MOE_BLOCK // tm) == 0)
    def _():
        count_ref[...] = jnp.zeros_like(count_ref)

    rank_ref[...] = jnp.where(member > 0, before + count_ref[:, 0:1], -1.0).astype(I32)
    count_ref[...] += jnp.sum(member, axis=1, keepdims=True)
    total_ref[...] = count_ref[...]


def _outproj(x, oda, o_f, o_b, rest, mod, subln, retn, hgn, g64, w_out16, norm_w, wr_hi, wr_lo, rbias, st):
    tm = st.tile
    t = st.n_tiles * tm
    row = lambda i: (i, 0)
    const = lambda i: (0, 0)
    pair = lambda width: [pl.BlockSpec((tm, width), lambda i: (st.ctx_idx(i), 0)),
                          pl.BlockSpec((tm, width), lambda i: (st.lat_idx(i), 0))]
    return pl.pallas_call(
        functools.partial(_outproj_kernel, n_ctx_tiles=st.n_ctx),
        grid=(st.n_tiles,),
        in_specs=pair(D_MODEL) + pair(D_V) + pair(2 * D_REC) + pair(2 * D_REC) + [
            pl.BlockSpec((tm, D_REC), lambda i: (i, 3)),
            pl.BlockSpec((tm, D_REC), lambda i: (i, 8)),
            pl.BlockSpec((1, 1, 6 * D_MODEL), lambda i: (st.mod_idx(i), 0, 0)),
            pl.BlockSpec((1, D_V), const),
            pl.BlockSpec((1, D_REC), const),
            pl.BlockSpec((1, D_REC), const),
            pl.BlockSpec((LANES, LANES), const),
            pl.BlockSpec((D_MODEL, D_MODEL), const, pipeline_mode=pl.Buffered(1)),
            pl.BlockSpec((1, D_MODEL), const),
            pl.BlockSpec((N_EXPERTS, D_MODEL), const),
            pl.BlockSpec((N_EXPERTS, D_MODEL), const),
            pl.BlockSpec((N_EXPERTS, 1), const)],
        out_specs=[pl.BlockSpec((tm, D_MODEL), row), pl.BlockSpec((tm, D_MODEL), row),
                   pl.BlockSpec((N_EXPERTS, tm), lambda i: (0, i)), pl.BlockSpec((8, tm), lambda i: (0, i)),
                   pl.BlockSpec((8, LANES), lambda i: (0, i // (MOE_BLOCK // tm)))],
        out_shape=[jax.ShapeDtypeStruct((t, D_MODEL), F32), jax.ShapeDtypeStruct((t, D_MODEL), BF16),
                   jax.ShapeDtypeStruct((N_EXPERTS, t), F32), jax.ShapeDtypeStruct((8, t), I32),
                   jax.ShapeDtypeStruct((8, t // MOE_BLOCK * LANES), F32)],
        scratch_shapes=[pltpu.VMEM((8, LANES), F32)],
        compiler_params=_params("arbitrary"),
        name="outproj",
    )(*x, *oda, *o_f, *o_b, rest, rest, mod, subln, retn, hgn, g64, w_out16, norm_w, wr_hi, wr_lo, rbias)


def _moe_plan(rank, totals, n_blocks):
    nb, tr, spb = MOE_BLOCK, MOE_TILE, MOE_SLOTS_PER_BLOCK
    n_slots = n_blocks * spb
    rank = rank[:N_GROUPS].reshape(N_GROUPS, n_blocks, nb)
    mb = rank >= 0
    mi = mb.astype(I32)
    count = totals[:N_GROUPS].reshape(N_GROUPS, n_blocks, LANES)[:, :, 0].astype(I32)
    tiles = (count + tr - 1) // tr
    t_end = jnp.cumsum(tiles, axis=0)
    t_off = t_end - tiles
    n_tiles = t_end[-1]
    dest = jnp.where(mb, t_off[..., None] * tr + rank, -1)
    order = jnp.cumsum(mi, axis=0) - mi
    n_mem = mi.sum(0)
    row_of = lambda k: jnp.where(n_mem > k, jnp.where(mb & (order == k), dest, 0).sum(0), -1)
    s = jnp.arange(n_slots, dtype=I32)
    s_blk, s_tile = s // spb, s % spb
    used = s_tile < jnp.repeat(n_tiles, spb)
    s_grp = (s_tile[None, :] >= jnp.repeat(t_end, spb, axis=1)).sum(0).astype(I32)
    key = jnp.where(used, s_grp, N_GROUPS) * n_slots + s
    pos = (key[None, :] < key[:, None]).sum(1).astype(I32)
    slot_of_item = jnp.where(pos[None, :] == s[:, None], s[None, :], 0).sum(1).astype(I32)
    n_valid = used.sum().astype(I32)
    src = slot_of_item[jnp.minimum(s, n_valid - 1)]
    item_g, item_b, item_tile = jnp.stack([s_grp, s_blk, s_tile])[:, src]
    blk = jnp.arange(n_blocks, dtype=I32)
    needs_hi = n_tiles > spb // 2
    hi_block = jnp.where((blk[None, :] <= blk[:, None]) & needs_hi[None, :], blk[None, :], 0).max(axis=1)
    return dict(dest=dest.reshape(N_GROUPS * n_blocks, 1, nb),
                row1=row_of(0).reshape(n_blocks, 1, nb), row2=row_of(1).reshape(n_blocks, 1, nb),
                n_tiles=n_tiles.astype(I32), hi_block=hi_block.astype(I32),
                item_g=item_g, item_b=item_b, item_tile=item_tile,
                item_valid=(s < n_valid).astype(I32), item_slot=slot_of_item)


def _moe_ffn_kernel(ig_ref, ib_ref, it_ref, iv_ref, is_ref, h_ref, dest_ref, gate_ref, wg_ref, wu_ref, wd_ref,
                    y_ref, wg16, wu16, wd16):
    i = pl.program_id(0)
    g = ig_ref[i]
    prev_g = ig_ref[jnp.maximum(i - 1, 0)]

    @pl.when((i == 0) | (g != prev_g))
    def _():
        wg16[...] = wg_ref[...].astype(BF16)
        wu16[...] = wu_ref[...].astype(BF16)
        wd16[...] = wd_ref[...].astype(BF16)

    @pl.when(iv_ref[i] == 0)
    def _():
        y_ref[...] = jnp.zeros_like(y_ref)

    @pl.when(iv_ref[i] == 1)
    def _():
        rows = lax.broadcasted_iota(I32, (MOE_TILE, 1), 0) + it_ref[i] * MOE_TILE
        sort = jnp.where(dest_ref[0] == rows, 1.0, 0.0).astype(BF16)
        x = _dot(sort, h_ref[...]).astype(BF16)
        gates = functools.reduce(jnp.add, [lax.dot_general(sort, p, _NT, preferred_element_type=F32)
                                           for p in _split_bf16(gate_ref[...])])
        lane = lax.broadcasted_iota(I32, (1, N_EXPERTS), 1)
        acc = jnp.zeros((MOE_TILE, D_MODEL), F32)
        for e in range(EXPERTS_PER_GROUP):
            ge = jnp.sum(jnp.where(lane == g * EXPERTS_PER_GROUP + e, gates, 0.0), axis=1, keepdims=True)
            act = _silu(_dot(x, wg16[e])) * _dot(x, wu16[e])
            acc = acc + _dot((act * ge).astype(BF16), wd16[e])
        y_ref[...] = acc.astype(BF16)


def _moe_unsort_kernel(nt_ref, hb_ref, ylo_ref, yhi_ref, r1_ref, r2_ref, x_ref, mod_ref, oa_ref, ob_ref, acc_ref, *,
                       n_ctx_blocks):
    b = pl.program_id(0)
    acc_ref[...] = jnp.zeros_like(acc_ref)
    chunk = max(MOE_TILE, MXU_DEPTH)
    tiles_per_chunk = chunk // MOE_TILE
    row1, row2 = r1_ref[0], r2_ref[0]

    def accumulate(y_part, first):
        def body(ci, carry):
            r0 = pl.multiple_of((ci - first) * chunk, chunk)
            rows = lax.broadcasted_iota(I32, (chunk, 1), 0) + ci * chunk
            sort = jnp.where(row1 == rows, 1.0, jnp.where(row2 == rows, 1.0, 0.0)).astype(BF16)
            acc_ref[...] += lax.dot_general(sort, y_part[pl.ds(r0, chunk), :], _TN, preferred_element_type=F32)
            return carry
        return body

    half = ylo_ref.shape[0] // chunk
    n_chunks = (nt_ref[b] + tiles_per_chunk - 1) // tiles_per_chunk
    lax.fori_loop(0, jnp.minimum(n_chunks, half), accumulate(ylo_ref, 0), 0)
    lax.fori_loop(half, jnp.maximum(n_chunks, half), accumulate(yhi_ref, half), 0)
    result = lambda: x_ref[...] + mod_ref[0][:, 5 * D_MODEL:6 * D_MODEL] * acc_ref[...]

    @pl.when(b < n_ctx_blocks)
    def _():
        oa_ref[...] = result()

    @pl.when(b >= n_ctx_blocks)
    def _():
        ob_ref[...] = result()


def _moe(hn2, gate_t, rank, totals, w_gate, w_up, w_down, x_mid, mod, st, layer):
    nb, tr, spb = st.tile, MOE_TILE, MOE_SLOTS_PER_BLOCK
    n_blocks = st.n_tiles
    plan = _moe_plan(rank, totals, n_blocks)

    wmap = lambda i, ig, ib, it, iv, isl: (layer * N_GROUPS + ig[i], 0, 0)
    once = pl.Buffered(1)
    y = pl.pallas_call(
        _moe_ffn_kernel,
        grid_spec=pltpu.PrefetchScalarGridSpec(
            num_scalar_prefetch=5,
            grid=(n_blocks * spb,),
            in_specs=[pl.BlockSpec((nb, D_MODEL), lambda i, ig, ib, it, iv, isl: (ib[i], 0)),
                      pl.BlockSpec((1, 1, nb), lambda i, ig, ib, it, iv, isl: (ig[i] * n_blocks + ib[i], 0, 0)),
                      pl.BlockSpec((N_EXPERTS, nb), lambda i, ig, ib, it, iv, isl: (0, ib[i])),
                      pl.BlockSpec((EXPERTS_PER_GROUP, D_MODEL, D_EXPERT), wmap, pipeline_mode=once),
                      pl.BlockSpec((EXPERTS_PER_GROUP, D_MODEL, D_EXPERT), wmap, pipeline_mode=once),
                      pl.BlockSpec((EXPERTS_PER_GROUP, D_EXPERT, D_MODEL), wmap, pipeline_mode=once)],
            out_specs=pl.BlockSpec((tr, D_MODEL), lambda i, ig, ib, it, iv, isl: (isl[i], 0)),
            scratch_shapes=[pltpu.VMEM((EXPERTS_PER_GROUP, D_MODEL, D_EXPERT), BF16),
                            pltpu.VMEM((EXPERTS_PER_GROUP, D_MODEL, D_EXPERT), BF16),
                            pltpu.VMEM((EXPERTS_PER_GROUP, D_EXPERT, D_MODEL), BF16)]),
        out_shape=jax.ShapeDtypeStruct((n_blocks * spb * tr, D_MODEL), BF16),
        compiler_params=_params("arbitrary"),
        name="moe_ffn",
    )(plan["item_g"], plan["item_b"], plan["item_tile"], plan["item_valid"], plan["item_slot"],
      hn2, plan["dest"], gate_t, w_gate, w_up, w_down)

    return pl.pallas_call(
        functools.partial(_moe_unsort_kernel, n_ctx_blocks=st.n_ctx),
        grid_spec=pltpu.PrefetchScalarGridSpec(
            num_scalar_prefetch=2,
            grid=(n_blocks,),
            in_specs=[pl.BlockSpec((spb // 2 * tr, D_MODEL), lambda b, nt, hb: (2 * b, 0)),
                      pl.BlockSpec((spb // 2 * tr, D_MODEL), lambda b, nt, hb: (2 * hb[b] + 1, 0)),
                      pl.BlockSpec((1, 1, nb), lambda b, nt, hb: (b, 0, 0)),
                      pl.BlockSpec((1, 1, nb), lambda b, nt, hb: (b, 0, 0)),
                      pl.BlockSpec((nb, D_MODEL), lambda b, nt, hb: (b, 0)),
                      pl.BlockSpec((1, 1, 6 * D_MODEL), lambda b, nt, hb: (st.mod_idx(b), 0, 0))],
            out_specs=[pl.BlockSpec((nb, D_MODEL), lambda b, nt, hb: (st.ctx_idx(b), 0)),
                       pl.BlockSpec((nb, D_MODEL), lambda b, nt, hb: (st.lat_idx(b), 0))],
            scratch_shapes=[pltpu.VMEM((nb, D_MODEL), F32)]),
        out_shape=[jax.ShapeDtypeStruct((st.n_ctx * nb, D_MODEL), F32),
                   jax.ShapeDtypeStruct(((n_blocks - st.n_ctx) * nb, D_MODEL), F32)],
        compiler_params=_params("arbitrary"),
        name="moe_unsort",
    )(plan["n_tiles"], plan["hi_block"], y, y, plan["row1"], plan["row2"], x_mid, mod)


def _block_avg(group):
    i = np.arange(LANES)
    return jnp.asarray((i[:, None] // group == i[None, :] // group) / group, dtype=BF16)


def _rope_tables(n_tokens, n_identity):
    rows = n_tokens // GRID_W
    pos_r = jnp.repeat(jnp.arange(rows, dtype=F32), GRID_W)
    pos_c = jnp.tile(jnp.arange(GRID_W, dtype=F32), rows)
    n_freq = DA_QK // 4
    inv = ROPE_BASE ** (-jnp.arange(n_freq, dtype=F32) / n_freq)
    ang = jnp.concatenate([pos_r[:, None] * inv, pos_c[:, None] * inv], axis=-1)
    cos = jnp.repeat(jnp.cos(ang), 2, axis=-1)
    sin = jnp.repeat(jnp.sin(ang), 2, axis=-1) * jnp.tile(jnp.asarray([-1.0, 1.0], F32), DA_QK // 2)
    cos = jnp.concatenate([jnp.ones((n_identity, DA_QK), F32), cos], axis=0)
    sin = jnp.concatenate([jnp.zeros((n_identity, DA_QK), F32), sin], axis=0)
    return jnp.tile(cos, (1, 2 * H_DA)), jnp.tile(sin, (1, 2 * H_DA))


def kernel(x_prompt, x_sample, cache_k, cache_v, state_ret, state_hgrn, c, c_ctx, w_in, w_out, w_ada, b_ada,
           norm_mix, norm_ffn, da_qnorm, da_knorm, da_lambda, da_subln, ret_decay, ret_norm, hg_lb, hg_norm,
           w_router, router_bias, w_gate, w_up, w_down):
    n_ctx, l_ctx, _ = x_prompt.shape
    n_lat, l_lat, _ = x_sample.shape
    t_ctx = n_ctx * l_ctx
    past = cache_k.shape[2]
    assert TOKEN_TILE % l_ctx == 0 and MOE_BLOCK % TOKEN_TILE == 0
    assert l_lat % MOE_BLOCK == 0 and t_ctx % MOE_BLOCK == 0 and l_lat % GRID_W == 0

    ctx_row = n_lat
    n_cond = -(-(n_lat + 1) // 8) * 8
    cond = jnp.zeros((n_cond, D_MODEL), F32).at[:n_lat].set(c).at[ctx_row].set(c_ctx)
    mod_all = _ada(cond, w_ada, b_ada)

    st = _Stream(t_ctx, n_lat, l_lat, TOKEN_TILE, ctx_row)
    st_moe = _Stream(t_ctx, n_lat, l_lat, MOE_BLOCK, ctx_row)
    g32, g64 = _block_avg(DA_QK), _block_avg(DA_V)
    cos_t, sin_t = _rope_tables(l_lat, TOKEN_TILE)
    lv = jnp.asarray(_level_table(REC_CHUNK))
    p_lb = jax.nn.softmax(hg_lb.astype(F32), axis=0)
    lb_all = jnp.cumsum(p_lb, axis=0) - p_lb[0]
    wr_hi = w_router.T.astype(BF16)
    wr_lo = (w_router.T - wr_hi.astype(F32)).astype(BF16)
    rbias = router_bias.astype(F32).reshape(N_EXPERTS, 1)
    flat_w = lambda w: w.reshape(DEPTH * N_EXPERTS, w.shape[2], w.shape[3])
    w_gate, w_up, w_down = flat_w(w_gate), flat_w(w_up), flat_w(w_down)
    flat_s = lambda s: s.astype(F32).reshape(n_lat, DEPTH, 2, D_REC, D_HEAD)
    states_in = (flat_s(state_ret), flat_s(state_hgrn))
    cache_k16 = cache_k.reshape(n_lat, DEPTH, past, D_QK).astype(BF16)
    cache_v16 = cache_v.reshape(n_lat, DEPTH, past, D_V).astype(BF16)

    x = (x_prompt.reshape(t_ctx, D_MODEL), x_sample.reshape(n_lat * l_lat, D_MODEL))
    caches, states_out = None, None
    for l in range(DEPTH):
        lam_init = 0.8 - 0.6 * math.exp(-0.3 * l)
        lp = da_lambda[l].astype(F32)
        lam = (jnp.exp(jnp.sum(lp[0] * lp[1])) - jnp.exp(jnp.sum(lp[2] * lp[3])) + lam_init).reshape(1)
        mod = mod_all[l].reshape(n_cond, 1, 6 * D_MODEL)
        qw = (jnp.tile(da_qnorm[l].astype(F32), 2 * H_DA) * (DA_QK ** -0.5)).reshape(1, D_QK)
        kw = jnp.tile(da_knorm[l].astype(F32), 2 * H_DA).reshape(1, D_QK)

        qb, kb, vb, rest, kc, vc = _inproj(x[0], x[1], mod, norm_mix[l].reshape(1, D_MODEL),
                                           w_in[l].astype(BF16), qw, kw, g32, cos_t, sin_t, st, l, n_ctx, l_ctx,
                                           caches)
        caches = (kc, vc)

        oda = (_attention(lam, qb, kb, vb, 0, n_ctx, l_ctx, CTX_PAIRS_PER_STEP),
               _attention(lam, qb, kb, vb, t_ctx, n_lat, l_lat, LAT_PAIRS_PER_STEP, l,
                          cache=(cache_k16, cache_v16)))

        ret_tabs = _retention_tables(-jnp.exp(ret_decay[l].astype(F32)), REC_CHUNK)
        lb_row = lb_all[l].reshape(1, D_REC)
        ofc, obc, s_ret, s_hg = _recurrence(rest, ret_tabs, lb_row, lv, 0, n_ctx, l_ctx, l,
                                            want_states=True, states_out=states_out)
        states_out = (s_ret, s_hg)
        ofl, obl = _recurrence(rest, ret_tabs, lb_row, lv, t_ctx, n_lat, l_lat, l, states_in=states_in)

        subln = (jnp.tile(da_subln[l].astype(F32), H_DA) * (1.0 - lam_init)).reshape(1, D_V)
        retn = jnp.tile(ret_norm[l].astype(F32), H_RET).reshape(1, D_REC)
        hgn = jnp.tile(hg_norm[l].astype(F32), H_HG).reshape(1, D_REC)
        x_mid, hn2, gate_t, rank, totals = _outproj(x, oda, (ofc, ofl), (obc, obl), rest, mod, subln, retn, hgn, g64,
                                      w_out[l].astype(BF16), norm_ffn[l].reshape(1, D_MODEL), wr_hi, wr_lo,
                                      rbias, st)
        x = _moe(hn2, gate_t, rank, totals, w_gate, w_up, w_down, x_mid, mod, st_moe, l)

    kc, vc = caches
    s_ret, s_hg = states_out
    return (x[0].reshape(n_ctx, l_ctx, D_MODEL), x[1].reshape(n_lat, l_lat, D_MODEL),
            kc.reshape(n_ctx, DEPTH, l_ctx, H_DA, 2, DA_QK), vc.reshape(n_ctx, DEPTH, l_ctx, H_DA, DA_V),
            s_ret.reshape(n_ctx, DEPTH, 2, H_RET, RET_DK, RET_DV),
            s_hg.reshape(n_ctx, DEPTH, 2, H_HG, HG_DK, HG_DV))
```

```python
import functools
import math

import numpy as np
import jax
import jax.numpy as jnp
from jax import lax
from jax.experimental import pallas as pl
from jax.experimental.pallas import tpu as pltpu

F32 = jnp.float32
BF16 = jnp.bfloat16
I32 = jnp.int32

D_MODEL = 1024
DEPTH = 2
GRID_W = 64
EPS = 1e-6
MIN_GATE = 1e-30
ROPE_BASE = 10000.0
H_DA, DA_QK, DA_V = 8, 32, 64
H_RET, RET_DK, RET_DV = 4, 64, 64
H_HG, HG_DK, HG_DV = 4, 64, 64
D_QK = H_DA * 2 * DA_QK
D_V = H_DA * DA_V
N_REC_HEADS = 4
D_HEAD = 64
D_REC = N_REC_HEADS * D_HEAD
D_REST = 9 * D_REC
D_IN = 2 * D_QK + D_V + D_REST
N_EXPERTS, N_GROUPS, EXPERTS_PER_GROUP = 16, 4, 4
D_EXPERT = 512
MASKED_SCORE = -2.0

LANES = 128
TOKEN_TILE = 512
Q_TILE = 1024
CTX_PAIRS_PER_STEP = 1
LAT_PAIRS_PER_STEP = 2
REC_CHUNK = 128
MOE_BLOCK = 1024
MOE_TILE = 256
MXU_DEPTH = 256
MOE_SLOTS_PER_BLOCK = -(-2 * MOE_BLOCK // MOE_TILE) + N_GROUPS
VMEM_LIMIT = 56 * 1024 * 1024

_NT = (((1,), (1,)), ((), ()))
_TN = (((0,), (0,)), ((), ()))


def _params(*sem):
    return pltpu.CompilerParams(dimension_semantics=sem, vmem_limit_bytes=VMEM_LIMIT)


def _dot(a, b):
    return jnp.dot(a, b, preferred_element_type=F32)


def _split_bf16(t, terms=2):
    out = []
    for _ in range(terms - 1):
        hi = t.astype(BF16)
        out.append(hi)
        t = t - hi.astype(F32)
    out.append(t.astype(BF16))
    return out


def _group_mean_sq(t, g):
    sq = (t * t).astype(BF16)
    outs = [_dot(sq[:, s * LANES:(s + 1) * LANES], g) for s in range(t.shape[1] // LANES)]
    return jnp.concatenate(outs, axis=1) if len(outs) > 1 else outs[0]


def _silu(t):
    return t * jax.nn.sigmoid(t)


class _Stream:
    def __init__(self, n_ctx_rows, n_lat_seq, lat_len, tile, ctx_row):
        self.tile = tile
        self.n_ctx = n_ctx_rows // tile
        self.per_seq = lat_len // tile
        self.n_tiles = self.n_ctx + n_lat_seq * self.per_seq
        self.ctx_row = ctx_row

    def ctx_idx(self, i):
        return jnp.minimum(i, self.n_ctx - 1)

    def lat_idx(self, i):
        return jnp.maximum(i - self.n_ctx, 0)

    def mod_idx(self, i):
        return jnp.where(i < self.n_ctx, self.ctx_row, (i - self.n_ctx) // self.per_seq)

    def pos_idx(self, i):
        return jnp.where(i < self.n_ctx, 0, 1 + (i - self.n_ctx) % self.per_seq)


def _ada_kernel(c_ref, w_ref, b_ref, o_ref):
    s = _silu(c_ref[...]).astype(BF16)
    o_ref[0] = _dot(s, w_ref[0].astype(BF16)) + b_ref[0]


def _ada(cond, w_ada, b_ada):
    rows = cond.shape[0]
    n_tile = 1536
    return pl.pallas_call(
        _ada_kernel,
        grid=(DEPTH, 6 * D_MODEL // n_tile),
        in_specs=[pl.BlockSpec((rows, D_MODEL), lambda l, j: (0, 0)),
                  pl.BlockSpec((1, D_MODEL, n_tile), lambda l, j: (l, 0, j)),
                  pl.BlockSpec((1, 1, n_tile), lambda l, j: (l, 0, j))],
        out_specs=pl.BlockSpec((1, rows, n_tile), lambda l, j: (l, 0, j)),
        out_shape=jax.ShapeDtypeStruct((DEPTH, rows, 6 * D_MODEL), F32),
        compiler_params=_params("arbitrary", "arbitrary"),
        name="ada",
    )(cond, w_ada, b_ada.reshape(DEPTH, 1, 6 * D_MODEL))


def _inproj_kernel(*refs, n_ctx_tiles, first):
    (xa_ref, xb_ref, mod_ref, nw_ref, w_ref, qw_ref, kw_ref, g32_ref, cos_ref, sin_ref) = refs[:10]
    q_ref, kb_ref, vb_ref, rest_ref, kc_ref, vc_ref = refs[10:] if first else refs[12:]
    is_ctx = pl.program_id(0) < n_ctx_tiles
    x = jnp.where(is_ctx, xa_ref[...], xb_ref[...])
    y = x * lax.rsqrt(jnp.mean(x * x, axis=-1, keepdims=True) + EPS) * nw_ref[...]
    mod = mod_ref[0]
    hn = (y * (1.0 + mod[:, D_MODEL:2 * D_MODEL]) + mod[:, 0:D_MODEL]).astype(BF16)
    g32 = g32_ref[...]
    lane = lax.broadcasted_iota(I32, (1, D_QK), 1)
    even = (lane & 1) == 0

    def qk_norm_rope(t, w):
        t = t * lax.rsqrt(_group_mean_sq(t, g32) + EPS) * w
        partner = jnp.where(even, pltpu.roll(t, D_QK - 1, 1), pltpu.roll(t, 1, 1))
        return t * cos_ref[...] + partner * sin_ref[...]

    q_ref[...] = qk_norm_rope(_dot(hn, w_ref[:, 0:D_QK]), qw_ref[...]).astype(BF16)
    k = qk_norm_rope(_dot(hn, w_ref[:, D_QK:2 * D_QK]), kw_ref[...])
    kb_ref[...] = k.astype(BF16)
    v = _dot(hn, w_ref[:, 2 * D_QK:2 * D_QK + D_V])
    vb_ref[...] = v.astype(BF16)
    rest_ref[...] = _dot(hn, w_ref[:, 2 * D_QK + D_V:D_IN])

    @pl.when(is_ctx)
    def _():
        n_seq, _, seq_len, _ = kc_ref.shape
        kc_ref[:, 0] = k.reshape(n_seq, seq_len, D_QK)
        vc_ref[:, 0] = v.reshape(n_seq, seq_len, D_V)
        if first:
            kc_ref[:, 1:] = jnp.zeros_like(kc_ref[:, 1:])
            vc_ref[:, 1:] = jnp.zeros_like(vc_ref[:, 1:])


def _inproj(xa, xb, mod, norm_w, w_in16, qw, kw, g32, cos_t, sin_t, st, layer, n_ctx_seq, l_ctx, caches):
    tm = st.tile
    t = st.n_tiles * tm
    first = caches is None
    row = lambda i: (i, 0)
    const = lambda i: (0, 0)
    in_specs = [pl.BlockSpec((tm, D_MODEL), lambda i: (st.ctx_idx(i), 0)),
                pl.BlockSpec((tm, D_MODEL), lambda i: (st.lat_idx(i), 0)),
                pl.BlockSpec((1, 1, 6 * D_MODEL), lambda i: (st.mod_idx(i), 0, 0)),
                pl.BlockSpec((1, D_MODEL), const),
                pl.BlockSpec((D_MODEL, D_IN), const, pipeline_mode=pl.Buffered(1)),
                pl.BlockSpec((1, D_QK), const),
                pl.BlockSpec((1, D_QK), const),
                pl.BlockSpec((LANES, LANES), const),
                pl.BlockSpec((tm, D_QK), lambda i: (st.pos_idx(i), 0)),
                pl.BlockSpec((tm, D_QK), lambda i: (st.pos_idx(i), 0))]
    args = [xa, xb, mod, norm_w, w_in16, qw, kw, g32, cos_t, sin_t]
    aliases = {}
    per_tile = tm // l_ctx
    if first:
        cache_spec = pl.BlockSpec((per_tile, DEPTH, l_ctx, D_QK), lambda i: (st.ctx_idx(i), 0, 0, 0))
    else:
        cache_spec = pl.BlockSpec((per_tile, 1, l_ctx, D_QK), lambda i: (st.ctx_idx(i), layer, 0, 0))
        in_specs += [pl.BlockSpec(memory_space=pl.ANY)] * 2
        args += list(caches)
        aliases = {10: 4, 11: 5}
    cache_shape = jax.ShapeDtypeStruct((n_ctx_seq, DEPTH, l_ctx, D_QK), F32)
    return pl.pallas_call(
        functools.partial(_inproj_kernel, n_ctx_tiles=st.n_ctx, first=first),
        grid=(st.n_tiles,),
        in_specs=in_specs,
        out_specs=[pl.BlockSpec((tm, D_QK), row), pl.BlockSpec((tm, D_QK), row),
                   pl.BlockSpec((tm, D_V), row), pl.BlockSpec((tm, D_REST), row), cache_spec, cache_spec],
        out_shape=[jax.ShapeDtypeStruct((t, D_QK), BF16), jax.ShapeDtypeStruct((t, D_QK), BF16),
                   jax.ShapeDtypeStruct((t, D_V), BF16), jax.ShapeDtypeStruct((t, D_REST), F32),
                   cache_shape, cache_shape],
        input_output_aliases=aliases,
        compiler_params=_params("arbitrary"),
        name="inproj",
    )(*args)


def _attn_kernel(*refs, with_cache):
    if with_cache:
        lam_ref, q_ref, k_ref, v_ref, ck_ref, cv_ref, o_ref = refs
    else:
        lam_ref, q_ref, k_ref, v_ref, o_ref = refs
    lam = lam_ref[0]
    lane = lax.broadcasted_iota(I32, (1, LANES), 1)
    left = lane < DA_V
    one = jnp.ones((), BF16)
    zero = jnp.zeros((), BF16)
    owns = [left, jnp.logical_not(left)]
    tq = min(Q_TILE, q_ref.shape[0])
    for p in range(q_ref.shape[1] // LANES):
        slab = slice(p * LANES, (p + 1) * LANES)
        segs = [(k_ref[:, slab], v_ref[:, slab])]
        if with_cache:
            segs.append((ck_ref[0, 0, :, slab], cv_ref[0, 0, :, slab]))
        vms = [[jnp.where(own, v, one) for _, v in segs] for own in owns]

        def q_block(i, carry, slab=slab, segs=segs, vms=vms):
            r0 = i * tq if isinstance(i, int) else pl.multiple_of(i * tq, tq)
            q = q_ref[pl.ds(r0, tq), slab]
            acc = jnp.zeros((tq, LANES), F32)
            for side in range(2):
                for r in range(2):
                    lo = side * DA_V + r * DA_QK
                    qm = jnp.where((lane >= lo) & (lane < lo + DA_QK), q, zero)
                    ss = [lax.dot_general(qm, k, _NT, preferred_element_type=F32) for k, _ in segs]
                    m = functools.reduce(jnp.maximum, [jnp.max(s, axis=-1, keepdims=True) for s in ss])
                    res = functools.reduce(
                        jnp.add, [_dot(jnp.exp(s - m).astype(BF16), vm) for s, vm in zip(ss, vms[side])])
                    den = pltpu.roll(res, DA_V, 1)
                    coef = 1.0 if r == 0 else -lam
                    acc = acc + jnp.where(owns[side], coef * (res / den), 0.0)
            o_ref[pl.ds(r0, tq), slab] = acc
            return carry

        if q_ref.shape[0] == tq:
            q_block(0, 0)
        else:
            lax.fori_loop(0, q_ref.shape[0] // tq, q_block, 0)


def _attention(lam, qb, kb, vb, row0, n_seq, seq_len, pairs_per_step, layer=0, cache=None):
    sb0 = row0 // seq_len
    width = pairs_per_step * LANES
    smap = lambda b, p: (sb0 + b, p)
    smem = pl.BlockSpec(memory_space=pltpu.SMEM)
    in_specs = [smem] + [pl.BlockSpec((seq_len, width), smap)] * 3
    args = [lam, qb, kb, vb]
    if cache is not None:
        ck, cv = cache
        cmap = lambda b, p: (b, layer, 0, p)
        in_specs += [pl.BlockSpec((1, 1, ck.shape[2], width), cmap),
                     pl.BlockSpec((1, 1, cv.shape[2], width), cmap)]
        args += [ck, cv]
    return pl.pallas_call(
        functools.partial(_attn_kernel, with_cache=cache is not None),
        grid=(n_seq, D_QK // width),
        in_specs=in_specs,
        out_specs=pl.BlockSpec((seq_len, width), lambda b, p: (b, p)),
        out_shape=jax.ShapeDtypeStruct((n_seq * seq_len, D_V), F32),
        compiler_params=_params("arbitrary", "arbitrary"),
        name="attn_lat" if cache is not None else "attn_ctx",
    )(*args)


def _level_table(c):
    t = np.arange(c)[:, None] ^ np.arange(c)[None, :]
    lv = np.where(t == 0, 32, 31 - np.floor(np.log2(np.maximum(t, 1))).astype(np.int64))
    return np.tile(lv, (N_REC_HEADS, 1)).astype(np.int32)


def _retention_tables(log_gamma, c):
    g = jnp.repeat(log_gamma, D_HEAD, axis=1)
    t = jnp.arange(c, dtype=F32)
    diff = t[:, None] - t[None, :]
    gh = log_gamma[:, :, None, None]
    d_f = jnp.where(diff >= 0, jnp.exp(gh[0] * jnp.maximum(diff, 0.0)), 0.0)
    d_b = jnp.where(diff <= 0, jnp.exp(gh[1] * jnp.maximum(-diff, 0.0)), 0.0)
    d = jnp.stack([d_f, d_b]).reshape(2, N_REC_HEADS * c, c)
    q_f, k_f = jnp.exp(g[0] * (t[:, None] + 1.0)), jnp.exp(g[0] * (c - 1.0 - t[:, None]))
    q_b, k_b = jnp.exp(g[1] * (c - t[:, None])), jnp.exp(g[1] * t[:, None])
    f = jnp.stack([jnp.stack([q_f, k_f]), jnp.stack([q_b, k_b])])
    a = jnp.exp(g * c).reshape(2, 1, D_REC)
    return d, f, a


def _rec_kernel(*refs, state_in, state_out, aliased):
    refs = list(refs)
    rf_ref, rb_ref, rd_ref, rfac_ref, ra_ref, lb_ref, lv_ref = refs[:7]
    del refs[:7]
    if state_in:
        sret_ref, shg_ref, t4_ref = refs[:3]
        del refs[:3]
    if state_out:
        t4t_ref = refs.pop(0)
    del refs[:aliased]
    of_ref, ob_ref = refs[:2]
    del refs[:2]
    if state_out:
        oret_ref, ohg_ref = refs[:2]
        del refs[:2]
    (st_scr,) = refs

    ci = pl.program_id(1)
    c = rf_ref.shape[0]
    lane = lax.broadcasted_iota(I32, (1, D_REC), 1)
    head = lane >> 6
    row = lax.broadcasted_iota(I32, (c, 1), 0)
    srow = lax.broadcasted_iota(I32, (D_REC, 1), 0)
    same_head = (srow >> 6) == head
    eye = srow == lane
    head_is = [head == h for h in range(N_REC_HEADS)]
    head_m16 = [jnp.broadcast_to(jnp.where(m, 1.0, 0.0), (c, D_REC)).astype(BF16) for m in head_is]
    head_m16_half = [jnp.broadcast_to(jnp.where(m, 1.0, 0.0), (c // 2, D_REC)).astype(BF16) for m in head_is]

    @pl.when(ci == 0)
    def _():
        if state_in:
            for d in range(2):
                for idx, ref in ((d, sret_ref), (2 + d, shg_ref)):
                    tiled = functools.reduce(jnp.add, [_dot(p, t4_ref[...]) for p in _split_bf16(ref[0, 0, d], 3)])
                    st_scr[idx] = jnp.where(same_head, tiled, 0.0)
        else:
            st_scr[...] = jnp.zeros_like(st_scr)

    def expand(t16):
        return jnp.concatenate([t16 * m for m in head_m16], axis=0)

    def nt(a, b):
        return lax.dot_general(a, b, _NT, preferred_element_type=F32)

    def half_rows(t, h, second):
        o = h if second else 0
        return jnp.concatenate([t[j * 2 * h + o:j * 2 * h + o + h] for j in range(c // (2 * h))], axis=0)

    def spread_rows(t, h, second):
        z = jnp.zeros((h, t.shape[1]), t.dtype)
        parts = []
        for j in range(c // (2 * h)):
            parts += [z, t[j * h:(j + 1) * h]] if second else [t[j * h:(j + 1) * h], z]
        return jnp.concatenate(parts, axis=0)

    def finish(q, k, v16, scores, qfac, kfac, a_row, st_ref, own_term=False):
        st = st_ref[...]
        o = _dot((q * qfac).astype(BF16), st.astype(BF16))
        if own_term:
            o = o + _dot((q * k).astype(BF16), jnp.where(same_head, 1.0, 0.0).astype(BF16)) * v16.astype(F32)
        o_stack = _dot(scores.astype(BF16), v16)
        for h in range(N_REC_HEADS):
            o = o + jnp.where(head_is[h], o_stack[h * c:(h + 1) * c], 0.0)
        kv = lax.dot_general((k * kfac).astype(BF16), v16, _TN, preferred_element_type=F32)
        a_col = jnp.sum(jnp.where(eye, a_row, 0.0), axis=1, keepdims=True)
        st_ref[...] = st * a_col + jnp.where(same_head, kv, 0.0)
        return o

    def tree_scores(q, k, la, reverse):
        lv = lv_ref[...]
        scores = jnp.zeros(lv.shape, F32)
        pre, tot = la, la
        h, level = 1, 31
        while h < c:
            bit = (row & h) != 0
            query_side = jnp.logical_not(bit) if reverse else bit
            x = jnp.exp(jnp.minimum(jnp.where(query_side, pre, tot - pre), 0.0))
            qx = q * x
            kx = jnp.where(query_side, 0.0, k * x).astype(BF16)
            if h < 8:
                s_level = nt(expand(qx.astype(BF16)), kx)
            else:
                qc = half_rows(qx, h, not reverse).astype(BF16)
                sc = nt(jnp.concatenate([qc * m for m in head_m16_half], axis=0), kx)
                s_level = jnp.concatenate([spread_rows(sc[hh * c // 2:(hh + 1) * c // 2], h, not reverse)
                                           for hh in range(N_REC_HEADS)], axis=0)
            scores = jnp.where(lv == level, s_level, scores)
            partner = jnp.where(bit, pltpu.roll(tot, h, 0), pltpu.roll(tot, c - h, 0))
            pre = pre + jnp.where(query_side, partner, 0.0)
            tot = tot + partner
            h, level = 2 * h, level - 1
        return scores, pre, tot

    lb = lb_ref[...]
    col = lambda ref, j: ref[:, j * D_REC:(j + 1) * D_REC]
    for d, (r_ref, o_ref) in enumerate(((rf_ref, of_ref), (rb_ref, ob_ref))):
        q, k, v16 = col(r_ref, 0), col(r_ref, 1) * (RET_DK ** -0.5), col(r_ref, 2).astype(BF16)
        scores = nt(expand(q.astype(BF16)), k.astype(BF16)) * rd_ref[d]
        o_ref[:, 0:D_REC] = finish(q, k, v16, scores, rfac_ref[d, 0], rfac_ref[d, 1], ra_ref[d], st_scr.at[d])
        z = col(r_ref, 5 + d)
        la = jnp.log(jnp.maximum(lb + (1.0 - lb) * jax.nn.sigmoid(z), MIN_GATE))
        q, k, v16 = _silu(col(r_ref, 4)), (1.0 - lb) * jax.nn.sigmoid(-z), col(r_ref, 7).astype(BF16)
        scores, pre, tot = tree_scores(q, k, la, reverse=d == 1)
        o_ref[:, D_REC:2 * D_REC] = finish(q, k, v16, scores, jnp.exp(jnp.minimum(pre, 0.0)),
                                           jnp.exp(jnp.minimum(tot - pre, 0.0)), jnp.exp(tot[0:1, :]),
                                           st_scr.at[2 + d], own_term=True)

    if state_out:
        @pl.when(ci == pl.num_programs(1) - 1)
        def _():
            for d in range(2):
                for idx, ref in ((d, oret_ref), (2 + d, ohg_ref)):
                    ref[0, 0, d] = functools.reduce(
                        jnp.add, [_dot(p, t4t_ref[...]) for p in _split_bf16(st_scr[idx], 3)])
            if state_out == "first":
                oret_ref[0, 1:] = jnp.zeros_like(oret_ref[0, 1:])
                ohg_ref[0, 1:] = jnp.zeros_like(ohg_ref[0, 1:])


def _recurrence(rest, ret_tabs, lb_row, lv, row0, n_seq, seq_len, layer, *, states_in=None, want_states=False,
                states_out=None):
    c = REC_CHUNK
    nc = seq_len // c
    cb0 = row0 // c
    const2 = lambda b, i: (0, 0)
    const3 = lambda b, i: (0, 0, 0)
    rd, rfac, ra = ret_tabs
    in_specs = [pl.BlockSpec((c, D_REST), lambda b, i: (cb0 + b * nc + i, 0)),
                pl.BlockSpec((c, D_REST), lambda b, i: (cb0 + b * nc + nc - 1 - i, 0)),
                pl.BlockSpec((2, N_REC_HEADS * c, c), const3),
                pl.BlockSpec((2, 2, c, D_REC), lambda b, i: (0, 0, 0, 0)),
                pl.BlockSpec((2, 1, D_REC), const3),
                pl.BlockSpec((1, D_REC), const2),
                pl.BlockSpec((N_REC_HEADS * c, c), const2)]
    args = [rest, rest, rd, rfac, ra, lb_row, lv]
    tile4 = jnp.tile(jnp.eye(D_HEAD, dtype=BF16), (1, N_REC_HEADS))
    if states_in is not None:
        blk = pl.BlockSpec((1, 1, 2, D_REC, D_HEAD), lambda b, i: (b, layer, 0, 0, 0))
        in_specs += [blk, blk, pl.BlockSpec((D_HEAD, D_REC), const2)]
        args += [states_in[0], states_in[1], tile4]
    if want_states:
        in_specs += [pl.BlockSpec((D_REC, D_HEAD), const2)]
        args += [tile4.T]
    out_specs = [pl.BlockSpec((c, 2 * D_REC), lambda b, i: (b * nc + i, 0)),
                 pl.BlockSpec((c, 2 * D_REC), lambda b, i: (b * nc + nc - 1 - i, 0))]
    out_shape = [jax.ShapeDtypeStruct((n_seq * seq_len, 2 * D_REC), F32)] * 2
    aliases = {}
    state_mode = None
    if want_states:
        st_shape = jax.ShapeDtypeStruct((n_seq, DEPTH, 2, D_REC, D_HEAD), F32)
        out_shape += [st_shape, st_shape]
        if states_out is None:
            state_mode = "first"
            out_specs += [pl.BlockSpec((1, DEPTH, 2, D_REC, D_HEAD), lambda b, i: (b, 0, 0, 0, 0))] * 2
        else:
            state_mode = "next"
            out_specs += [pl.BlockSpec((1, 1, 2, D_REC, D_HEAD), lambda b, i: (b, layer, 0, 0, 0))] * 2
            aliases = {len(args): 2, len(args) + 1: 3}
            in_specs += [pl.BlockSpec(memory_space=pl.ANY)] * 2
            args += list(states_out)
    return pl.pallas_call(
        functools.partial(_rec_kernel, state_in=states_in is not None, state_out=state_mode,
                          aliased=len(aliases)),
        grid=(n_seq, nc),
        in_specs=in_specs,
        out_specs=out_specs,
        out_shape=out_shape,
        scratch_shapes=[pltpu.VMEM((4, D_REC, D_REC), F32)],
        input_output_aliases=aliases,
        compiler_params=_params("arbitrary", "arbitrary"),
        name="rec_lat" if states_in is not None else "rec_ctx",
    )(*args)


def _route(logits, bias):
    scores = jax.nn.sigmoid(logits)
    sel = scores + bias
    srow = [scores[e:e + 1, :] for e in range(N_EXPERTS)]
    rows = [sel[e:e + 1, :] for e in range(N_EXPERTS)]
    gs = []
    for g in range(N_GROUPS):
        a, b, c, d = rows[4 * g:4 * g + 4]
        gs.append(functools.reduce(jnp.maximum, [a + b, a + c, a + d, b + c, b + d, c + d]))
    best = jnp.zeros_like(gs[0], dtype=I32)
    best_v = gs[0]
    for g in range(1, N_GROUPS):
        upd = gs[g] > best_v
        best = jnp.where(upd, g, best)
        best_v = jnp.where(upd, gs[g], best_v)
    masked = [jnp.where(best == e // EXPERTS_PER_GROUP, rows[e], MASKED_SCORE) for e in range(N_EXPERTS)]
    i1 = jnp.zeros_like(best)
    v1 = masked[0]
    for e in range(1, N_EXPERTS):
        upd = masked[e] > v1
        i1 = jnp.where(upd, e, i1)
        v1 = jnp.where(upd, masked[e], v1)
    i2 = jnp.zeros_like(best)
    v2 = jnp.full_like(v1, -jnp.inf)
    for e in range(N_EXPERTS):
        upd = (masked[e] > v2) & (i1 != e)
        i2 = jnp.where(upd, e, i2)
        v2 = jnp.where(upd, masked[e], v2)
    w1 = functools.reduce(jnp.add, [jnp.where(i1 == e, srow[e], 0.0) for e in range(N_EXPERTS)])
    w2 = functools.reduce(jnp.add, [jnp.where(i2 == e, srow[e], 0.0) for e in range(N_EXPERTS)])
    tot = w1 + w2
    w1, w2 = w1 / tot, w2 / tot
    expert = lax.broadcasted_iota(I32, logits.shape, 0)
    return jnp.where(expert == i1, w1, 0.0) + jnp.where(expert == i2, w2, 0.0)


def _outproj_kernel(xa_ref, xb_ref, odaa_ref, odab_ref, ofa_ref, ofb_ref, oba_ref, obb_ref, rg_ref, gg_ref,
                    mod_ref, subln_ref, retn_ref, hgn_ref, g64_ref, wout_ref, nffn_ref, wrh_ref, wrl_ref, rb_ref,
                    xmid_ref, hn2_ref, gate_ref, rank_ref, total_ref, count_ref, *, n_ctx_tiles):
    g64 = g64_ref[...]
    is_ctx = pl.program_id(0) < n_ctx_tiles
    pick = lambda a_ref, b_ref: jnp.where(is_ctx, a_ref[...], b_ref[...])

    def gnorm(t, w):
        return t * lax.rsqrt(_group_mean_sq(t, g64) + EPS) * w

    o_da = gnorm(pick(odaa_ref, odab_ref), subln_ref[...])
    o_rec = pick(ofa_ref, ofb_ref) + pick(oba_ref, obb_ref)
    o_ret = gnorm(o_rec[:, 0:D_REC], retn_ref[...]) * _silu(rg_ref[...])
    o_hg = gnorm(o_rec[:, D_REC:2 * D_REC], hgn_ref[...]) * _silu(gg_ref[...])
    mixed = jnp.concatenate([o_da, o_ret, o_hg], axis=1).astype(BF16)
    mod = mod_ref[0]
    x = pick(xa_ref, xb_ref) + mod[:, 2 * D_MODEL:3 * D_MODEL] * _dot(mixed, wout_ref[...])
    xmid_ref[...] = x
    y = x * lax.rsqrt(jnp.mean(x * x, axis=-1, keepdims=True) + EPS) * nffn_ref[...]
    hn = y * (1.0 + mod[:, 4 * D_MODEL:5 * D_MODEL]) + mod[:, 3 * D_MODEL:4 * D_MODEL]
    hi, lo = _split_bf16(hn)
    hn2_ref[...] = hi
    wrh = wrh_ref[...]
    logits = (lax.dot_general(wrh, hi, _NT, preferred_element_type=F32)
              + lax.dot_general(wrh, lo, _NT, preferred_element_type=F32)
              + lax.dot_general(wrl_ref[...], hi, _NT, preferred_element_type=F32))
    gate = _route(logits, rb_ref[...])
    gate_ref[...] = gate

    tm = gate.shape[1]
    member = jnp.concatenate(
        [jnp.max(gate[EXPERTS_PER_GROUP * g:EXPERTS_PER_GROUP * (g + 1)], axis=0, keepdims=True)
         for g in range(N_GROUPS)] + [jnp.zeros((8 - N_GROUPS, tm), F32)], axis=0)
    member = jnp.where(member > 0, 1.0, 0.0)
    earlier = (lax.broadcasted_iota(I32, (tm, tm), 0) < lax.broadcasted_iota(I32, (tm, tm), 1))
    before = _dot(member.astype(BF16), jnp.where(earlier, 1.0, 0.0).astype(BF16))

    @pl.when(pl.program_id(0) % (MOE_BLOCK // tm) == 0)
    def _():
        count_ref[...] = jnp.zeros_like(count_ref)

    rank_ref[...] = jnp.where(member > 0, before + count_ref[:, 0:1], -1.0).astype(I32)
    count_ref[...] += jnp.sum(member, axis=1, keepdims=True)
    total_ref[...] = count_ref[...]


def _outproj(x, oda, o_f, o_b, rest, mod, subln, retn, hgn, g64, w_out16, norm_w, wr_hi, wr_lo, rbias, st):
    tm = st.tile
    t = st.n_tiles * tm
    row = lambda i: (i, 0)
    const = lambda i: (0, 0)
    pair = lambda width: [pl.BlockSpec((tm, width), lambda i: (st.ctx_idx(i), 0)),
                          pl.BlockSpec((tm, width), lambda i: (st.lat_idx(i), 0))]
    return pl.pallas_call(
        functools.partial(_outproj_kernel, n_ctx_tiles=st.n_ctx),
        grid=(st.n_tiles,),
        in_specs=pair(D_MODEL) + pair(D_V) + pair(2 * D_REC) + pair(2 * D_REC) + [
            pl.BlockSpec((tm, D_REC), lambda i: (i, 3)),
            pl.BlockSpec((tm, D_REC), lambda i: (i, 8)),
            pl.BlockSpec((1, 1, 6 * D_MODEL), lambda i: (st.mod_idx(i), 0, 0)),
            pl.BlockSpec((1, D_V), const),
            pl.BlockSpec((1, D_REC), const),
            pl.BlockSpec((1, D_REC), const),
            pl.BlockSpec((LANES, LANES), const),
            pl.BlockSpec((D_MODEL, D_MODEL), const, pipeline_mode=pl.Buffered(1)),
            pl.BlockSpec((1, D_MODEL), const),
            pl.BlockSpec((N_EXPERTS, D_MODEL), const),
            pl.BlockSpec((N_EXPERTS, D_MODEL), const),
            pl.BlockSpec((N_EXPERTS, 1), const)],
        out_specs=[pl.BlockSpec((tm, D_MODEL), row), pl.BlockSpec((tm, D_MODEL), row),
                   pl.BlockSpec((N_EXPERTS, tm), lambda i: (0, i)), pl.BlockSpec((8, tm), lambda i: (0, i)),
                   pl.BlockSpec((8, LANES), lambda i: (0, i // (MOE_BLOCK // tm)))],
        out_shape=[jax.ShapeDtypeStruct((t, D_MODEL), F32), jax.ShapeDtypeStruct((t, D_MODEL), BF16),
                   jax.ShapeDtypeStruct((N_EXPERTS, t), F32), jax.ShapeDtypeStruct((8, t), I32),
                   jax.ShapeDtypeStruct((8, t // MOE_BLOCK * LANES), F32)],
        scratch_shapes=[pltpu.VMEM((8, LANES), F32)],
        compiler_params=_params("arbitrary"),
        name="outproj",
    )(*x, *oda, *o_f, *o_b, rest, rest, mod, subln, retn, hgn, g64, w_out16, norm_w, wr_hi, wr_lo, rbias)


def _moe_plan(rank, totals, n_blocks):
    nb, tr, spb = MOE_BLOCK, MOE_TILE, MOE_SLOTS_PER_BLOCK
    n_slots = n_blocks * spb
    rank = rank[:N_GROUPS].reshape(N_GROUPS, n_blocks, nb)
    mb = rank >= 0
    mi = mb.astype(I32)
    count = totals[:N_GROUPS].reshape(N_GROUPS, n_blocks, LANES)[:, :, 0].astype(I32)
    tiles = (count + tr - 1) // tr
    t_end = jnp.cumsum(tiles, axis=0)
    t_off = t_end - tiles
    n_tiles = t_end[-1]
    dest = jnp.where(mb, t_off[..., None] * tr + rank, -1)
    order = jnp.cumsum(mi, axis=0) - mi
    n_mem = mi.sum(0)
    row_of = lambda k: jnp.where(n_mem > k, jnp.where(mb & (order == k), dest, 0).sum(0), -1)
    s = jnp.arange(n_slots, dtype=I32)
    s_blk, s_tile = s // spb, s % spb
    used = s_tile < jnp.repeat(n_tiles, spb)
    s_grp = (s_tile[None, :] >= jnp.repeat(t_end, spb, axis=1)).sum(0).astype(I32)
    key = jnp.where(used, s_grp, N_GROUPS) * n_slots + s
    pos = (key[None, :] < key[:, None]).sum(1).astype(I32)
    slot_of_item = jnp.where(pos[None, :] == s[:, None], s[None, :], 0).sum(1).astype(I32)
    n_valid = used.sum().astype(I32)
    src = slot_of_item[jnp.minimum(s, n_valid - 1)]
    item_g, item_b, item_tile = jnp.stack([s_grp, s_blk, s_tile])[:, src]
    later = jnp.where(item_g[None, :] > item_g[:, None], item_g[None, :], N_GROUPS).min(axis=1)
    next_g = jnp.where(later < N_GROUPS, later, -1).astype(I32)
    blk = jnp.arange(n_blocks, dtype=I32)
    needs_hi = n_tiles > spb // 2
    hi_block = jnp.where((blk[None, :] <= blk[:, None]) & needs_hi[None, :], blk[None, :], 0).max(axis=1)
    return dict(dest=dest.reshape(N_GROUPS * n_blocks, 1, nb),
                row1=row_of(0).reshape(n_blocks, 1, nb), row2=row_of(1).reshape(n_blocks, 1, nb),
                n_tiles=n_tiles.astype(I32), hi_block=hi_block.astype(I32),
                item_g=item_g, item_b=item_b, item_tile=item_tile, next_g=next_g,
                item_valid=(s < n_valid).astype(I32), item_slot=slot_of_item)


def _moe_ffn_kernel(ig_ref, ib_ref, it_ref, iv_ref, is_ref, ng_ref, h_ref, dest_ref, gate_ref, wg_hbm, wu_hbm, wd_hbm,
                    y_ref, wg32, wu32, wd32, wg16, wu16, wd16, sem, *, layer):
    i = pl.program_id(0)
    g = ig_ref[i]
    prev_g = ig_ref[jnp.maximum(i - 1, 0)]

    def weight_copies(group):
        first = (layer * N_GROUPS + group) * EXPERTS_PER_GROUP
        return [pltpu.make_async_copy(src.at[pl.ds(first, EXPERTS_PER_GROUP)], dst, sem.at[k])
                for k, (src, dst) in enumerate(((wg_hbm, wg32), (wu_hbm, wu32), (wd_hbm, wd32)))]

    @pl.when(i == 0)
    def _():
        for c in weight_copies(g):
            c.start()

    @pl.when((i == 0) | (g != prev_g))
    def _():
        for c in weight_copies(g):
            c.wait()
        wg16[...] = wg32[...].astype(BF16)
        wu16[...] = wu32[...].astype(BF16)
        wd16[...] = wd32[...].astype(BF16)

        @pl.when(ng_ref[i] >= 0)
        def _():
            for c in weight_copies(ng_ref[i]):
                c.start()

    @pl.when(iv_ref[i] == 0)
    def _():
        y_ref[...] = jnp.zeros_like(y_ref)

    @pl.when(iv_ref[i] == 1)
    def _():
        rows = lax.broadcasted_iota(I32, (MOE_TILE, 1), 0) + it_ref[i] * MOE_TILE
        sort = jnp.where(dest_ref[0] == rows, 1.0, 0.0).astype(BF16)
        x = _dot(sort, h_ref[...]).astype(BF16)
        gates = functools.reduce(jnp.add, [lax.dot_general(sort, p, _NT, preferred_element_type=F32)
                                           for p in _split_bf16(gate_ref[...])])
        lane = lax.broadcasted_iota(I32, (1, N_EXPERTS), 1)
        acc = jnp.zeros((MOE_TILE, D_MODEL), F32)
        for e in range(EXPERTS_PER_GROUP):
            ge = jnp.sum(jnp.where(lane == g * EXPERTS_PER_GROUP + e, gates, 0.0), axis=1, keepdims=True)
            act = _silu(_dot(x, wg16[e])) * _dot(x, wu16[e])
            acc = acc + _dot((act * ge).astype(BF16), wd16[e])
        y_ref[...] = acc.astype(BF16)


def _moe_unsort_kernel(nt_ref, hb_ref, ylo_ref, yhi_ref, r1_ref, r2_ref, x_ref, mod_ref, oa_ref, ob_ref, acc_ref, *,
                       n_ctx_blocks):
    b = pl.program_id(0)
    acc_ref[...] = jnp.zeros_like(acc_ref)
    chunk = max(MOE_TILE, MXU_DEPTH)
    tiles_per_chunk = chunk // MOE_TILE
    row1, row2 = r1_ref[0], r2_ref[0]

    def accumulate(y_part, first):
        def body(ci, carry):
            r0 = pl.multiple_of((ci - first) * chunk, chunk)
            rows = lax.broadcasted_iota(I32, (chunk, 1), 0) + ci * chunk
            sort = jnp.where(row1 == rows, 1.0, jnp.where(row2 == rows, 1.0, 0.0)).astype(BF16)
            acc_ref[...] += lax.dot_general(sort, y_part[pl.ds(r0, chunk), :], _TN, preferred_element_type=F32)
            return carry
        return body

    half = ylo_ref.shape[0] // chunk
    n_chunks = (nt_ref[b] + tiles_per_chunk - 1) // tiles_per_chunk
    lax.fori_loop(0, jnp.minimum(n_chunks, half), accumulate(ylo_ref, 0), 0)
    lax.fori_loop(half, jnp.maximum(n_chunks, half), accumulate(yhi_ref, half), 0)
    result = lambda: x_ref[...] + mod_ref[0][:, 5 * D_MODEL:6 * D_MODEL] * acc_ref[...]

    @pl.when(b < n_ctx_blocks)
    def _():
        oa_ref[...] = result()

    @pl.when(b >= n_ctx_blocks)
    def _():
        ob_ref[...] = result()


def _moe(hn2, gate_t, rank, totals, w_gate, w_up, w_down, x_mid, mod, st, layer):
    nb, tr, spb = st.tile, MOE_TILE, MOE_SLOTS_PER_BLOCK
    n_blocks = st.n_tiles
    plan = _moe_plan(rank, totals, n_blocks)

    hbm = pl.BlockSpec(memory_space=pl.ANY)
    gu_shape, d_shape = (EXPERTS_PER_GROUP, D_MODEL, D_EXPERT), (EXPERTS_PER_GROUP, D_EXPERT, D_MODEL)
    y = pl.pallas_call(
        functools.partial(_moe_ffn_kernel, layer=layer),
        grid_spec=pltpu.PrefetchScalarGridSpec(
            num_scalar_prefetch=6,
            grid=(n_blocks * spb,),
            in_specs=[pl.BlockSpec((nb, D_MODEL), lambda i, ig, ib, *_: (ib[i], 0)),
                      pl.BlockSpec((1, 1, nb), lambda i, ig, ib, *_: (ig[i] * n_blocks + ib[i], 0, 0)),
                      pl.BlockSpec((N_EXPERTS, nb), lambda i, ig, ib, *_: (0, ib[i])),
                      hbm, hbm, hbm],
            out_specs=pl.BlockSpec((tr, D_MODEL), lambda i, ig, ib, it, iv, isl, ng: (isl[i], 0)),
            scratch_shapes=[pltpu.VMEM(gu_shape, F32), pltpu.VMEM(gu_shape, F32), pltpu.VMEM(d_shape, F32),
                            pltpu.VMEM(gu_shape, BF16), pltpu.VMEM(gu_shape, BF16), pltpu.VMEM(d_shape, BF16),
                            pltpu.SemaphoreType.DMA((3,))]),
        out_shape=jax.ShapeDtypeStruct((n_blocks * spb * tr, D_MODEL), BF16),
        compiler_params=_params("arbitrary"),
        name="moe_ffn",
    )(plan["item_g"], plan["item_b"], plan["item_tile"], plan["item_valid"], plan["item_slot"], plan["next_g"],
      hn2, plan["dest"], gate_t, w_gate, w_up, w_down)

    return pl.pallas_call(
        functools.partial(_moe_unsort_kernel, n_ctx_blocks=st.n_ctx),
        grid_spec=pltpu.PrefetchScalarGridSpec(
            num_scalar_prefetch=2,
            grid=(n_blocks,),
            in_specs=[pl.BlockSpec((spb // 2 * tr, D_MODEL), lambda b, nt, hb: (2 * b, 0)),
                      pl.BlockSpec((spb // 2 * tr, D_MODEL), lambda b, nt, hb: (2 * hb[b] + 1, 0)),
                      pl.BlockSpec((1, 1, nb), lambda b, nt, hb: (b, 0, 0)),
                      pl.BlockSpec((1, 1, nb), lambda b, nt, hb: (b, 0, 0)),
                      pl.BlockSpec((nb, D_MODEL), lambda b, nt, hb: (b, 0)),
                      pl.BlockSpec((1, 1, 6 * D_MODEL), lambda b, nt, hb: (st.mod_idx(b), 0, 0))],
            out_specs=[pl.BlockSpec((nb, D_MODEL), lambda b, nt, hb: (st.ctx_idx(b), 0)),
                       pl.BlockSpec((nb, D_MODEL), lambda b, nt, hb: (st.lat_idx(b), 0))],
            scratch_shapes=[pltpu.VMEM((nb, D_MODEL), F32)]),
        out_shape=[jax.ShapeDtypeStruct((st.n_ctx * nb, D_MODEL), F32),
                   jax.ShapeDtypeStruct(((n_blocks - st.n_ctx) * nb, D_MODEL), F32)],
        compiler_params=_params("arbitrary"),
        name="moe_unsort",
    )(plan["n_tiles"], plan["hi_block"], y, y, plan["row1"], plan["row2"], x_mid, mod)


def _block_avg(group):
    i = np.arange(LANES)
    return jnp.asarray((i[:, None] // group == i[None, :] // group) / group, dtype=BF16)


def _rope_tables(n_tokens, n_identity):
    rows = n_tokens // GRID_W
    pos_r = jnp.repeat(jnp.arange(rows, dtype=F32), GRID_W)
    pos_c = jnp.tile(jnp.arange(GRID_W, dtype=F32), rows)
    n_freq = DA_QK // 4
    inv = ROPE_BASE ** (-jnp.arange(n_freq, dtype=F32) / n_freq)
    ang = jnp.concatenate([pos_r[:, None] * inv, pos_c[:, None] * inv], axis=-1)
    cos = jnp.repeat(jnp.cos(ang), 2, axis=-1)
    sin = jnp.repeat(jnp.sin(ang), 2, axis=-1) * jnp.tile(jnp.asarray([-1.0, 1.0], F32), DA_QK // 2)
    cos = jnp.concatenate([jnp.ones((n_identity, DA_QK), F32), cos], axis=0)
    sin = jnp.concatenate([jnp.zeros((n_identity, DA_QK), F32), sin], axis=0)
    return jnp.tile(cos, (1, 2 * H_DA)), jnp.tile(sin, (1, 2 * H_DA))


def kernel(x_prompt, x_sample, cache_k, cache_v, state_ret, state_hgrn, c, c_ctx, w_in, w_out, w_ada, b_ada,
           norm_mix, norm_ffn, da_qnorm, da_knorm, da_lambda, da_subln, ret_decay, ret_norm, hg_lb, hg_norm,
           w_router, router_bias, w_gate, w_up, w_down):
    n_ctx, l_ctx, _ = x_prompt.shape
    n_lat, l_lat, _ = x_sample.shape
    t_ctx = n_ctx * l_ctx
    past = cache_k.shape[2]
    assert TOKEN_TILE % l_ctx == 0 and MOE_BLOCK % TOKEN_TILE == 0
    assert l_lat % MOE_BLOCK == 0 and t_ctx % MOE_BLOCK == 0 and l_lat % GRID_W == 0

    ctx_row = n_lat
    n_cond = -(-(n_lat + 1) // 8) * 8
    cond = jnp.zeros((n_cond, D_MODEL), F32).at[:n_lat].set(c).at[ctx_row].set(c_ctx)
    mod_all = _ada(cond, w_ada, b_ada)

    st = _Stream(t_ctx, n_lat, l_lat, TOKEN_TILE, ctx_row)
    st_moe = _Stream(t_ctx, n_lat, l_lat, MOE_BLOCK, ctx_row)
    g32, g64 = _block_avg(DA_QK), _block_avg(DA_V)
    cos_t, sin_t = _rope_tables(l_lat, TOKEN_TILE)
    lv = jnp.asarray(_level_table(REC_CHUNK))
    p_lb = jax.nn.softmax(hg_lb.astype(F32), axis=0)
    lb_all = jnp.cumsum(p_lb, axis=0) - p_lb[0]
    wr_hi = w_router.T.astype(BF16)
    wr_lo = (w_router.T - wr_hi.astype(F32)).astype(BF16)
    rbias = router_bias.astype(F32).reshape(N_EXPERTS, 1)
    flat_w = lambda w: w.reshape(DEPTH * N_EXPERTS, w.shape[2], w.shape[3])
    w_gate, w_up, w_down = flat_w(w_gate), flat_w(w_up), flat_w(w_down)
    flat_s = lambda s: s.astype(F32).reshape(n_lat, DEPTH, 2, D_REC, D_HEAD)
    states_in = (flat_s(state_ret), flat_s(state_hgrn))
    cache_k16 = cache_k.reshape(n_lat, DEPTH, past, D_QK).astype(BF16)
    cache_v16 = cache_v.reshape(n_lat, DEPTH, past, D_V).astype(BF16)

    x = (x_prompt.reshape(t_ctx, D_MODEL), x_sample.reshape(n_lat * l_lat, D_MODEL))
    caches, states_out = None, None
    for l in range(DEPTH):
        lam_init = 0.8 - 0.6 * math.exp(-0.3 * l)
        lp = da_lambda[l].astype(F32)
        lam = (jnp.exp(jnp.sum(lp[0] * lp[1])) - jnp.exp(jnp.sum(lp[2] * lp[3])) + lam_init).reshape(1)
        mod = mod_all[l].reshape(n_cond, 1, 6 * D_MODEL)
        qw = (jnp.tile(da_qnorm[l].astype(F32), 2 * H_DA) * (DA_QK ** -0.5)).reshape(1, D_QK)
        kw = jnp.tile(da_knorm[l].astype(F32), 2 * H_DA).reshape(1, D_QK)

        qb, kb, vb, rest, kc, vc = _inproj(x[0], x[1], mod, norm_mix[l].reshape(1, D_MODEL),
                                           w_in[l].astype(BF16), qw, kw, g32, cos_t, sin_t, st, l, n_ctx, l_ctx,
                                           caches)
        caches = (kc, vc)

        oda = (_attention(lam, qb, kb, vb, 0, n_ctx, l_ctx, CTX_PAIRS_PER_STEP),
               _attention(lam, qb, kb, vb, t_ctx, n_lat, l_lat, LAT_PAIRS_PER_STEP, l,
                          cache=(cache_k16, cache_v16)))

        ret_tabs = _retention_tables(-jnp.exp(ret_decay[l].astype(F32)), REC_CHUNK)
        lb_row = lb_all[l].reshape(1, D_REC)
        ofc, obc, s_ret, s_hg = _recurrence(rest, ret_tabs, lb_row, lv, 0, n_ctx, l_ctx, l,
                                            want_states=True, states_out=states_out)
        states_out = (s_ret, s_hg)
        ofl, obl = _recurrence(rest, ret_tabs, lb_row, lv, t_ctx, n_lat, l_lat, l, states_in=states_in)

        subln = (jnp.tile(da_subln[l].astype(F32), H_DA) * (1.0 - lam_init)).reshape(1, D_V)
        retn = jnp.tile(ret_norm[l].astype(F32), H_RET).reshape(1, D_REC)
        hgn = jnp.tile(hg_norm[l].astype(F32), H_HG).reshape(1, D_REC)
        x_mid, hn2, gate_t, rank, totals = _outproj(x, oda, (ofc, ofl), (obc, obl), rest, mod, subln, retn, hgn, g64,
                                      w_out[l].astype(BF16), norm_ffn[l].reshape(1, D_MODEL), wr_hi, wr_lo,
                                      rbias, st)
        x = _moe(hn2, gate_t, rank, totals, w_gate, w_up, w_down, x_mid, mod, st_moe, l)

    kc, vc = caches
    s_ret, s_hg = states_out
    return (x[0].reshape(n_ctx, l_ctx, D_MODEL), x[1].reshape(n_lat, l_lat, D_MODEL),
            kc.reshape(n_ctx, DEPTH, l_ctx, H_DA, 2, DA_QK), vc.reshape(n_ctx, DEPTH, l_ctx, H_DA, DA_V),
            s_ret.reshape(n_ctx, DEPTH, 2, H_RET, RET_DK, RET_DV),
            s_hg.reshape(n_ctx, DEPTH, 2, H_HG, HG_DK, HG_DV))
```

```python
import functools
import math

import numpy as np
import jax
import jax.numpy as jnp
from jax import lax
from jax.experimental import pallas as pl
from jax.experimental.pallas import tpu as pltpu

F32 = jnp.float32
BF16 = jnp.bfloat16
I32 = jnp.int32

D_MODEL = 1024
DEPTH = 2
GRID_W = 64
EPS = 1e-6
MIN_GATE = 1e-30
ROPE_BASE = 10000.0
H_DA, DA_QK, DA_V = 8, 32, 64
H_RET, RET_DK, RET_DV = 4, 64, 64
H_HG, HG_DK, HG_DV = 4, 64, 64
D_QK = H_DA * 2 * DA_QK
D_V = H_DA * DA_V
N_REC_HEADS = 4
D_HEAD = 64
D_REC = N_REC_HEADS * D_HEAD
D_REST = 9 * D_REC
D_IN = 2 * D_QK + D_V + D_REST
N_EXPERTS, N_GROUPS, EXPERTS_PER_GROUP = 16, 4, 4
D_EXPERT = 512
MASKED_SCORE = -2.0

LANES = 128
TOKEN_TILE = 512
Q_TILE = 1024
CTX_PAIRS_PER_STEP = 1
LAT_PAIRS_PER_STEP = 2
REC_CHUNK = 128
MOE_BLOCK = 1024
MOE_TILE = 256
MXU_DEPTH = 256
MOE_SLOTS_PER_BLOCK = -(-2 * MOE_BLOCK // MOE_TILE) + N_GROUPS
VMEM_LIMIT = 56 * 1024 * 1024

_NT = (((1,), (1,)), ((), ()))
_TN = (((0,), (0,)), ((), ()))


def _params(*sem):
    return pltpu.CompilerParams(dimension_semantics=sem, vmem_limit_bytes=VMEM_LIMIT)


def _dot(a, b):
    return jnp.dot(a, b, preferred_element_type=F32)


def _split_bf16(t, terms=2):
    out = []
    for _ in range(terms - 1):
        hi = t.astype(BF16)
        out.append(hi)
        t = t - hi.astype(F32)
    out.append(t.astype(BF16))
    return out


def _group_mean_sq(t, g):
    sq = (t * t).astype(BF16)
    outs = [_dot(sq[:, s * LANES:(s + 1) * LANES], g) for s in range(t.shape[1] // LANES)]
    return jnp.concatenate(outs, axis=1) if len(outs) > 1 else outs[0]


def _silu(t):
    return t * jax.nn.sigmoid(t)


class _Stream:
    def __init__(self, n_ctx_rows, n_lat_seq, lat_len, tile, ctx_row):
        self.tile = tile
        self.n_ctx = n_ctx_rows // tile
        self.per_seq = lat_len // tile
        self.n_tiles = self.n_ctx + n_lat_seq * self.per_seq
        self.ctx_row = ctx_row

    def ctx_idx(self, i):
        return jnp.minimum(i, self.n_ctx - 1)

    def lat_idx(self, i):
        return jnp.maximum(i - self.n_ctx, 0)

    def mod_idx(self, i):
        return jnp.where(i < self.n_ctx, self.ctx_row, (i - self.n_ctx) // self.per_seq)

    def pos_idx(self, i):
        return jnp.where(i < self.n_ctx, 0, 1 + (i - self.n_ctx) % self.per_seq)


def _ada_kernel(c_ref, w_ref, b_ref, o_ref):
    s = _silu(c_ref[...]).astype(BF16)
    o_ref[0] = _dot(s, w_ref[0].astype(BF16)) + b_ref[0]


def _ada(cond, w_ada, b_ada):
    rows = cond.shape[0]
    n_tile = 1536
    return pl.pallas_call(
        _ada_kernel,
        grid=(DEPTH, 6 * D_MODEL // n_tile),
        in_specs=[pl.BlockSpec((rows, D_MODEL), lambda l, j: (0, 0)),
                  pl.BlockSpec((1, D_MODEL, n_tile), lambda l, j: (l, 0, j)),
                  pl.BlockSpec((1, 1, n_tile), lambda l, j: (l, 0, j))],
        out_specs=pl.BlockSpec((1, rows, n_tile), lambda l, j: (l, 0, j)),
        out_shape=jax.ShapeDtypeStruct((DEPTH, rows, 6 * D_MODEL), F32),
        compiler_params=_params("arbitrary", "arbitrary"),
        name="ada",
    )(cond, w_ada, b_ada.reshape(DEPTH, 1, 6 * D_MODEL))


def _inproj_kernel(*refs, n_ctx_tiles, first):
    (xa_ref, xb_ref, mod_ref, nw_ref, w_ref, qw_ref, kw_ref, g32_ref, cos_ref, sin_ref) = refs[:10]
    q_ref, kb_ref, vb_ref, rest_ref, kc_ref, vc_ref = refs[10:] if first else refs[12:]
    is_ctx = pl.program_id(0) < n_ctx_tiles
    x = jnp.where(is_ctx, xa_ref[...], xb_ref[...])
    y = x * lax.rsqrt(jnp.mean(x * x, axis=-1, keepdims=True) + EPS) * nw_ref[...]
    mod = mod_ref[0]
    hn = (y * (1.0 + mod[:, D_MODEL:2 * D_MODEL]) + mod[:, 0:D_MODEL]).astype(BF16)
    g32 = g32_ref[...]
    lane = lax.broadcasted_iota(I32, (1, D_QK), 1)
    even = (lane & 1) == 0

    def qk_norm_rope(t, w):
        t = t * lax.rsqrt(_group_mean_sq(t, g32) + EPS) * w
        partner = jnp.where(even, pltpu.roll(t, D_QK - 1, 1), pltpu.roll(t, 1, 1))
        return t * cos_ref[...] + partner * sin_ref[...]

    q_ref[...] = qk_norm_rope(_dot(hn, w_ref[:, 0:D_QK]), qw_ref[...]).astype(BF16)
    k = qk_norm_rope(_dot(hn, w_ref[:, D_QK:2 * D_QK]), kw_ref[...])
    kb_ref[...] = k.astype(BF16)
    v = _dot(hn, w_ref[:, 2 * D_QK:2 * D_QK + D_V])
    vb_ref[...] = v.astype(BF16)
    rest_ref[...] = _dot(hn, w_ref[:, 2 * D_QK + D_V:D_IN])

    @pl.when(is_ctx)
    def _():
        n_seq, _, seq_len, _ = kc_ref.shape
        kc_ref[:, 0] = k.reshape(n_seq, seq_len, D_QK)
        vc_ref[:, 0] = v.reshape(n_seq, seq_len, D_V)
        if first:
            kc_ref[:, 1:] = jnp.zeros_like(kc_ref[:, 1:])
            vc_ref[:, 1:] = jnp.zeros_like(vc_ref[:, 1:])


def _inproj(xa, xb, mod, norm_w, w_in16, qw, kw, g32, cos_t, sin_t, st, layer, n_ctx_seq, l_ctx, caches):
    tm = st.tile
    t = st.n_tiles * tm
    first = caches is None
    row = lambda i: (i, 0)
    const = lambda i: (0, 0)
    in_specs = [pl.BlockSpec((tm, D_MODEL), lambda i: (st.ctx_idx(i), 0)),
                pl.BlockSpec((tm, D_MODEL), lambda i: (st.lat_idx(i), 0)),
                pl.BlockSpec((1, 1, 6 * D_MODEL), lambda i: (st.mod_idx(i), 0, 0)),
                pl.BlockSpec((1, D_MODEL), const),
                pl.BlockSpec((D_MODEL, D_IN), const, pipeline_mode=pl.Buffered(1)),
                pl.BlockSpec((1, D_QK), const),
                pl.BlockSpec((1, D_QK), const),
                pl.BlockSpec((LANES, LANES), const),
                pl.BlockSpec((tm, D_QK), lambda i: (st.pos_idx(i), 0)),
                pl.BlockSpec((tm, D_QK), lambda i: (st.pos_idx(i), 0))]
    args = [xa, xb, mod, norm_w, w_in16, qw, kw, g32, cos_t, sin_t]
    aliases = {}
    per_tile = tm // l_ctx
    if first:
        cache_spec = pl.BlockSpec((per_tile, DEPTH, l_ctx, D_QK), lambda i: (st.ctx_idx(i), 0, 0, 0))
    else:
        cache_spec = pl.BlockSpec((per_tile, 1, l_ctx, D_QK), lambda i: (st.ctx_idx(i), layer, 0, 0))
        in_specs += [pl.BlockSpec(memory_space=pl.ANY)] * 2
        args += list(caches)
        aliases = {10: 4, 11: 5}
    cache_shape = jax.ShapeDtypeStruct((n_ctx_seq, DEPTH, l_ctx, D_QK), F32)
    return pl.pallas_call(
        functools.partial(_inproj_kernel, n_ctx_tiles=st.n_ctx, first=first),
        grid=(st.n_tiles,),
        in_specs=in_specs,
        out_specs=[pl.BlockSpec((tm, D_QK), row), pl.BlockSpec((tm, D_QK), row),
                   pl.BlockSpec((tm, D_V), row), pl.BlockSpec((tm, D_REST), row), cache_spec, cache_spec],
        out_shape=[jax.ShapeDtypeStruct((t, D_QK), BF16), jax.ShapeDtypeStruct((t, D_QK), BF16),
                   jax.ShapeDtypeStruct((t, D_V), BF16), jax.ShapeDtypeStruct((t, D_REST), F32),
                   cache_shape, cache_shape],
        input_output_aliases=aliases,
        compiler_params=_params("arbitrary"),
        name="inproj",
    )(*args)


def _attn_kernel(*refs, with_cache):
    if with_cache:
        lam_ref, q_ref, k_ref, v_ref, ck_ref, cv_ref, o_ref = refs
    else:
        lam_ref, q_ref, k_ref, v_ref, o_ref = refs
    lam = lam_ref[0]
    lane = lax.broadcasted_iota(I32, (1, LANES), 1)
    left = lane < DA_V
    one = jnp.ones((), BF16)
    zero = jnp.zeros((), BF16)
    owns = [left, jnp.logical_not(left)]
    tq = min(Q_TILE, q_ref.shape[0])
    for p in range(q_ref.shape[1] // LANES):
        slab = slice(p * LANES, (p + 1) * LANES)
        segs = [(k_ref[:, slab], v_ref[:, slab])]
        if with_cache:
            segs.append((ck_ref[0, 0, :, slab], cv_ref[0, 0, :, slab]))
        vms = [[jnp.where(own, v, one) for _, v in segs] for own in owns]

        def q_block(i, carry, slab=slab, segs=segs, vms=vms):
            r0 = i * tq if isinstance(i, int) else pl.multiple_of(i * tq, tq)
            q = q_ref[pl.ds(r0, tq), slab]
            acc = jnp.zeros((tq, LANES), F32)
            for side in range(2):
                for r in range(2):
                    lo = side * DA_V + r * DA_QK
                    qm = jnp.where((lane >= lo) & (lane < lo + DA_QK), q, zero)
                    ss = [lax.dot_general(qm, k, _NT, preferred_element_type=F32) for k, _ in segs]
                    m = functools.reduce(jnp.maximum, [jnp.max(s, axis=-1, keepdims=True) for s in ss])
                    res = functools.reduce(
                        jnp.add, [_dot(jnp.exp(s - m).astype(BF16), vm) for s, vm in zip(ss, vms[side])])
                    den = pltpu.roll(res, DA_V, 1)
                    coef = 1.0 if r == 0 else -lam
                    acc = acc + jnp.where(owns[side], coef * (res / den), 0.0)
            o_ref[pl.ds(r0, tq), slab] = acc
            return carry

        if q_ref.shape[0] == tq:
            q_block(0, 0)
        else:
            lax.fori_loop(0, q_ref.shape[0] // tq, q_block, 0)


def _attention(lam, qb, kb, vb, row0, n_seq, seq_len, pairs_per_step, layer=0, cache=None):
    sb0 = row0 // seq_len
    width = pairs_per_step * LANES
    smap = lambda b, p: (sb0 + b, p)
    smem = pl.BlockSpec(memory_space=pltpu.SMEM)
    in_specs = [smem] + [pl.BlockSpec((seq_len, width), smap)] * 3
    args = [lam, qb, kb, vb]
    if cache is not None:
        ck, cv = cache
        cmap = lambda b, p: (b, layer, 0, p)
        in_specs += [pl.BlockSpec((1, 1, ck.shape[2], width), cmap),
                     pl.BlockSpec((1, 1, cv.shape[2], width), cmap)]
        args += [ck, cv]
    return pl.pallas_call(
        functools.partial(_attn_kernel, with_cache=cache is not None),
        grid=(n_seq, D_QK // width),
        in_specs=in_specs,
        out_specs=pl.BlockSpec((seq_len, width), lambda b, p: (b, p)),
        out_shape=jax.ShapeDtypeStruct((n_seq * seq_len, D_V), F32),
        compiler_params=_params("arbitrary", "arbitrary"),
        name="attn_lat" if cache is not None else "attn_ctx",
    )(*args)


def _level_table(c):
    t = np.arange(c)[:, None] ^ np.arange(c)[None, :]
    lv = np.where(t == 0, 32, 31 - np.floor(np.log2(np.maximum(t, 1))).astype(np.int64))
    return np.tile(lv, (N_REC_HEADS, 1)).astype(np.int32)


def _retention_tables(log_gamma, c):
    g = jnp.repeat(log_gamma, D_HEAD, axis=1)
    t = jnp.arange(c, dtype=F32)
    diff = t[:, None] - t[None, :]
    gh = log_gamma[:, :, None, None]
    d_f = jnp.where(diff >= 0, jnp.exp(gh[0] * jnp.maximum(diff, 0.0)), 0.0)
    d_b = jnp.where(diff <= 0, jnp.exp(gh[1] * jnp.maximum(-diff, 0.0)), 0.0)
    d = jnp.stack([d_f, d_b]).reshape(2, N_REC_HEADS * c, c)
    q_f, k_f = jnp.exp(g[0] * (t[:, None] + 1.0)), jnp.exp(g[0] * (c - 1.0 - t[:, None]))
    q_b, k_b = jnp.exp(g[1] * (c - t[:, None])), jnp.exp(g[1] * t[:, None])
    f = jnp.stack([jnp.stack([q_f, k_f]), jnp.stack([q_b, k_b])])
    a = jnp.exp(g * c).reshape(2, 1, D_REC)
    return d, f, a


def _rec_kernel(*refs, state_in, state_out, aliased):
    refs = list(refs)
    rf_ref, rb_ref, rd_ref, rfac_ref, ra_ref, lb_ref, lv_ref = refs[:7]
    del refs[:7]
    if state_in:
        sret_ref, shg_ref, t4_ref = refs[:3]
        del refs[:3]
    if state_out:
        t4t_ref = refs.pop(0)
    del refs[:aliased]
    of_ref, ob_ref = refs[:2]
    del refs[:2]
    if state_out:
        oret_ref, ohg_ref = refs[:2]
        del refs[:2]
    (st_scr,) = refs

    ci = pl.program_id(1)
    c = rf_ref.shape[0]
    lane = lax.broadcasted_iota(I32, (1, D_REC), 1)
    head = lane >> 6
    row = lax.broadcasted_iota(I32, (c, 1), 0)
    srow = lax.broadcasted_iota(I32, (D_REC, 1), 0)
    same_head = (srow >> 6) == head
    eye = srow == lane
    head_is = [head == h for h in range(N_REC_HEADS)]
    head_m16 = [jnp.broadcast_to(jnp.where(m, 1.0, 0.0), (c, D_REC)).astype(BF16) for m in head_is]
    head_m16_half = [jnp.broadcast_to(jnp.where(m, 1.0, 0.0), (c // 2, D_REC)).astype(BF16) for m in head_is]

    @pl.when(ci == 0)
    def _():
        if state_in:
            for d in range(2):
                for idx, ref in ((d, sret_ref), (2 + d, shg_ref)):
                    tiled = functools.reduce(jnp.add, [_dot(p, t4_ref[...]) for p in _split_bf16(ref[0, 0, d], 3)])
                    st_scr[idx] = jnp.where(same_head, tiled, 0.0)
        else:
            st_scr[...] = jnp.zeros_like(st_scr)

    def expand(t16):
        return jnp.concatenate([t16 * m for m in head_m16], axis=0)

    def nt(a, b):
        return lax.dot_general(a, b, _NT, preferred_element_type=F32)

    def half_rows(t, h, second):
        o = h if second else 0
        return jnp.concatenate([t[j * 2 * h + o:j * 2 * h + o + h] for j in range(c // (2 * h))], axis=0)

    def spread_rows(t, h, second):
        z = jnp.zeros((h, t.shape[1]), t.dtype)
        parts = []
        for j in range(c // (2 * h)):
            parts += [z, t[j * h:(j + 1) * h]] if second else [t[j * h:(j + 1) * h], z]
        return jnp.concatenate(parts, axis=0)

    def finish(q, k, v16, scores, qfac, kfac, a_row, st_ref, own_term=False):
        st = st_ref[...]
        o = _dot((q * qfac).astype(BF16), st.astype(BF16))
        if own_term:
            o = o + _dot((q * k).astype(BF16), jnp.where(same_head, 1.0, 0.0).astype(BF16)) * v16.astype(F32)
        o_stack = _dot(scores.astype(BF16), v16)
        for h in range(N_REC_HEADS):
            o = o + jnp.where(head_is[h], o_stack[h * c:(h + 1) * c], 0.0)
        kv = lax.dot_general((k * kfac).astype(BF16), v16, _TN, preferred_element_type=F32)
        a_col = jnp.sum(jnp.where(eye, a_row, 0.0), axis=1, keepdims=True)
        st_ref[...] = st * a_col + jnp.where(same_head, kv, 0.0)
        return o

    def tree_scores(q, k, la, reverse):
        lv = lv_ref[...]
        scores = jnp.zeros(lv.shape, F32)
        pre, tot = la, la
        h, level = 1, 31
        while h < c:
            bit = (row & h) != 0
            query_side = jnp.logical_not(bit) if reverse else bit
            x = jnp.exp(jnp.minimum(jnp.where(query_side, pre, tot - pre), 0.0))
            qx = q * x
            kx = jnp.where(query_side, 0.0, k * x).astype(BF16)
            if h < 8:
                s_level = nt(expand(qx.astype(BF16)), kx)
            else:
                qc = half_rows(qx, h, not reverse).astype(BF16)
                sc = nt(jnp.concatenate([qc * m for m in head_m16_half], axis=0), kx)
                s_level = jnp.concatenate([spread_rows(sc[hh * c // 2:(hh + 1) * c // 2], h, not reverse)
                                           for hh in range(N_REC_HEADS)], axis=0)
            scores = jnp.where(lv == level, s_level, scores)
            partner = jnp.where(bit, pltpu.roll(tot, h, 0), pltpu.roll(tot, c - h, 0))
            pre = pre + jnp.where(query_side, partner, 0.0)
            tot = tot + partner
            h, level = 2 * h, level - 1
        return scores, pre, tot

    lb = lb_ref[...]
    col = lambda ref, j: ref[:, j * D_REC:(j + 1) * D_REC]
    for d, (r_ref, o_ref) in enumerate(((rf_ref, of_ref), (rb_ref, ob_ref))):
        q, k, v16 = col(r_ref, 0), col(r_ref, 1) * (RET_DK ** -0.5), col(r_ref, 2).astype(BF16)
        scores = nt(expand(q.astype(BF16)), k.astype(BF16)) * rd_ref[d]
        o_ref[:, 0:D_REC] = finish(q, k, v16, scores, rfac_ref[d, 0], rfac_ref[d, 1], ra_ref[d], st_scr.at[d])
        z = col(r_ref, 5 + d)
        la = jnp.log(jnp.maximum(lb + (1.0 - lb) * jax.nn.sigmoid(z), MIN_GATE))
        q, k, v16 = _silu(col(r_ref, 4)), (1.0 - lb) * jax.nn.sigmoid(-z), col(r_ref, 7).astype(BF16)
        scores, pre, tot = tree_scores(q, k, la, reverse=d == 1)
        o_ref[:, D_REC:2 * D_REC] = finish(q, k, v16, scores, jnp.exp(jnp.minimum(pre, 0.0)),
                                           jnp.exp(jnp.minimum(tot - pre, 0.0)), jnp.exp(tot[0:1, :]),
                                           st_scr.at[2 + d], own_term=True)

    if state_out:
        @pl.when(ci == pl.num_programs(1) - 1)
        def _():
            for d in range(2):
                for idx, ref in ((d, oret_ref), (2 + d, ohg_ref)):
                    ref[0, 0, d] = functools.reduce(
                        jnp.add, [_dot(p, t4t_ref[...]) for p in _split_bf16(st_scr[idx], 3)])
            if state_out == "first":
                oret_ref[0, 1:] = jnp.zeros_like(oret_ref[0, 1:])
                ohg_ref[0, 1:] = jnp.zeros_like(ohg_ref[0, 1:])


def _recurrence(rest, ret_tabs, lb_row, lv, row0, n_seq, seq_len, layer, *, states_in=None, want_states=False,
                states_out=None):
    c = REC_CHUNK
    nc = seq_len // c
    cb0 = row0 // c
    const2 = lambda b, i: (0, 0)
    const3 = lambda b, i: (0, 0, 0)
    rd, rfac, ra = ret_tabs
    in_specs = [pl.BlockSpec((c, D_REST), lambda b, i: (cb0 + b * nc + i, 0)),
                pl.BlockSpec((c, D_REST), lambda b, i: (cb0 + b * nc + nc - 1 - i, 0)),
                pl.BlockSpec((2, N_REC_HEADS * c, c), const3),
                pl.BlockSpec((2, 2, c, D_REC), lambda b, i: (0, 0, 0, 0)),
                pl.BlockSpec((2, 1, D_REC), const3),
                pl.BlockSpec((1, D_REC), const2),
                pl.BlockSpec((N_REC_HEADS * c, c), const2)]
    args = [rest, rest, rd, rfac, ra, lb_row, lv]
    tile4 = jnp.tile(jnp.eye(D_HEAD, dtype=BF16), (1, N_REC_HEADS))
    if states_in is not None:
        blk = pl.BlockSpec((1, 1, 2, D_REC, D_HEAD), lambda b, i: (b, layer, 0, 0, 0))
        in_specs += [blk, blk, pl.BlockSpec((D_HEAD, D_REC), const2)]
        args += [states_in[0], states_in[1], tile4]
    if want_states:
        in_specs += [pl.BlockSpec((D_REC, D_HEAD), const2)]
        args += [tile4.T]
    out_specs = [pl.BlockSpec((c, 2 * D_REC), lambda b, i: (b * nc + i, 0)),
                 pl.BlockSpec((c, 2 * D_REC), lambda b, i: (b * nc + nc - 1 - i, 0))]
    out_shape = [jax.ShapeDtypeStruct((n_seq * seq_len, 2 * D_REC), F32)] * 2
    aliases = {}
    state_mode = None
    if want_states:
        st_shape = jax.ShapeDtypeStruct((n_seq, DEPTH, 2, D_REC, D_HEAD), F32)
        out_shape += [st_shape, st_shape]
        if states_out is None:
            state_mode = "first"
            out_specs += [pl.BlockSpec((1, DEPTH, 2, D_REC, D_HEAD), lambda b, i: (b, 0, 0, 0, 0))] * 2
        else:
            state_mode = "next"
            out_specs += [pl.BlockSpec((1, 1, 2, D_REC, D_HEAD), lambda b, i: (b, layer, 0, 0, 0))] * 2
            aliases = {len(args): 2, len(args) + 1: 3}
            in_specs += [pl.BlockSpec(memory_space=pl.ANY)] * 2
            args += list(states_out)
    return pl.pallas_call(
        functools.partial(_rec_kernel, state_in=states_in is not None, state_out=state_mode,
                          aliased=len(aliases)),
        grid=(n_seq, nc),
        in_specs=in_specs,
        out_specs=out_specs,
        out_shape=out_shape,
        scratch_shapes=[pltpu.VMEM((4, D_REC, D_REC), F32)],
        input_output_aliases=aliases,
        compiler_params=_params("arbitrary", "arbitrary"),
        name="rec_lat" if states_in is not None else "rec_ctx",
    )(*args)


def _route(logits, bias):
    scores = jax.nn.sigmoid(logits)
    sel = scores + bias
    srow = [scores[e:e + 1, :] for e in range(N_EXPERTS)]
    rows = [sel[e:e + 1, :] for e in range(N_EXPERTS)]
    gs = []
    for g in range(N_GROUPS):
        a, b, c, d = rows[4 * g:4 * g + 4]
        gs.append(functools.reduce(jnp.maximum, [a + b, a + c, a + d, b + c, b + d, c + d]))
    best = jnp.zeros_like(gs[0], dtype=I32)
    best_v = gs[0]
    for g in range(1, N_GROUPS):
        upd = gs[g] > best_v
        best = jnp.where(upd, g, best)
        best_v = jnp.where(upd, gs[g], best_v)
    masked = [jnp.where(best == e // EXPERTS_PER_GROUP, rows[e], MASKED_SCORE) for e in range(N_EXPERTS)]
    i1 = jnp.zeros_like(best)
    v1 = masked[0]
    for e in range(1, N_EXPERTS):
        upd = masked[e] > v1
        i1 = jnp.where(upd, e, i1)
        v1 = jnp.where(upd, masked[e], v1)
    i2 = jnp.zeros_like(best)
    v2 = jnp.full_like(v1, -jnp.inf)
    for e in range(N_EXPERTS):
        upd = (masked[e] > v2) & (i1 != e)
        i2 = jnp.where(upd, e, i2)
        v2 = jnp.where(upd, masked[e], v2)
    w1 = functools.reduce(jnp.add, [jnp.where(i1 == e, srow[e], 0.0) for e in range(N_EXPERTS)])
    w2 = functools.reduce(jnp.add, [jnp.where(i2 == e, srow[e], 0.0) for e in range(N_EXPERTS)])
    tot = w1 + w2
    w1, w2 = w1 / tot, w2 / tot
    expert = lax.broadcasted_iota(I32, logits.shape, 0)
    return jnp.where(expert == i1, w1, 0.0) + jnp.where(expert == i2, w2, 0.0)


def _outproj_kernel(xa_ref, xb_ref, odaa_ref, odab_ref, ofa_ref, ofb_ref, oba_ref, obb_ref, rg_ref, gg_ref,
                    mod_ref, subln_ref, retn_ref, hgn_ref, g64_ref, wout_ref, nffn_ref, wrh_ref, wrl_ref, rb_ref,
                    xmid_ref, hn2_ref, gate_ref, rank_ref, total_ref, count_ref, *, n_ctx_tiles):
    g64 = g64_ref[...]
    is_ctx = pl.program_id(0) < n_ctx_tiles
    pick = lambda a_ref, b_ref: jnp.where(is_ctx, a_ref[...], b_ref[...])

    def gnorm(t, w):
        return t * lax.rsqrt(_group_mean_sq(t, g64) + EPS) * w

    o_da = gnorm(pick(odaa_ref, odab_ref), subln_ref[...])
    o_rec = pick(ofa_ref, ofb_ref) + pick(oba_ref, obb_ref)
    o_ret = gnorm(o_rec[:, 0:D_REC], retn_ref[...]) * _silu(rg_ref[...])
    o_hg = gnorm(o_rec[:, D_REC:2 * D_REC], hgn_ref[...]) * _silu(gg_ref[...])
    mixed = jnp.concatenate([o_da, o_ret, o_hg], axis=1).astype(BF16)
    mod = mod_ref[0]
    x = pick(xa_ref, xb_ref) + mod[:, 2 * D_MODEL:3 * D_MODEL] * _dot(mixed, wout_ref[...])
    xmid_ref[...] = x
    y = x * lax.rsqrt(jnp.mean(x * x, axis=-1, keepdims=True) + EPS) * nffn_ref[...]
    hn = y * (1.0 + mod[:, 4 * D_MODEL:5 * D_MODEL]) + mod[:, 3 * D_MODEL:4 * D_MODEL]
    hi, lo = _split_bf16(hn)
    hn2_ref[...] = hi
    wrh = wrh_ref[...]
    logits = (lax.dot_general(wrh, hi, _NT, preferred_element_type=F32)
              + lax.dot_general(wrh, lo, _NT, preferred_element_type=F32)
              + lax.dot_general(wrl_ref[...], hi, _NT, preferred_element_type=F32))
    gate = _route(logits, rb_ref[...])
    gate_ref[...] = gate

    tm = gate.shape[1]
    member = jnp.concatenate(
        [jnp.max(gate[EXPERTS_PER_GROUP * g:EXPERTS_PER_GROUP * (g + 1)], axis=0, keepdims=True)
         for g in range(N_GROUPS)] + [jnp.zeros((8 - N_GROUPS, tm), F32)], axis=0)
    member = jnp.where(member > 0, 1.0, 0.0)
    earlier = (lax.broadcasted_iota(I32, (tm, tm), 0) < lax.broadcasted_iota(I32, (tm, tm), 1))
    before = _dot(member.astype(BF16), jnp.where(earlier, 1.0, 0.0).astype(BF16))

    @pl.when(pl.program_id(0) % (MOE_BLOCK // tm) == 0)
    def _():
        count_ref[...] = jnp.zeros_like(count_ref)

    rank_ref[...] = jnp.where(member > 0, before + count_ref[:, 0:1], -1.0).astype(I32)
    count_ref[...] += jnp.sum(member, axis=1, keepdims=True)
    total_ref[...] = count_ref[...]


def _outproj(x, oda, o_f, o_b, rest, mod, subln, retn, hgn, g64, w_out16, norm_w, wr_hi, wr_lo, rbias, st):
    tm = st.tile
    t = st.n_tiles * tm
    row = lambda i: (i, 0)
    const = lambda i: (0, 0)
    pair = lambda width: [pl.BlockSpec((tm, width), lambda i: (st.ctx_idx(i), 0)),
                          pl.BlockSpec((tm, width), lambda i: (st.lat_idx(i), 0))]
    return pl.pallas_call(
        functools.partial(_outproj_kernel, n_ctx_tiles=st.n_ctx),
        grid=(st.n_tiles,),
        in_specs=pair(D_MODEL) + pair(D_V) + pair(2 * D_REC) + pair(2 * D_REC) + [
            pl.BlockSpec((tm, D_REC), lambda i: (i, 3)),
            pl.BlockSpec((tm, D_REC), lambda i: (i, 8)),
            pl.BlockSpec((1, 1, 6 * D_MODEL), lambda i: (st.mod_idx(i), 0, 0)),
            pl.BlockSpec((1, D_V), const),
            pl.BlockSpec((1, D_REC), const),
            pl.BlockSpec((1, D_REC), const),
            pl.BlockSpec((LANES, LANES), const),
            pl.BlockSpec((D_MODEL, D_MODEL), const, pipeline_mode=pl.Buffered(1)),
            pl.BlockSpec((1, D_MODEL), const),
            pl.BlockSpec((N_EXPERTS, D_MODEL), const),
            pl.BlockSpec((N_EXPERTS, D_MODEL), const),
            pl.BlockSpec((N_EXPERTS, 1), const)],
        out_specs=[pl.BlockSpec((tm, D_MODEL), row), pl.BlockSpec((tm, D_MODEL), row),
                   pl.BlockSpec((N_EXPERTS, tm), lambda i: (0, i)), pl.BlockSpec((8, tm), lambda i: (0, i)),
                   pl.BlockSpec((8, LANES), lambda i: (0, i // (MOE_BLOCK // tm)))],
        out_shape=[jax.ShapeDtypeStruct((t, D_MODEL), F32), jax.ShapeDtypeStruct((t, D_MODEL), BF16),
                   jax.ShapeDtypeStruct((N_EXPERTS, t), F32), jax.ShapeDtypeStruct((8, t), I32),
                   jax.ShapeDtypeStruct((8, t // MOE_BLOCK * LANES), F32)],
        scratch_shapes=[pltpu.VMEM((8, LANES), F32)],
        compiler_params=_params("arbitrary"),
        name="outproj",
    )(*x, *oda, *o_f, *o_b, rest, rest, mod, subln, retn, hgn, g64, w_out16, norm_w, wr_hi, wr_lo, rbias)


def _moe_plan(rank, totals, n_blocks):
    nb, tr, spb = MOE_BLOCK, MOE_TILE, MOE_SLOTS_PER_BLOCK
    n_slots = n_blocks * spb
    rank = rank[:N_GROUPS].reshape(N_GROUPS, n_blocks, nb)
    mb = rank >= 0
    mi = mb.astype(I32)
    count = totals[:N_GROUPS].reshape(N_GROUPS, n_blocks, LANES)[:, :, 0].astype(I32)
    tiles = (count + tr - 1) // tr
    t_end = jnp.cumsum(tiles, axis=0)
    t_off = t_end - tiles
    n_tiles = t_end[-1]
    dest = jnp.where(mb, t_off[..., None] * tr + rank, -1)
    order = jnp.cumsum(mi, axis=0) - mi
    n_mem = mi.sum(0)
    row_of = lambda k: jnp.where(n_mem > k, jnp.where(mb & (order == k), dest, 0).sum(0), -1)
    s = jnp.arange(n_slots, dtype=I32)
    s_blk, s_tile = s // spb, s % spb
    used = s_tile < jnp.repeat(n_tiles, spb)
    s_grp = (s_tile[None, :] >= jnp.repeat(t_end, spb, axis=1)).sum(0).astype(I32)
    key = jnp.where(used, s_grp, N_GROUPS) * n_slots + s
    pos = (key[None, :] < key[:, None]).sum(1).astype(I32)
    slot_of_item = jnp.where(pos[None, :] == s[:, None], s[None, :], 0).sum(1).astype(I32)
    n_valid = used.sum().astype(I32)
    src = slot_of_item[jnp.minimum(s, n_valid - 1)]
    item_g, item_b, item_tile = jnp.stack([s_grp, s_blk, s_tile])[:, src]
    later = jnp.where(item_g[None, :] > item_g[:, None], item_g[None, :], N_GROUPS).min(axis=1)
    next_g = jnp.where(later < N_GROUPS, later, -1).astype(I32)
    blk = jnp.arange(n_blocks, dtype=I32)
    needs_hi = n_tiles > spb // 2
    hi_block = jnp.where((blk[None, :] <= blk[:, None]) & needs_hi[None, :], blk[None, :], 0).max(axis=1)
    return dict(dest=dest.reshape(N_GROUPS * n_blocks, 1, nb),
                row1=row_of(0).reshape(n_blocks, 1, nb), row2=row_of(1).reshape(n_blocks, 1, nb),
                n_tiles=n_tiles.astype(I32), hi_block=hi_block.astype(I32),
                item_g=item_g, item_b=item_b, item_tile=item_tile, next_g=next_g,
                item_valid=(s < n_valid).astype(I32), item_slot=slot_of_item)


def _moe_ffn_kernel(ig_ref, ib_ref, it_ref, iv_ref, is_ref, ng_ref, h_ref, dest_ref, gate_ref, wg_hbm, wu_hbm, wd_hbm,
                    y_ref, wg32, wu32, wd32, wg16, wu16, wd16, sem, *, layer):
    i = pl.program_id(0)
    g = ig_ref[i]
    prev_g = ig_ref[jnp.maximum(i - 1, 0)]

    def weight_copies(group):
        first = (layer * N_GROUPS + group) * EXPERTS_PER_GROUP
        return [pltpu.make_async_copy(src.at[pl.ds(first, EXPERTS_PER_GROUP)], dst, sem.at[k])
                for k, (src, dst) in enumerate(((wg_hbm, wg32), (wu_hbm, wu32), (wd_hbm, wd32)))]

    @pl.when(i == 0)
    def _():
        for c in weight_copies(g):
            c.start()

    @pl.when((i == 0) | (g != prev_g))
    def _():
        for c in weight_copies(g):
            c.wait()
        wg16[...] = wg32[...].astype(BF16)
        wu16[...] = wu32[...].astype(BF16)
        wd16[...] = wd32[...].astype(BF16)

        @pl.when(ng_ref[i] >= 0)
        def _():
            for c in weight_copies(ng_ref[i]):
                c.start(priority=1)

    @pl.when(iv_ref[i] == 0)
    def _():
        y_ref[...] = jnp.zeros_like(y_ref)

    @pl.when(iv_ref[i] == 1)
    def _():
        rows = lax.broadcasted_iota(I32, (MOE_TILE, 1), 0) + it_ref[i] * MOE_TILE
        sort = jnp.where(dest_ref[0] == rows, 1.0, 0.0).astype(BF16)
        x = _dot(sort, h_ref[...]).astype(BF16)
        gates = functools.reduce(jnp.add, [lax.dot_general(sort, p, _NT, preferred_element_type=F32)
                                           for p in _split_bf16(gate_ref[...])])
        lane = lax.broadcasted_iota(I32, (1, N_EXPERTS), 1)
        acc = jnp.zeros((MOE_TILE, D_MODEL), F32)
        for e in range(EXPERTS_PER_GROUP):
            ge = jnp.sum(jnp.where(lane == g * EXPERTS_PER_GROUP + e, gates, 0.0), axis=1, keepdims=True)
            act = _silu(_dot(x, wg16[e])) * _dot(x, wu16[e])
            acc = acc + _dot((act * ge).astype(BF16), wd16[e])
        y_ref[...] = acc.astype(BF16)


def _moe_unsort_kernel(nt_ref, hb_ref, ylo_ref, yhi_ref, r1_ref, r2_ref, x_ref, mod_ref, oa_ref, ob_ref, acc_ref, *,
                       n_ctx_blocks):
    b = pl.program_id(0)
    acc_ref[...] = jnp.zeros_like(acc_ref)
    chunk = max(MOE_TILE, MXU_DEPTH)
    tiles_per_chunk = chunk // MOE_TILE
    row1, row2 = r1_ref[0], r2_ref[0]

    def accumulate(y_part, first):
        def body(ci, carry):
            r0 = pl.multiple_of((ci - first) * chunk, chunk)
            rows = lax.broadcasted_iota(I32, (chunk, 1), 0) + ci * chunk
            sort = jnp.where(row1 == rows, 1.0, jnp.where(row2 == rows, 1.0, 0.0)).astype(BF16)
            acc_ref[...] += lax.dot_general(sort, y_part[pl.ds(r0, chunk), :], _TN, preferred_element_type=F32)
            return carry
        return body

    half = ylo_ref.shape[0] // chunk
    n_chunks = (nt_ref[b] + tiles_per_chunk - 1) // tiles_per_chunk
    lax.fori_loop(0, jnp.minimum(n_chunks, half), accumulate(ylo_ref, 0), 0)
    lax.fori_loop(half, jnp.maximum(n_chunks, half), accumulate(yhi_ref, half), 0)
    result = lambda: x_ref[...] + mod_ref[0][:, 5 * D_MODEL:6 * D_MODEL] * acc_ref[...]

    @pl.when(b < n_ctx_blocks)
    def _():
        oa_ref[...] = result()

    @pl.when(b >= n_ctx_blocks)
    def _():
        ob_ref[...] = result()


def _moe(hn2, gate_t, rank, totals, w_gate, w_up, w_down, x_mid, mod, st, layer):
    nb, tr, spb = st.tile, MOE_TILE, MOE_SLOTS_PER_BLOCK
    n_blocks = st.n_tiles
    plan = _moe_plan(rank, totals, n_blocks)

    hbm = pl.BlockSpec(memory_space=pl.ANY)
    gu_shape, d_shape = (EXPERTS_PER_GROUP, D_MODEL, D_EXPERT), (EXPERTS_PER_GROUP, D_EXPERT, D_MODEL)
    y = pl.pallas_call(
        functools.partial(_moe_ffn_kernel, layer=layer),
        grid_spec=pltpu.PrefetchScalarGridSpec(
            num_scalar_prefetch=6,
            grid=(n_blocks * spb,),
            in_specs=[pl.BlockSpec((nb, D_MODEL), lambda i, ig, ib, *_: (ib[i], 0)),
                      pl.BlockSpec((1, 1, nb), lambda i, ig, ib, *_: (ig[i] * n_blocks + ib[i], 0, 0)),
                      pl.BlockSpec((N_EXPERTS, nb), lambda i, ig, ib, *_: (0, ib[i])),
                      hbm, hbm, hbm],
            out_specs=pl.BlockSpec((tr, D_MODEL), lambda i, ig, ib, it, iv, isl, ng: (isl[i], 0)),
            scratch_shapes=[pltpu.VMEM(gu_shape, F32), pltpu.VMEM(gu_shape, F32), pltpu.VMEM(d_shape, F32),
                            pltpu.VMEM(gu_shape, BF16), pltpu.VMEM(gu_shape, BF16), pltpu.VMEM(d_shape, BF16),
                            pltpu.SemaphoreType.DMA((3,))]),
        out_shape=jax.ShapeDtypeStruct((n_blocks * spb * tr, D_MODEL), BF16),
        compiler_params=_params("arbitrary"),
        name="moe_ffn",
    )(plan["item_g"], plan["item_b"], plan["item_tile"], plan["item_valid"], plan["item_slot"], plan["next_g"],
      hn2, plan["dest"], gate_t, w_gate, w_up, w_down)

    return pl.pallas_call(
        functools.partial(_moe_unsort_kernel, n_ctx_blocks=st.n_ctx),
        grid_spec=pltpu.PrefetchScalarGridSpec(
            num_scalar_prefetch=2,
            grid=(n_blocks,),
            in_specs=[pl.BlockSpec((spb // 2 * tr, D_MODEL), lambda b, nt, hb: (2 * b, 0)),
                      pl.BlockSpec((spb // 2 * tr, D_MODEL), lambda b, nt, hb: (2 * hb[b] + 1, 0)),
                      pl.BlockSpec((1, 1, nb), lambda b, nt, hb: (b, 0, 0)),
                      pl.BlockSpec((1, 1, nb), lambda b, nt, hb: (b, 0, 0)),
                      pl.BlockSpec((nb, D_MODEL), lambda b, nt, hb: (b, 0)),
                      pl.BlockSpec((1, 1, 6 * D_MODEL), lambda b, nt, hb: (st.mod_idx(b), 0, 0))],
            out_specs=[pl.BlockSpec((nb, D_MODEL), lambda b, nt, hb: (st.ctx_idx(b), 0)),
                       pl.BlockSpec((nb, D_MODEL), lambda b, nt, hb: (st.lat_idx(b), 0))],
            scratch_shapes=[pltpu.VMEM((nb, D_MODEL), F32)]),
        out_shape=[jax.ShapeDtypeStruct((st.n_ctx * nb, D_MODEL), F32),
                   jax.ShapeDtypeStruct(((n_blocks - st.n_ctx) * nb, D_MODEL), F32)],
        compiler_params=_params("arbitrary"),
        name="moe_unsort",
    )(plan["n_tiles"], plan["hi_block"], y, y, plan["row1"], plan["row2"], x_mid, mod)


def _block_avg(group):
    i = np.arange(LANES)
    return jnp.asarray((i[:, None] // group == i[None, :] // group) / group, dtype=BF16)


def _rope_tables(n_tokens, n_identity):
    rows = n_tokens // GRID_W
    pos_r = jnp.repeat(jnp.arange(rows, dtype=F32), GRID_W)
    pos_c = jnp.tile(jnp.arange(GRID_W, dtype=F32), rows)
    n_freq = DA_QK // 4
    inv = ROPE_BASE ** (-jnp.arange(n_freq, dtype=F32) / n_freq)
    ang = jnp.concatenate([pos_r[:, None] * inv, pos_c[:, None] * inv], axis=-1)
    cos = jnp.repeat(jnp.cos(ang), 2, axis=-1)
    sin = jnp.repeat(jnp.sin(ang), 2, axis=-1) * jnp.tile(jnp.asarray([-1.0, 1.0], F32), DA_QK // 2)
    cos = jnp.concatenate([jnp.ones((n_identity, DA_QK), F32), cos], axis=0)
    sin = jnp.concatenate([jnp.zeros((n_identity, DA_QK), F32), sin], axis=0)
    return jnp.tile(cos, (1, 2 * H_DA)), jnp.tile(sin, (1, 2 * H_DA))


def kernel(x_prompt, x_sample, cache_k, cache_v, state_ret, state_hgrn, c, c_ctx, w_in, w_out, w_ada, b_ada,
           norm_mix, norm_ffn, da_qnorm, da_knorm, da_lambda, da_subln, ret_decay, ret_norm, hg_lb, hg_norm,
           w_router, router_bias, w_gate, w_up, w_down):
    n_ctx, l_ctx, _ = x_prompt.shape
    n_lat, l_lat, _ = x_sample.shape
    t_ctx = n_ctx * l_ctx
    past = cache_k.shape[2]
    assert TOKEN_TILE % l_ctx == 0 and MOE_BLOCK % TOKEN_TILE == 0
    assert l_lat % MOE_BLOCK == 0 and t_ctx % MOE_BLOCK == 0 and l_lat % GRID_W == 0

    ctx_row = n_lat
    n_cond = -(-(n_lat + 1) // 8) * 8
    cond = jnp.zeros((n_cond, D_MODEL), F32).at[:n_lat].set(c).at[ctx_row].set(c_ctx)
    mod_all = _ada(cond, w_ada, b_ada)

    st = _Stream(t_ctx, n_lat, l_lat, TOKEN_TILE, ctx_row)
    st_moe = _Stream(t_ctx, n_lat, l_lat, MOE_BLOCK, ctx_row)
    g32, g64 = _block_avg(DA_QK), _block_avg(DA_V)
    cos_t, sin_t = _rope_tables(l_lat, TOKEN_TILE)
    lv = jnp.asarray(_level_table(REC_CHUNK))
    p_lb = jax.nn.softmax(hg_lb.astype(F32), axis=0)
    lb_all = jnp.cumsum(p_lb, axis=0) - p_lb[0]
    wr_hi = w_router.T.astype(BF16)
    wr_lo = (w_router.T - wr_hi.astype(F32)).astype(BF16)
    rbias = router_bias.astype(F32).reshape(N_EXPERTS, 1)
    flat_w = lambda w: w.reshape(DEPTH * N_EXPERTS, w.shape[2], w.shape[3])
    w_gate, w_up, w_down = flat_w(w_gate), flat_w(w_up), flat_w(w_down)
    flat_s = lambda s: s.astype(F32).reshape(n_lat, DEPTH, 2, D_REC, D_HEAD)
    states_in = (flat_s(state_ret), flat_s(state_hgrn))
    cache_k16 = cache_k.reshape(n_lat, DEPTH, past, D_QK).astype(BF16)
    cache_v16 = cache_v.reshape(n_lat, DEPTH, past, D_V).astype(BF16)

    x = (x_prompt.reshape(t_ctx, D_MODEL), x_sample.reshape(n_lat * l_lat, D_MODEL))
    caches, states_out = None, None
    for l in range(DEPTH):
        lam_init = 0.8 - 0.6 * math.exp(-0.3 * l)
        lp = da_lambda[l].astype(F32)
        lam = (jnp.exp(jnp.sum(lp[0] * lp[1])) - jnp.exp(jnp.sum(lp[2] * lp[3])) + lam_init).reshape(1)
        mod = mod_all[l].reshape(n_cond, 1, 6 * D_MODEL)
        qw = (jnp.tile(da_qnorm[l].astype(F32), 2 * H_DA) * (DA_QK ** -0.5)).reshape(1, D_QK)
        kw = jnp.tile(da_knorm[l].astype(F32), 2 * H_DA).reshape(1, D_QK)

        qb, kb, vb, rest, kc, vc = _inproj(x[0], x[1], mod, norm_mix[l].reshape(1, D_MODEL),
                                           w_in[l].astype(BF16), qw, kw, g32, cos_t, sin_t, st, l, n_ctx, l_ctx,
                                           caches)
        caches = (kc, vc)

        oda = (_attention(lam, qb, kb, vb, 0, n_ctx, l_ctx, CTX_PAIRS_PER_STEP),
               _attention(lam, qb, kb, vb, t_ctx, n_lat, l_lat, LAT_PAIRS_PER_STEP, l,
                          cache=(cache_k16, cache_v16)))

        ret_tabs = _retention_tables(-jnp.exp(ret_decay[l].astype(F32)), REC_CHUNK)
        lb_row = lb_all[l].reshape(1, D_REC)
        ofc, obc, s_ret, s_hg = _recurrence(rest, ret_tabs, lb_row, lv, 0, n_ctx, l_ctx, l,
                                            want_states=True, states_out=states_out)
        states_out = (s_ret, s_hg)
        ofl, obl = _recurrence(rest, ret_tabs, lb_row, lv, t_ctx, n_lat, l_lat, l, states_in=states_in)

        subln = (jnp.tile(da_subln[l].astype(F32), H_DA) * (1.0 - lam_init)).reshape(1, D_V)
        retn = jnp.tile(ret_norm[l].astype(F32), H_RET).reshape(1, D_REC)
        hgn = jnp.tile(hg_norm[l].astype(F32), H_HG).reshape(1, D_REC)
        x_mid, hn2, gate_t, rank, totals = _outproj(x, oda, (ofc, ofl), (obc, obl), rest, mod, subln, retn, hgn, g64,
                                      w_out[l].astype(BF16), norm_ffn[l].reshape(1, D_MODEL), wr_hi, wr_lo,
                                      rbias, st)
        x = _moe(hn2, gate_t, rank, totals, w_gate, w_up, w_down, x_mid, mod, st_moe, l)

    kc, vc = caches
    s_ret, s_hg = states_out
    return (x[0].reshape(n_ctx, l_ctx, D_MODEL), x[1].reshape(n_lat, l_lat, D_MODEL),
            kc.reshape(n_ctx, DEPTH, l_ctx, H_DA, 2, DA_QK), vc.reshape(n_ctx, DEPTH, l_ctx, H_DA, DA_V),
            s_ret.reshape(n_ctx, DEPTH, 2, H_RET, RET_DK, RET_DV),
            s_hg.reshape(n_ctx, DEPTH, 2, H_HG, HG_DK, HG_DV))
```
